```python
import math
import jax
import jax.numpy as jnp
from jax import lax
import numpy as np

D_MODEL = 1024
BATCH = 8
SEQ = 2048
DEPTH = 2

GRID_W = 64
CTX_LEN = 256
HEAD_DIM = 64
ROPE_THETA = 10000.0
EPS = 1e-6
Q_BLOCK = 128
FFN_DIM = 2816
N_MOD = 9

A_Q_HEADS = 4
A_KV_HEADS = 2
WINDOW = 128
B_Q_HEADS = 4
B_KV_HEADS = 2
C_HEADS = 4
C_Q_LORA = 256
C_KV_LORA = 128
C_NOPE = 64
C_ROPE = 32
C_V = 64
D_HEADS = 4
D_HEAD_DIM = 64
D_STATE = 128
D_GROUPS = 2
D_CONV = 5
SSD_CHUNK = 128

MIX_A = A_Q_HEADS * HEAD_DIM
MIX_B = B_Q_HEADS * HEAD_DIM
MIX_C = C_HEADS * C_V
MIX_D = D_HEADS * D_HEAD_DIM
MIX_WIDTH = MIX_A + MIX_B + MIX_C + MIX_D
IN_A = (A_Q_HEADS + 2 * A_KV_HEADS) * HEAD_DIM
IN_B = (B_Q_HEADS + 2 * B_KV_HEADS) * HEAD_DIM
IN_C = C_Q_LORA + C_KV_LORA + C_ROPE
D_CONV_CH = MIX_D + 2 * D_GROUPS * D_STATE
IN_D = MIX_D + D_CONV_CH + 2 * D_HEADS
IN_WIDTH = IN_A + IN_B + IN_C + IN_D

kernel_name = "hybrid_prefix_dit_block"


def _rms(x, g):
    xf = x.astype(jnp.float32)
    y = xf * lax.rsqrt(jnp.mean(xf * xf, axis=-1, keepdims=True) + EPS)
    return (y * g.astype(jnp.float32)).astype(x.dtype)


def _modulate(n, shift, scale):
    return n * (1.0 + scale) + shift


def _swiglu(n, wi, wo):
    a, b = jnp.split(n @ wi, 2, axis=-1)
    return (jax.nn.silu(a) * b) @ wo


def _ffn_half(h, mods, norm_g, wi, wo):
    shift, scale, gate = mods
    n = _modulate(_rms(h, norm_g), shift, scale)
    return h + 0.5 * gate * _swiglu(n, wi, wo)


def _axial_angles(seq_len, rot_dim):
    rows = seq_len // GRID_W
    r = jnp.repeat(jnp.arange(rows, dtype=jnp.float32), GRID_W)
    col = jnp.tile(jnp.arange(GRID_W, dtype=jnp.float32), rows)
    axis_dim = rot_dim // 2
    inv = ROPE_THETA ** (-jnp.arange(0, axis_dim, 2, dtype=jnp.float32) / axis_dim)
    return (r[:, None] * inv[None, :], col[:, None] * inv[None, :])


def _rope_axis(x, ang):
    shape = (1, ang.shape[0]) + (1,) * (x.ndim - 3) + (ang.shape[1],)
    cos = jnp.cos(ang).reshape(shape)
    sin = jnp.sin(ang).reshape(shape)
    x1, x2 = jnp.split(x.astype(jnp.float32), 2, axis=-1)
    return jnp.concatenate([x1 * cos - x2 * sin, x2 * cos + x1 * sin], axis=-1).astype(x.dtype)


def _rope_2d(x, angs):
    xr, xc = jnp.split(x, 2, axis=-1)
    return jnp.concatenate([_rope_axis(xr, angs[0]), _rope_axis(xc, angs[1])], axis=-1)


def _attend(q, k, v, scale, sink=None):
    s = jnp.einsum('bqkgd,btkd->bkgqt', q, k).astype(jnp.float32) * scale
    if sink is not None:
        col = jnp.broadcast_to(sink.astype(jnp.float32)[None, :, :, None, None], s.shape[:-1] + (1,))
        p = jax.nn.softmax(jnp.concatenate([s, col], axis=-1), axis=-1)[..., :-1]
    else:
        p = jax.nn.softmax(s, axis=-1)
    return jnp.einsum('bkgqt,btke->bqkge', p.astype(v.dtype), v)


def _attend_blocked(q, k, v, scale):
    bsz, s = q.shape[:2]
    nb = s // Q_BLOCK
    qb = jnp.moveaxis(q.reshape((bsz, nb, Q_BLOCK) + q.shape[2:]), 1, 0)
    out = lax.map(lambda qi: _attend(qi, k, v, scale), qb)
    return jnp.moveaxis(out, 0, 1).reshape((bsz, s) + out.shape[3:])


def _window_attend(q, k, v, kc, vc, sink, scale):
    bsz, s = q.shape[:2]
    nb = s // WINDOW

    def bands(t):
        tp = jnp.pad(t, ((0, 0), (WINDOW, WINDOW), (0, 0), (0, 0)))
        tb = tp.reshape((bsz, nb + 2, WINDOW) + t.shape[2:])
        return jnp.concatenate([tb[:, :-2], tb[:, 1:-1], tb[:, 2:]], axis=2)

    kb, vb = bands(k), bands(v)
    qb = q.reshape((bsz, nb, WINDOW) + q.shape[2:])
    blk = jnp.arange(nb)[:, None] * WINDOW
    qpos = blk + jnp.arange(WINDOW)[None, :]
    kpos = blk - WINDOW + jnp.arange(3 * WINDOW)[None, :]
    valid = ((jnp.abs(qpos[:, :, None] - kpos[:, None, :]) <= WINDOW)
             & (kpos >= 0)[:, None, :] & (kpos < s)[:, None, :])
    s_loc = jnp.einsum('bnqkgd,bnjkd->bnkgqj', qb, kb).astype(jnp.float32) * scale
    s_loc = jnp.where(valid[None, :, None, None], s_loc, -jnp.inf)
    s_ctx = jnp.einsum('bnqkgd,bckd->bnkgqc', qb, kc).astype(jnp.float32) * scale
    s_sink = jnp.broadcast_to(sink.astype(jnp.float32)[None, None, :, :, None, None], s_loc.shape[:-1] + (1,))
    p = jax.nn.softmax(jnp.concatenate([s_loc, s_ctx, s_sink], axis=-1), axis=-1)
    p_loc = p[..., :3 * WINDOW].astype(v.dtype)
    p_ctx = p[..., 3 * WINDOW:3 * WINDOW + kc.shape[1]].astype(v.dtype)
    out = (jnp.einsum('bnkgqj,bnjkd->bnqkgd', p_loc, vb)
           + jnp.einsum('bnkgqc,bckd->bnqkgd', p_ctx, vc))
    return out.reshape((bsz, s) + out.shape[3:])


def _dwconv_centred(u, w, b):
    ch = u.shape[-1]
    y = lax.conv_general_dilated(u, w.reshape(w.shape[0], 1, ch).astype(u.dtype), window_strides=(1,),
                                 padding=[(D_CONV // 2, D_CONV // 2)],
                                 dimension_numbers=('NWC', 'WIO', 'NWC'), feature_group_count=ch)
    return y + b


def _ssd(x, dt, a_neg, bh, ch, h0):
    bsz, n, nh, hp = x.shape
    nc = n // SSD_CHUNK

    def chunk(t):
        return t.astype(jnp.float32).reshape((bsz, nc, SSD_CHUNK) + t.shape[2:])

    a = chunk(dt * a_neg)
    xdt = chunk(x.astype(jnp.float32) * dt[..., None])
    bc, cc = chunk(bh), chunk(ch)
    a_cs = jnp.cumsum(a, axis=2)
    seg = a_cs[:, :, :, None, :] - a_cs[:, :, None, :, :]
    lower = jnp.tril(jnp.ones((SSD_CHUNK, SSD_CHUNK), dtype=bool))
    decay = jnp.exp(jnp.where(lower[None, None, :, :, None], seg, -jnp.inf))
    scores = jnp.einsum('bcihn,bcjhn->bcijh', cc, bc) * decay
    y_diag = jnp.einsum('bcijh,bcjhp->bcihp', scores, xdt)
    decay_to_end = jnp.exp(a_cs[:, :, -1:, :] - a_cs)
    states = jnp.einsum('bcjhn,bcjh,bcjhp->bchpn', bc, decay_to_end, xdt)
    chunk_decay = jnp.exp(a_cs[:, :, -1, :])

    def step(h, inp):
        st, d = inp
        return d[:, :, None, None] * h + st, h

    h_final, h_in = lax.scan(step, h0, (jnp.swapaxes(states, 0, 1), jnp.swapaxes(chunk_decay, 0, 1)))
    h_in = jnp.swapaxes(h_in, 0, 1)
    y_off = jnp.einsum('bcihn,bchpn->bcihp', cc, h_in) * jnp.exp(a_cs)[..., None]
    y = (y_diag + y_off).reshape(bsz, n, nh, hp)
    return y.astype(x.dtype), h_final


def _mixer_a(px, pc, sink, ang, with_ctx):
    scale = HEAD_DIM ** -0.5
    grp = A_Q_HEADS // A_KV_HEADS
    nq, nkv = A_Q_HEADS * HEAD_DIM, A_KV_HEADS * HEAD_DIM

    def heads(p):
        bsz, n = p.shape[:2]
        q = p[..., :nq].reshape(bsz, n, A_KV_HEADS, grp, HEAD_DIM)
        k = p[..., nq:nq + nkv].reshape(bsz, n, A_KV_HEADS, HEAD_DIM)
        v = p[..., nq + nkv:].reshape(bsz, n, A_KV_HEADS, HEAD_DIM)
        return q, k, v

    qx, kx, vx = heads(px)
    qc, kc, vc = heads(pc)
    qx, kx = _rope_2d(qx, ang), _rope_2d(kx, ang)
    sink_kg = sink.reshape(A_KV_HEADS, grp)
    ox = _window_attend(qx, kx, vx, kc, vc, sink_kg, scale).reshape(px.shape[0], px.shape[1], MIX_A)
    oc = _attend(qc, kc, vc, scale, sink_kg).reshape(pc.shape[0], pc.shape[1], MIX_A) if with_ctx else None
    return ox, oc


def _mixer_b(px, pc, q_norm, k_norm, ang, with_ctx):
    scale = HEAD_DIM ** -0.5
    grp = B_Q_HEADS // B_KV_HEADS
    nq, nkv = B_Q_HEADS * HEAD_DIM, B_KV_HEADS * HEAD_DIM

    def heads(p):
        bsz, n = p.shape[:2]
        q = _rms(p[..., :nq].reshape(bsz, n, B_KV_HEADS, grp, HEAD_DIM), q_norm)
        k = _rms(p[..., nq:nq + nkv].reshape(bsz, n, B_KV_HEADS, HEAD_DIM), k_norm)
        v = p[..., nq + nkv:].reshape(bsz, n, B_KV_HEADS, HEAD_DIM)
        return q, k, v

    qx, kx, vx = heads(px)
    qc, kc, vc = heads(pc)
    qx, kx = _rope_2d(qx, ang), _rope_2d(kx, ang)
    k_all = jnp.concatenate([kx, kc], axis=1)
    v_all = jnp.concatenate([vx, vc], axis=1)
    ox = _attend_blocked(qx, k_all, v_all, scale).reshape(px.shape[0], px.shape[1], MIX_B)
    oc = _attend(qc, kc, vc, scale).reshape(pc.shape[0], pc.shape[1], MIX_B) if with_ctx else None
    return ox, oc


def _mixer_c(px, pc, q_norm, w_uq, kv_norm, w_ukv, ang, with_ctx):
    scale = (C_NOPE + C_ROPE) ** -0.5

    def proj(p, rope):
        bsz, n = p.shape[:2]
        cq = p[..., :C_Q_LORA]
        ckv = p[..., C_Q_LORA:C_Q_LORA + C_KV_LORA]
        k_rope = p[..., C_Q_LORA + C_KV_LORA:][:, :, None, :]
        q = (_rms(cq, q_norm) @ w_uq).reshape(bsz, n, C_HEADS, C_NOPE + C_ROPE)
        kv = (_rms(ckv, kv_norm) @ w_ukv).reshape(bsz, n, C_HEADS, C_NOPE + C_V)
        q_nope, q_rope = q[..., :C_NOPE], q[..., C_NOPE:]
        k_nope, v = kv[..., :C_NOPE], kv[..., C_NOPE:]
        if rope:
            q_rope, k_rope = _rope_2d(q_rope, ang), _rope_2d(k_rope, ang)
        q = jnp.concatenate([q_nope, q_rope], axis=-1)[:, :, :, None, :]
        k = jnp.concatenate([k_nope, jnp.broadcast_to(k_rope, k_nope.shape[:-1] + (C_ROPE,))], axis=-1)
        return q, k, v

    qx, kx, vx = proj(px, True)
    qc, kc, vc = proj(pc, False)
    k_all = jnp.concatenate([kx, kc], axis=1)
    v_all = jnp.concatenate([vx, vc], axis=1)
    ox = _attend_blocked(qx, k_all, v_all, scale).reshape(px.shape[0], px.shape[1], MIX_C)
    oc = _attend(qc, kc, vc, scale).reshape(pc.shape[0], pc.shape[1], MIX_C) if with_ctx else None
    return ox, oc


def _mixer_d(px, pc, conv_w, conv_b, a_log, dt_bias, d_skip, out_norm, with_ctx):
    a_neg = -jnp.exp(a_log.astype(jnp.float32))
    rep = D_HEADS // D_GROUPS
    gn = D_GROUPS * D_STATE

    def prep(p):
        bsz, n = p.shape[:2]
        z = p[..., :MIX_D]
        xbc = jax.nn.silu(_dwconv_centred(p[..., MIX_D:MIX_D + D_CONV_CH], conv_w, conv_b))
        xs = xbc[..., :MIX_D].reshape(bsz, n, D_HEADS, D_HEAD_DIM)
        bm = jnp.repeat(xbc[..., MIX_D:MIX_D + gn].reshape(bsz, n, D_GROUPS, D_STATE), rep, axis=2)
        cm = jnp.repeat(xbc[..., MIX_D + gn:].reshape(bsz, n, D_GROUPS, D_STATE), rep, axis=2)
        dt = jax.nn.softplus(p[..., MIX_D + D_CONV_CH:].astype(jnp.float32).reshape(bsz, n, 2, D_HEADS)
                             + dt_bias.astype(jnp.float32))
        return z, xs, bm, cm, dt

    def flip(t):
        return jnp.flip(t, axis=1)

    def scan_pair(xs, bm, cm, dt, h_f, h_b):
        y_f, s_f = _ssd(xs, dt[:, :, 0], a_neg[0], bm, cm, h_f)
        y_b, s_b = _ssd(flip(xs), flip(dt[:, :, 1]), a_neg[1], flip(bm), flip(cm), h_b)
        return y_f + flip(y_b), s_f, s_b

    def finish(y, xs, z):
        bsz, n = z.shape[:2]
        y = (y + d_skip[:, None] * xs).reshape(bsz, n, MIX_D)
        return _rms(y * jax.nn.silu(z), out_norm)

    zc, xc, bc, cc, dtc = prep(pc)
    h0 = jnp.zeros((pc.shape[0], D_HEADS, D_HEAD_DIM, D_STATE), jnp.float32)
    yc, sc_f, sc_b = scan_pair(xc, bc, cc, dtc, h0, h0)
    zx, xx, bx, cx, dtx = prep(px)
    yx, _, _ = scan_pair(xx, bx, cx, dtx, sc_f, sc_b)
    ox = finish(yx, xx, zx)
    oc = finish(yc, xc, zc) if with_ctx else None
    return ox, oc


def setup_inputs(seed: int = 0) -> dict:
    key = jax.random.key(seed)
    ks = list(jax.random.split(key, 32))

    def nrm(shape, scale):
        return jax.random.normal(ks.pop(), shape, jnp.float32) * scale

    def gain(shape):
        return 1.0 + nrm(shape, 0.05)

    L, D = DEPTH, D_MODEL
    x = nrm((BATCH, SEQ, D), 1.0)
    c = nrm((BATCH, D), 1.0)
    ctx = nrm((BATCH, CTX_LEN, D), 1.0)
    c_ctx = nrm((D,), 1.0)
    ada_w = nrm((L, D, N_MOD * D), 0.5 * D ** -0.5)
    ada_b = nrm((L, N_MOD * D), 0.02)
    ffn1_norm = gain((L, D))
    ffn1_wi = nrm((L, D, 2 * FFN_DIM), D ** -0.5)
    ffn1_wo = nrm((L, FFN_DIM, D), FFN_DIM ** -0.5)
    mix_norm = gain((L, D))
    w_in = nrm((L, D, IN_WIDTH), D ** -0.5)
    w_out = nrm((L, MIX_WIDTH, D), MIX_WIDTH ** -0.5)
    a_sink = nrm((L, A_Q_HEADS), 0.5)
    b_q_norm = gain((L, HEAD_DIM))
    b_k_norm = gain((L, HEAD_DIM))
    c_q_norm = gain((L, C_Q_LORA))
    c_w_uq = nrm((L, C_Q_LORA, C_HEADS * (C_NOPE + C_ROPE)), C_Q_LORA ** -0.5)
    c_kv_norm = gain((L, C_KV_LORA))
    c_w_ukv = nrm((L, C_KV_LORA, C_HEADS * (C_NOPE + C_V)), C_KV_LORA ** -0.5)
    d_conv_w = nrm((L, D_CONV, D_CONV_CH), D_CONV ** -0.5)
    d_conv_b = nrm((L, D_CONV_CH), 0.02)
    dt0 = jnp.exp(jax.random.uniform(ks.pop(), (L, 2, D_HEADS), jnp.float32,
                                     minval=math.log(1e-3), maxval=math.log(1e-1)))
    d_dt_bias = dt0 + jnp.log(-jnp.expm1(-dt0))
    d_a_log = jnp.log(jax.random.uniform(ks.pop(), (L, 2, D_HEADS), jnp.float32, minval=1.0, maxval=16.0))
    d_skip = gain((L, D_HEADS))
    d_out_norm = gain((L, MIX_D))
    ffn2_norm = gain((L, D))
    ffn2_wi = nrm((L, D, 2 * FFN_DIM), D ** -0.5)
    ffn2_wo = nrm((L, FFN_DIM, D), FFN_DIM ** -0.5)
    final_norm = gain((D,))
    return {"x": x, "c": c, "ctx": ctx, "c_ctx": c_ctx, "ada_w": ada_w, "ada_b": ada_b,
            "ffn1_norm": ffn1_norm, "ffn1_wi": ffn1_wi, "ffn1_wo": ffn1_wo,
            "mix_norm": mix_norm, "w_in": w_in, "w_out": w_out, "a_sink": a_sink,
            "b_q_norm": b_q_norm, "b_k_norm": b_k_norm, "c_q_norm": c_q_norm, "c_w_uq": c_w_uq,
            "c_kv_norm": c_kv_norm, "c_w_ukv": c_w_ukv, "d_conv_w": d_conv_w, "d_conv_b": d_conv_b,
            "d_a_log": d_a_log, "d_dt_bias": d_dt_bias, "d_skip": d_skip, "d_out_norm": d_out_norm,
            "ffn2_norm": ffn2_norm, "ffn2_wi": ffn2_wi, "ffn2_wo": ffn2_wo, "final_norm": final_norm}


def reference(x, c, ctx, c_ctx, ada_w, ada_b, ffn1_norm, ffn1_wi, ffn1_wo, mix_norm, w_in, w_out,
              a_sink, b_q_norm, b_k_norm, c_q_norm, c_w_uq, c_kv_norm, c_w_ukv, d_conv_w, d_conv_b,
              d_a_log, d_dt_bias, d_skip, d_out_norm, ffn2_norm, ffn2_wi, ffn2_wo, final_norm):
    hx, hc = x, ctx
    seq_len = x.shape[1]
    ang_head = _axial_angles(seq_len, HEAD_DIM)
    ang_mla = _axial_angles(seq_len, C_ROPE)
    cuts = [IN_A, IN_A + IN_B, IN_A + IN_B + IN_C]
    for l in range(DEPTH):
        with_ctx = l < DEPTH - 1
        mx = jnp.split((jax.nn.silu(c) @ ada_w[l] + ada_b[l])[:, None, :], N_MOD, axis=-1)
        mc = jnp.split((jax.nn.silu(c_ctx) @ ada_w[l] + ada_b[l])[None, None, :], N_MOD, axis=-1)
        hx = _ffn_half(hx, mx[0:3], ffn1_norm[l], ffn1_wi[l], ffn1_wo[l])
        hc = _ffn_half(hc, mc[0:3], ffn1_norm[l], ffn1_wi[l], ffn1_wo[l])
        px = _modulate(_rms(hx, mix_norm[l]), mx[3], mx[4]) @ w_in[l]
        pc = _modulate(_rms(hc, mix_norm[l]), mc[3], mc[4]) @ w_in[l]
        pxa, pxb, pxc, pxd = jnp.split(px, cuts, axis=-1)
        pca, pcb, pcc, pcd = jnp.split(pc, cuts, axis=-1)
        oxa, oca = _mixer_a(pxa, pca, a_sink[l], ang_head, with_ctx)
        oxb, ocb = _mixer_b(pxb, pcb, b_q_norm[l], b_k_norm[l], ang_head, with_ctx)
        oxc, occ = _mixer_c(pxc, pcc, c_q_norm[l], c_w_uq[l], c_kv_norm[l], c_w_ukv[l], ang_mla, with_ctx)
        oxd, ocd = _mixer_d(pxd, pcd, d_conv_w[l], d_conv_b[l], d_a_log[l], d_dt_bias[l], d_skip[l],
                            d_out_norm[l], with_ctx)
        hx = hx + mx[5] * (jnp.concatenate([oxa, oxb, oxc, oxd], axis=-1) @ w_out[l])
        hx = _ffn_half(hx, mx[6:9], ffn2_norm[l], ffn2_wi[l], ffn2_wo[l])
        if with_ctx:
            hc = hc + mc[5] * (jnp.concatenate([oca, ocb, occ, ocd], axis=-1) @ w_out[l])
            hc = _ffn_half(hc, mc[6:9], ffn2_norm[l], ffn2_wi[l], ffn2_wo[l])
    return _rms(hx, final_norm)
```

```python
import functools
import math

import numpy as np
import jax
import jax.numpy as jnp
from jax import lax
from jax.experimental import pallas as pl
from jax.experimental.pallas import tpu as pltpu

D_MODEL = 1024
BATCH = 8
SEQ = 2048
DEPTH = 2
GRID_W = 64
CTX_LEN = 256
HEAD_DIM = 64
ROPE_THETA = 10000.0
EPS = 1e-6
FFN_DIM = 2816
N_MOD = 9
WINDOW = 128
C_HEADS = 4
C_Q_LORA = 256
C_KV_LORA = 128
C_NOPE = 64
C_ROPE = 32
C_V = 64
D_HEADS = 4
D_STATE = 128
D_CONV = 5
SSD_CHUNK = 128
MIX = 256
IN_AB = 1024
IN_C = C_Q_LORA + C_KV_LORA + C_ROPE
D_CONV_CH = MIX + 2 * 2 * D_STATE
IN_D = MIX + D_CONV_CH + 2 * D_HEADS

NSEG = BATCH + 1
SEG = SEQ
assert BATCH * CTX_LEN == SEG
TOK = CTX_LEN + SEQ
N_CHUNK = TOK // SSD_CHUNK

LANE = 128
VMEM_LIMIT = 56 * 1024 * 1024

F32 = jnp.float32
BF16 = jnp.bfloat16


def _dot(a, b):
    return jnp.dot(a, b, preferred_element_type=F32)


def _dot_nt(a, b):
    return lax.dot_general(a, b, (((1,), (1,)), ((), ())), preferred_element_type=F32)


def _dot_tn(a, b):
    return lax.dot_general(a, b, (((0,), (0,)), ((), ())), preferred_element_type=F32)


def _sigmoid(x):
    return 1.0 / (1.0 + jnp.exp(-x))


def _rms(x, g):
    return x * lax.rsqrt(jnp.mean(x * x, axis=-1, keepdims=True) + EPS) * g


def _full(shape):
    nd = len(shape)
    return pl.BlockSpec(shape, lambda *_: (0,) * nd)


def _resident(shape):
    nd = len(shape)
    return pl.BlockSpec(shape, lambda *_: (0,) * nd, pipeline_mode=pl.Buffered(1))


def _params(sem):
    return pltpu.CompilerParams(dimension_semantics=sem, vmem_limit_bytes=VMEM_LIMIT)


def _mods_kernel(c_ref, w_ref, b_ref, o_ref):
    c = c_ref[...]
    s = (c * _sigmoid(c)).astype(BF16)
    o_ref[...] = _dot(s, w_ref[...].astype(BF16)) + b_ref[...]


def _mods_call(cvec, ada_w, ada_b):
    n_l = ada_w.shape[0]
    return pl.pallas_call(
        _mods_kernel,
        out_shape=jax.ShapeDtypeStruct((n_l, 16, N_MOD * D_MODEL), F32),
        grid=(n_l, N_MOD),
        in_specs=[
            pl.BlockSpec((16, D_MODEL), lambda l, n: (0, 0)),
            pl.BlockSpec((None, D_MODEL, D_MODEL), lambda l, n: (l, 0, n)),
            pl.BlockSpec((None, 1, D_MODEL), lambda l, n: (l, 0, n)),
        ],
        out_specs=pl.BlockSpec((None, 16, D_MODEL), lambda l, n: (l, 0, n)),
        compiler_params=_params(("arbitrary", "arbitrary")),
        name="mods",
    )(cvec, ada_w, ada_b.reshape(n_l, 1, N_MOD * D_MODEL))


FFN_TM = 512
FFN_FC = 256


def _ffn_kernel(*refs, k0, has_pre, has_final):
    h_ref, mods_ref, g_ref, wi_ref, wo_ref = refs[:5]
    pos = 5
    if has_pre:
        oa_ref, ob_ref, oc_ref, od_ref, wout_ref = refs[pos:pos + 5]
        pos += 5
    if has_final:
        gf_ref = refs[pos]
        pos += 1
    out_ref, hm_ref = refs[pos], refs[pos + 1]

    x = h_ref[...]
    m = mods_ref[...]
    if has_pre:
        o = _dot(oa_ref[...], wout_ref[0:MIX, :])
        o = o + _dot(ob_ref[...], wout_ref[MIX:2 * MIX, :])
        o = o + _dot(oc_ref[...], wout_ref[2 * MIX:3 * MIX, :])
        o = o + _dot(od_ref[...], wout_ref[3 * MIX:4 * MIX, :])
        x = x + m[5:6] * o
    n = _rms(x, g_ref[...]) * (1.0 + m[k0 + 1:k0 + 2]) + m[k0:k0 + 1]
    nb = n.astype(BF16)
    for c in range(FFN_DIM // FFN_FC):
        lo = c * FFN_FC
        a = _dot(nb, wi_ref[:, lo:lo + FFN_FC])
        b = _dot(nb, wi_ref[:, FFN_DIM + lo:FFN_DIM + lo + FFN_FC])
        hm_ref[:, lo:lo + FFN_FC] = (a * _sigmoid(a) * b).astype(BF16)
    y = _dot(hm_ref[...], wo_ref[...])
    out = x + 0.5 * m[k0 + 2:k0 + 3] * y
    if has_final:
        out = _rms(out, gf_ref[...])
    out_ref[...] = out


def _ffn_call(h, mods_l, g, wi, wo, *, k0, seg_off, pre=None, o_off=0, final_g=None, name):
    nseg = h.shape[0] - seg_off
    tm = FFN_TM
    tok = lambda s, i: (s + seg_off, i, 0)
    in_specs = [
        pl.BlockSpec((None, tm, D_MODEL), tok),
        pl.BlockSpec((None, N_MOD, D_MODEL), lambda s, i: (s + seg_off, 0, 0)),
        _full((1, D_MODEL)),
        _resident((D_MODEL, 2 * FFN_DIM)),
        _resident((FFN_DIM, D_MODEL)),
    ]
    args = [h, mods_l, g.reshape(1, D_MODEL), wi, wo]
    if pre is not None:
        oa, ob, oc, od, wout = pre
        ospec = pl.BlockSpec((None, tm, MIX), lambda s, i: (s + o_off, i, 0))
        in_specs += [ospec, ospec, ospec, ospec, _resident((4 * MIX, D_MODEL))]
        args += [oa, ob, oc, od, wout]
    if final_g is not None:
        in_specs.append(_full((1, D_MODEL)))
        args.append(final_g.reshape(1, D_MODEL))
    kern = functools.partial(_ffn_kernel, k0=k0, has_pre=pre is not None, has_final=final_g is not None)
    return pl.pallas_call(
        kern,
        out_shape=jax.ShapeDtypeStruct((nseg, SEG, D_MODEL), F32),
        grid=(nseg, SEG // tm),
        in_specs=in_specs,
        out_specs=pl.BlockSpec((None, tm, D_MODEL), lambda s, i: (s, i, 0)),
        scratch_shapes=[pltpu.VMEM((tm, FFN_DIM), BF16)],
        compiler_params=_params(("arbitrary", "arbitrary")),
        name=name,
    )(*args)


PROJ_TM = 256
COL_AB = 0
COL_C = IN_AB
COL_D = COL_C + 512
PROJ_W = COL_D + MIX + D_CONV_CH + LANE


def _swap_halves(x, half):
    lane = lax.broadcasted_iota(jnp.int32, x.shape, 1)
    first = (lane & half) == 0
    up = pltpu.roll(x, LANE - half, axis=1)
    dn = pltpu.roll(x, half, axis=1)
    return jnp.where(first, up, dn)


def _rope(x, cos, sin, half):
    return x * cos + _swap_halves(x, half) * sin


def _head_rms(x, bd, g):
    sq = x * x
    hi = sq.astype(BF16)
    lo = (sq - hi.astype(F32)).astype(BF16)
    ms = _dot(hi, bd) + _dot(lo, bd)
    return x * lax.rsqrt(ms + EPS) * g


def _inproj_kernel(h_ref, mods_ref, g_ref, w_ref, cab_ref, sab_ref, cm_ref, sm_ref, place_ref, bd_ref,
                   bqn_ref, bkn_ref, cqn_ref, wuq_ref, ckvn_ref, wukvk_ref, wukvv_ref,
                   qa_ref, ka_ref, va_ref, qb_ref, kb_ref, vb_ref, qc_ref, kc_ref, vc_ref,
                   z_ref, xbc_ref, dt_ref):
    x = h_ref[...]
    m = mods_ref[...]
    n = _rms(x, g_ref[...]) * (1.0 + m[4:5]) + m[3:4]
    nb = n.astype(BF16)
    cab, sab = cab_ref[...], sab_ref[...]
    cm, sm = cm_ref[...], sm_ref[...]
    place = place_ref[...]
    bd = bd_ref[...]
    scale_ab = HEAD_DIM ** -0.5
    scale_c = (C_NOPE + C_ROPE) ** -0.5

    for mixer, (q_ref, k_ref, v_ref) in enumerate(((qa_ref, ka_ref, va_ref), (qb_ref, kb_ref, vb_ref))):
        base = COL_AB + mixer * 512
        p = _dot(nb, w_ref[:, base:base + 512])
        for s in range(2):
            q = p[:, s * LANE:(s + 1) * LANE]
            if mixer == 1:
                q = _head_rms(q, bd, bqn_ref[...])
            q = _rope(q, cab, sab, 16) * scale_ab
            q_ref[:, s * LANE:(s + 1) * LANE] = q.astype(BF16)
        k = p[:, 2 * LANE:3 * LANE]
        if mixer == 1:
            k = _head_rms(k, bd, bkn_ref[...])
        k = _rope(k, cab, sab, 16)
        k_ref[...] = _dot(k.astype(BF16), place).astype(BF16)
        v = p[:, 3 * LANE:4 * LANE]
        v_ref[...] = _dot(v.astype(BF16), place).astype(BF16)

    p = _dot(nb, w_ref[:, COL_C:COL_C + 512])
    cq = _rms(p[:, 0:C_Q_LORA], cqn_ref[...]).astype(BF16)
    q = _dot(cq, wuq_ref[...])
    ckv = _rms(p[:, C_Q_LORA:C_Q_LORA + C_KV_LORA], ckvn_ref[...]).astype(BF16)
    kn = _dot(ckv, wukvk_ref[...])
    vc_ref[...] = _dot(ckv, wukvv_ref[...]).astype(BF16)
    kr = _rope(p[:, 3 * LANE:4 * LANE], cm, sm, 8)
    for hh in range(C_HEADS):
        sl = slice(hh * LANE, (hh + 1) * LANE)
        qc_ref[:, sl] = (_rope(q[:, sl], cm, sm, 8) * scale_c).astype(BF16)
        kc_ref[:, sl] = (kn[:, sl] + kr).astype(BF16)

    z_ref[...] = _dot(nb, w_ref[:, COL_D:COL_D + MIX])
    for c in range(D_CONV_CH // 256):
        lo = COL_D + MIX + c * 256
        xbc_ref[:, c * 256:(c + 1) * 256] = _dot(nb, w_ref[:, lo:lo + 256])
    lo = COL_D + MIX + D_CONV_CH
    dt_ref[...] = _dot(nb, w_ref[:, lo:lo + LANE])


def _inproj_call(h, mods_l, g, w, tabs, consts, bqn, bkn, cqn, wuq, ckvn, wukvk, wukvv):
    tm = PROJ_TM
    cab, sab, cm, sm = tabs
    place, bd = consts
    tok = lambda s, i: (s, i, 0)
    tab = pl.BlockSpec((None, tm, LANE), lambda s, i: (jnp.minimum(s, 1), i, 0))
    widths = ([(MIX, BF16), (512, BF16), (512, BF16)] * 2 + [(512, BF16)] * 3
              + [(MIX, F32), (D_CONV_CH, F32), (LANE, F32)])
    return pl.pallas_call(
        _inproj_kernel,
        out_shape=[jax.ShapeDtypeStruct((NSEG, SEG, wd), dt) for wd, dt in widths],
        grid=(NSEG, SEG // tm),
        in_specs=[
            pl.BlockSpec((None, tm, D_MODEL), tok),
            pl.BlockSpec((None, N_MOD, D_MODEL), lambda s, i: (s, 0, 0)),
            _full((1, D_MODEL)),
            _resident((D_MODEL, PROJ_W)),
            tab, tab, tab, tab,
            _full((LANE, 512)), _full((LANE, LANE)),
            _full((1, LANE)), _full((1, LANE)),
            _full((1, C_Q_LORA)), _full((C_Q_LORA, 512)),
            _full((1, C_KV_LORA)), _full((C_KV_LORA, 512)), _full((C_KV_LORA, 512)),
        ],
        out_specs=[pl.BlockSpec((None, tm, wd), tok) for wd, _ in widths],
        compiler_params=_params(("arbitrary", "arbitrary")),
        name="inproj",
    )(h, mods_l, g.reshape(1, D_MODEL), w, cab, sab, cm, sm, place, bd,
      bqn, bkn, cqn, wuq, ckvn, wukvk, wukvv)


ATT_TQ = 256


def _attn_dense_kernel(q_ref, kc_ref, kx_ref, vc_ref, vx_ref, o_ref, *, qslab, ctx_first):
    def run(with_x):
        outs = [None, None]
        for hh in range(4):
            sl = slice(hh * LANE, (hh + 1) * LANE)
            qh = q_ref[:, qslab[hh] * LANE:(qslab[hh] + 1) * LANE]
            s_c = _dot_nt(qh, kc_ref[:, sl])
            mx = jnp.max(s_c, axis=-1, keepdims=True)
            if with_x:
                s_x = _dot_nt(qh, kx_ref[:, sl])
                mx = jnp.maximum(mx, jnp.max(s_x, axis=-1, keepdims=True))
            p_c = jnp.exp(s_c - mx)
            den = jnp.sum(p_c, axis=-1, keepdims=True)
            o = _dot(p_c.astype(BF16), vc_ref[:, sl])
            if with_x:
                p_x = jnp.exp(s_x - mx)
                den = den + jnp.sum(p_x, axis=-1, keepdims=True)
                o = o + _dot(p_x.astype(BF16), vx_ref[:, sl])
            o = o * (1.0 / den)
            outs[hh // 2] = o if outs[hh // 2] is None else outs[hh // 2] + o
        o_ref[:, 0:LANE] = outs[0].astype(o_ref.dtype)
        o_ref[:, LANE:2 * LANE] = outs[1].astype(o_ref.dtype)

    if ctx_first:
        j = pl.program_id(1)

        @pl.when(j == 0)
        def _():
            run(False)

        @pl.when(j > 0)
        def _():
            run(True)
    else:
        run(True)


def _attn_dense_call(q, k, v, *, qslab, with_ctx, name):
    tq = ATT_TQ
    nq = SEQ // tq
    qw = q.shape[-1]
    if with_ctx:
        grid = (BATCH, nq + 1)
        qmap = lambda b, j: (jnp.where(j == 0, 0, b + 1), jnp.where(j == 0, b, j - 1), 0)
        out_shape = jax.ShapeDtypeStruct((NSEG, SEG, MIX), BF16)
        omap = qmap
    else:
        grid = (BATCH, nq)
        qmap = lambda b, j: (b + 1, j, 0)
        out_shape = jax.ShapeDtypeStruct((BATCH, SEG, MIX), BF16)
        omap = lambda b, j: (b, j, 0)
    cspec = pl.BlockSpec((None, CTX_LEN, 512), lambda b, j: (0, b, 0))
    xspec = pl.BlockSpec((None, SEQ, 512), lambda b, j: (b + 1, 0, 0))
    kern = functools.partial(_attn_dense_kernel, qslab=qslab, ctx_first=with_ctx)
    return pl.pallas_call(
        kern,
        out_shape=out_shape,
        grid=grid,
        in_specs=[pl.BlockSpec((None, tq, qw), qmap), cspec, xspec, cspec, xspec],
        out_specs=pl.BlockSpec((None, tq, MIX), omap),
        compiler_params=_params(("arbitrary", "arbitrary")),
        name=name,
    )(q, k, k, v, v)


def _attn_win_kernel(sink_ref, q_ref, kc_ref, kx_ref, vc_ref, vx_ref, o_ref, *, n_ctx_blocks):
    j = pl.program_id(1)

    def finish(outs):
        o_ref[:, 0:LANE] = outs[0].astype(o_ref.dtype)
        o_ref[:, LANE:2 * LANE] = outs[1].astype(o_ref.dtype)

    def run(local):
        if local:
            n = j - n_ctx_blocks
            start = jnp.clip((n - 1) * WINDOW, 0, SEQ - 3 * WINDOW)
            start = pl.multiple_of(start, WINDOW)
            qpos = n * WINDOW + lax.broadcasted_iota(jnp.int32, (WINDOW, 3 * WINDOW), 0)
            kpos = start + lax.broadcasted_iota(jnp.int32, (WINDOW, 3 * WINDOW), 1)
            valid = jnp.abs(qpos - kpos) <= WINDOW
        outs = [None, None]
        for hh in range(4):
            sl = slice(hh * LANE, (hh + 1) * LANE)
            qh = q_ref[:, (hh // 2) * LANE:(hh // 2 + 1) * LANE]
            sink = sink_ref[hh]
            s_c = _dot_nt(qh, kc_ref[:, sl])
            mx = jnp.maximum(jnp.max(s_c, axis=-1, keepdims=True), sink)
            if local:
                s_l = _dot_nt(qh, kx_ref[pl.ds(start, 3 * WINDOW), sl])
                s_l = jnp.where(valid, s_l, -jnp.inf)
                mx = jnp.maximum(mx, jnp.max(s_l, axis=-1, keepdims=True))
            p_c = jnp.exp(s_c - mx)
            den = jnp.sum(p_c, axis=-1, keepdims=True) + jnp.exp(sink - mx)
            o = _dot(p_c.astype(BF16), vc_ref[:, sl])
            if local:
                p_l = jnp.exp(s_l - mx)
                den = den + jnp.sum(p_l, axis=-1, keepdims=True)
                o = o + _dot(p_l.astype(BF16), vx_ref[pl.ds(start, 3 * WINDOW), sl])
            o = o * (1.0 / den)
            outs[hh // 2] = o if outs[hh // 2] is None else outs[hh // 2] + o
        finish(outs)

    if n_ctx_blocks:
        @pl.when(j < n_ctx_blocks)
        def _():
            run(False)

        @pl.when(j >= n_ctx_blocks)
        def _():
            run(True)
    else:
        run(True)


def _attn_win_call(sink, q, k, v, *, with_ctx):
    tq = WINDOW
    nq = SEQ // tq
    ncb = CTX_LEN // tq if with_ctx else 0
    if with_ctx:
        qmap = lambda b, j: (jnp.where(j < ncb, 0, b + 1), jnp.where(j < ncb, ncb * b + j, j - ncb), 0)
        out_shape = jax.ShapeDtypeStruct((NSEG, SEG, MIX), BF16)
        omap = qmap
    else:
        qmap = lambda b, j: (b + 1, j, 0)
        out_shape = jax.ShapeDtypeStruct((BATCH, SEG, MIX), BF16)
        omap = lambda b, j: (b, j, 0)
    cspec = pl.BlockSpec((None, CTX_LEN, 512), lambda b, j: (0, b, 0))
    xspec = pl.BlockSpec((None, SEQ, 512), lambda b, j: (b + 1, 0, 0))
    kern = functools.partial(_attn_win_kernel, n_ctx_blocks=ncb)
    return pl.pallas_call(
        kern,
        out_shape=out_shape,
        grid=(BATCH, nq + ncb),
        in_specs=[pl.BlockSpec(memory_space=pltpu.SMEM),
                  pl.BlockSpec((None, tq, MIX), qmap), cspec, xspec, cspec, xspec],
        out_specs=pl.BlockSpec((None, tq, MIX), omap),
        compiler_params=_params(("arbitrary", "arbitrary")),
        name="attn_win",
    )(sink, q, k, k, v, v)


Q = SSD_CHUNK
PAD = 8
U_CTX = PAD
U_X = PAD + CTX_LEN + PAD
U_ROWS = U_X + SEQ + PAD


def _split3(a):
    a1 = a.astype(BF16)
    r1 = a - a1.astype(F32)
    a2 = r1.astype(BF16)
    a3 = (r1 - a2.astype(F32)).astype(BF16)
    return a1, a2, a3


def _ssd_kernel(zc_ref, zx_ref, uc_ref, ux_ref, dtc_ref, dtx_ref, cw_ref, cb_ref, dtb_ref, alog_ref,
                dskip_ref, onorm_ref, yc_ref, yx_ref, upad, xs_s, bm_s, cm_s, dt_s, y_s, h_s):
    zpad = jnp.zeros((PAD, D_CONV_CH), F32)
    upad[0:PAD, :] = zpad
    upad[U_CTX:U_CTX + CTX_LEN, :] = uc_ref[...]
    upad[U_CTX + CTX_LEN:U_X, :] = zpad
    upad[U_X:U_X + SEQ, :] = ux_ref[...]
    upad[U_X + SEQ:U_ROWS, :] = zpad

    for c in range(N_CHUNK):
        base = U_CTX + c * Q if c * Q < CTX_LEN else U_X + c * Q - CTX_LEN
        for gi, dst in enumerate((xs_s, bm_s, cm_s)):
            cols = slice(gi * 256, (gi + 1) * 256)
            acc = jnp.broadcast_to(cb_ref[:, cols], (Q, 256))
            for k in range(D_CONV):
                lo = base + k - D_CONV // 2
                acc = acc + upad[lo:lo + Q, cols] * cw_ref[k:k + 1, cols]
            dst[c * Q:(c + 1) * Q, :] = acc * _sigmoid(acc)

    def softplus(v):
        return jnp.maximum(v, 0.0) + jnp.log(1.0 + jnp.exp(-jnp.abs(v)))

    dt_s[0:CTX_LEN, :] = softplus(dtc_ref[...] + dtb_ref[...])
    dt_s[CTX_LEN:TOK, :] = softplus(dtx_ref[...] + dtb_ref[...])

    a_neg = -jnp.exp(alog_ref[...])
    row = lax.broadcasted_iota(jnp.int32, (Q, Q), 0)
    col = lax.broadcasted_iota(jnp.int32, (Q, Q), 1)
    causal = (col <= row, col >= row)
    tri = (causal[0].astype(BF16), causal[1].astype(BF16))
    lo_half = col < 64
    h_s[...] = jnp.zeros(h_s.shape, F32)
    y_s[...] = jnp.zeros(y_s.shape, F32)

    def bcast_col(v, idx):
        return jnp.broadcast_to(v[:, idx:idx + 1], (Q, Q))

    def scan_chunk(d, c):
        r0 = pl.multiple_of(c * Q, Q)
        rows = pl.ds(r0, Q)
        dt = dt_s[rows, :]
        a1, a2, a3 = _split3(dt * a_neg)
        cs = _dot(tri[d], a1) + _dot(tri[d], a2) + _dot(tri[d], a3)
        cs_t = cs.T
        last = Q - 1 if d == 0 else 0
        for g in range(2):
            gs = slice(g * Q, (g + 1) * Q)
            cmg = cm_s[rows, gs].astype(BF16)
            bmg = bm_s[rows, gs].astype(BF16)
            cbm = _dot_nt(cmg, bmg)
            gmat, colb, dtb = [], [], []
            for hh in range(2):
                idx = d * D_HEADS + g * 2 + hh
                cb_ = bcast_col(cs, idx)
                rb_ = jnp.broadcast_to(cs_t[idx:idx + 1, :], (Q, Q))
                dec = jnp.where(causal[d], jnp.exp(jnp.minimum(cb_ - rb_, 0.0)), 0.0)
                gmat.append((cbm * dec).astype(BF16))
                colb.append(cb_)
                dtb.append(bcast_col(dt, idx))
            colg = jnp.where(lo_half, colb[0], colb[1])
            xdt = xs_s[rows, gs] * jnp.where(lo_half, dtb[0], dtb[1])
            y = _dot(gmat[0], jnp.where(lo_half, xdt, 0.0).astype(BF16))
            y = y + _dot(gmat[1], jnp.where(lo_half, 0.0, xdt).astype(BF16))
            h_in = h_s[d * 2 + g]
            y = y + _dot(cmg, h_in.astype(BF16)) * jnp.exp(colg)
            y_s[rows, gs] += y
            lastg = colg[last:last + 1, :]
            xdec = (xdt * jnp.exp(lastg - colg)).astype(BF16)
            h_s[d * 2 + g] = jnp.exp(lastg) * h_in + _dot_tn(bmg, xdec)

    def body(i, carry):
        scan_chunk(0, i)
        n_c = CTX_LEN // Q
        scan_chunk(1, jnp.where(i < n_c, n_c - 1 - i, N_CHUNK - 1 + n_c - i))
        return carry

    lax.fori_loop(0, N_CHUNK, body, 0)

    dskip = dskip_ref[...]
    onorm = onorm_ref[...]

    def finish(rows, z):
        y = y_s[rows, :] + dskip * xs_s[rows, :]
        return _rms(y * (z * _sigmoid(z)), onorm).astype(BF16)

    for c in range(CTX_LEN // Q):
        yc_ref[c * Q:(c + 1) * Q, :] = finish(slice(c * Q, (c + 1) * Q), zc_ref[c * Q:(c + 1) * Q, :])

    def fin_body(c, carry):
        r0 = pl.multiple_of(c * Q, Q)
        yx_ref[pl.ds(r0, Q), :] = finish(pl.ds(CTX_LEN + r0, Q), zx_ref[pl.ds(r0, Q), :])
        return carry

    lax.fori_loop(0, SEQ // Q, fin_body, 0)


def _ssd_call(z, xbc, dt, cw, cb, dtb, alog, dskip, onorm):
    def cspec(wd):
        return pl.BlockSpec((None, CTX_LEN, wd), lambda b: (0, b, 0))

    def xspec(wd):
        return pl.BlockSpec((None, SEQ, wd), lambda b: (b + 1, 0, 0))

    return pl.pallas_call(
        _ssd_kernel,
        out_shape=[jax.ShapeDtypeStruct((BATCH, CTX_LEN, MIX), BF16),
                   jax.ShapeDtypeStruct((BATCH, SEQ, MIX), BF16)],
        grid=(BATCH,),
        in_specs=[cspec(MIX), xspec(MIX), cspec(D_CONV_CH), xspec(D_CONV_CH), cspec(LANE), xspec(LANE),
                  _full((8, D_CONV_CH)), _full((1, D_CONV_CH)), _full((1, LANE)), _full((1, LANE)),
                  _full((1, MIX)), _full((1, MIX))],
        out_specs=[pl.BlockSpec((None, CTX_LEN, MIX), lambda b: (b, 0, 0)),
                   pl.BlockSpec((None, SEQ, MIX), lambda b: (b, 0, 0))],
        scratch_shapes=[pltpu.VMEM((U_ROWS, D_CONV_CH), F32),
                        pltpu.VMEM((TOK, MIX), F32), pltpu.VMEM((TOK, MIX), F32), pltpu.VMEM((TOK, MIX), F32),
                        pltpu.VMEM((TOK, LANE), F32), pltpu.VMEM((TOK, MIX), F32),
                        pltpu.VMEM((4, Q, Q), F32)],
        compiler_params=_params(("arbitrary",)),
        name="ssd",
    )(z, z, xbc, xbc, dt, dt, cw, cb, dtb, alog, dskip, onorm)


def _rope_tables():
    rows = SEQ // GRID_W
    r = jnp.repeat(jnp.arange(rows, dtype=F32), GRID_W)
    c = jnp.tile(jnp.arange(GRID_W, dtype=F32), rows)

    def tables(rot_dim):
        axis_dim = rot_dim // 2
        inv = ROPE_THETA ** (-jnp.arange(0, axis_dim, 2, dtype=F32) / axis_dim)
        ar = r[:, None] * inv[None, :]
        ac = c[:, None] * inv[None, :]
        cos = jnp.concatenate([jnp.cos(ar), jnp.cos(ar), jnp.cos(ac), jnp.cos(ac)], axis=-1)
        sin = jnp.concatenate([-jnp.sin(ar), jnp.sin(ar), -jnp.sin(ac), jnp.sin(ac)], axis=-1)
        return cos, sin

    c64, s64 = tables(HEAD_DIM)
    cab = jnp.tile(c64, (1, 2))
    sab = jnp.tile(s64, (1, 2))
    c32, s32 = tables(C_ROPE)
    ones = jnp.ones((SEQ, 1), F32)
    cm = jnp.concatenate([ones * jnp.ones((1, C_NOPE), F32), c32, ones * jnp.ones((1, 32), F32)], axis=-1)
    sm = jnp.concatenate([jnp.zeros((SEQ, C_NOPE), F32), s32, jnp.zeros((SEQ, 32), F32)], axis=-1)

    def with_identity(tab, fill):
        return jnp.stack([jnp.full_like(tab, fill), tab])

    return (with_identity(cab, 1.0), with_identity(sab, 0.0), with_identity(cm, 1.0), with_identity(sm, 0.0))


def _placement_consts():
    place = np.zeros((LANE, 512), np.float32)
    for c in range(LANE):
        kv, i = divmod(c, HEAD_DIM)
        for g in range(2):
            place[c, (2 * kv + g) * LANE + g * HEAD_DIM + i] = 1.0
    bd = np.zeros((LANE, LANE), np.float32)
    for c in range(LANE):
        h0 = (c // HEAD_DIM) * HEAD_DIM
        bd[c, h0:h0 + HEAD_DIM] = 1.0 / HEAD_DIM
    return jnp.asarray(place, BF16), jnp.asarray(bd, BF16)


def _layer_weights(w_in, c_w_uq, c_w_ukv):
    d = w_in.shape[0]
    zc = lambda n: jnp.zeros((d, n), w_in.dtype)
    o_c = IN_AB
    o_d = IN_AB + IN_C
    w = jnp.concatenate([
        w_in[:, :o_c + C_Q_LORA + C_KV_LORA],
        zc(C_NOPE), w_in[:, o_c + C_Q_LORA + C_KV_LORA:o_d], zc(LANE - C_NOPE - C_ROPE),
        w_in[:, o_d:], zc(LANE - 2 * D_HEADS),
    ], axis=1).astype(BF16)
    assert w.shape[1] == PROJ_W
    dq = C_NOPE + C_ROPE
    wuq = jnp.zeros((C_Q_LORA, 512), F32)
    wk = jnp.zeros((C_KV_LORA, 512), F32)
    wv = jnp.zeros((C_KV_LORA, 512), F32)
    for hh in range(C_HEADS):
        wuq = wuq.at[:, hh * LANE:hh * LANE + dq].set(c_w_uq[:, hh * dq:(hh + 1) * dq])
        wk = wk.at[:, hh * LANE:hh * LANE + C_NOPE].set(c_w_ukv[:, hh * 128:hh * 128 + C_NOPE])
        off = hh * LANE + (hh % 2) * C_V
        wv = wv.at[:, off:off + C_V].set(c_w_ukv[:, hh * 128 + C_NOPE:(hh + 1) * 128])
    return w, wuq.astype(BF16), wk.astype(BF16), wv.astype(BF16)


def _lane_row(v, width=LANE):
    v = v.reshape(1, -1).astype(F32)
    return jnp.pad(v, ((0, 0), (0, width - v.shape[1])))


def kernel(x, c, ctx, c_ctx, ada_w, ada_b, ffn1_norm, ffn1_wi, ffn1_wo, mix_norm, w_in, w_out, a_sink, b_q_norm, b_k_norm, c_q_norm, c_w_uq, c_kv_norm, c_w_ukv, d_conv_w, d_conv_b, d_a_log, d_dt_bias, d_skip, d_out_norm, ffn2_norm, ffn2_wi, ffn2_wo, final_norm):
    h = jnp.concatenate([ctx.reshape(1, BATCH * CTX_LEN, D_MODEL), x], axis=0)
    cvec = jnp.concatenate([c_ctx[None, :], c, jnp.zeros((16 - NSEG, D_MODEL), F32)], axis=0)
    mods = _mods_call(cvec, ada_w, ada_b).reshape(DEPTH, 16, N_MOD, D_MODEL)[:, :NSEG]
    tabs = _rope_tables()
    consts = _placement_consts()

    out = None
    for l in range(DEPTH):
        with_ctx = l < DEPTH - 1
        mods_l = mods[l]
        wi1, wo1 = ffn1_wi[l].astype(BF16), ffn1_wo[l].astype(BF16)
        wi2, wo2 = ffn2_wi[l].astype(BF16), ffn2_wo[l].astype(BF16)
        w, wuq, wk, wv = _layer_weights(w_in[l], c_w_uq[l], c_w_ukv[l])

        h = _ffn_call(h, mods_l, ffn1_norm[l], wi1, wo1, k0=0, seg_off=0, name=f"ffn1_{l}")
        (qa, ka, va, qb, kb, vb, qc, kc, vc, z, xbc, dt) = _inproj_call(
            h, mods_l, mix_norm[l], w, tabs, consts,
            jnp.tile(b_q_norm[l], 2).reshape(1, LANE), jnp.tile(b_k_norm[l], 2).reshape(1, LANE),
            c_q_norm[l].reshape(1, C_Q_LORA), wuq, c_kv_norm[l].reshape(1, C_KV_LORA), wk, wv)

        oa = _attn_win_call(a_sink[l], qa, ka, va, with_ctx=with_ctx)
        ob = _attn_dense_call(qb, kb, vb, qslab=(0, 0, 1, 1), with_ctx=with_ctx, name=f"attn_b_{l}")
        oc = _attn_dense_call(qc, kc, vc, qslab=(0, 1, 2, 3), with_ctx=with_ctx, name=f"attn_c_{l}")
        cw = jnp.pad(d_conv_w[l], ((0, 8 - D_CONV), (0, 0)))
        yc, yx = _ssd_call(z, xbc, dt, cw, d_conv_b[l].reshape(1, D_CONV_CH),
                           _lane_row(d_dt_bias[l]), _lane_row(d_a_log[l]),
                           jnp.repeat(d_skip[l], HEAD_DIM).reshape(1, MIX), d_out_norm[l].reshape(1, MIX))
        if with_ctx:
            od = jnp.concatenate([yc.reshape(1, SEG, MIX), yx], axis=0)
            h = _ffn_call(h, mods_l, ffn2_norm[l], wi2, wo2, k0=6, seg_off=0,
                          pre=(oa, ob, oc, od, w_out[l].astype(BF16)), o_off=0, name=f"ffn2_{l}")
        else:
            out = _ffn_call(h, mods_l, ffn2_norm[l], wi2, wo2, k0=6, seg_off=1,
                            pre=(oa, ob, oc, yx, w_out[l].astype(BF16)), o_off=0,
                            final_g=final_norm, name=f"ffn2_{l}")
    return out
```

```python
import functools

import jax
import jax.numpy as jnp
from jax import lax
from jax.experimental import pallas as pl
from jax.experimental.pallas import tpu as pltpu

D_MODEL = 1024
BATCH = 8
SEQ = 2048
DEPTH = 2
GRID_W = 64
CTX_LEN = 256
HEAD_DIM = 64
ROPE_THETA = 10000.0
EPS = 1e-6
FFN_DIM = 2816
N_MOD = 9
WINDOW = 128
C_HEADS = 4
C_Q_LORA = 256
C_KV_LORA = 128
C_NOPE = 64
C_ROPE = 32
C_V = 64
D_HEADS = 4
D_STATE = 128
D_CONV = 5
SSD_CHUNK = 128
MIX = 256
IN_AB = 1024
IN_C = C_Q_LORA + C_KV_LORA + C_ROPE
D_CONV_CH = MIX + 2 * 2 * D_STATE
IN_D = MIX + D_CONV_CH + 2 * D_HEADS

NSEG = BATCH + 1
SEG = SEQ
assert BATCH * CTX_LEN == SEG
TOK = CTX_LEN + SEQ
N_CHUNK = TOK // SSD_CHUNK

LANE = 128
VMEM_LIMIT = 56 * 1024 * 1024

F32 = jnp.float32
BF16 = jnp.bfloat16


def _dot(a, b):
    return jnp.dot(a, b, preferred_element_type=F32)


def _dot_nt(a, b):
    return lax.dot_general(a, b, (((1,), (1,)), ((), ())), preferred_element_type=F32)


def _dot_tn(a, b):
    return lax.dot_general(a, b, (((0,), (0,)), ((), ())), preferred_element_type=F32)


def _sigmoid(x):
    return 1.0 / (1.0 + jnp.exp(-x))


def _rms(x, g):
    return x * lax.rsqrt(jnp.mean(x * x, axis=-1, keepdims=True) + EPS) * g


def _full(shape):
    nd = len(shape)
    return pl.BlockSpec(shape, lambda *_: (0,) * nd)


def _resident(shape):
    nd = len(shape)
    return pl.BlockSpec(shape, lambda *_: (0,) * nd, pipeline_mode=pl.Buffered(1))


def _params(sem):
    return pltpu.CompilerParams(dimension_semantics=sem, vmem_limit_bytes=VMEM_LIMIT)


def _mods_kernel(c_ref, w_ref, b_ref, o_ref):
    c = c_ref[...]
    s = (c * _sigmoid(c)).astype(BF16)
    o_ref[...] = _dot(s, w_ref[...].astype(BF16)) + b_ref[...]


def _mods_call(cvec, ada_w, ada_b):
    n_l = ada_w.shape[0]
    return pl.pallas_call(
        _mods_kernel,
        out_shape=jax.ShapeDtypeStruct((n_l, 16, N_MOD * D_MODEL), F32),
        grid=(n_l, N_MOD),
        in_specs=[
            pl.BlockSpec((16, D_MODEL), lambda l, n: (0, 0)),
            pl.BlockSpec((None, D_MODEL, D_MODEL), lambda l, n: (l, 0, n)),
            pl.BlockSpec((None, 1, D_MODEL), lambda l, n: (l, 0, n)),
        ],
        out_specs=pl.BlockSpec((None, 16, D_MODEL), lambda l, n: (l, 0, n)),
        compiler_params=_params(("arbitrary", "arbitrary")),
        name="mods",
    )(cvec, ada_w, ada_b.reshape(n_l, 1, N_MOD * D_MODEL))


FFN_TM = 512
FFN_FC = 256


def _ffn_kernel(*refs, k0, has_pre, has_final):
    h_ref, mods_ref, g_ref, wi_ref, wo_ref = refs[:5]
    pos = 5
    if has_pre:
        oa_ref, ob_ref, oc_ref, od_ref, wout_ref = refs[pos:pos + 5]
        pos += 5
    if has_final:
        gf_ref = refs[pos]
        pos += 1
    out_ref, hm_ref = refs[pos], refs[pos + 1]

    x = h_ref[...]
    m = mods_ref[...]
    if has_pre:
        o = _dot(oa_ref[...], wout_ref[0:MIX, :])
        o = o + _dot(ob_ref[...], wout_ref[MIX:2 * MIX, :])
        o = o + _dot(oc_ref[...], wout_ref[2 * MIX:3 * MIX, :])
        o = o + _dot(od_ref[...], wout_ref[3 * MIX:4 * MIX, :])
        x = x + m[5:6] * o
    n = _rms(x, g_ref[...]) * (1.0 + m[k0 + 1:k0 + 2]) + m[k0:k0 + 1]
    nb = n.astype(BF16)
    for c in range(FFN_DIM // FFN_FC):
        lo = c * FFN_FC
        a = _dot(nb, wi_ref[:, lo:lo + FFN_FC])
        b = _dot(nb, wi_ref[:, FFN_DIM + lo:FFN_DIM + lo + FFN_FC])
        hm_ref[:, lo:lo + FFN_FC] = (a * _sigmoid(a) * b).astype(BF16)
    y = _dot(hm_ref[...], wo_ref[...])
    out = x + 0.5 * m[k0 + 2:k0 + 3] * y
    if has_final:
        out = _rms(out, gf_ref[...])
    out_ref[...] = out


def _ffn_call(h, mods_l, g, wi, wo, *, k0, seg_off, pre=None, o_off=0, final_g=None, name):
    nseg = h.shape[0] - seg_off
    tm = FFN_TM
    tok = lambda s, i: (s + seg_off, i, 0)
    in_specs = [
        pl.BlockSpec((None, tm, D_MODEL), tok),
        pl.BlockSpec((None, N_MOD, D_MODEL), lambda s, i: (s + seg_off, 0, 0)),
        _full((1, D_MODEL)),
        _resident((D_MODEL, 2 * FFN_DIM)),
        _resident((FFN_DIM, D_MODEL)),
    ]
    args = [h, mods_l, g.reshape(1, D_MODEL), wi, wo]
    if pre is not None:
        oa, ob, oc, od, wout = pre
        ospec = pl.BlockSpec((None, tm, MIX), lambda s, i: (s + o_off, i, 0))
        in_specs += [ospec, ospec, ospec, ospec, _resident((4 * MIX, D_MODEL))]
        args += [oa, ob, oc, od, wout]
    if final_g is not None:
        in_specs.append(_full((1, D_MODEL)))
        args.append(final_g.reshape(1, D_MODEL))
    kern = functools.partial(_ffn_kernel, k0=k0, has_pre=pre is not None, has_final=final_g is not None)
    return pl.pallas_call(
        kern,
        out_shape=jax.ShapeDtypeStruct((nseg, SEG, D_MODEL), F32),
        grid=(nseg, SEG // tm),
        in_specs=in_specs,
        out_specs=pl.BlockSpec((None, tm, D_MODEL), lambda s, i: (s, i, 0)),
        scratch_shapes=[pltpu.VMEM((tm, FFN_DIM), BF16)],
        compiler_params=_params(("arbitrary", "arbitrary")),
        name=name,
    )(*args)


PROJ_TM = 512
COL_AB = 0
COL_C = IN_AB
COL_D = COL_C + 512
PROJ_W = COL_D + MIX + D_CONV_CH + LANE


def _swap_halves(x, half):
    lane = lax.broadcasted_iota(jnp.int32, x.shape, 1)
    first = (lane & half) == 0
    up = pltpu.roll(x, LANE - half, axis=1)
    dn = pltpu.roll(x, half, axis=1)
    return jnp.where(first, up, dn)


def _rope(x, cos, sin, half):
    return x * cos + _swap_halves(x, half) * sin


def _head_rms(x, bd, g):
    sq = x * x
    hi = sq.astype(BF16)
    lo = (sq - hi.astype(F32)).astype(BF16)
    ms = _dot(hi, bd) + _dot(lo, bd)
    return x * lax.rsqrt(ms + EPS) * g


def _inproj_kernel(h_ref, mods_ref, g_ref, w_ref, cab_ref, sab_ref, cm_ref, sm_ref, bd_ref,
                   bqn_ref, bkn_ref, cqn_ref, wuq_ref, ckvn_ref, wukvk_ref, wukvv_ref,
                   qa_ref, ka_ref, va_ref, qb_ref, kb_ref, vb_ref, qc_ref, kc_ref, vc_ref,
                   z_ref, xbc_ref, dt_ref):
    x = h_ref[...]
    m = mods_ref[...]
    n = _rms(x, g_ref[...]) * (1.0 + m[4:5]) + m[3:4]
    nb = n.astype(BF16)
    cab, sab = cab_ref[...], sab_ref[...]
    cm, sm = cm_ref[...], sm_ref[...]
    bd = bd_ref[...]
    scale_ab = HEAD_DIM ** -0.5
    scale_c = (C_NOPE + C_ROPE) ** -0.5

    for mixer, (q_ref, k_ref, v_ref) in enumerate(((qa_ref, ka_ref, va_ref), (qb_ref, kb_ref, vb_ref))):
        base = COL_AB + mixer * 512
        p = _dot(nb, w_ref[:, base:base + 512])
        for s in range(2):
            q = p[:, s * LANE:(s + 1) * LANE]
            if mixer == 1:
                q = _head_rms(q, bd, bqn_ref[...])
            q = _rope(q, cab, sab, 16) * scale_ab
            q_ref[:, s * LANE:(s + 1) * LANE] = q.astype(BF16)
        k = p[:, 2 * LANE:3 * LANE]
        if mixer == 1:
            k = _head_rms(k, bd, bkn_ref[...])
        k_ref[...] = _rope(k, cab, sab, 16).astype(BF16)
        v_ref[...] = p[:, 3 * LANE:4 * LANE].astype(BF16)

    p = _dot(nb, w_ref[:, COL_C:COL_C + 512])
    cq = _rms(p[:, 0:C_Q_LORA], cqn_ref[...]).astype(BF16)
    q = _dot(cq, wuq_ref[...])
    ckv = _rms(p[:, C_Q_LORA:C_Q_LORA + C_KV_LORA], ckvn_ref[...]).astype(BF16)
    kn = _dot(ckv, wukvk_ref[...])
    vc_ref[...] = _dot(ckv, wukvv_ref[...]).astype(BF16)
    kr = _rope(p[:, 3 * LANE:4 * LANE], cm, sm, 8)
    for hh in range(C_HEADS):
        sl = slice(hh * LANE, (hh + 1) * LANE)
        qc_ref[:, sl] = (_rope(q[:, sl], cm, sm, 8) * scale_c).astype(BF16)
        kc_ref[:, sl] = (kn[:, sl] + kr).astype(BF16)

    z_ref[...] = _dot(nb, w_ref[:, COL_D:COL_D + MIX])
    for c in range(D_CONV_CH // 256):
        lo = COL_D + MIX + c * 256
        xbc_ref[:, c * 256:(c + 1) * 256] = _dot(nb, w_ref[:, lo:lo + 256])
    lo = COL_D + MIX + D_CONV_CH
    dt_ref[...] = _dot(nb, w_ref[:, lo:lo + LANE])


PROJ_OUT = ([(MIX, BF16), (LANE, BF16), (LANE, BF16)] * 2 + [(512, BF16), (512, BF16), (MIX, BF16)]
            + [(MIX, F32), (D_CONV_CH, F32), (LANE, F32)])


def _inproj_call(h, mods_l, g, w, tabs, bd, bqn, bkn, cqn, wuq, ckvn, wukvk, wukvv):
    tm = PROJ_TM
    cab, sab, cm, sm = tabs
    tok = lambda s, i: (s, i, 0)
    tab = pl.BlockSpec((None, tm, LANE), lambda s, i: (jnp.minimum(s, 1), i, 0))
    return pl.pallas_call(
        _inproj_kernel,
        out_shape=[jax.ShapeDtypeStruct((NSEG, SEG, wd), dt) for wd, dt in PROJ_OUT],
        grid=(NSEG, SEG // tm),
        in_specs=[
            pl.BlockSpec((None, tm, D_MODEL), tok),
            pl.BlockSpec((None, N_MOD, D_MODEL), lambda s, i: (s, 0, 0)),
            _full((1, D_MODEL)),
            _resident((D_MODEL, PROJ_W)),
            tab, tab, tab, tab,
            _full((LANE, LANE)),
            _full((1, LANE)), _full((1, LANE)),
            _full((1, C_Q_LORA)), _full((C_Q_LORA, 512)),
            _full((1, C_KV_LORA)), _full((C_KV_LORA, 512)), _full((C_KV_LORA, MIX)),
        ],
        out_specs=[pl.BlockSpec((None, tm, wd), tok) for wd, _ in PROJ_OUT],
        compiler_params=_params(("arbitrary", "arbitrary")),
        name="inproj",
    )(h, mods_l, g.reshape(1, D_MODEL), w, cab, sab, cm, sm, bd,
      bqn, bkn, cqn, wuq, ckvn, wukvk, wukvv)


HEADS_GQA = tuple((g, kv, 0, 0, g, kv) for g in range(2) for kv in range(2))
HEADS_MLA = tuple((h, None, h, h // 2, h // 2, h % 2) for h in range(C_HEADS))


def _masked_q(q, half):
    if half is None:
        return q
    lane = lax.broadcasted_iota(jnp.int32, q.shape, 1)
    keep = (lane < HEAD_DIM) if half == 0 else (lane >= HEAD_DIM)
    return jnp.where(keep, q, jnp.zeros_like(q))


def _store_heads(o_ref, rows, outs):
    lane = lax.broadcasted_iota(jnp.int32, outs[(0, 0)].shape, 1)
    for s in range(2):
        o = jnp.where(lane < HEAD_DIM, outs[(s, 0)], outs[(s, 1)])
        o_ref[rows, s * LANE:(s + 1) * LANE] = o.astype(o_ref.dtype)


ATT_TQ = 256


def _attn_dense_kernel(q_ref, kc_ref, kx_ref, vc_ref, vx_ref, o_ref, *, heads, ctx_first):
    def run(with_x):
        outs = {}
        for qs, qhalf, ks, vs, os_, ohalf in heads:
            qh = _masked_q(q_ref[:, qs * LANE:(qs + 1) * LANE], qhalf)
            ksl = slice(ks * LANE, (ks + 1) * LANE)
            vsl = slice(vs * LANE, (vs + 1) * LANE)
            s_c = _dot_nt(qh, kc_ref[:, ksl])
            mx = jnp.max(s_c, axis=-1, keepdims=True)
            if with_x:
                s_x = _dot_nt(qh, kx_ref[:, ksl])
                mx = jnp.maximum(mx, jnp.max(s_x, axis=-1, keepdims=True))
            p_c = jnp.exp(s_c - mx)
            den = jnp.sum(p_c, axis=-1, keepdims=True)
            o = _dot(p_c.astype(BF16), vc_ref[:, vsl])
            if with_x:
                p_x = jnp.exp(s_x - mx)
                den = den + jnp.sum(p_x, axis=-1, keepdims=True)
                o = o + _dot(p_x.astype(BF16), vx_ref[:, vsl])
            outs[(os_, ohalf)] = o * (1.0 / den)
        _store_heads(o_ref, slice(None), outs)

    if ctx_first:
        j = pl.program_id(1)

        @pl.when(j == 0)
        def _():
            run(False)

        @pl.when(j > 0)
        def _():
            run(True)
    else:
        run(True)


def _attn_specs(q, k, v, tq, with_ctx):
    nq = SEQ // tq
    ncb = CTX_LEN // tq if with_ctx else 0
    if with_ctx:
        qmap = lambda b, j: (jnp.where(j < ncb, 0, b + 1), jnp.where(j < ncb, ncb * b + j, j - ncb), 0)
        out_shape = jax.ShapeDtypeStruct((NSEG, SEG, MIX), BF16)
        omap = qmap
    else:
        qmap = lambda b, j: (b + 1, j, 0)
        out_shape = jax.ShapeDtypeStruct((BATCH, SEG, MIX), BF16)
        omap = lambda b, j: (b, j, 0)
    kw, vw = k.shape[-1], v.shape[-1]
    in_specs = [pl.BlockSpec((None, tq, q.shape[-1]), qmap),
                pl.BlockSpec((None, CTX_LEN, kw), lambda b, j: (0, b, 0)),
                pl.BlockSpec((None, SEQ, kw), lambda b, j: (b + 1, 0, 0)),
                pl.BlockSpec((None, CTX_LEN, vw), lambda b, j: (0, b, 0)),
                pl.BlockSpec((None, SEQ, vw), lambda b, j: (b + 1, 0, 0))]
    return (BATCH, nq + ncb), in_specs, pl.BlockSpec((None, tq, MIX), omap), out_shape, ncb


def _attn_dense_call(q, k, v, *, heads, with_ctx, name):
    grid, in_specs, out_spec, out_shape, _ = _attn_specs(q, k, v, ATT_TQ, with_ctx)
    kern = functools.partial(_attn_dense_kernel, heads=heads, ctx_first=with_ctx)
    return pl.pallas_call(
        kern,
        out_shape=out_shape,
        grid=grid,
        in_specs=in_specs,
        out_specs=out_spec,
        compiler_params=_params(("arbitrary", "arbitrary")),
        name=name,
    )(q, k, k, v, v)


WIN_TQ = 256


def _attn_win_kernel(sink_ref, q_ref, kc_ref, kx_ref, vc_ref, vx_ref, o_ref, *, n_ctx_blocks):
    j = pl.program_id(1)
    n_sub = WIN_TQ // WINDOW

    def run(local):
        for sub in range(n_sub):
            rows = slice(sub * WINDOW, (sub + 1) * WINDOW)
            if local:
                n = (j - n_ctx_blocks) * n_sub + sub
                start = jnp.clip((n - 1) * WINDOW, 0, SEQ - 3 * WINDOW)
                start = pl.multiple_of(start, WINDOW)
                qpos = n * WINDOW + lax.broadcasted_iota(jnp.int32, (WINDOW, 3 * WINDOW), 0)
                kpos = start + lax.broadcasted_iota(jnp.int32, (WINDOW, 3 * WINDOW), 1)
                valid = jnp.abs(qpos - kpos) <= WINDOW
                k_l = kx_ref[pl.ds(start, 3 * WINDOW), :]
                v_l = vx_ref[pl.ds(start, 3 * WINDOW), :]
            outs = {}
            for qs, qhalf, _, _, os_, ohalf in HEADS_GQA:
                qh = _masked_q(q_ref[rows, qs * LANE:(qs + 1) * LANE], qhalf)
                sink = sink_ref[qhalf * 2 + qs]
                s_c = _dot_nt(qh, kc_ref[...])
                mx = jnp.maximum(jnp.max(s_c, axis=-1, keepdims=True), sink)
                if local:
                    s_l = jnp.where(valid, _dot_nt(qh, k_l), -jnp.inf)
                    mx = jnp.maximum(mx, jnp.max(s_l, axis=-1, keepdims=True))
                p_c = jnp.exp(s_c - mx)
                den = jnp.sum(p_c, axis=-1, keepdims=True) + jnp.exp(sink - mx)
                o = _dot(p_c.astype(BF16), vc_ref[...])
                if local:
                    p_l = jnp.exp(s_l - mx)
                    den = den + jnp.sum(p_l, axis=-1, keepdims=True)
                    o = o + _dot(p_l.astype(BF16), v_l)
                outs[(os_, ohalf)] = o * (1.0 / den)
            _store_heads(o_ref, rows, outs)

    if n_ctx_blocks:
        @pl.when(j < n_ctx_blocks)
        def _():
            run(False)

        @pl.when(j >= n_ctx_blocks)
        def _():
            run(True)
    else:
        run(True)


def _attn_win_call(sink, q, k, v, *, with_ctx):
    grid, in_specs, out_spec, out_shape, ncb = _attn_specs(q, k, v, WIN_TQ, with_ctx)
    kern = functools.partial(_attn_win_kernel, n_ctx_blocks=ncb)
    return pl.pallas_call(
        kern,
        out_shape=out_shape,
        grid=grid,
        in_specs=[pl.BlockSpec(memory_space=pltpu.SMEM)] + in_specs,
        out_specs=out_spec,
        compiler_params=_params(("arbitrary", "arbitrary")),
        name="attn_win",
    )(sink, q, k, k, v, v)


Q = SSD_CHUNK
PAD = 8
U_CTX = PAD
U_X = PAD + CTX_LEN + PAD
U_ROWS = U_X + SEQ + PAD
N_CTX_CHUNK = CTX_LEN // Q


def _split3(a):
    a1 = a.astype(BF16)
    r1 = a - a1.astype(F32)
    a2 = r1.astype(BF16)
    a3 = (r1 - a2.astype(F32)).astype(BF16)
    return a1, a2, a3


def _ssd_kernel(zc_ref, zx_ref, uc_ref, ux_ref, dtc_ref, dtx_ref, cw_ref, cb_ref, dtb_ref, alog_ref,
                dskip_ref, onorm_ref, yc_ref, yx_ref, upad, xs_s, bm_s, cm_s, dt_s, cs_s, y_s, st_s, h_s):
    zpad = jnp.zeros((PAD, D_CONV_CH), F32)
    upad[0:PAD, :] = zpad
    upad[U_CTX:U_CTX + CTX_LEN, :] = uc_ref[...]
    upad[U_CTX + CTX_LEN:U_X, :] = zpad
    upad[U_X:U_X + SEQ, :] = ux_ref[...]
    upad[U_X + SEQ:U_ROWS, :] = zpad

    for c in range(N_CHUNK):
        base = U_CTX + c * Q if c < N_CTX_CHUNK else U_X + (c - N_CTX_CHUNK) * Q
        for gi, dst in enumerate((xs_s, bm_s, cm_s)):
            cols = slice(gi * 256, (gi + 1) * 256)
            acc = jnp.broadcast_to(cb_ref[:, cols], (Q, 256))
            for k in range(D_CONV):
                lo = base + k - D_CONV // 2
                acc = acc + upad[lo:lo + Q, cols] * cw_ref[k:k + 1, cols]
            dst[c * Q:(c + 1) * Q, :] = (acc * _sigmoid(acc)).astype(dst.dtype)

    def softplus(v):
        return jnp.maximum(v, 0.0) + jnp.log(1.0 + jnp.exp(-jnp.abs(v)))

    dt_s[0:CTX_LEN, :] = softplus(dtc_ref[...] + dtb_ref[...])
    dt_s[CTX_LEN:TOK, :] = softplus(dtx_ref[...] + dtb_ref[...])

    a_neg = -jnp.exp(alog_ref[...])
    row = lax.broadcasted_iota(jnp.int32, (Q, Q), 0)
    col = lax.broadcasted_iota(jnp.int32, (Q, Q), 1)
    causal = (col <= row, col >= row)
    tri = (causal[0].astype(BF16), causal[1].astype(BF16))
    lo_half = col < HEAD_DIM
    last_row = (Q - 1, 0)

    def bcast_col(v, idx):
        return jnp.broadcast_to(v[:, idx:idx + 1], (Q, Q))

    def group_cols(v, d, g):
        idx = d * D_HEADS + g * 2
        return jnp.where(lo_half, bcast_col(v, idx), bcast_col(v, idx + 1))

    def local_body(c, carry):
        r0 = pl.multiple_of(c * Q, Q)
        rows = pl.ds(r0, Q)
        dt = dt_s[rows, :]
        a1, a2, a3 = _split3(dt * a_neg)
        cs_f = _dot(tri[0], a1) + _dot(tri[0], a2) + _dot(tri[0], a3)
        cs_b = _dot(tri[1], a1) + _dot(tri[1], a2) + _dot(tri[1], a3)
        cs = jnp.where(col < D_HEADS, cs_f, cs_b)
        cs_s[rows, :] = cs
        cs_t = cs.T
        for g in range(2):
            gs = slice(g * Q, (g + 1) * Q)
            cmg = cm_s[rows, gs]
            bmg = bm_s[rows, gs]
            cbm = _dot_nt(cmg, bmg)
            xsg = xs_s[rows, gs]
            y = None
            for d in range(2):
                gmat = []
                for hh in range(2):
                    idx = d * D_HEADS + g * 2 + hh
                    seg = bcast_col(cs, idx) - jnp.broadcast_to(cs_t[idx:idx + 1, :], (Q, Q))
                    dec = jnp.where(causal[d], jnp.exp(jnp.minimum(seg, 0.0)), 0.0)
                    gmat.append((cbm * dec).astype(BF16))
                colg = group_cols(cs, d, g)
                xdt = xsg * group_cols(dt, d, g)
                yd = _dot(gmat[0], jnp.where(lo_half, xdt, 0.0).astype(BF16))
                yd = yd + _dot(gmat[1], jnp.where(lo_half, 0.0, xdt).astype(BF16))
                y = yd if y is None else y + yd
                lastg = colg[last_row[d]:last_row[d] + 1, :]
                xdec = (xdt * jnp.exp(lastg - colg)).astype(BF16)
                st_s[c * 4 + d * 2 + g] = _dot_tn(bmg, xdec)
            y_s[rows, gs] = y
        return carry

    lax.fori_loop(0, N_CHUNK, local_body, 0, unroll=2)

    h_s[...] = jnp.zeros(h_s.shape, F32)

    def scan_chunk(d, c):
        r0 = pl.multiple_of(c * Q, Q)
        rows = pl.ds(r0, Q)
        cs = cs_s[rows, :]
        for g in range(2):
            gs = slice(g * Q, (g + 1) * Q)
            colg = group_cols(cs, d, g)
            h_in = h_s[d * 2 + g]
            y_s[rows, gs] += _dot(cm_s[rows, gs], h_in.astype(BF16)) * jnp.exp(colg)
            lastg = colg[last_row[d]:last_row[d] + 1, :]
            h_s[d * 2 + g] = jnp.exp(lastg) * h_in + st_s[c * 4 + d * 2 + g]

    def scan_body(i, carry):
        scan_chunk(0, i)
        scan_chunk(1, jnp.where(i < N_CTX_CHUNK, N_CTX_CHUNK - 1 - i, N_CHUNK - 1 + N_CTX_CHUNK - i))
        return carry

    lax.fori_loop(0, N_CHUNK, scan_body, 0, unroll=2)

    dskip = dskip_ref[...]
    onorm = onorm_ref[...]

    def finish(rows, z):
        y = y_s[rows, :] + dskip * xs_s[rows, :]
        return _rms(y * (z * _sigmoid(z)), onorm).astype(BF16)

    for c in range(N_CTX_CHUNK):
        yc_ref[c * Q:(c + 1) * Q, :] = finish(slice(c * Q, (c + 1) * Q), zc_ref[c * Q:(c + 1) * Q, :])

    def fin_body(c, carry):
        r0 = pl.multiple_of(c * Q, Q)
        yx_ref[pl.ds(r0, Q), :] = finish(pl.ds(CTX_LEN + r0, Q), zx_ref[pl.ds(r0, Q), :])
        return carry

    lax.fori_loop(0, SEQ // Q, fin_body, 0, unroll=2)


def _ssd_call(z, xbc, dt, cw, cb, dtb, alog, dskip, onorm):
    def cspec(wd):
        return pl.BlockSpec((None, CTX_LEN, wd), lambda b: (0, b, 0))

    def xspec(wd):
        return pl.BlockSpec((None, SEQ, wd), lambda b: (b + 1, 0, 0))

    return pl.pallas_call(
        _ssd_kernel,
        out_shape=[jax.ShapeDtypeStruct((BATCH, CTX_LEN, MIX), BF16),
                   jax.ShapeDtypeStruct((BATCH, SEQ, MIX), BF16)],
        grid=(BATCH,),
        in_specs=[cspec(MIX), xspec(MIX), cspec(D_CONV_CH), xspec(D_CONV_CH), cspec(LANE), xspec(LANE),
                  _full((8, D_CONV_CH)), _full((1, D_CONV_CH)), _full((1, LANE)), _full((1, LANE)),
                  _full((1, MIX)), _full((1, MIX))],
        out_specs=[pl.BlockSpec((None, CTX_LEN, MIX), lambda b: (b, 0, 0)),
                   pl.BlockSpec((None, SEQ, MIX), lambda b: (b, 0, 0))],
        scratch_shapes=[pltpu.VMEM((U_ROWS, D_CONV_CH), F32),
                        pltpu.VMEM((TOK, MIX), F32), pltpu.VMEM((TOK, MIX), BF16), pltpu.VMEM((TOK, MIX), BF16),
                        pltpu.VMEM((TOK, LANE), F32), pltpu.VMEM((TOK, LANE), F32), pltpu.VMEM((TOK, MIX), F32),
                        pltpu.VMEM((N_CHUNK * 4, Q, Q), F32), pltpu.VMEM((4, Q, Q), F32)],
        compiler_params=_params(("arbitrary",)),
        name="ssd",
    )(z, z, xbc, xbc, dt, dt, cw, cb, dtb, alog, dskip, onorm)


def _rope_tables():
    rows = SEQ // GRID_W
    r = jnp.repeat(jnp.arange(rows, dtype=F32), GRID_W)
    c = jnp.tile(jnp.arange(GRID_W, dtype=F32), rows)

    def tables(rot_dim):
        axis_dim = rot_dim // 2
        inv = ROPE_THETA ** (-jnp.arange(0, axis_dim, 2, dtype=F32) / axis_dim)
        ar = r[:, None] * inv[None, :]
        ac = c[:, None] * inv[None, :]
        cos = jnp.concatenate([jnp.cos(ar), jnp.cos(ar), jnp.cos(ac), jnp.cos(ac)], axis=-1)
        sin = jnp.concatenate([-jnp.sin(ar), jnp.sin(ar), -jnp.sin(ac), jnp.sin(ac)], axis=-1)
        return cos, sin

    c64, s64 = tables(HEAD_DIM)
    cab = jnp.tile(c64, (1, 2))
    sab = jnp.tile(s64, (1, 2))
    c32, s32 = tables(C_ROPE)
    cm = jnp.concatenate([jnp.ones((SEQ, C_NOPE), F32), c32, jnp.ones((SEQ, 32), F32)], axis=-1)
    sm = jnp.concatenate([jnp.zeros((SEQ, C_NOPE), F32), s32, jnp.zeros((SEQ, 32), F32)], axis=-1)

    def with_identity(tab, fill):
        return jnp.stack([jnp.full_like(tab, fill), tab])

    return (with_identity(cab, 1.0), with_identity(sab, 0.0), with_identity(cm, 1.0), with_identity(sm, 0.0))


def _head_mean_matrix():
    lane = jnp.arange(LANE)
    same = (lane[:, None] // HEAD_DIM) == (lane[None, :] // HEAD_DIM)
    return jnp.where(same, 1.0 / HEAD_DIM, 0.0).astype(BF16)


def _gqa_order(w, axis):
    shp = w.shape
    w = w.reshape(shp[:axis] + (2, 2, HEAD_DIM) + shp[axis + 1:])
    w = jnp.swapaxes(w, axis, axis + 1)
    return w.reshape(shp)


def _layer_weights(w_in, c_w_uq, c_w_ukv, w_out):
    d = w_in.shape[0]
    zc = lambda n: jnp.zeros((d, n), w_in.dtype)
    o_c = IN_AB
    o_d = IN_AB + IN_C
    w = jnp.concatenate([
        _gqa_order(w_in[:, 0:MIX], 1), w_in[:, MIX:512],
        _gqa_order(w_in[:, 512:512 + MIX], 1), w_in[:, 512 + MIX:o_c + C_Q_LORA + C_KV_LORA],
        zc(C_NOPE), w_in[:, o_c + C_Q_LORA + C_KV_LORA:o_d], zc(LANE - C_NOPE - C_ROPE),
        w_in[:, o_d:], zc(LANE - 2 * D_HEADS),
    ], axis=1).astype(BF16)
    assert w.shape[1] == PROJ_W
    dq = C_NOPE + C_ROPE
    wuq = jnp.pad(c_w_uq.reshape(C_Q_LORA, C_HEADS, dq), ((0, 0), (0, 0), (0, LANE - dq)))
    wkv = c_w_ukv.reshape(C_KV_LORA, C_HEADS, C_NOPE + C_V)
    wk = jnp.pad(wkv[:, :, :C_NOPE], ((0, 0), (0, 0), (0, LANE - C_NOPE)))
    wv = wkv[:, :, C_NOPE:]
    wout = jnp.concatenate([_gqa_order(w_out[0:MIX], 0), _gqa_order(w_out[MIX:2 * MIX], 0), w_out[2 * MIX:]], axis=0)
    return (w, wuq.reshape(C_Q_LORA, 512).astype(BF16), wk.reshape(C_KV_LORA, 512).astype(BF16),
            wv.reshape(C_KV_LORA, MIX).astype(BF16), wout.astype(BF16))


def _lane_row(v, width=LANE):
    v = v.reshape(1, -1).astype(F32)
    return jnp.pad(v, ((0, 0), (0, width - v.shape[1])))


def kernel(x, c, ctx, c_ctx, ada_w, ada_b, ffn1_norm, ffn1_wi, ffn1_wo, mix_norm, w_in, w_out, a_sink, b_q_norm, b_k_norm, c_q_norm, c_w_uq, c_kv_norm, c_w_ukv, d_conv_w, d_conv_b, d_a_log, d_dt_bias, d_skip, d_out_norm, ffn2_norm, ffn2_wi, ffn2_wo, final_norm):
    h = jnp.concatenate([ctx.reshape(1, BATCH * CTX_LEN, D_MODEL), x], axis=0)
    cvec = jnp.concatenate([c_ctx[None, :], c, jnp.zeros((16 - NSEG, D_MODEL), F32)], axis=0)
    mods = _mods_call(cvec, ada_w, ada_b).reshape(DEPTH, 16, N_MOD, D_MODEL)[:, :NSEG]
    tabs = _rope_tables()
    bd = _head_mean_matrix()

    out = None
    for l in range(DEPTH):
        with_ctx = l < DEPTH - 1
        mods_l = mods[l]
        wi1, wo1 = ffn1_wi[l].astype(BF16), ffn1_wo[l].astype(BF16)
        wi2, wo2 = ffn2_wi[l].astype(BF16), ffn2_wo[l].astype(BF16)
        w, wuq, wk, wv, wout = _layer_weights(w_in[l], c_w_uq[l], c_w_ukv[l], w_out[l])

        h = _ffn_call(h, mods_l, ffn1_norm[l], wi1, wo1, k0=0, seg_off=0, name=f"ffn1_{l}")
        (qa, ka, va, qb, kb, vb, qc, kc, vc, z, xbc, dt) = _inproj_call(
            h, mods_l, mix_norm[l], w, tabs, bd,
            jnp.tile(b_q_norm[l], 2).reshape(1, LANE), jnp.tile(b_k_norm[l], 2).reshape(1, LANE),
            c_q_norm[l].reshape(1, C_Q_LORA), wuq, c_kv_norm[l].reshape(1, C_KV_LORA), wk, wv)

        oa = _attn_win_call(a_sink[l], qa, ka, va, with_ctx=with_ctx)
        ob = _attn_dense_call(qb, kb, vb, heads=HEADS_GQA, with_ctx=with_ctx, name=f"attn_b_{l}")
        oc = _attn_dense_call(qc, kc, vc, heads=HEADS_MLA, with_ctx=with_ctx, name=f"attn_c_{l}")
        cw = jnp.pad(d_conv_w[l], ((0, 8 - D_CONV), (0, 0)))
        yc, yx = _ssd_call(z, xbc, dt, cw, d_conv_b[l].reshape(1, D_CONV_CH),
                           _lane_row(d_dt_bias[l]), _lane_row(d_a_log[l]),
                           jnp.repeat(d_skip[l], HEAD_DIM).reshape(1, MIX), d_out_norm[l].reshape(1, MIX))
        if with_ctx:
            od = jnp.concatenate([yc.reshape(1, SEG, MIX), yx], axis=0)
            h = _ffn_call(h, mods_l, ffn2_norm[l], wi2, wo2, k0=6, seg_off=0,
                          pre=(oa, ob, oc, od, wout), o_off=0, name=f"ffn2_{l}")
        else:
            out = _ffn_call(h, mods_l, ffn2_norm[l], wi2, wo2, k0=6, seg_off=1,
                            pre=(oa, ob, oc, yx, wout), o_off=0, final_g=final_norm, name=f"ffn2_{l}")
    return out
```

```python
import functools

import jax
import jax.numpy as jnp
from jax import lax
from jax.experimental import pallas as pl
from jax.experimental.pallas import tpu as pltpu

D_MODEL = 1024
BATCH = 8
SEQ = 2048
DEPTH = 2
GRID_W = 64
CTX_LEN = 256
HEAD_DIM = 64
ROPE_THETA = 10000.0
EPS = 1e-6
FFN_DIM = 2816
N_MOD = 9
WINDOW = 128
C_HEADS = 4
C_Q_LORA = 256
C_KV_LORA = 128
C_NOPE = 64
C_ROPE = 32
C_V = 64
D_HEADS = 4
D_STATE = 128
D_CONV = 5
SSD_CHUNK = 128
MIX = 256
IN_AB = 1024
IN_C = C_Q_LORA + C_KV_LORA + C_ROPE
D_CONV_CH = MIX + 2 * 2 * D_STATE
IN_D = MIX + D_CONV_CH + 2 * D_HEADS

NSEG = BATCH + 1
SEG = SEQ
assert BATCH * CTX_LEN == SEG
TOK = CTX_LEN + SEQ
N_CHUNK = TOK // SSD_CHUNK

LANE = 128
VMEM_LIMIT = 56 * 1024 * 1024

F32 = jnp.float32
BF16 = jnp.bfloat16
LOG2E = 1.4426950408889634


def _dot(a, b):
    return jnp.dot(a, b, preferred_element_type=F32)


def _dot_nt(a, b):
    return lax.dot_general(a, b, (((1,), (1,)), ((), ())), preferred_element_type=F32)


def _dot_tn(a, b):
    return lax.dot_general(a, b, (((0,), (0,)), ((), ())), preferred_element_type=F32)


def _sigmoid(x):
    return 1.0 / (1.0 + jnp.exp(-x))


def _rms(x, g):
    return x * lax.rsqrt(jnp.mean(x * x, axis=-1, keepdims=True) + EPS) * g


def _full(shape):
    nd = len(shape)
    return pl.BlockSpec(shape, lambda *_: (0,) * nd)


def _layer_slab(shape, layer):
    nd = len(shape)
    return pl.BlockSpec((None,) + tuple(shape), lambda *_: (layer,) + (0,) * nd, pipeline_mode=pl.Buffered(1))


def _params(sem):
    return pltpu.CompilerParams(dimension_semantics=sem, vmem_limit_bytes=VMEM_LIMIT)


def _mods_kernel(c_ref, w_ref, b_ref, o_ref):
    c = c_ref[...]
    s = (c * _sigmoid(c)).astype(BF16)
    o_ref[...] = _dot(s, w_ref[...].astype(BF16)) + b_ref[...]


def _mods_call(cvec, ada_w, ada_b):
    n_l = ada_w.shape[0]
    return pl.pallas_call(
        _mods_kernel,
        out_shape=jax.ShapeDtypeStruct((n_l, 16, N_MOD * D_MODEL), F32),
        grid=(n_l, N_MOD),
        in_specs=[
            pl.BlockSpec((16, D_MODEL), lambda l, n: (0, 0)),
            pl.BlockSpec((None, D_MODEL, D_MODEL), lambda l, n: (l, 0, n)),
            pl.BlockSpec((None, 1, D_MODEL), lambda l, n: (l, 0, n)),
        ],
        out_specs=pl.BlockSpec((None, 16, D_MODEL), lambda l, n: (l, 0, n)),
        compiler_params=_params(("arbitrary", "arbitrary")),
        name="mods",
    )(cvec, ada_w, ada_b.reshape(n_l, 1, N_MOD * D_MODEL))


FFN_TM = 512
FFN_FC = 256


def _ffn_kernel(*refs, k0, split_in, has_pre, has_final):
    if split_in:
        hc_ref, hx_ref = refs[:2]
        refs = refs[1:]
    h_ref, mods_ref, g_ref, wi_ref, wo_ref = refs[:5]
    pos = 5
    if has_pre:
        oa_ref, ob_ref, oc_ref, od_ref, wout_ref = refs[pos:pos + 5]
        pos += 5
    if has_final:
        gf_ref = refs[pos]
        pos += 1
    out_ref, hm_ref = refs[pos], refs[pos + 1]

    if split_in:
        x = jnp.where(pl.program_id(0) == 0, hc_ref[...], hx_ref[...])
    else:
        x = h_ref[...]
    m = mods_ref[...]
    if has_pre:
        o = _dot(oa_ref[...], wout_ref[0:MIX, :])
        o = o + _dot(ob_ref[...], wout_ref[MIX:2 * MIX, :])
        o = o + _dot(oc_ref[...], wout_ref[2 * MIX:3 * MIX, :])
        o = o + _dot(od_ref[...], wout_ref[3 * MIX:4 * MIX, :])
        x = x + m[5:6] * o
    n = _rms(x, g_ref[...]) * (1.0 + m[k0 + 1:k0 + 2]) + m[k0:k0 + 1]
    nb = n.astype(BF16)
    for c in range(FFN_DIM // FFN_FC):
        lo = c * FFN_FC
        a = _dot(nb, wi_ref[:, lo:lo + FFN_FC])
        b = _dot(nb, wi_ref[:, FFN_DIM + lo:FFN_DIM + lo + FFN_FC])
        hm_ref[:, lo:lo + FFN_FC] = (a * _sigmoid(a) * b).astype(BF16)
    y = _dot(hm_ref[...], wo_ref[...])
    out = x + 0.5 * m[k0 + 2:k0 + 3] * y
    if has_final:
        out = _rms(out, gf_ref[...])
    out_ref[...] = out


def _ffn_call(h, mods_l, g, wi, wo, layer, *, k0, seg_off, pre=None, o_off=0, final_g=None, name):
    split_in = isinstance(h, tuple)
    tm = FFN_TM
    if split_in:
        assert seg_off == 0
        nseg = NSEG
        h_args = list(h)
        h_specs = [
            pl.BlockSpec((None, tm, D_MODEL), lambda s, i: (0, jnp.where(s == 0, i, 0), 0)),
            pl.BlockSpec((None, tm, D_MODEL), lambda s, i: (jnp.maximum(s - 1, 0), jnp.where(s == 0, 0, i), 0)),
        ]
    else:
        nseg = h.shape[0] - seg_off
        h_args = [h]
        h_specs = [pl.BlockSpec((None, tm, D_MODEL), lambda s, i: (s + seg_off, i, 0))]
    in_specs = h_specs + [
        pl.BlockSpec((None, N_MOD, D_MODEL), lambda s, i: (s + seg_off, 0, 0)),
        _full((1, D_MODEL)),
        _layer_slab((D_MODEL, 2 * FFN_DIM), layer),
        _layer_slab((FFN_DIM, D_MODEL), layer),
    ]
    args = h_args + [mods_l, g.reshape(1, D_MODEL), wi, wo]
    if pre is not None:
        oa, ob, oc, od, wout = pre
        ospec = pl.BlockSpec((None, tm, MIX), lambda s, i: (s + o_off, i, 0))
        in_specs += [ospec, ospec, ospec, ospec, _layer_slab((4 * MIX, D_MODEL), layer)]
        args += [oa, ob, oc, od, wout]
    if final_g is not None:
        in_specs.append(_full((1, D_MODEL)))
        args.append(final_g.reshape(1, D_MODEL))
    kern = functools.partial(_ffn_kernel, k0=k0, split_in=split_in, has_pre=pre is not None,
                             has_final=final_g is not None)
    return pl.pallas_call(
        kern,
        out_shape=jax.ShapeDtypeStruct((nseg, SEG, D_MODEL), F32),
        grid=(nseg, SEG // tm),
        in_specs=in_specs,
        out_specs=pl.BlockSpec((None, tm, D_MODEL), lambda s, i: (s, i, 0)),
        scratch_shapes=[pltpu.VMEM((tm, FFN_DIM), BF16)],
        compiler_params=_params(("arbitrary", "arbitrary")),
        name=name,
    )(*args)


PROJ_TM = 512
COL_AB = 0
COL_C = IN_AB
COL_D = COL_C + 512
PROJ_W = COL_D + MIX + D_CONV_CH + LANE


def _swap_halves(x, half):
    lane = lax.broadcasted_iota(jnp.int32, x.shape, 1)
    first = (lane & half) == 0
    up = pltpu.roll(x, LANE - half, axis=1)
    dn = pltpu.roll(x, half, axis=1)
    return jnp.where(first, up, dn)


def _rope(x, cos, sin, half):
    return x * cos + _swap_halves(x, half) * sin


def _head_rms(x, bd, g):
    sq = x * x
    hi = sq.astype(BF16)
    lo = (sq - hi.astype(F32)).astype(BF16)
    ms = _dot(hi, bd) + _dot(lo, bd)
    return x * lax.rsqrt(ms + EPS) * g


def _inproj_kernel(h_ref, mods_ref, g_ref, w_ref, cab_ref, sab_ref, cm_ref, sm_ref, bd_ref,
                   bqn_ref, bkn_ref, cqn_ref, wuq_ref, ckvn_ref, wukvk_ref, wukvv_ref,
                   qa_ref, ka_ref, va_ref, qb_ref, kb_ref, vb_ref, qc_ref, kc_ref, vc_ref,
                   z_ref, xbc_ref, dt_ref):
    x = h_ref[...]
    m = mods_ref[...]
    n = _rms(x, g_ref[...]) * (1.0 + m[4:5]) + m[3:4]
    nb = n.astype(BF16)
    cab, sab = cab_ref[...], sab_ref[...]
    cm, sm = cm_ref[...], sm_ref[...]
    bd = bd_ref[...]
    scale_ab = HEAD_DIM ** -0.5 * LOG2E
    scale_c = (C_NOPE + C_ROPE) ** -0.5 * LOG2E

    for mixer, (q_ref, k_ref, v_ref) in enumerate(((qa_ref, ka_ref, va_ref), (qb_ref, kb_ref, vb_ref))):
        base = COL_AB + mixer * 512
        p = _dot(nb, w_ref[:, base:base + 512])
        for s in range(2):
            q = p[:, s * LANE:(s + 1) * LANE]
            if mixer == 1:
                q = _head_rms(q, bd, bqn_ref[...])
            q = _rope(q, cab, sab, 16) * scale_ab
            q_ref[:, s * LANE:(s + 1) * LANE] = q.astype(BF16)
        k = p[:, 2 * LANE:3 * LANE]
        if mixer == 1:
            k = _head_rms(k, bd, bkn_ref[...])
        k_ref[...] = _rope(k, cab, sab, 16).astype(BF16)
        v_ref[...] = p[:, 3 * LANE:4 * LANE].astype(BF16)

    p = _dot(nb, w_ref[:, COL_C:COL_C + 512])
    cq = _rms(p[:, 0:C_Q_LORA], cqn_ref[...]).astype(BF16)
    q = _dot(cq, wuq_ref[...])
    ckv = _rms(p[:, C_Q_LORA:C_Q_LORA + C_KV_LORA], ckvn_ref[...]).astype(BF16)
    kn = _dot(ckv, wukvk_ref[...])
    vc_ref[...] = _dot(ckv, wukvv_ref[...]).astype(BF16)
    kr = _rope(p[:, 3 * LANE:4 * LANE], cm, sm, 8)
    for hh in range(C_HEADS):
        sl = slice(hh * LANE, (hh + 1) * LANE)
        qc_ref[:, sl] = (_rope(q[:, sl], cm, sm, 8) * scale_c).astype(BF16)
        kc_ref[:, sl] = (kn[:, sl] + kr).astype(BF16)

    z_ref[...] = _dot(nb, w_ref[:, COL_D:COL_D + MIX])
    for c in range(D_CONV_CH // 256):
        lo = COL_D + MIX + c * 256
        xbc_ref[:, c * 256:(c + 1) * 256] = _dot(nb, w_ref[:, lo:lo + 256])
    lo = COL_D + MIX + D_CONV_CH
    dt_ref[...] = _dot(nb, w_ref[:, lo:lo + LANE])


PROJ_OUT = ([(MIX, BF16), (LANE, BF16), (LANE, BF16)] * 2 + [(512, BF16), (512, BF16), (MIX, BF16)]
            + [(MIX, F32), (D_CONV_CH, F32), (LANE, F32)])


def _inproj_call(h, mods_l, g, w, layer, tabs, bd, bqn, bkn, cqn, wuq, ckvn, wukvk, wukvv):
    tm = PROJ_TM
    cab, sab, cm, sm = tabs
    tok = lambda s, i: (s, i, 0)
    tab = pl.BlockSpec((None, tm, LANE), lambda s, i: (jnp.minimum(s, 1), i, 0))
    return pl.pallas_call(
        _inproj_kernel,
        out_shape=[jax.ShapeDtypeStruct((NSEG, SEG, wd), dt) for wd, dt in PROJ_OUT],
        grid=(NSEG, SEG // tm),
        in_specs=[
            pl.BlockSpec((None, tm, D_MODEL), tok),
            pl.BlockSpec((None, N_MOD, D_MODEL), lambda s, i: (s, 0, 0)),
            _full((1, D_MODEL)),
            _layer_slab((D_MODEL, PROJ_W), layer),
            tab, tab, tab, tab,
            _full((LANE, LANE)),
            _full((1, LANE)), _full((1, LANE)),
            _full((1, C_Q_LORA)), _layer_slab((C_Q_LORA, 512), layer),
            _full((1, C_KV_LORA)), _layer_slab((C_KV_LORA, 512), layer), _layer_slab((C_KV_LORA, MIX), layer),
        ],
        out_specs=[pl.BlockSpec((None, tm, wd), tok) for wd, _ in PROJ_OUT],
        compiler_params=_params(("arbitrary", "arbitrary")),
        name="inproj",
    )(h, mods_l, g.reshape(1, D_MODEL), w, cab, sab, cm, sm, bd,
      bqn, bkn, cqn, wuq, ckvn, wukvk, wukvv)


HEADS_GQA = tuple((g, kv, 0, 0, g, kv) for g in range(2) for kv in range(2))
HEADS_MLA = tuple((h, None, h, h // 2, h // 2, h % 2) for h in range(C_HEADS))


def _masked_q(q, half):
    if half is None:
        return q
    lane = lax.broadcasted_iota(jnp.int32, q.shape, 1)
    keep = (lane < HEAD_DIM) if half == 0 else (lane >= HEAD_DIM)
    return jnp.where(keep, q, jnp.zeros_like(q))


def _store_heads(o_ref, rows, outs):
    lane = lax.broadcasted_iota(jnp.int32, outs[(0, 0)].shape, 1)
    for s in range(2):
        o = jnp.where(lane < HEAD_DIM, outs[(s, 0)], outs[(s, 1)])
        o_ref[rows, s * LANE:(s + 1) * LANE] = o.astype(o_ref.dtype)


ATT_TQ = 256


def _attn_dense_kernel(q_ref, kc_ref, kx_ref, vc_ref, vx_ref, o_ref, *, heads, ctx_first):
    def run(with_x):
        outs = {}
        for qs, qhalf, ks, vs, os_, ohalf in heads:
            qh = _masked_q(q_ref[:, qs * LANE:(qs + 1) * LANE], qhalf)
            ksl = slice(ks * LANE, (ks + 1) * LANE)
            vsl = slice(vs * LANE, (vs + 1) * LANE)
            s_c = _dot_nt(qh, kc_ref[:, ksl])
            mx = jnp.max(s_c, axis=-1, keepdims=True)
            if with_x:
                s_x = _dot_nt(qh, kx_ref[:, ksl])
                mx = jnp.maximum(mx, jnp.max(s_x, axis=-1, keepdims=True))
            p_c = jnp.exp2(s_c - mx)
            den = jnp.sum(p_c, axis=-1, keepdims=True)
            o = _dot(p_c.astype(BF16), vc_ref[:, vsl])
            if with_x:
                p_x = jnp.exp2(s_x - mx)
                den = den + jnp.sum(p_x, axis=-1, keepdims=True)
                o = o + _dot(p_x.astype(BF16), vx_ref[:, vsl])
            outs[(os_, ohalf)] = o * (1.0 / den)
        _store_heads(o_ref, slice(None), outs)

    if ctx_first:
        j = pl.program_id(1)

        @pl.when(j == 0)
        def _():
            run(False)

        @pl.when(j > 0)
        def _():
            run(True)
    else:
        run(True)


def _attn_specs(q, k, v, tq, with_ctx):
    nq = SEQ // tq
    ncb = CTX_LEN // tq if with_ctx else 0
    if with_ctx:
        qmap = lambda b, j: (jnp.where(j < ncb, 0, b + 1), jnp.where(j < ncb, ncb * b + j, j - ncb), 0)
        out_shape = jax.ShapeDtypeStruct((NSEG, SEG, MIX), BF16)
        omap = qmap
    else:
        qmap = lambda b, j: (b + 1, j, 0)
        out_shape = jax.ShapeDtypeStruct((BATCH, SEG, MIX), BF16)
        omap = lambda b, j: (b, j, 0)
    kw, vw = k.shape[-1], v.shape[-1]
    in_specs = [pl.BlockSpec((None, tq, q.shape[-1]), qmap),
                pl.BlockSpec((None, CTX_LEN, kw), lambda b, j: (0, b, 0)),
                pl.BlockSpec((None, SEQ, kw), lambda b, j: (b + 1, 0, 0)),
                pl.BlockSpec((None, CTX_LEN, vw), lambda b, j: (0, b, 0)),
                pl.BlockSpec((None, SEQ, vw), lambda b, j: (b + 1, 0, 0))]
    return (BATCH, nq + ncb), in_specs, pl.BlockSpec((None, tq, MIX), omap), out_shape, ncb


def _attn_dense_call(q, k, v, *, heads, with_ctx, name):
    grid, in_specs, out_spec, out_shape, _ = _attn_specs(q, k, v, ATT_TQ, with_ctx)
    kern = functools.partial(_attn_dense_kernel, heads=heads, ctx_first=with_ctx)
    return pl.pallas_call(
        kern,
        out_shape=out_shape,
        grid=grid,
        in_specs=in_specs,
        out_specs=out_spec,
        compiler_params=_params(("arbitrary", "arbitrary")),
        name=name,
    )(q, k, k, v, v)


WIN_TQ = 256


def _attn_win_kernel(sink_ref, q_ref, kc_ref, kx_ref, vc_ref, vx_ref, o_ref, *, n_ctx_blocks):
    j = pl.program_id(1)
    n_sub = WIN_TQ // WINDOW

    n_h = len(HEADS_GQA)
    head_of_row = lax.broadcasted_iota(jnp.int32, (n_h * WINDOW, 1), 0) // WINDOW
    sink = jnp.zeros((n_h * WINDOW, 1), F32)
    for i, (qs, qhalf, _, _, _, _) in enumerate(HEADS_GQA):
        sink = jnp.where(head_of_row == i, sink_ref[qhalf * 2 + qs] * LOG2E, sink)

    def run(local):
        for sub in range(n_sub):
            rows = slice(sub * WINDOW, (sub + 1) * WINDOW)
            q4 = jnp.concatenate([_masked_q(q_ref[rows, qs * LANE:(qs + 1) * LANE], qhalf)
                                  for qs, qhalf, _, _, _, _ in HEADS_GQA], axis=0)
            s_c = _dot_nt(q4, kc_ref[...])
            mx = jnp.maximum(jnp.max(s_c, axis=-1, keepdims=True), sink)
            if local:
                n = (j - n_ctx_blocks) * n_sub + sub
                start = jnp.clip((n - 1) * WINDOW, 0, SEQ - 3 * WINDOW)
                start = pl.multiple_of(start, WINDOW)
                shape = (n_h * WINDOW, 3 * WINDOW)
                qpos = n * WINDOW + lax.broadcasted_iota(jnp.int32, shape, 0) % WINDOW
                kpos = start + lax.broadcasted_iota(jnp.int32, shape, 1)
                valid = jnp.abs(qpos - kpos) <= WINDOW
                s_l = jnp.where(valid, _dot_nt(q4, kx_ref[pl.ds(start, 3 * WINDOW), :]), -jnp.inf)
                mx = jnp.maximum(mx, jnp.max(s_l, axis=-1, keepdims=True))
            p_c = jnp.exp2(s_c - mx)
            den = jnp.sum(p_c, axis=-1, keepdims=True) + jnp.exp2(sink - mx)
            o = _dot(p_c.astype(BF16), vc_ref[...])
            if local:
                p_l = jnp.exp2(s_l - mx)
                den = den + jnp.sum(p_l, axis=-1, keepdims=True)
                o = o + _dot(p_l.astype(BF16), vx_ref[pl.ds(start, 3 * WINDOW), :])
            o = o * (1.0 / den)
            outs = {(os_, ohalf): o[i * WINDOW:(i + 1) * WINDOW]
                    for i, (_, _, _, _, os_, ohalf) in enumerate(HEADS_GQA)}
            _store_heads(o_ref, rows, outs)

    if n_ctx_blocks:
        @pl.when(j < n_ctx_blocks)
        def _():
            run(False)

        @pl.when(j >= n_ctx_blocks)
        def _():
            run(True)
    else:
        run(True)


def _attn_win_call(sink, q, k, v, *, with_ctx):
    grid, in_specs, out_spec, out_shape, ncb = _attn_specs(q, k, v, WIN_TQ, with_ctx)
    kern = functools.partial(_attn_win_kernel, n_ctx_blocks=ncb)
    return pl.pallas_call(
        kern,
        out_shape=out_shape,
        grid=grid,
        in_specs=[pl.BlockSpec(memory_space=pltpu.SMEM)] + in_specs,
        out_specs=out_spec,
        compiler_params=_params(("arbitrary", "arbitrary")),
        name="attn_win",
    )(sink, q, k, k, v, v)


Q = SSD_CHUNK
PAD = 8
U_CTX = PAD
U_X = PAD + CTX_LEN + PAD
U_ROWS = U_X + SEQ + PAD
N_CTX_CHUNK = CTX_LEN // Q


def _split3(a):
    a1 = a.astype(BF16)
    r1 = a - a1.astype(F32)
    a2 = r1.astype(BF16)
    a3 = (r1 - a2.astype(F32)).astype(BF16)
    return a1, a2, a3


def _ssd_kernel(zc_ref, zx_ref, uc_ref, ux_ref, dtc_ref, dtx_ref, cw_ref, cb_ref, dtb_ref, alog_ref,
                dskip_ref, onorm_ref, yc_ref, yx_ref, upad, xs_s, bm_s, cm_s, dt_s, cs_s, y_s, st_s, h_s):
    zpad = jnp.zeros((PAD, D_CONV_CH), F32)
    upad[0:PAD, :] = zpad
    upad[U_CTX:U_CTX + CTX_LEN, :] = uc_ref[...]
    upad[U_CTX + CTX_LEN:U_X, :] = zpad
    upad[U_X:U_X + SEQ, :] = ux_ref[...]
    upad[U_X + SEQ:U_ROWS, :] = zpad

    for c in range(N_CHUNK):
        base = U_CTX + c * Q if c < N_CTX_CHUNK else U_X + (c - N_CTX_CHUNK) * Q
        for gi, dst in enumerate((xs_s, bm_s, cm_s)):
            cols = slice(gi * 256, (gi + 1) * 256)
            ext = upad[base - PAD:base + Q + PAD, cols]
            acc = jnp.broadcast_to(cb_ref[:, cols], (Q, 256))
            for k in range(D_CONV):
                shift = (D_CONV // 2 - k) % (Q + 2 * PAD)
                tap = ext if shift == 0 else pltpu.roll(ext, shift, axis=0)
                acc = acc + tap[PAD:PAD + Q] * cw_ref[k:k + 1, cols]
            dst[c * Q:(c + 1) * Q, :] = (acc * _sigmoid(acc)).astype(dst.dtype)

    def softplus(v):
        return jnp.maximum(v, 0.0) + jnp.log(1.0 + jnp.exp(-jnp.abs(v)))

    dt_s[0:CTX_LEN, :] = softplus(dtc_ref[...] + dtb_ref[...])
    dt_s[CTX_LEN:TOK, :] = softplus(dtx_ref[...] + dtb_ref[...])

    a_neg = -jnp.exp(alog_ref[...])
    row = lax.broadcasted_iota(jnp.int32, (Q, Q), 0)
    col = lax.broadcasted_iota(jnp.int32, (Q, Q), 1)
    causal = (col <= row, col >= row)
    tri = (causal[0].astype(BF16), causal[1].astype(BF16))
    lo_half = col < HEAD_DIM
    last_row = (Q - 1, 0)

    def bcast_col(v, idx):
        return jnp.broadcast_to(v[:, idx:idx + 1], (Q, Q))

    def group_cols(v, d, g):
        idx = d * D_HEADS + g * 2
        return jnp.where(lo_half, bcast_col(v, idx), bcast_col(v, idx + 1))

    def local_body(c, carry):
        r0 = pl.multiple_of(c * Q, Q)
        rows = pl.ds(r0, Q)
        dt = dt_s[rows, :]
        a1, a2, a3 = _split3(dt * a_neg)
        cs_f = _dot(tri[0], a1) + _dot(tri[0], a2) + _dot(tri[0], a3)
        cs_b = _dot(tri[1], a1) + _dot(tri[1], a2) + _dot(tri[1], a3)
        cs = jnp.where(col < D_HEADS, cs_f, cs_b)
        cs_s[rows, :] = cs
        cs_t = cs.T
        for g in range(2):
            gs = slice(g * Q, (g + 1) * Q)
            cmg = cm_s[rows, gs]
            bmg = bm_s[rows, gs]
            cbm = _dot_nt(cmg, bmg)
            xsg = xs_s[rows, gs]
            y = None
            for d in range(2):
                gmat = []
                for hh in range(2):
                    idx = d * D_HEADS + g * 2 + hh
                    seg = bcast_col(cs, idx) - jnp.broadcast_to(cs_t[idx:idx + 1, :], (Q, Q))
                    dec = jnp.where(causal[d], jnp.exp(jnp.minimum(seg, 0.0)), 0.0)
                    gmat.append((cbm * dec).astype(BF16))
                colg = group_cols(cs, d, g)
                xdt = xsg * group_cols(dt, d, g)
                yd = _dot(gmat[0], jnp.where(lo_half, xdt, 0.0).astype(BF16))
                yd = yd + _dot(gmat[1], jnp.where(lo_half, 0.0, xdt).astype(BF16))
                y = yd if y is None else y + yd
                lastg = colg[last_row[d]:last_row[d] + 1, :]
                xdec = (xdt * jnp.exp(lastg - colg)).astype(BF16)
                st_s[c * 4 + d * 2 + g] = _dot_tn(bmg, xdec)
            y_s[rows, gs] = y
        return carry

    lax.fori_loop(0, N_CHUNK, local_body, 0, unroll=2)

    h_s[...] = jnp.zeros(h_s.shape, F32)

    def scan_chunk(d, c):
        r0 = pl.multiple_of(c * Q, Q)
        rows = pl.ds(r0, Q)
        cs = cs_s[rows, :]
        for g in range(2):
            gs = slice(g * Q, (g + 1) * Q)
            colg = group_cols(cs, d, g)
            h_in = h_s[d * 2 + g]
            y_s[rows, gs] += _dot(cm_s[rows, gs], h_in.astype(BF16)) * jnp.exp(colg)
            lastg = colg[last_row[d]:last_row[d] + 1, :]
            h_s[d * 2 + g] = jnp.exp(lastg) * h_in + st_s[c * 4 + d * 2 + g]

    def scan_body(i, carry):
        scan_chunk(0, i)
        scan_chunk(1, jnp.where(i < N_CTX_CHUNK, N_CTX_CHUNK - 1 - i, N_CHUNK - 1 + N_CTX_CHUNK - i))
        return carry

    lax.fori_loop(0, N_CHUNK, scan_body, 0, unroll=2)

    dskip = dskip_ref[...]
    onorm = onorm_ref[...]

    def finish(rows, z):
        y = y_s[rows, :] + dskip * xs_s[rows, :]
        return _rms(y * (z * _sigmoid(z)), onorm).astype(BF16)

    for c in range(N_CTX_CHUNK):
        yc_ref[c * Q:(c + 1) * Q, :] = finish(slice(c * Q, (c + 1) * Q), zc_ref[c * Q:(c + 1) * Q, :])

    def fin_body(c, carry):
        r0 = pl.multiple_of(c * Q, Q)
        yx_ref[pl.ds(r0, Q), :] = finish(pl.ds(CTX_LEN + r0, Q), zx_ref[pl.ds(r0, Q), :])
        return carry

    lax.fori_loop(0, SEQ // Q, fin_body, 0, unroll=2)


def _ssd_call(z, xbc, dt, cw, cb, dtb, alog, dskip, onorm):
    def cspec(wd):
        return pl.BlockSpec((None, CTX_LEN, wd), lambda b: (0, b, 0))

    def xspec(wd):
        return pl.BlockSpec((None, SEQ, wd), lambda b: (b + 1, 0, 0))

    return pl.pallas_call(
        _ssd_kernel,
        out_shape=[jax.ShapeDtypeStruct((BATCH, CTX_LEN, MIX), BF16),
                   jax.ShapeDtypeStruct((BATCH, SEQ, MIX), BF16)],
        grid=(BATCH,),
        in_specs=[cspec(MIX), xspec(MIX), cspec(D_CONV_CH), xspec(D_CONV_CH), cspec(LANE), xspec(LANE),
                  _full((8, D_CONV_CH)), _full((1, D_CONV_CH)), _full((1, LANE)), _full((1, LANE)),
                  _full((1, MIX)), _full((1, MIX))],
        out_specs=[pl.BlockSpec((None, CTX_LEN, MIX), lambda b: (b, 0, 0)),
                   pl.BlockSpec((None, SEQ, MIX), lambda b: (b, 0, 0))],
        scratch_shapes=[pltpu.VMEM((U_ROWS, D_CONV_CH), F32),
                        pltpu.VMEM((TOK, MIX), F32), pltpu.VMEM((TOK, MIX), BF16), pltpu.VMEM((TOK, MIX), BF16),
                        pltpu.VMEM((TOK, LANE), F32), pltpu.VMEM((TOK, LANE), F32), pltpu.VMEM((TOK, MIX), F32),
                        pltpu.VMEM((N_CHUNK * 4, Q, Q), F32), pltpu.VMEM((4, Q, Q), F32)],
        compiler_params=_params(("arbitrary",)),
        name="ssd",
    )(z, z, xbc, xbc, dt, dt, cw, cb, dtb, alog, dskip, onorm)


def _rope_tables():
    rows = SEQ // GRID_W
    r = jnp.repeat(jnp.arange(rows, dtype=F32), GRID_W)
    c = jnp.tile(jnp.arange(GRID_W, dtype=F32), rows)

    def tables(rot_dim):
        axis_dim = rot_dim // 2
        inv = ROPE_THETA ** (-jnp.arange(0, axis_dim, 2, dtype=F32) / axis_dim)
        ar = r[:, None] * inv[None, :]
        ac = c[:, None] * inv[None, :]
        cos = jnp.concatenate([jnp.cos(ar), jnp.cos(ar), jnp.cos(ac), jnp.cos(ac)], axis=-1)
        sin = jnp.concatenate([-jnp.sin(ar), jnp.sin(ar), -jnp.sin(ac), jnp.sin(ac)], axis=-1)
        return cos, sin

    c64, s64 = tables(HEAD_DIM)
    cab = jnp.tile(c64, (1, 2))
    sab = jnp.tile(s64, (1, 2))
    c32, s32 = tables(C_ROPE)
    cm = jnp.concatenate([jnp.ones((SEQ, C_NOPE), F32), c32, jnp.ones((SEQ, 32), F32)], axis=-1)
    sm = jnp.concatenate([jnp.zeros((SEQ, C_NOPE), F32), s32, jnp.zeros((SEQ, 32), F32)], axis=-1)

    def with_identity(tab, fill):
        return jnp.stack([jnp.full_like(tab, fill), tab])

    return (with_identity(cab, 1.0), with_identity(sab, 0.0), with_identity(cm, 1.0), with_identity(sm, 0.0))


def _head_mean_matrix():
    lane = jnp.arange(LANE)
    same = (lane[:, None] // HEAD_DIM) == (lane[None, :] // HEAD_DIM)
    return jnp.where(same, 1.0 / HEAD_DIM, 0.0).astype(BF16)


def _gqa_order(w, axis):
    shp = w.shape
    w = w.reshape(shp[:axis] + (2, 2, HEAD_DIM) + shp[axis + 1:])
    w = jnp.swapaxes(w, axis, axis + 1)
    return w.reshape(shp)


def _stacked_weights(w_in, c_w_uq, c_w_ukv, w_out):
    n_l, d = w_in.shape[:2]
    zc = lambda n: jnp.zeros((n_l, d, n), BF16)
    wb = w_in.astype(BF16)
    o_c = IN_AB
    o_d = IN_AB + IN_C
    w = jnp.concatenate([
        _gqa_order(wb[..., 0:MIX], 2), wb[..., MIX:512],
        _gqa_order(wb[..., 512:512 + MIX], 2), wb[..., 512 + MIX:o_c + C_Q_LORA + C_KV_LORA],
        zc(C_NOPE), wb[..., o_c + C_Q_LORA + C_KV_LORA:o_d], zc(LANE - C_NOPE - C_ROPE),
        wb[..., o_d:], zc(LANE - 2 * D_HEADS),
    ], axis=2)
    assert w.shape[2] == PROJ_W
    dq = C_NOPE + C_ROPE
    pad_last = lambda t, n: jnp.pad(t, ((0, 0),) * (t.ndim - 1) + ((0, n),))
    wuq = pad_last(c_w_uq.reshape(n_l, C_Q_LORA, C_HEADS, dq), LANE - dq)
    wkv = c_w_ukv.reshape(n_l, C_KV_LORA, C_HEADS, C_NOPE + C_V)
    wk = pad_last(wkv[..., :C_NOPE], LANE - C_NOPE)
    wv = wkv[..., C_NOPE:]
    wout = jnp.concatenate([_gqa_order(w_out[:, 0:MIX], 1), _gqa_order(w_out[:, MIX:2 * MIX], 1),
                            w_out[:, 2 * MIX:]], axis=1)
    return (w, wuq.reshape(n_l, C_Q_LORA, 512).astype(BF16), wk.reshape(n_l, C_KV_LORA, 512).astype(BF16),
            wv.reshape(n_l, C_KV_LORA, MIX).astype(BF16), wout.astype(BF16))


def _lane_row(v, width=LANE):
    v = v.reshape(1, -1).astype(F32)
    return jnp.pad(v, ((0, 0), (0, width - v.shape[1])))


def kernel(x, c, ctx, c_ctx, ada_w, ada_b, ffn1_norm, ffn1_wi, ffn1_wo, mix_norm, w_in, w_out, a_sink, b_q_norm, b_k_norm, c_q_norm, c_w_uq, c_kv_norm, c_w_ukv, d_conv_w, d_conv_b, d_a_log, d_dt_bias, d_skip, d_out_norm, ffn2_norm, ffn2_wi, ffn2_wo, final_norm):
    cvec = jnp.concatenate([c_ctx[None, :], c, jnp.zeros((16 - NSEG, D_MODEL), F32)], axis=0)
    mods = _mods_call(cvec, ada_w, ada_b).reshape(DEPTH, 16, N_MOD, D_MODEL)[:, :NSEG]
    tabs = _rope_tables()
    bd = _head_mean_matrix()
    wi1, wo1 = ffn1_wi.astype(BF16), ffn1_wo.astype(BF16)
    wi2, wo2 = ffn2_wi.astype(BF16), ffn2_wo.astype(BF16)
    w, wuq, wk, wv, wout = _stacked_weights(w_in, c_w_uq, c_w_ukv, w_out)

    h = (ctx.reshape(1, BATCH * CTX_LEN, D_MODEL), x)
    out = None
    for l in range(DEPTH):
        with_ctx = l < DEPTH - 1
        mods_l = mods[l]
        h = _ffn_call(h, mods_l, ffn1_norm[l], wi1, wo1, l, k0=0, seg_off=0, name=f"ffn1_{l}")
        (qa, ka, va, qb, kb, vb, qc, kc, vc, z, xbc, dt) = _inproj_call(
            h, mods_l, mix_norm[l], w, l, tabs, bd,
            jnp.tile(b_q_norm[l], 2).reshape(1, LANE), jnp.tile(b_k_norm[l], 2).reshape(1, LANE),
            c_q_norm[l].reshape(1, C_Q_LORA), wuq, c_kv_norm[l].reshape(1, C_KV_LORA), wk, wv)

        oa = _attn_win_call(a_sink[l], qa, ka, va, with_ctx=with_ctx)
        ob = _attn_dense_call(qb, kb, vb, heads=HEADS_GQA, with_ctx=with_ctx, name=f"attn_b_{l}")
        oc = _attn_dense_call(qc, kc, vc, heads=HEADS_MLA, with_ctx=with_ctx, name=f"attn_c_{l}")
        cw = jnp.pad(d_conv_w[l], ((0, 8 - D_CONV), (0, 0)))
        yc, yx = _ssd_call(z, xbc, dt, cw, d_conv_b[l].reshape(1, D_CONV_CH),
                           _lane_row(d_dt_bias[l]), _lane_row(d_a_log[l]),
                           jnp.repeat(d_skip[l], HEAD_DIM).reshape(1, MIX), d_out_norm[l].reshape(1, MIX))
        if with_ctx:
            od = jnp.concatenate([yc.reshape(1, SEG, MIX), yx], axis=0)
            h = _ffn_call(h, mods_l, ffn2_norm[l], wi2, wo2, l, k0=6, seg_off=0,
                          pre=(oa, ob, oc, od, wout), o_off=0, name=f"ffn2_{l}")
        else:
            out = _ffn_call(h, mods_l, ffn2_norm[l], wi2, wo2, l, k0=6, seg_off=1,
                            pre=(oa, ob, oc, yx, wout), o_off=0, final_g=final_norm, name=f"ffn2_{l}")
    return out
```

```python
import functools

import jax
import jax.numpy as jnp
from jax import lax
from jax.experimental import pallas as pl
from jax.experimental.pallas import tpu as pltpu

D_MODEL = 1024
BATCH = 8
SEQ = 2048
DEPTH = 2
GRID_W = 64
CTX_LEN = 256
HEAD_DIM = 64
ROPE_THETA = 10000.0
EPS = 1e-6
FFN_DIM = 2816
N_MOD = 9
WINDOW = 128
C_HEADS = 4
C_Q_LORA = 256
C_KV_LORA = 128
C_NOPE = 64
C_ROPE = 32
C_V = 64
D_HEADS = 4
D_STATE = 128
D_CONV = 5
SSD_CHUNK = 128
MIX = 256
IN_AB = 1024
IN_C = C_Q_LORA + C_KV_LORA + C_ROPE
D_CONV_CH = MIX + 2 * 2 * D_STATE
IN_D = MIX + D_CONV_CH + 2 * D_HEADS

NSEG = BATCH + 1
SEG = SEQ
assert BATCH * CTX_LEN == SEG
TOK = CTX_LEN + SEQ
N_CHUNK = TOK // SSD_CHUNK

LANE = 128
VMEM_LIMIT = 56 * 1024 * 1024

F32 = jnp.float32
BF16 = jnp.bfloat16
LOG2E = 1.4426950408889634


def _dot(a, b):
    return jnp.dot(a, b, preferred_element_type=F32)


def _dot_nt(a, b):
    return lax.dot_general(a, b, (((1,), (1,)), ((), ())), preferred_element_type=F32)


def _dot_tn(a, b):
    return lax.dot_general(a, b, (((0,), (0,)), ((), ())), preferred_element_type=F32)


def _sigmoid(x):
    return 1.0 / (1.0 + jnp.exp(-x))


def _rms(x, g):
    return x * lax.rsqrt(jnp.mean(x * x, axis=-1, keepdims=True) + EPS) * g


def _full(shape):
    nd = len(shape)
    return pl.BlockSpec(shape, lambda *_: (0,) * nd)


def _layer_slab(shape, layer):
    nd = len(shape)
    return pl.BlockSpec((None,) + tuple(shape), lambda *_: (layer,) + (0,) * nd, pipeline_mode=pl.Buffered(1))


def _params(sem):
    return pltpu.CompilerParams(dimension_semantics=sem, vmem_limit_bytes=VMEM_LIMIT)


def _mods_kernel(c_ref, w_ref, b_ref, o_ref):
    c = c_ref[...]
    s = (c * _sigmoid(c)).astype(BF16)
    o_ref[...] = _dot(s, w_ref[...].astype(BF16)) + b_ref[...]


def _mods_call(cvec, ada_w, ada_b):
    n_l = ada_w.shape[0]
    return pl.pallas_call(
        _mods_kernel,
        out_shape=jax.ShapeDtypeStruct((n_l, 16, N_MOD * D_MODEL), F32),
        grid=(n_l, N_MOD),
        in_specs=[
            pl.BlockSpec((16, D_MODEL), lambda l, n: (0, 0)),
            pl.BlockSpec((None, D_MODEL, D_MODEL), lambda l, n: (l, 0, n)),
            pl.BlockSpec((None, 1, D_MODEL), lambda l, n: (l, 0, n)),
        ],
        out_specs=pl.BlockSpec((None, 16, D_MODEL), lambda l, n: (l, 0, n)),
        compiler_params=_params(("arbitrary", "arbitrary")),
        name="mods",
    )(cvec, ada_w, ada_b.reshape(n_l, 1, N_MOD * D_MODEL))


FFN_TM = 512
FFN_FC = 256


def _ffn_kernel(*refs, k0, split_in, has_pre, has_final):
    if split_in:
        hc_ref, hx_ref = refs[:2]
        refs = refs[1:]
    h_ref, mods_ref, g_ref, wi_ref, wo_ref = refs[:5]
    pos = 5
    if has_pre:
        oa_ref, ob_ref, oc_ref, od_ref, wout_ref = refs[pos:pos + 5]
        pos += 5
    if has_final:
        gf_ref = refs[pos]
        pos += 1
    out_ref, hm_ref = refs[pos], refs[pos + 1]

    if split_in:
        x = jnp.where(pl.program_id(0) == 0, hc_ref[...], hx_ref[...])
    else:
        x = h_ref[...]
    m = mods_ref[...]
    if has_pre:
        o = _dot(oa_ref[...], wout_ref[0:MIX, :])
        o = o + _dot(ob_ref[...], wout_ref[MIX:2 * MIX, :])
        o = o + _dot(oc_ref[...], wout_ref[2 * MIX:3 * MIX, :])
        o = o + _dot(od_ref[...], wout_ref[3 * MIX:4 * MIX, :])
        x = x + m[5:6] * o
    n = _rms(x, g_ref[...]) * (1.0 + m[k0 + 1:k0 + 2]) + m[k0:k0 + 1]
    nb = n.astype(BF16)
    for c in range(FFN_DIM // FFN_FC):
        lo = c * FFN_FC
        a = _dot(nb, wi_ref[:, lo:lo + FFN_FC])
        b = _dot(nb, wi_ref[:, FFN_DIM + lo:FFN_DIM + lo + FFN_FC])
        hm_ref[:, lo:lo + FFN_FC] = (a * _sigmoid(a) * b).astype(BF16)
    y = _dot(hm_ref[...], wo_ref[...])
    out = x + 0.5 * m[k0 + 2:k0 + 3] * y
    if has_final:
        out = _rms(out, gf_ref[...])
    out_ref[...] = out


def _ffn_call(h, mods_l, g, wi, wo, layer, *, k0, seg_off, pre=None, o_off=0, final_g=None, name):
    split_in = isinstance(h, tuple)
    tm = FFN_TM
    if split_in:
        assert seg_off == 0
        nseg = NSEG
        h_args = list(h)
        h_specs = [
            pl.BlockSpec((None, tm, D_MODEL), lambda s, i: (0, jnp.where(s == 0, i, 0), 0)),
            pl.BlockSpec((None, tm, D_MODEL), lambda s, i: (jnp.maximum(s - 1, 0), jnp.where(s == 0, 0, i), 0)),
        ]
    else:
        nseg = h.shape[0] - seg_off
        h_args = [h]
        h_specs = [pl.BlockSpec((None, tm, D_MODEL), lambda s, i: (s + seg_off, i, 0))]
    in_specs = h_specs + [
        pl.BlockSpec((None, N_MOD, D_MODEL), lambda s, i: (s + seg_off, 0, 0)),
        _full((1, D_MODEL)),
        _layer_slab((D_MODEL, 2 * FFN_DIM), layer),
        _layer_slab((FFN_DIM, D_MODEL), layer),
    ]
    args = h_args + [mods_l, g.reshape(1, D_MODEL), wi, wo]
    if pre is not None:
        oa, ob, oc, od, wout = pre
        ospec = pl.BlockSpec((None, tm, MIX), lambda s, i: (s + o_off, i, 0))
        in_specs += [ospec, ospec, ospec, ospec, _layer_slab((4 * MIX, D_MODEL), layer)]
        args += [oa, ob, oc, od, wout]
    if final_g is not None:
        in_specs.append(_full((1, D_MODEL)))
        args.append(final_g.reshape(1, D_MODEL))
    kern = functools.partial(_ffn_kernel, k0=k0, split_in=split_in, has_pre=pre is not None,
                             has_final=final_g is not None)
    return pl.pallas_call(
        kern,
        out_shape=jax.ShapeDtypeStruct((nseg, SEG, D_MODEL), F32),
        grid=(nseg, SEG // tm),
        in_specs=in_specs,
        out_specs=pl.BlockSpec((None, tm, D_MODEL), lambda s, i: (s, i, 0)),
        scratch_shapes=[pltpu.VMEM((tm, FFN_DIM), BF16)],
        compiler_params=_params(("arbitrary", "arbitrary")),
        name=name,
    )(*args)


PROJ_TM = 512
COL_AB = 0
COL_C = IN_AB
COL_D = COL_C + 512
PROJ_W = COL_D + MIX + D_CONV_CH + LANE


def _swap_halves(x, half):
    lane = lax.broadcasted_iota(jnp.int32, x.shape, 1)
    first = (lane & half) == 0
    up = pltpu.roll(x, LANE - half, axis=1)
    dn = pltpu.roll(x, half, axis=1)
    return jnp.where(first, up, dn)


def _rope(x, cos, sin, half):
    return x * cos + _swap_halves(x, half) * sin


def _head_rms(x, bd, g):
    sq = x * x
    hi = sq.astype(BF16)
    lo = (sq - hi.astype(F32)).astype(BF16)
    ms = _dot(hi, bd) + _dot(lo, bd)
    return x * lax.rsqrt(ms + EPS) * g


def _inproj_kernel(h_ref, mods_ref, g_ref, w_ref, cab_ref, sab_ref, cm_ref, sm_ref, bd_ref,
                   bqn_ref, bkn_ref, cqn_ref, wuq_ref, ckvn_ref, wukvk_ref, wukvv_ref,
                   qa_ref, ka_ref, va_ref, qb_ref, kb_ref, vb_ref, qc_ref, kc_ref, vc_ref,
                   z_ref, xbc_ref, dt_ref):
    x = h_ref[...]
    m = mods_ref[...]
    n = _rms(x, g_ref[...]) * (1.0 + m[4:5]) + m[3:4]
    nb = n.astype(BF16)
    cab, sab = cab_ref[...], sab_ref[...]
    cm, sm = cm_ref[...], sm_ref[...]
    bd = bd_ref[...]
    scale_ab = HEAD_DIM ** -0.5 * LOG2E
    scale_c = (C_NOPE + C_ROPE) ** -0.5 * LOG2E

    for mixer, (q_ref, k_ref, v_ref) in enumerate(((qa_ref, ka_ref, va_ref), (qb_ref, kb_ref, vb_ref))):
        base = COL_AB + mixer * 512
        p = _dot(nb, w_ref[:, base:base + 512])
        for s in range(2):
            q = p[:, s * LANE:(s + 1) * LANE]
            if mixer == 1:
                q = _head_rms(q, bd, bqn_ref[...])
            q = _rope(q, cab, sab, 16) * scale_ab
            q_ref[:, s * LANE:(s + 1) * LANE] = q.astype(BF16)
        k = p[:, 2 * LANE:3 * LANE]
        if mixer == 1:
            k = _head_rms(k, bd, bkn_ref[...])
        k_ref[...] = _rope(k, cab, sab, 16).astype(BF16)
        v_ref[...] = p[:, 3 * LANE:4 * LANE].astype(BF16)

    p = _dot(nb, w_ref[:, COL_C:COL_C + 512])
    cq = _rms(p[:, 0:C_Q_LORA], cqn_ref[...]).astype(BF16)
    q = _dot(cq, wuq_ref[...])
    ckv = _rms(p[:, C_Q_LORA:C_Q_LORA + C_KV_LORA], ckvn_ref[...]).astype(BF16)
    kn = _dot(ckv, wukvk_ref[...])
    vc_ref[...] = _dot(ckv, wukvv_ref[...]).astype(BF16)
    kr = _rope(p[:, 3 * LANE:4 * LANE], cm, sm, 8)
    for hh in range(C_HEADS):
        sl = slice(hh * LANE, (hh + 1) * LANE)
        qc_ref[:, sl] = (_rope(q[:, sl], cm, sm, 8) * scale_c).astype(BF16)
        kc_ref[:, sl] = (kn[:, sl] + kr).astype(BF16)

    z_ref[...] = _dot(nb, w_ref[:, COL_D:COL_D + MIX])
    for c in range(D_CONV_CH // 256):
        lo = COL_D + MIX + c * 256
        xbc_ref[:, c * 256:(c + 1) * 256] = _dot(nb, w_ref[:, lo:lo + 256])
    lo = COL_D + MIX + D_CONV_CH
    dt_ref[...] = _dot(nb, w_ref[:, lo:lo + LANE])


PROJ_OUT = ([(MIX, BF16), (LANE, BF16), (LANE, BF16)] * 2 + [(512, BF16), (512, BF16), (MIX, BF16)]
            + [(MIX, F32), (D_CONV_CH, F32), (LANE, F32)])


def _inproj_call(h, mods_l, g, w, layer, tabs, bd, bqn, bkn, cqn, wuq, ckvn, wukvk, wukvv):
    tm = PROJ_TM
    cab, sab, cm, sm = tabs
    tok = lambda s, i: (s, i, 0)
    tab = pl.BlockSpec((None, tm, LANE), lambda s, i: (jnp.minimum(s, 1), i, 0))
    return pl.pallas_call(
        _inproj_kernel,
        out_shape=[jax.ShapeDtypeStruct((NSEG, SEG, wd), dt) for wd, dt in PROJ_OUT],
        grid=(NSEG, SEG // tm),
        in_specs=[
            pl.BlockSpec((None, tm, D_MODEL), tok),
            pl.BlockSpec((None, N_MOD, D_MODEL), lambda s, i: (s, 0, 0)),
            _full((1, D_MODEL)),
            _layer_slab((D_MODEL, PROJ_W), layer),
            tab, tab, tab, tab,
            _full((LANE, LANE)),
            _full((1, LANE)), _full((1, LANE)),
            _full((1, C_Q_LORA)), _layer_slab((C_Q_LORA, 512), layer),
            _full((1, C_KV_LORA)), _layer_slab((C_KV_LORA, 512), layer), _layer_slab((C_KV_LORA, MIX), layer),
        ],
        out_specs=[pl.BlockSpec((None, tm, wd), tok) for wd, _ in PROJ_OUT],
        compiler_params=_params(("arbitrary", "arbitrary")),
        name="inproj",
    )(h, mods_l, g.reshape(1, D_MODEL), w, cab, sab, cm, sm, bd,
      bqn, bkn, cqn, wuq, ckvn, wukvk, wukvv)


HEADS_GQA = tuple((g, kv, 0, 0, g, kv) for g in range(2) for kv in range(2))
HEADS_MLA = tuple((h, None, h, h // 2, h // 2, h % 2) for h in range(C_HEADS))


def _masked_q(q, half):
    if half is None:
        return q
    lane = lax.broadcasted_iota(jnp.int32, q.shape, 1)
    keep = (lane < HEAD_DIM) if half == 0 else (lane >= HEAD_DIM)
    return jnp.where(keep, q, jnp.zeros_like(q))


def _store_heads(o_ref, rows, outs):
    lane = lax.broadcasted_iota(jnp.int32, outs[(0, 0)].shape, 1)
    for s in range(2):
        o = jnp.where(lane < HEAD_DIM, outs[(s, 0)], outs[(s, 1)])
        o_ref[rows, s * LANE:(s + 1) * LANE] = o.astype(o_ref.dtype)


ATT_TQ = 256


def _attn_dense_kernel(q_ref, kc_ref, kx_ref, vc_ref, vx_ref, o_ref, *, heads, ctx_first):
    def run(with_x):
        outs = {}
        for qs, qhalf, ks, vs, os_, ohalf in heads:
            qh = _masked_q(q_ref[:, qs * LANE:(qs + 1) * LANE], qhalf)
            ksl = slice(ks * LANE, (ks + 1) * LANE)
            vsl = slice(vs * LANE, (vs + 1) * LANE)
            s_c = _dot_nt(qh, kc_ref[:, ksl])
            mx = jnp.max(s_c, axis=-1, keepdims=True)
            if with_x:
                s_x = _dot_nt(qh, kx_ref[:, ksl])
                mx = jnp.maximum(mx, jnp.max(s_x, axis=-1, keepdims=True))
            p_c = jnp.exp2(s_c - mx)
            den = jnp.sum(p_c, axis=-1, keepdims=True)
            o = _dot(p_c.astype(BF16), vc_ref[:, vsl])
            if with_x:
                p_x = jnp.exp2(s_x - mx)
                den = den + jnp.sum(p_x, axis=-1, keepdims=True)
                o = o + _dot(p_x.astype(BF16), vx_ref[:, vsl])
            outs[(os_, ohalf)] = o * (1.0 / den)
        _store_heads(o_ref, slice(None), outs)

    if ctx_first:
        j = pl.program_id(1)

        @pl.when(j == 0)
        def _():
            run(False)

        @pl.when(j > 0)
        def _():
            run(True)
    else:
        run(True)


def _attn_specs(q, k, v, tq, with_ctx):
    nq = SEQ // tq
    ncb = CTX_LEN // tq if with_ctx else 0
    if with_ctx:
        qmap = lambda b, j: (jnp.where(j < ncb, 0, b + 1), jnp.where(j < ncb, ncb * b + j, j - ncb), 0)
        out_shape = jax.ShapeDtypeStruct((NSEG, SEG, MIX), BF16)
        omap = qmap
    else:
        qmap = lambda b, j: (b + 1, j, 0)
        out_shape = jax.ShapeDtypeStruct((BATCH, SEG, MIX), BF16)
        omap = lambda b, j: (b, j, 0)
    kw, vw = k.shape[-1], v.shape[-1]
    in_specs = [pl.BlockSpec((None, tq, q.shape[-1]), qmap),
                pl.BlockSpec((None, CTX_LEN, kw), lambda b, j: (0, b, 0)),
                pl.BlockSpec((None, SEQ, kw), lambda b, j: (b + 1, 0, 0)),
                pl.BlockSpec((None, CTX_LEN, vw), lambda b, j: (0, b, 0)),
                pl.BlockSpec((None, SEQ, vw), lambda b, j: (b + 1, 0, 0))]
    return (BATCH, nq + ncb), in_specs, pl.BlockSpec((None, tq, MIX), omap), out_shape, ncb


def _attn_dense_call(q, k, v, *, heads, with_ctx, name):
    grid, in_specs, out_spec, out_shape, _ = _attn_specs(q, k, v, ATT_TQ, with_ctx)
    kern = functools.partial(_attn_dense_kernel, heads=heads, ctx_first=with_ctx)
    return pl.pallas_call(
        kern,
        out_shape=out_shape,
        grid=grid,
        in_specs=in_specs,
        out_specs=out_spec,
        compiler_params=_params(("arbitrary", "arbitrary")),
        name=name,
    )(q, k, k, v, v)


WIN_TQ = 256


def _attn_win_kernel(sink_ref, q_ref, kc_ref, kx_ref, vc_ref, vx_ref, o_ref, *, n_ctx_blocks):
    j = pl.program_id(1)
    n_sub = WIN_TQ // WINDOW

    n_h = len(HEADS_GQA)
    head_of_row = lax.broadcasted_iota(jnp.int32, (n_h * WINDOW, 1), 0) // WINDOW
    sink = jnp.zeros((n_h * WINDOW, 1), F32)
    for i, (qs, qhalf, _, _, _, _) in enumerate(HEADS_GQA):
        sink = jnp.where(head_of_row == i, sink_ref[qhalf * 2 + qs] * LOG2E, sink)

    def run(local):
        for sub in range(n_sub):
            rows = slice(sub * WINDOW, (sub + 1) * WINDOW)
            q4 = jnp.concatenate([_masked_q(q_ref[rows, qs * LANE:(qs + 1) * LANE], qhalf)
                                  for qs, qhalf, _, _, _, _ in HEADS_GQA], axis=0)
            s_c = _dot_nt(q4, kc_ref[...])
            mx = jnp.maximum(jnp.max(s_c, axis=-1, keepdims=True), sink)
            if local:
                n = (j - n_ctx_blocks) * n_sub + sub
                start = jnp.clip((n - 1) * WINDOW, 0, SEQ - 3 * WINDOW)
                start = pl.multiple_of(start, WINDOW)
                shape = (n_h * WINDOW, 3 * WINDOW)
                qpos = n * WINDOW + lax.broadcasted_iota(jnp.int32, shape, 0) % WINDOW
                kpos = start + lax.broadcasted_iota(jnp.int32, shape, 1)
                valid = jnp.abs(qpos - kpos) <= WINDOW
                s_l = jnp.where(valid, _dot_nt(q4, kx_ref[pl.ds(start, 3 * WINDOW), :]), -jnp.inf)
                mx = jnp.maximum(mx, jnp.max(s_l, axis=-1, keepdims=True))
            p_c = jnp.exp2(s_c - mx)
            den = jnp.sum(p_c, axis=-1, keepdims=True) + jnp.exp2(sink - mx)
            o = _dot(p_c.astype(BF16), vc_ref[...])
            if local:
                p_l = jnp.exp2(s_l - mx)
                den = den + jnp.sum(p_l, axis=-1, keepdims=True)
                o = o + _dot(p_l.astype(BF16), vx_ref[pl.ds(start, 3 * WINDOW), :])
            o = o * (1.0 / den)
            outs = {(os_, ohalf): o[i * WINDOW:(i + 1) * WINDOW]
                    for i, (_, _, _, _, os_, ohalf) in enumerate(HEADS_GQA)}
            _store_heads(o_ref, rows, outs)

    if n_ctx_blocks:
        @pl.when(j < n_ctx_blocks)
        def _():
            run(False)

        @pl.when(j >= n_ctx_blocks)
        def _():
            run(True)
    else:
        run(True)


def _attn_win_call(sink, q, k, v, *, with_ctx):
    grid, in_specs, out_spec, out_shape, ncb = _attn_specs(q, k, v, WIN_TQ, with_ctx)
    kern = functools.partial(_attn_win_kernel, n_ctx_blocks=ncb)
    return pl.pallas_call(
        kern,
        out_shape=out_shape,
        grid=grid,
        in_specs=[pl.BlockSpec(memory_space=pltpu.SMEM)] + in_specs,
        out_specs=out_spec,
        compiler_params=_params(("arbitrary", "arbitrary")),
        name="attn_win",
    )(sink, q, k, k, v, v)


Q = SSD_CHUNK
PAD = 8
U_CTX = PAD
U_X = PAD + CTX_LEN + PAD
U_ROWS = U_X + SEQ + PAD
N_CTX_CHUNK = CTX_LEN // Q
LOCAL_CHUNKS = 3
SCAN_STEPS = 2


def _split3(a):
    a1 = a.astype(BF16)
    r1 = a - a1.astype(F32)
    a2 = r1.astype(BF16)
    a3 = (r1 - a2.astype(F32)).astype(BF16)
    return a1, a2, a3


def _ssd_kernel(zc_ref, zx_ref, uc_ref, ux_ref, dtc_ref, dtx_ref, cw_ref, cb_ref, dtb_ref, alog_ref,
                dskip_ref, onorm_ref, yc_ref, yx_ref, upad, xs_s, bm_s, cm_s, dt_s, y_s, st_s, cme_s, da_s, h_s):
    n_slab = D_CONV_CH // LANE
    zpad = jnp.zeros((PAD, LANE), F32)
    for j in range(n_slab):
        sl = slice(j * LANE, (j + 1) * LANE)
        upad[j, 0:PAD, :] = zpad
        upad[j, U_CTX:U_CTX + CTX_LEN, :] = uc_ref[:, sl]
        upad[j, U_CTX + CTX_LEN:U_X, :] = zpad
        upad[j, U_X:U_X + SEQ, :] = ux_ref[:, sl]
        upad[j, U_X + SEQ:U_ROWS, :] = zpad

    for c in range(N_CHUNK):
        base = U_CTX + c * Q if c < N_CTX_CHUNK else U_X + (c - N_CTX_CHUNK) * Q
        for j in range(n_slab):
            sl = slice(j * LANE, (j + 1) * LANE)
            dst = (xs_s, bm_s, cm_s)[j // 2]
            acc = jnp.broadcast_to(cb_ref[:, sl], (Q, LANE))
            for k in range(D_CONV):
                lo = base + k - D_CONV // 2
                acc = acc + upad[j, lo:lo + Q, :] * cw_ref[k:k + 1, sl]
            dcol = (j % 2) * LANE
            dst[c * Q:(c + 1) * Q, dcol:dcol + LANE] = (acc * _sigmoid(acc)).astype(dst.dtype)

    def softplus(v):
        return jnp.maximum(v, 0.0) + jnp.log(1.0 + jnp.exp(-jnp.abs(v)))

    dt_s[0:CTX_LEN, :] = softplus(dtc_ref[...] + dtb_ref[...])
    dt_s[CTX_LEN:TOK, :] = softplus(dtx_ref[...] + dtb_ref[...])

    a_neg = -jnp.exp(alog_ref[...]) * LOG2E
    row = lax.broadcasted_iota(jnp.int32, (Q, Q), 0)
    col = lax.broadcasted_iota(jnp.int32, (Q, Q), 1)
    causal = (col <= row, col >= row)
    tri = (causal[0].astype(BF16), causal[1].astype(BF16))
    lo_half = col < HEAD_DIM
    last_row = (Q - 1, 0)

    def bcast_col(v, idx):
        return jnp.broadcast_to(v[:, idx:idx + 1], (Q, Q))

    def chunk_rows(c):
        return pl.ds(c * Q if isinstance(c, int) else pl.multiple_of(c * Q, Q), Q)

    def local_load(c):
        rows = chunk_rows(c)
        groups = [(cm_s[rows, g * Q:(g + 1) * Q], bm_s[rows, g * Q:(g + 1) * Q], xs_s[rows, g * Q:(g + 1) * Q])
                  for g in range(2)]
        return dt_s[rows, :], groups

    def local_sums(dt, groups):
        a1, a2, a3 = _split3(dt * a_neg)
        cs_f = _dot(tri[0], a1) + _dot(tri[0], a2) + _dot(tri[0], a3)
        cs_b = _dot(tri[1], a1) + _dot(tri[1], a2) + _dot(tri[1], a3)
        cs = jnp.where(col < D_HEADS, cs_f, cs_b)
        return cs, cs.T, dt.T, [_dot_nt(cmg, bmg) for cmg, bmg, _ in groups]

    def local_compute(groups, cs, cs_t, dt_t, cbms):
        ys, sts, das, cmes = [], {}, {}, {}
        for g, (cmg, bmg, xsg) in enumerate(groups):
            cbm = cbms[g]
            cmg32 = cmg.astype(F32)
            bm_t = bmg.astype(F32).T
            xh = (jnp.where(lo_half, xsg, 0.0).astype(BF16), jnp.where(lo_half, 0.0, xsg).astype(BF16))
            y = None
            for d in range(2):
                st = None
                da = []
                for hh in range(2):
                    idx = d * D_HEADS + g * 2 + hh
                    colb = bcast_col(cs, idx)
                    cs_row = cs_t[idx:idx + 1, :]
                    dt_row = dt_t[idx:idx + 1, :]
                    seg = colb - (cs_row - jnp.log2(dt_row))
                    dec = jnp.exp2(jnp.where(causal[d], seg, -jnp.inf))
                    yd = _dot((cbm * dec).astype(BF16), xh[hh])
                    y = yd if y is None else y + yd
                    last = cs_row[:, last_row[d]:last_row[d] + 1]
                    w_row = dt_row * jnp.exp2(last - cs_row)
                    sth = _dot((bm_t * w_row).astype(BF16), xh[hh])
                    st = sth if st is None else st + sth
                    cmes[d * 4 + g * 2 + hh] = (cmg32 * jnp.exp2(colb)).astype(BF16)
                    da.append(jnp.broadcast_to(jnp.exp2(last), (1, Q)))
                sts[d * 2 + g] = st
                das[d * 2 + g] = jnp.where(lo_half[0:1, :], da[0], da[1])
            ys.append(y)
        return ys, sts, das, cmes

    def local_store(c, res):
        ys, sts, das, cmes = res
        rows = chunk_rows(c)
        for g in range(2):
            y_s[rows, g * Q:(g + 1) * Q] = ys[g]
        for k, v in sts.items():
            st_s[c * 4 + k] = v
        for k, v in das.items():
            da_s[c * 4 + k, 0:1, :] = v
        for k, v in cmes.items():
            cme_s[c * 8 + k] = v

    def local_body(i, carry):
        chunks = [i * LOCAL_CHUNKS + u for u in range(LOCAL_CHUNKS)]
        loaded = [local_load(c) for c in chunks]
        sums = [local_sums(dt, groups) for dt, groups in loaded]
        results = [local_compute(groups, *sm) for (_, groups), sm in zip(loaded, sums)]
        for c, res in zip(chunks, results):
            local_store(c, res)
        return carry

    lax.fori_loop(0, N_CHUNK // LOCAL_CHUNKS, local_body, 0)

    h_s[...] = jnp.zeros(h_s.shape, F32)

    def scan_steps(steps):
        work = []
        for c_fwd, c_bwd in steps:
            work += [(d, g, c) for d, c in ((0, c_fwd), (1, c_bwd)) for g in range(2)]
        loaded = []
        for d, g, c in work:
            k = c * 4 + d * 2 + g
            loaded.append((cme_s[2 * k], cme_s[2 * k + 1], da_s[k, 0:1, :], st_s[k],
                           y_s[chunk_rows(c), g * Q:(g + 1) * Q]))
        h = [h_s[k] for k in range(4)]
        ys = []
        for (d, g, c), (cme0, cme1, da, st, y) in zip(work, loaded):
            h_in = h[d * 2 + g]
            hb = h_in.astype(BF16)
            zero = jnp.zeros_like(hb)
            ys.append(y + _dot(cme0, jnp.where(lo_half, hb, zero)) + _dot(cme1, jnp.where(lo_half, zero, hb)))
            h[d * 2 + g] = da * h_in + st
        for (d, g, c), y in zip(work, ys):
            y_s[chunk_rows(c), g * Q:(g + 1) * Q] = y
        for k in range(4):
            h_s[k] = h[k]

    for i in range(N_CTX_CHUNK):
        scan_steps([(i, N_CTX_CHUNK - 1 - i)])

    def scan_body(t, carry):
        i0 = N_CTX_CHUNK + t * SCAN_STEPS
        scan_steps([(i0 + u, N_CHUNK - 1 + N_CTX_CHUNK - (i0 + u)) for u in range(SCAN_STEPS)])
        return carry

    lax.fori_loop(0, (N_CHUNK - N_CTX_CHUNK) // SCAN_STEPS, scan_body, 0)

    dskip = dskip_ref[...]
    onorm = onorm_ref[...]

    def finish(rows, z):
        y = y_s[rows, :] + dskip * xs_s[rows, :]
        return _rms(y * (z * _sigmoid(z)), onorm).astype(BF16)

    for c in range(N_CTX_CHUNK):
        yc_ref[c * Q:(c + 1) * Q, :] = finish(slice(c * Q, (c + 1) * Q), zc_ref[c * Q:(c + 1) * Q, :])

    def fin_body(c, carry):
        r0 = pl.multiple_of(c * Q, Q)
        yx_ref[pl.ds(r0, Q), :] = finish(pl.ds(CTX_LEN + r0, Q), zx_ref[pl.ds(r0, Q), :])
        return carry

    lax.fori_loop(0, SEQ // Q, fin_body, 0, unroll=2)


def _ssd_call(z, xbc, dt, cw, cb, dtb, alog, dskip, onorm):
    def cspec(wd):
        return pl.BlockSpec((None, CTX_LEN, wd), lambda b: (0, b, 0))

    def xspec(wd):
        return pl.BlockSpec((None, SEQ, wd), lambda b: (b + 1, 0, 0))

    return pl.pallas_call(
        _ssd_kernel,
        out_shape=[jax.ShapeDtypeStruct((BATCH, CTX_LEN, MIX), BF16),
                   jax.ShapeDtypeStruct((BATCH, SEQ, MIX), BF16)],
        grid=(BATCH,),
        in_specs=[cspec(MIX), xspec(MIX), cspec(D_CONV_CH), xspec(D_CONV_CH), cspec(LANE), xspec(LANE),
                  _full((8, D_CONV_CH)), _full((1, D_CONV_CH)), _full((1, LANE)), _full((1, LANE)),
                  _full((1, MIX)), _full((1, MIX))],
        out_specs=[pl.BlockSpec((None, CTX_LEN, MIX), lambda b: (b, 0, 0)),
                   pl.BlockSpec((None, SEQ, MIX), lambda b: (b, 0, 0))],
        scratch_shapes=[pltpu.VMEM((D_CONV_CH // LANE, U_ROWS, LANE), F32),
                        pltpu.VMEM((TOK, MIX), F32), pltpu.VMEM((TOK, MIX), BF16), pltpu.VMEM((TOK, MIX), BF16),
                        pltpu.VMEM((TOK, LANE), F32), pltpu.VMEM((TOK, MIX), F32),
                        pltpu.VMEM((N_CHUNK * 4, Q, Q), F32), pltpu.VMEM((N_CHUNK * 8, Q, Q), BF16),
                        pltpu.VMEM((N_CHUNK * 4, 8, LANE), F32), pltpu.VMEM((4, Q, Q), F32)],
        compiler_params=_params(("arbitrary",)),
        name="ssd",
    )(z, z, xbc, xbc, dt, dt, cw, cb, dtb, alog, dskip, onorm)


def _rope_tables():
    rows = SEQ // GRID_W
    r = jnp.repeat(jnp.arange(rows, dtype=F32), GRID_W)
    c = jnp.tile(jnp.arange(GRID_W, dtype=F32), rows)

    def tables(rot_dim):
        axis_dim = rot_dim // 2
        inv = ROPE_THETA ** (-jnp.arange(0, axis_dim, 2, dtype=F32) / axis_dim)
        ar = r[:, None] * inv[None, :]
        ac = c[:, None] * inv[None, :]
        cos = jnp.concatenate([jnp.cos(ar), jnp.cos(ar), jnp.cos(ac), jnp.cos(ac)], axis=-1)
        sin = jnp.concatenate([-jnp.sin(ar), jnp.sin(ar), -jnp.sin(ac), jnp.sin(ac)], axis=-1)
        return cos, sin

    c64, s64 = tables(HEAD_DIM)
    cab = jnp.tile(c64, (1, 2))
    sab = jnp.tile(s64, (1, 2))
    c32, s32 = tables(C_ROPE)
    cm = jnp.concatenate([jnp.ones((SEQ, C_NOPE), F32), c32, jnp.ones((SEQ, 32), F32)], axis=-1)
    sm = jnp.concatenate([jnp.zeros((SEQ, C_NOPE), F32), s32, jnp.zeros((SEQ, 32), F32)], axis=-1)

    def with_identity(tab, fill):
        return jnp.stack([jnp.full_like(tab, fill), tab])

    return (with_identity(cab, 1.0), with_identity(sab, 0.0), with_identity(cm, 1.0), with_identity(sm, 0.0))


def _head_mean_matrix():
    lane = jnp.arange(LANE)
    same = (lane[:, None] // HEAD_DIM) == (lane[None, :] // HEAD_DIM)
    return jnp.where(same, 1.0 / HEAD_DIM, 0.0).astype(BF16)


def _gqa_order(w, axis):
    shp = w.shape
    w = w.reshape(shp[:axis] + (2, 2, HEAD_DIM) + shp[axis + 1:])
    w = jnp.swapaxes(w, axis, axis + 1)
    return w.reshape(shp)


def _stacked_weights(w_in, c_w_uq, c_w_ukv, w_out):
    n_l, d = w_in.shape[:2]
    zc = lambda n: jnp.zeros((n_l, d, n), BF16)
    wb = w_in.astype(BF16)
    o_c = IN_AB
    o_d = IN_AB + IN_C
    w = jnp.concatenate([
        _gqa_order(wb[..., 0:MIX], 2), wb[..., MIX:512],
        _gqa_order(wb[..., 512:512 + MIX], 2), wb[..., 512 + MIX:o_c + C_Q_LORA + C_KV_LORA],
        zc(C_NOPE), wb[..., o_c + C_Q_LORA + C_KV_LORA:o_d], zc(LANE - C_NOPE - C_ROPE),
        wb[..., o_d:], zc(LANE - 2 * D_HEADS),
    ], axis=2)
    assert w.shape[2] == PROJ_W
    dq = C_NOPE + C_ROPE
    pad_last = lambda t, n: jnp.pad(t, ((0, 0),) * (t.ndim - 1) + ((0, n),))
    wuq = pad_last(c_w_uq.reshape(n_l, C_Q_LORA, C_HEADS, dq), LANE - dq)
    wkv = c_w_ukv.reshape(n_l, C_KV_LORA, C_HEADS, C_NOPE + C_V)
    wk = pad_last(wkv[..., :C_NOPE], LANE - C_NOPE)
    wv = wkv[..., C_NOPE:]
    wout = jnp.concatenate([_gqa_order(w_out[:, 0:MIX], 1), _gqa_order(w_out[:, MIX:2 * MIX], 1),
                            w_out[:, 2 * MIX:]], axis=1)
    return (w, wuq.reshape(n_l, C_Q_LORA, 512).astype(BF16), wk.reshape(n_l, C_KV_LORA, 512).astype(BF16),
            wv.reshape(n_l, C_KV_LORA, MIX).astype(BF16), wout.astype(BF16))


def _lane_row(v, width=LANE):
    v = v.reshape(1, -1).astype(F32)
    return jnp.pad(v, ((0, 0), (0, width - v.shape[1])))


def kernel(x, c, ctx, c_ctx, ada_w, ada_b, ffn1_norm, ffn1_wi, ffn1_wo, mix_norm, w_in, w_out, a_sink, b_q_norm, b_k_norm, c_q_norm, c_w_uq, c_kv_norm, c_w_ukv, d_conv_w, d_conv_b, d_a_log, d_dt_bias, d_skip, d_out_norm, ffn2_norm, ffn2_wi, ffn2_wo, final_norm):
    cvec = jnp.concatenate([c_ctx[None, :], c, jnp.zeros((16 - NSEG, D_MODEL), F32)], axis=0)
    mods = _mods_call(cvec, ada_w, ada_b).reshape(DEPTH, 16, N_MOD, D_MODEL)[:, :NSEG]
    tabs = _rope_tables()
    bd = _head_mean_matrix()
    wi1, wo1 = ffn1_wi.astype(BF16), ffn1_wo.astype(BF16)
    wi2, wo2 = ffn2_wi.astype(BF16), ffn2_wo.astype(BF16)
    w, wuq, wk, wv, wout = _stacked_weights(w_in, c_w_uq, c_w_ukv, w_out)

    h = (ctx.reshape(1, BATCH * CTX_LEN, D_MODEL), x)
    out = None
    for l in range(DEPTH):
        with_ctx = l < DEPTH - 1
        mods_l = mods[l]
        h = _ffn_call(h, mods_l, ffn1_norm[l], wi1, wo1, l, k0=0, seg_off=0, name=f"ffn1_{l}")
        (qa, ka, va, qb, kb, vb, qc, kc, vc, z, xbc, dt) = _inproj_call(
            h, mods_l, mix_norm[l], w, l, tabs, bd,
            jnp.tile(b_q_norm[l], 2).reshape(1, LANE), jnp.tile(b_k_norm[l], 2).reshape(1, LANE),
            c_q_norm[l].reshape(1, C_Q_LORA), wuq, c_kv_norm[l].reshape(1, C_KV_LORA), wk, wv)

        oa = _attn_win_call(a_sink[l], qa, ka, va, with_ctx=with_ctx)
        ob = _attn_dense_call(qb, kb, vb, heads=HEADS_GQA, with_ctx=with_ctx, name=f"attn_b_{l}")
        oc = _attn_dense_call(qc, kc, vc, heads=HEADS_MLA, with_ctx=with_ctx, name=f"attn_c_{l}")
        cw = jnp.pad(d_conv_w[l], ((0, 8 - D_CONV), (0, 0)))
        yc, yx = _ssd_call(z, xbc, dt, cw, d_conv_b[l].reshape(1, D_CONV_CH),
                           _lane_row(d_dt_bias[l]), _lane_row(d_a_log[l]),
                           jnp.repeat(d_skip[l], HEAD_DIM).reshape(1, MIX), d_out_norm[l].reshape(1, MIX))
        if with_ctx:
            od = jnp.concatenate([yc.reshape(1, SEG, MIX), yx], axis=0)
            h = _ffn_call(h, mods_l, ffn2_norm[l], wi2, wo2, l, k0=6, seg_off=0,
                          pre=(oa, ob, oc, od, wout), o_off=0, name=f"ffn2_{l}")
        else:
            out = _ffn_call(h, mods_l, ffn2_norm[l], wi2, wo2, l, k0=6, seg_off=1,
                            pre=(oa, ob, oc, yx, wout), o_off=0, final_g=final_norm, name=f"ffn2_{l}")
    return out
```

```python
import functools

import jax
import jax.numpy as jnp
from jax import lax
from jax.experimental import pallas as pl
from jax.experimental.pallas import tpu as pltpu

D_MODEL = 1024
BATCH = 8
SEQ = 2048
DEPTH = 2
GRID_W = 64
CTX_LEN = 256
HEAD_DIM = 64
ROPE_THETA = 10000.0
EPS = 1e-6
FFN_DIM = 2816
N_MOD = 9
WINDOW = 128
C_HEADS = 4
C_Q_LORA = 256
C_KV_LORA = 128
C_NOPE = 64
C_ROPE = 32
C_V = 64
D_HEADS = 4
D_STATE = 128
D_CONV = 5
SSD_CHUNK = 128
MIX = 256
IN_AB = 1024
IN_C = C_Q_LORA + C_KV_LORA + C_ROPE
D_CONV_CH = MIX + 2 * 2 * D_STATE
IN_D = MIX + D_CONV_CH + 2 * D_HEADS

NSEG = BATCH + 1
SEG = SEQ
assert BATCH * CTX_LEN == SEG
TOK = CTX_LEN + SEQ
N_CHUNK = TOK // SSD_CHUNK

LANE = 128
VMEM_LIMIT = 56 * 1024 * 1024

F32 = jnp.float32
BF16 = jnp.bfloat16
LOG2E = 1.4426950408889634


def _dot(a, b):
    return jnp.dot(a, b, preferred_element_type=F32)


def _dot_nt(a, b):
    return lax.dot_general(a, b, (((1,), (1,)), ((), ())), preferred_element_type=F32)


def _dot_tn(a, b):
    return lax.dot_general(a, b, (((0,), (0,)), ((), ())), preferred_element_type=F32)


def _sigmoid(x):
    return 1.0 / (1.0 + jnp.exp(-x))


def _rms(x, g):
    return x * lax.rsqrt(jnp.mean(x * x, axis=-1, keepdims=True) + EPS) * g


def _full(shape):
    nd = len(shape)
    return pl.BlockSpec(shape, lambda *_: (0,) * nd)


def _layer_slab(shape, layer):
    nd = len(shape)
    return pl.BlockSpec((None,) + tuple(shape), lambda *_: (layer,) + (0,) * nd, pipeline_mode=pl.Buffered(1))


def _params(sem):
    return pltpu.CompilerParams(dimension_semantics=sem, vmem_limit_bytes=VMEM_LIMIT)


def _mods_kernel(c_ref, w_ref, b_ref, o_ref):
    c = c_ref[...]
    s = (c * _sigmoid(c)).astype(BF16)
    o_ref[...] = _dot(s, w_ref[...].astype(BF16)) + b_ref[...]


def _mods_call(cvec, ada_w, ada_b):
    n_l = ada_w.shape[0]
    return pl.pallas_call(
        _mods_kernel,
        out_shape=jax.ShapeDtypeStruct((n_l, 16, N_MOD * D_MODEL), F32),
        grid=(n_l, N_MOD),
        in_specs=[
            pl.BlockSpec((16, D_MODEL), lambda l, n: (0, 0)),
            pl.BlockSpec((None, D_MODEL, D_MODEL), lambda l, n: (l, 0, n)),
            pl.BlockSpec((None, 1, D_MODEL), lambda l, n: (l, 0, n)),
        ],
        out_specs=pl.BlockSpec((None, 16, D_MODEL), lambda l, n: (l, 0, n)),
        compiler_params=_params(("arbitrary", "arbitrary")),
        name="mods",
    )(cvec, ada_w, ada_b.reshape(n_l, 1, N_MOD * D_MODEL))


FFN_TM = 512
FFN_FC = 256


def _ffn_kernel(*refs, k0, split_in, has_pre, has_final):
    if split_in:
        hc_ref, hx_ref = refs[:2]
        refs = refs[1:]
    h_ref, mods_ref, g_ref, wi_ref, wo_ref = refs[:5]
    pos = 5
    if has_pre:
        oa_ref, ob_ref, oc_ref, od_ref, wout_ref = refs[pos:pos + 5]
        pos += 5
    if has_final:
        gf_ref = refs[pos]
        pos += 1
    out_ref, hm_ref = refs[pos], refs[pos + 1]

    if split_in:
        x = jnp.where(pl.program_id(0) == 0, hc_ref[...], hx_ref[...])
    else:
        x = h_ref[...]
    m = mods_ref[...]
    if has_pre:
        o = _dot(oa_ref[...], wout_ref[0:MIX, :])
        o = o + _dot(ob_ref[...], wout_ref[MIX:2 * MIX, :])
        o = o + _dot(oc_ref[...], wout_ref[2 * MIX:3 * MIX, :])
        o = o + _dot(od_ref[...], wout_ref[3 * MIX:4 * MIX, :])
        x = x + m[5:6] * o
    n = _rms(x, g_ref[...]) * (1.0 + m[k0 + 1:k0 + 2]) + m[k0:k0 + 1]
    nb = n.astype(BF16)
    for c in range(FFN_DIM // FFN_FC):
        lo = c * FFN_FC
        a = _dot(nb, wi_ref[:, lo:lo + FFN_FC])
        b = _dot(nb, wi_ref[:, FFN_DIM + lo:FFN_DIM + lo + FFN_FC])
        hm_ref[:, lo:lo + FFN_FC] = (a * _sigmoid(a) * b).astype(BF16)
    y = _dot(hm_ref[...], wo_ref[...])
    out = x + 0.5 * m[k0 + 2:k0 + 3] * y
    if has_final:
        out = _rms(out, gf_ref[...])
    out_ref[...] = out


def _ffn_call(h, mods_l, g, wi, wo, layer, *, k0, seg_off, pre=None, o_off=0, final_g=None, name):
    split_in = isinstance(h, tuple)
    tm = FFN_TM
    if split_in:
        assert seg_off == 0
        nseg = NSEG
        h_args = list(h)
        h_specs = [
            pl.BlockSpec((None, tm, D_MODEL), lambda s, i: (0, jnp.where(s == 0, i, 0), 0)),
            pl.BlockSpec((None, tm, D_MODEL), lambda s, i: (jnp.maximum(s - 1, 0), jnp.where(s == 0, 0, i), 0)),
        ]
    else:
        nseg = h.shape[0] - seg_off
        h_args = [h]
        h_specs = [pl.BlockSpec((None, tm, D_MODEL), lambda s, i: (s + seg_off, i, 0))]
    in_specs = h_specs + [
        pl.BlockSpec((None, N_MOD, D_MODEL), lambda s, i: (s + seg_off, 0, 0)),
        _full((1, D_MODEL)),
        _layer_slab((D_MODEL, 2 * FFN_DIM), layer),
        _layer_slab((FFN_DIM, D_MODEL), layer),
    ]
    args = h_args + [mods_l, g.reshape(1, D_MODEL), wi, wo]
    if pre is not None:
        oa, ob, oc, od, wout = pre
        ospec = pl.BlockSpec((None, tm, MIX), lambda s, i: (s + o_off, i, 0))
        in_specs += [ospec, ospec, ospec, ospec, _layer_slab((4 * MIX, D_MODEL), layer)]
        args += [oa, ob, oc, od, wout]
    if final_g is not None:
        in_specs.append(_full((1, D_MODEL)))
        args.append(final_g.reshape(1, D_MODEL))
    kern = functools.partial(_ffn_kernel, k0=k0, split_in=split_in, has_pre=pre is not None,
                             has_final=final_g is not None)
    return pl.pallas_call(
        kern,
        out_shape=jax.ShapeDtypeStruct((nseg, SEG, D_MODEL), F32),
        grid=(nseg, SEG // tm),
        in_specs=in_specs,
        out_specs=pl.BlockSpec((None, tm, D_MODEL), lambda s, i: (s, i, 0)),
        scratch_shapes=[pltpu.VMEM((tm, FFN_DIM), BF16)],
        compiler_params=_params(("arbitrary", "arbitrary")),
        name=name,
    )(*args)


PROJ_TM = 512
COL_AB = 0
COL_C = IN_AB
COL_D = COL_C + 512
PROJ_W = COL_D + MIX + D_CONV_CH + LANE


def _swap_halves(x, half):
    lane = lax.broadcasted_iota(jnp.int32, x.shape, 1)
    first = (lane & half) == 0
    up = pltpu.roll(x, LANE - half, axis=1)
    dn = pltpu.roll(x, half, axis=1)
    return jnp.where(first, up, dn)


def _rope(x, cos, sin, half):
    return x * cos + _swap_halves(x, half) * sin


def _head_rms(x, bd, g):
    sq = x * x
    hi = sq.astype(BF16)
    lo = (sq - hi.astype(F32)).astype(BF16)
    ms = _dot(hi, bd) + _dot(lo, bd)
    return x * lax.rsqrt(ms + EPS) * g


def _inproj_kernel(h_ref, mods_ref, g_ref, w_ref, cab_ref, sab_ref, cm_ref, sm_ref, bd_ref,
                   bqn_ref, bkn_ref, cqn_ref, wuq_ref, ckvn_ref, wukvk_ref, wukvv_ref,
                   qa_ref, ka_ref, va_ref, qb_ref, kb_ref, vb_ref, qc_ref, kc_ref, vc_ref,
                   z_ref, xbc_ref, dt_ref):
    x = h_ref[...]
    m = mods_ref[...]
    n = _rms(x, g_ref[...]) * (1.0 + m[4:5]) + m[3:4]
    nb = n.astype(BF16)
    cab, sab = cab_ref[...], sab_ref[...]
    cm, sm = cm_ref[...], sm_ref[...]
    bd = bd_ref[...]
    scale_ab = HEAD_DIM ** -0.5 * LOG2E
    scale_c = (C_NOPE + C_ROPE) ** -0.5 * LOG2E

    p_ab = [_dot(nb, w_ref[:, COL_AB + mixer * 512:COL_AB + (mixer + 1) * 512]) for mixer in range(2)]
    p_c = _dot(nb, w_ref[:, COL_C:COL_C + 512])
    z_ref[...] = _dot(nb, w_ref[:, COL_D:COL_D + MIX])
    for c in range(D_CONV_CH // 256):
        lo = COL_D + MIX + c * 256
        xbc_ref[:, c * 256:(c + 1) * 256] = _dot(nb, w_ref[:, lo:lo + 256])
    lo = COL_D + MIX + D_CONV_CH
    dt_ref[...] = _dot(nb, w_ref[:, lo:lo + LANE])

    for mixer, (q_ref, k_ref, v_ref) in enumerate(((qa_ref, ka_ref, va_ref), (qb_ref, kb_ref, vb_ref))):
        p = p_ab[mixer]
        for s in range(2):
            q = p[:, s * LANE:(s + 1) * LANE]
            if mixer == 1:
                q = _head_rms(q, bd, bqn_ref[...])
            q = _rope(q, cab, sab, 16) * scale_ab
            q_ref[:, s * LANE:(s + 1) * LANE] = q.astype(BF16)
        k = p[:, 2 * LANE:3 * LANE]
        if mixer == 1:
            k = _head_rms(k, bd, bkn_ref[...])
        k_ref[...] = _rope(k, cab, sab, 16).astype(BF16)
        v_ref[...] = p[:, 3 * LANE:4 * LANE].astype(BF16)

    p = p_c
    cq = _rms(p[:, 0:C_Q_LORA], cqn_ref[...]).astype(BF16)
    q = _dot(cq, wuq_ref[...])
    ckv = _rms(p[:, C_Q_LORA:C_Q_LORA + C_KV_LORA], ckvn_ref[...]).astype(BF16)
    kn = _dot(ckv, wukvk_ref[...])
    vc_ref[...] = _dot(ckv, wukvv_ref[...]).astype(BF16)
    kr = _rope(p[:, 3 * LANE:4 * LANE], cm, sm, 8)
    for hh in range(C_HEADS):
        sl = slice(hh * LANE, (hh + 1) * LANE)
        qc_ref[:, sl] = (_rope(q[:, sl], cm, sm, 8) * scale_c).astype(BF16)
        kc_ref[:, sl] = (kn[:, sl] + kr).astype(BF16)


PROJ_OUT = ([(MIX, BF16), (LANE, BF16), (LANE, BF16)] * 2 + [(512, BF16), (512, BF16), (MIX, BF16)]
            + [(MIX, F32), (D_CONV_CH, F32), (LANE, F32)])


def _inproj_call(h, mods_l, g, w, layer, tabs, bd, bqn, bkn, cqn, wuq, ckvn, wukvk, wukvv):
    tm = PROJ_TM
    cab, sab, cm, sm = tabs
    tok = lambda s, i: (s, i, 0)
    tab = pl.BlockSpec((None, tm, LANE), lambda s, i: (jnp.minimum(s, 1), i, 0))
    return pl.pallas_call(
        _inproj_kernel,
        out_shape=[jax.ShapeDtypeStruct((NSEG, SEG, wd), dt) for wd, dt in PROJ_OUT],
        grid=(NSEG, SEG // tm),
        in_specs=[
            pl.BlockSpec((None, tm, D_MODEL), tok),
            pl.BlockSpec((None, N_MOD, D_MODEL), lambda s, i: (s, 0, 0)),
            _full((1, D_MODEL)),
            _layer_slab((D_MODEL, PROJ_W), layer),
            tab, tab, tab, tab,
            _full((LANE, LANE)),
            _full((1, LANE)), _full((1, LANE)),
            _full((1, C_Q_LORA)), _layer_slab((C_Q_LORA, 512), layer),
            _full((1, C_KV_LORA)), _layer_slab((C_KV_LORA, 512), layer), _layer_slab((C_KV_LORA, MIX), layer),
        ],
        out_specs=[pl.BlockSpec((None, tm, wd), tok) for wd, _ in PROJ_OUT],
        compiler_params=_params(("arbitrary", "arbitrary")),
        name="inproj",
    )(h, mods_l, g.reshape(1, D_MODEL), w, cab, sab, cm, sm, bd,
      bqn, bkn, cqn, wuq, ckvn, wukvk, wukvv)


HEADS_GQA = tuple((g, kv, 0, 0, g, kv) for g in range(2) for kv in range(2))
HEADS_MLA = tuple((h, None, h, h // 2, h // 2, h % 2) for h in range(C_HEADS))


def _masked_q(q, half):
    if half is None:
        return q
    lane = lax.broadcasted_iota(jnp.int32, q.shape, 1)
    keep = (lane < HEAD_DIM) if half == 0 else (lane >= HEAD_DIM)
    return jnp.where(keep, q, jnp.zeros_like(q))


def _store_heads(o_ref, rows, outs):
    lane = lax.broadcasted_iota(jnp.int32, outs[(0, 0)].shape, 1)
    for s in range(2):
        o = jnp.where(lane < HEAD_DIM, outs[(s, 0)], outs[(s, 1)])
        o_ref[rows, s * LANE:(s + 1) * LANE] = o.astype(o_ref.dtype)


ATT_TQ = 256
ATT_LOOKAHEAD = 2


def _attn_dense_kernel(q_ref, kc_ref, kx_ref, vc_ref, vx_ref, o_ref, *, heads, ctx_first):
    def run(with_x):
        def scores(head):
            qs, qhalf, ks = head[:3]
            qh = _masked_q(q_ref[:, qs * LANE:(qs + 1) * LANE], qhalf)
            ksl = slice(ks * LANE, (ks + 1) * LANE)
            return _dot_nt(qh, kc_ref[:, ksl]), (_dot_nt(qh, kx_ref[:, ksl]) if with_x else None)

        outs = {}
        pending = [scores(hd) for hd in heads[:ATT_LOOKAHEAD]]
        for i, (_, _, _, vs, os_, ohalf) in enumerate(heads):
            s_c, s_x = pending.pop(0)
            if i + ATT_LOOKAHEAD < len(heads):
                pending.append(scores(heads[i + ATT_LOOKAHEAD]))
            vsl = slice(vs * LANE, (vs + 1) * LANE)
            mx = jnp.max(s_c, axis=-1, keepdims=True)
            if with_x:
                mx = jnp.maximum(mx, jnp.max(s_x, axis=-1, keepdims=True))
            p_c = jnp.exp2(s_c - mx)
            den = jnp.sum(p_c, axis=-1, keepdims=True)
            o = _dot(p_c.astype(BF16), vc_ref[:, vsl])
            if with_x:
                p_x = jnp.exp2(s_x - mx)
                den = den + jnp.sum(p_x, axis=-1, keepdims=True)
                o = o + _dot(p_x.astype(BF16), vx_ref[:, vsl])
            outs[(os_, ohalf)] = o * (1.0 / den)
        _store_heads(o_ref, slice(None), outs)

    if ctx_first:
        j = pl.program_id(1)

        @pl.when(j == 0)
        def _():
            run(False)

        @pl.when(j > 0)
        def _():
            run(True)
    else:
        run(True)


def _attn_specs(q, k, v, tq, with_ctx):
    nq = SEQ // tq
    ncb = CTX_LEN // tq if with_ctx else 0
    if with_ctx:
        qmap = lambda b, j: (jnp.where(j < ncb, 0, b + 1), jnp.where(j < ncb, ncb * b + j, j - ncb), 0)
        out_shape = jax.ShapeDtypeStruct((NSEG, SEG, MIX), BF16)
        omap = qmap
    else:
        qmap = lambda b, j: (b + 1, j, 0)
        out_shape = jax.ShapeDtypeStruct((BATCH, SEG, MIX), BF16)
        omap = lambda b, j: (b, j, 0)
    kw, vw = k.shape[-1], v.shape[-1]
    in_specs = [pl.BlockSpec((None, tq, q.shape[-1]), qmap),
                pl.BlockSpec((None, CTX_LEN, kw), lambda b, j: (0, b, 0)),
                pl.BlockSpec((None, SEQ, kw), lambda b, j: (b + 1, 0, 0)),
                pl.BlockSpec((None, CTX_LEN, vw), lambda b, j: (0, b, 0)),
                pl.BlockSpec((None, SEQ, vw), lambda b, j: (b + 1, 0, 0))]
    return (BATCH, nq + ncb), in_specs, pl.BlockSpec((None, tq, MIX), omap), out_shape, ncb


def _attn_dense_call(q, k, v, *, heads, with_ctx, name):
    grid, in_specs, out_spec, out_shape, _ = _attn_specs(q, k, v, ATT_TQ, with_ctx)
    kern = functools.partial(_attn_dense_kernel, heads=heads, ctx_first=with_ctx)
    return pl.pallas_call(
        kern,
        out_shape=out_shape,
        grid=grid,
        in_specs=in_specs,
        out_specs=out_spec,
        compiler_params=_params(("arbitrary", "arbitrary")),
        name=name,
    )(q, k, k, v, v)


WIN_TQ = 256


def _attn_win_kernel(sink_ref, q_ref, kc_ref, kx_ref, vc_ref, vx_ref, o_ref, *, n_ctx_blocks):
    j = pl.program_id(1)
    n_sub = WIN_TQ // WINDOW

    n_h = len(HEADS_GQA)
    head_of_row = lax.broadcasted_iota(jnp.int32, (n_h * WINDOW, 1), 0) // WINDOW
    sink = jnp.zeros((n_h * WINDOW, 1), F32)
    for i, (qs, qhalf, _, _, _, _) in enumerate(HEADS_GQA):
        sink = jnp.where(head_of_row == i, sink_ref[qhalf * 2 + qs] * LOG2E, sink)

    def run(local):
        def scores(sub):
            rows = slice(sub * WINDOW, (sub + 1) * WINDOW)
            q4 = jnp.concatenate([_masked_q(q_ref[rows, qs * LANE:(qs + 1) * LANE], qhalf)
                                  for qs, qhalf, _, _, _, _ in HEADS_GQA], axis=0)
            s_c = _dot_nt(q4, kc_ref[...])
            if not local:
                return s_c, None, None
            n = (j - n_ctx_blocks) * n_sub + sub
            start = pl.multiple_of(jnp.clip((n - 1) * WINDOW, 0, SEQ - 3 * WINDOW), WINDOW)
            shape = (n_h * WINDOW, 3 * WINDOW)
            qpos = n * WINDOW + lax.broadcasted_iota(jnp.int32, shape, 0) % WINDOW
            kpos = start + lax.broadcasted_iota(jnp.int32, shape, 1)
            valid = jnp.abs(qpos - kpos) <= WINDOW
            s_l = jnp.where(valid, _dot_nt(q4, kx_ref[pl.ds(start, 3 * WINDOW), :]), -jnp.inf)
            return s_c, s_l, start

        all_scores = [scores(sub) for sub in range(n_sub)]
        for sub, (s_c, s_l, start) in enumerate(all_scores):
            rows = slice(sub * WINDOW, (sub + 1) * WINDOW)
            mx = jnp.maximum(jnp.max(s_c, axis=-1, keepdims=True), sink)
            if local:
                mx = jnp.maximum(mx, jnp.max(s_l, axis=-1, keepdims=True))
            p_c = jnp.exp2(s_c - mx)
            den = jnp.sum(p_c, axis=-1, keepdims=True) + jnp.exp2(sink - mx)
            o = _dot(p_c.astype(BF16), vc_ref[...])
            if local:
                p_l = jnp.exp2(s_l - mx)
                den = den + jnp.sum(p_l, axis=-1, keepdims=True)
                o = o + _dot(p_l.astype(BF16), vx_ref[pl.ds(start, 3 * WINDOW), :])
            o = o * (1.0 / den)
            outs = {(os_, ohalf): o[i * WINDOW:(i + 1) * WINDOW]
                    for i, (_, _, _, _, os_, ohalf) in enumerate(HEADS_GQA)}
            _store_heads(o_ref, rows, outs)

    if n_ctx_blocks:
        @pl.when(j < n_ctx_blocks)
        def _():
            run(False)

        @pl.when(j >= n_ctx_blocks)
        def _():
            run(True)
    else:
        run(True)


def _attn_win_call(sink, q, k, v, *, with_ctx):
    grid, in_specs, out_spec, out_shape, ncb = _attn_specs(q, k, v, WIN_TQ, with_ctx)
    kern = functools.partial(_attn_win_kernel, n_ctx_blocks=ncb)
    return pl.pallas_call(
        kern,
        out_shape=out_shape,
        grid=grid,
        in_specs=[pl.BlockSpec(memory_space=pltpu.SMEM)] + in_specs,
        out_specs=out_spec,
        compiler_params=_params(("arbitrary", "arbitrary")),
        name="attn_win",
    )(sink, q, k, k, v, v)


Q = SSD_CHUNK
PAD = 8
U_CTX = PAD
U_X = PAD + CTX_LEN + PAD
U_ROWS = U_X + SEQ + PAD
N_CTX_CHUNK = CTX_LEN // Q
LOCAL_CHUNKS = 3
SCAN_STEPS = 2


def _split3(a):
    a1 = a.astype(BF16)
    r1 = a - a1.astype(F32)
    a2 = r1.astype(BF16)
    a3 = (r1 - a2.astype(F32)).astype(BF16)
    return a1, a2, a3


def _ssd_kernel(zc_ref, zx_ref, uc_ref, ux_ref, dtc_ref, dtx_ref, cw_ref, cb_ref, dtb_ref, alog_ref,
                dskip_ref, onorm_ref, yc_ref, yx_ref, upad, xs_s, bm_s, cm_s, dt_s, y_s, st_s, cme_s, da_s, h_s):
    n_slab = D_CONV_CH // LANE
    zpad = jnp.zeros((PAD, LANE), F32)
    for j in range(n_slab):
        sl = slice(j * LANE, (j + 1) * LANE)
        upad[j, 0:PAD, :] = zpad
        upad[j, U_CTX:U_CTX + CTX_LEN, :] = uc_ref[:, sl]
        upad[j, U_CTX + CTX_LEN:U_X, :] = zpad
        upad[j, U_X:U_X + SEQ, :] = ux_ref[:, sl]
        upad[j, U_X + SEQ:U_ROWS, :] = zpad

    for c in range(N_CHUNK):
        base = U_CTX + c * Q if c < N_CTX_CHUNK else U_X + (c - N_CTX_CHUNK) * Q
        for j in range(n_slab):
            sl = slice(j * LANE, (j + 1) * LANE)
            dst = (xs_s, bm_s, cm_s)[j // 2]
            acc = jnp.broadcast_to(cb_ref[:, sl], (Q, LANE))
            for k in range(D_CONV):
                lo = base + k - D_CONV // 2
                acc = acc + upad[j, lo:lo + Q, :] * cw_ref[k:k + 1, sl]
            dcol = (j % 2) * LANE
            dst[c * Q:(c + 1) * Q, dcol:dcol + LANE] = (acc * _sigmoid(acc)).astype(dst.dtype)

    def softplus(v):
        return jnp.maximum(v, 0.0) + jnp.log(1.0 + jnp.exp(-jnp.abs(v)))

    dt_s[0:CTX_LEN, :] = softplus(dtc_ref[...] + dtb_ref[...])
    dt_s[CTX_LEN:TOK, :] = softplus(dtx_ref[...] + dtb_ref[...])

    a_neg = -jnp.exp(alog_ref[...]) * LOG2E
    row = lax.broadcasted_iota(jnp.int32, (Q, Q), 0)
    col = lax.broadcasted_iota(jnp.int32, (Q, Q), 1)
    causal = (col <= row, col >= row)
    tri = (causal[0].astype(BF16), causal[1].astype(BF16))
    lo_half = col < HEAD_DIM
    last_row = (Q - 1, 0)

    def bcast_col(v, idx):
        return jnp.broadcast_to(v[:, idx:idx + 1], (Q, Q))

    def chunk_rows(c):
        return pl.ds(c * Q if isinstance(c, int) else pl.multiple_of(c * Q, Q), Q)

    def local_load(c):
        rows = chunk_rows(c)
        groups = [(cm_s[rows, g * Q:(g + 1) * Q], bm_s[rows, g * Q:(g + 1) * Q], xs_s[rows, g * Q:(g + 1) * Q])
                  for g in range(2)]
        return dt_s[rows, :], groups

    def local_sums(dt, groups):
        a1, a2, a3 = _split3(dt * a_neg)
        cs_f = _dot(tri[0], a1) + _dot(tri[0], a2) + _dot(tri[0], a3)
        cs_b = _dot(tri[1], a1) + _dot(tri[1], a2) + _dot(tri[1], a3)
        cs = jnp.where(col < D_HEADS, cs_f, cs_b)
        return cs, cs.T, dt.T, [_dot_nt(cmg, bmg) for cmg, bmg, _ in groups]

    def local_compute(groups, cs, cs_t, dt_t, cbms):
        ys, sts, das, cmes = [], {}, {}, {}
        for g, (cmg, bmg, xsg) in enumerate(groups):
            cbm = cbms[g]
            cmg32 = cmg.astype(F32)
            bm_t = bmg.astype(F32).T
            xh = (jnp.where(lo_half, xsg, 0.0).astype(BF16), jnp.where(lo_half, 0.0, xsg).astype(BF16))
            y = None
            for d in range(2):
                st = None
                da = []
                for hh in range(2):
                    idx = d * D_HEADS + g * 2 + hh
                    colb = bcast_col(cs, idx)
                    cs_row = cs_t[idx:idx + 1, :]
                    dt_row = dt_t[idx:idx + 1, :]
                    seg = colb - (cs_row - jnp.log2(dt_row))
                    dec = jnp.exp2(jnp.where(causal[d], seg, -jnp.inf))
                    yd = _dot((cbm * dec).astype(BF16), xh[hh])
                    y = yd if y is None else y + yd
                    last = cs_row[:, last_row[d]:last_row[d] + 1]
                    w_row = dt_row * jnp.exp2(last - cs_row)
                    sth = _dot((bm_t * w_row).astype(BF16), xh[hh])
                    st = sth if st is None else st + sth
                    cmes[d * 4 + g * 2 + hh] = (cmg32 * jnp.exp2(colb)).astype(BF16)
                    da.append(jnp.broadcast_to(jnp.exp2(last), (1, Q)))
                sts[d * 2 + g] = st
                das[d * 2 + g] = jnp.where(lo_half[0:1, :], da[0], da[1])
            ys.append(y)
        return ys, sts, das, cmes

    def local_store(c, res):
        ys, sts, das, cmes = res
        rows = chunk_rows(c)
        for g in range(2):
            y_s[rows, g * Q:(g + 1) * Q] = ys[g]
        for k, v in sts.items():
            st_s[c * 4 + k] = v
        for k, v in das.items():
            da_s[c * 4 + k, 0:1, :] = v
        for k, v in cmes.items():
            cme_s[c * 8 + k] = v

    def local_body(i, carry):
        chunks = [i * LOCAL_CHUNKS + u for u in range(LOCAL_CHUNKS)]
        loaded = [local_load(c) for c in chunks]
        sums = [local_sums(dt, groups) for dt, groups in loaded]
        results = [local_compute(groups, *sm) for (_, groups), sm in zip(loaded, sums)]
        for c, res in zip(chunks, results):
            local_store(c, res)
        return carry

    lax.fori_loop(0, N_CHUNK // LOCAL_CHUNKS, local_body, 0)

    h_s[...] = jnp.zeros(h_s.shape, F32)

    def scan_steps(steps):
        work = []
        for c_fwd, c_bwd in steps:
            work += [(d, g, c) for d, c in ((0, c_fwd), (1, c_bwd)) for g in range(2)]
        loaded = []
        for d, g, c in work:
            k = c * 4 + d * 2 + g
            loaded.append((cme_s[2 * k], cme_s[2 * k + 1], da_s[k, 0:1, :], st_s[k],
                           y_s[chunk_rows(c), g * Q:(g + 1) * Q]))
        h = [h_s[k] for k in range(4)]
        ys = []
        for (d, g, c), (cme0, cme1, da, st, y) in zip(work, loaded):
            h_in = h[d * 2 + g]
            hb = h_in.astype(BF16)
            zero = jnp.zeros_like(hb)
            ys.append(y + _dot(cme0, jnp.where(lo_half, hb, zero)) + _dot(cme1, jnp.where(lo_half, zero, hb)))
            h[d * 2 + g] = da * h_in + st
        for (d, g, c), y in zip(work, ys):
            y_s[chunk_rows(c), g * Q:(g + 1) * Q] = y
        for k in range(4):
            h_s[k] = h[k]

    for i in range(N_CTX_CHUNK):
        scan_steps([(i, N_CTX_CHUNK - 1 - i)])

    def scan_body(t, carry):
        i0 = N_CTX_CHUNK + t * SCAN_STEPS
        scan_steps([(i0 + u, N_CHUNK - 1 + N_CTX_CHUNK - (i0 + u)) for u in range(SCAN_STEPS)])
        return carry

    lax.fori_loop(0, (N_CHUNK - N_CTX_CHUNK) // SCAN_STEPS, scan_body, 0)

    dskip = dskip_ref[...]
    onorm = onorm_ref[...]

    def finish(rows, z):
        y = y_s[rows, :] + dskip * xs_s[rows, :]
        return _rms(y * (z * _sigmoid(z)), onorm).astype(BF16)

    for c in range(N_CTX_CHUNK):
        yc_ref[c * Q:(c + 1) * Q, :] = finish(slice(c * Q, (c + 1) * Q), zc_ref[c * Q:(c + 1) * Q, :])

    def fin_body(c, carry):
        r0 = pl.multiple_of(c * Q, Q)
        yx_ref[pl.ds(r0, Q), :] = finish(pl.ds(CTX_LEN + r0, Q), zx_ref[pl.ds(r0, Q), :])
        return carry

    lax.fori_loop(0, SEQ // Q, fin_body, 0, unroll=2)


def _ssd_call(z, xbc, dt, cw, cb, dtb, alog, dskip, onorm):
    def cspec(wd):
        return pl.BlockSpec((None, CTX_LEN, wd), lambda b: (0, b, 0))

    def xspec(wd):
        return pl.BlockSpec((None, SEQ, wd), lambda b: (b + 1, 0, 0))

    return pl.pallas_call(
        _ssd_kernel,
        out_shape=[jax.ShapeDtypeStruct((BATCH, CTX_LEN, MIX), BF16),
                   jax.ShapeDtypeStruct((BATCH, SEQ, MIX), BF16)],
        grid=(BATCH,),
        in_specs=[cspec(MIX), xspec(MIX), cspec(D_CONV_CH), xspec(D_CONV_CH), cspec(LANE), xspec(LANE),
                  _full((8, D_CONV_CH)), _full((1, D_CONV_CH)), _full((1, LANE)), _full((1, LANE)),
                  _full((1, MIX)), _full((1, MIX))],
        out_specs=[pl.BlockSpec((None, CTX_LEN, MIX), lambda b: (b, 0, 0)),
                   pl.BlockSpec((None, SEQ, MIX), lambda b: (b, 0, 0))],
        scratch_shapes=[pltpu.VMEM((D_CONV_CH // LANE, U_ROWS, LANE), F32),
                        pltpu.VMEM((TOK, MIX), F32), pltpu.VMEM((TOK, MIX), BF16), pltpu.VMEM((TOK, MIX), BF16),
                        pltpu.VMEM((TOK, LANE), F32), pltpu.VMEM((TOK, MIX), F32),
                        pltpu.VMEM((N_CHUNK * 4, Q, Q), F32), pltpu.VMEM((N_CHUNK * 8, Q, Q), BF16),
                        pltpu.VMEM((N_CHUNK * 4, 8, LANE), F32), pltpu.VMEM((4, Q, Q), F32)],
        compiler_params=_params(("arbitrary",)),
        name="ssd",
    )(z, z, xbc, xbc, dt, dt, cw, cb, dtb, alog, dskip, onorm)


def _rope_tables():
    rows = SEQ // GRID_W
    r = jnp.repeat(jnp.arange(rows, dtype=F32), GRID_W)
    c = jnp.tile(jnp.arange(GRID_W, dtype=F32), rows)

    def tables(rot_dim):
        axis_dim = rot_dim // 2
        inv = ROPE_THETA ** (-jnp.arange(0, axis_dim, 2, dtype=F32) / axis_dim)
        ar = r[:, None] * inv[None, :]
        ac = c[:, None] * inv[None, :]
        cos = jnp.concatenate([jnp.cos(ar), jnp.cos(ar), jnp.cos(ac), jnp.cos(ac)], axis=-1)
        sin = jnp.concatenate([-jnp.sin(ar), jnp.sin(ar), -jnp.sin(ac), jnp.sin(ac)], axis=-1)
        return cos, sin

    c64, s64 = tables(HEAD_DIM)
    cab = jnp.tile(c64, (1, 2))
    sab = jnp.tile(s64, (1, 2))
    c32, s32 = tables(C_ROPE)
    cm = jnp.concatenate([jnp.ones((SEQ, C_NOPE), F32), c32, jnp.ones((SEQ, 32), F32)], axis=-1)
    sm = jnp.concatenate([jnp.zeros((SEQ, C_NOPE), F32), s32, jnp.zeros((SEQ, 32), F32)], axis=-1)

    def with_identity(tab, fill):
        return jnp.stack([jnp.full_like(tab, fill), tab])

    return (with_identity(cab, 1.0), with_identity(sab, 0.0), with_identity(cm, 1.0), with_identity(sm, 0.0))


def _head_mean_matrix():
    lane = jnp.arange(LANE)
    same = (lane[:, None] // HEAD_DIM) == (lane[None, :] // HEAD_DIM)
    return jnp.where(same, 1.0 / HEAD_DIM, 0.0).astype(BF16)


def _gqa_order(w, axis):
    shp = w.shape
    w = w.reshape(shp[:axis] + (2, 2, HEAD_DIM) + shp[axis + 1:])
    w = jnp.swapaxes(w, axis, axis + 1)
    return w.reshape(shp)


def _stacked_weights(w_in, c_w_uq, c_w_ukv, w_out):
    n_l, d = w_in.shape[:2]
    zc = lambda n: jnp.zeros((n_l, d, n), BF16)
    wb = w_in.astype(BF16)
    o_c = IN_AB
    o_d = IN_AB + IN_C
    w = jnp.concatenate([
        _gqa_order(wb[..., 0:MIX], 2), wb[..., MIX:512],
        _gqa_order(wb[..., 512:512 + MIX], 2), wb[..., 512 + MIX:o_c + C_Q_LORA + C_KV_LORA],
        zc(C_NOPE), wb[..., o_c + C_Q_LORA + C_KV_LORA:o_d], zc(LANE - C_NOPE - C_ROPE),
        wb[..., o_d:], zc(LANE - 2 * D_HEADS),
    ], axis=2)
    assert w.shape[2] == PROJ_W
    dq = C_NOPE + C_ROPE
    pad_last = lambda t, n: jnp.pad(t, ((0, 0),) * (t.ndim - 1) + ((0, n),))
    wuq = pad_last(c_w_uq.reshape(n_l, C_Q_LORA, C_HEADS, dq), LANE - dq)
    wkv = c_w_ukv.reshape(n_l, C_KV_LORA, C_HEADS, C_NOPE + C_V)
    wk = pad_last(wkv[..., :C_NOPE], LANE - C_NOPE)
    wv = wkv[..., C_NOPE:]
    wout = jnp.concatenate([_gqa_order(w_out[:, 0:MIX], 1), _gqa_order(w_out[:, MIX:2 * MIX], 1),
                            w_out[:, 2 * MIX:]], axis=1)
    return (w, wuq.reshape(n_l, C_Q_LORA, 512).astype(BF16), wk.reshape(n_l, C_KV_LORA, 512).astype(BF16),
            wv.reshape(n_l, C_KV_LORA, MIX).astype(BF16), wout.astype(BF16))


def _lane_row(v, width=LANE):
    v = v.reshape(1, -1).astype(F32)
    return jnp.pad(v, ((0, 0), (0, width - v.shape[1])))


def kernel(x, c, ctx, c_ctx, ada_w, ada_b, ffn1_norm, ffn1_wi, ffn1_wo, mix_norm, w_in, w_out, a_sink, b_q_norm, b_k_norm, c_q_norm, c_w_uq, c_kv_norm, c_w_ukv, d_conv_w, d_conv_b, d_a_log, d_dt_bias, d_skip, d_out_norm, ffn2_norm, ffn2_wi, ffn2_wo, final_norm):
    cvec = jnp.concatenate([c_ctx[None, :], c, jnp.zeros((16 - NSEG, D_MODEL), F32)], axis=0)
    mods = _mods_call(cvec, ada_w, ada_b).reshape(DEPTH, 16, N_MOD, D_MODEL)[:, :NSEG]
    tabs = _rope_tables()
    bd = _head_mean_matrix()
    wi1, wo1 = ffn1_wi.astype(BF16), ffn1_wo.astype(BF16)
    wi2, wo2 = ffn2_wi.astype(BF16), ffn2_wo.astype(BF16)
    w, wuq, wk, wv, wout = _stacked_weights(w_in, c_w_uq, c_w_ukv, w_out)

    h = (ctx.reshape(1, BATCH * CTX_LEN, D_MODEL), x)
    out = None
    for l in range(DEPTH):
        with_ctx = l < DEPTH - 1
        mods_l = mods[l]
        h = _ffn_call(h, mods_l, ffn1_norm[l], wi1, wo1, l, k0=0, seg_off=0, name=f"ffn1_{l}")
        (qa, ka, va, qb, kb, vb, qc, kc, vc, z, xbc, dt) = _inproj_call(
            h, mods_l, mix_norm[l], w, l, tabs, bd,
            jnp.tile(b_q_norm[l], 2).reshape(1, LANE), jnp.tile(b_k_norm[l], 2).reshape(1, LANE),
            c_q_norm[l].reshape(1, C_Q_LORA), wuq, c_kv_norm[l].reshape(1, C_KV_LORA), wk, wv)

        oa = _attn_win_call(a_sink[l], qa, ka, va, with_ctx=with_ctx)
        ob = _attn_dense_call(qb, kb, vb, heads=HEADS_GQA, with_ctx=with_ctx, name=f"attn_b_{l}")
        oc = _attn_dense_call(qc, kc, vc, heads=HEADS_MLA, with_ctx=with_ctx, name=f"attn_c_{l}")
        cw = jnp.pad(d_conv_w[l], ((0, 8 - D_CONV), (0, 0)))
        yc, yx = _ssd_call(z, xbc, dt, cw, d_conv_b[l].reshape(1, D_CONV_CH),
                           _lane_row(d_dt_bias[l]), _lane_row(d_a_log[l]),
                           jnp.repeat(d_skip[l], HEAD_DIM).reshape(1, MIX), d_out_norm[l].reshape(1, MIX))
        if with_ctx:
            od = jnp.concatenate([yc.reshape(1, SEG, MIX), yx], axis=0)
            h = _ffn_call(h, mods_l, ffn2_norm[l], wi2, wo2, l, k0=6, seg_off=0,
                          pre=(oa, ob, oc, od, wout), o_off=0, name=f"ffn2_{l}")
        else:
            out = _ffn_call(h, mods_l, ffn2_norm[l], wi2, wo2, l, k0=6, seg_off=1,
                            pre=(oa, ob, oc, yx, wout), o_off=0, final_g=final_norm, name=f"ffn2_{l}")
    return out
```

```python
import functools

import numpy as np
import jax
import jax.numpy as jnp
from jax import lax
from jax.experimental import pallas as pl
from jax.experimental.pallas import tpu as pltpu

D_MODEL = 1024
BATCH = 8
SEQ = 2048
DEPTH = 2
GRID_W = 64
CTX_LEN = 256
HEAD_DIM = 64
ROPE_THETA = 10000.0
EPS = 1e-6
FFN_DIM = 2816
N_MOD = 9
WINDOW = 128
C_HEADS = 4
C_Q_LORA = 256
C_KV_LORA = 128
C_NOPE = 64
C_ROPE = 32
C_V = 64
D_HEADS = 4
D_STATE = 128
D_CONV = 5
SSD_CHUNK = 128
MIX = 256
IN_AB = 1024
IN_C = C_Q_LORA + C_KV_LORA + C_ROPE
D_CONV_CH = MIX + 2 * 2 * D_STATE
IN_D = MIX + D_CONV_CH + 2 * D_HEADS

NSEG = BATCH + 1
SEG = SEQ
assert BATCH * CTX_LEN == SEG
TOK = CTX_LEN + SEQ
N_CHUNK = TOK // SSD_CHUNK

LANE = 128
VMEM_LIMIT = 56 * 1024 * 1024

F32 = jnp.float32
BF16 = jnp.bfloat16
LOG2E = 1.4426950408889634


def _dot(a, b):
    return jnp.dot(a, b, preferred_element_type=F32)


def _dot_nt(a, b):
    return lax.dot_general(a, b, (((1,), (1,)), ((), ())), preferred_element_type=F32)


def _dot_tn(a, b):
    return lax.dot_general(a, b, (((0,), (0,)), ((), ())), preferred_element_type=F32)


def _sigmoid(x):
    return 1.0 / (1.0 + jnp.exp(-x))


def _rms(x, g):
    return x * lax.rsqrt(jnp.mean(x * x, axis=-1, keepdims=True) + EPS) * g


def _full(shape):
    nd = len(shape)
    return pl.BlockSpec(shape, lambda *_: (0,) * nd)


def _layer_slab(shape, layer):
    nd = len(shape)
    return pl.BlockSpec((None,) + tuple(shape), lambda *_: (layer,) + (0,) * nd, pipeline_mode=pl.Buffered(1))


def _params(sem):
    return pltpu.CompilerParams(dimension_semantics=sem, vmem_limit_bytes=VMEM_LIMIT)


def _mods_kernel(c_ref, w_ref, b_ref, o_ref):
    c = c_ref[...]
    s = (c * _sigmoid(c)).astype(BF16)
    o_ref[...] = _dot(s, w_ref[...].astype(BF16)) + b_ref[...]


def _mods_call(cvec, ada_w, ada_b):
    n_l = ada_w.shape[0]
    return pl.pallas_call(
        _mods_kernel,
        out_shape=jax.ShapeDtypeStruct((n_l, 16, N_MOD * D_MODEL), F32),
        grid=(n_l, N_MOD),
        in_specs=[
            pl.BlockSpec((16, D_MODEL), lambda l, n: (0, 0)),
            pl.BlockSpec((None, D_MODEL, D_MODEL), lambda l, n: (l, 0, n)),
            pl.BlockSpec((None, 1, D_MODEL), lambda l, n: (l, 0, n)),
        ],
        out_specs=pl.BlockSpec((None, 16, D_MODEL), lambda l, n: (l, 0, n)),
        compiler_params=_params(("arbitrary", "arbitrary")),
        name="mods",
    )(cvec, ada_w, ada_b.reshape(n_l, 1, N_MOD * D_MODEL))


FFN_TM = 512
FFN_FC = 256


def _ffn_kernel(*refs, k0, split_in, has_pre, has_final):
    if split_in:
        hc_ref, hx_ref = refs[:2]
        refs = refs[1:]
    h_ref, mods_ref, g_ref, wi_ref, wo_ref = refs[:5]
    pos = 5
    if has_pre:
        oa_ref, ob_ref, oc_ref, od_ref, wout_ref = refs[pos:pos + 5]
        pos += 5
    if has_final:
        gf_ref = refs[pos]
        pos += 1
    out_ref, hm_ref = refs[pos], refs[pos + 1]

    if split_in:
        x = jnp.where(pl.program_id(0) == 0, hc_ref[...], hx_ref[...])
    else:
        x = h_ref[...]
    m = mods_ref[...]
    if has_pre:
        o = _dot(oa_ref[...], wout_ref[0:MIX, :])
        o = o + _dot(ob_ref[...], wout_ref[MIX:2 * MIX, :])
        o = o + _dot(oc_ref[...], wout_ref[2 * MIX:3 * MIX, :])
        o = o + _dot(od_ref[...], wout_ref[3 * MIX:4 * MIX, :])
        x = x + m[5:6] * o
    n = _rms(x, g_ref[...]) * (1.0 + m[k0 + 1:k0 + 2]) + m[k0:k0 + 1]
    nb = n.astype(BF16)
    for c in range(FFN_DIM // FFN_FC):
        lo = c * FFN_FC
        a = _dot(nb, wi_ref[:, lo:lo + FFN_FC])
        b = _dot(nb, wi_ref[:, FFN_DIM + lo:FFN_DIM + lo + FFN_FC])
        hm_ref[:, lo:lo + FFN_FC] = (a * _sigmoid(a) * b).astype(BF16)
    y = _dot(hm_ref[...], wo_ref[...])
    out = x + 0.5 * m[k0 + 2:k0 + 3] * y
    if has_final:
        out = _rms(out, gf_ref[...])
    out_ref[...] = out


def _ffn_call(h, mods_l, g, wi, wo, layer, *, k0, seg_off, pre=None, o_off=0, final_g=None, name):
    split_in = isinstance(h, tuple)
    tm = FFN_TM
    if split_in:
        assert seg_off == 0
        nseg = NSEG
        h_args = list(h)
        h_specs = [
            pl.BlockSpec((None, tm, D_MODEL), lambda s, i: (0, jnp.where(s == 0, i, 0), 0)),
            pl.BlockSpec((None, tm, D_MODEL), lambda s, i: (jnp.maximum(s - 1, 0), jnp.where(s == 0, 0, i), 0)),
        ]
    else:
        nseg = h.shape[0] - seg_off
        h_args = [h]
        h_specs = [pl.BlockSpec((None, tm, D_MODEL), lambda s, i: (s + seg_off, i, 0))]
    in_specs = h_specs + [
        pl.BlockSpec((None, N_MOD, D_MODEL), lambda s, i: (s + seg_off, 0, 0)),
        _full((1, D_MODEL)),
        _layer_slab((D_MODEL, 2 * FFN_DIM), layer),
        _layer_slab((FFN_DIM, D_MODEL), layer),
    ]
    args = h_args + [mods_l, g.reshape(1, D_MODEL), wi, wo]
    if pre is not None:
        oa, ob, oc, od, wout = pre
        ospec = pl.BlockSpec((None, tm, MIX), lambda s, i: (s + o_off, i, 0))
        in_specs += [ospec, ospec, ospec, ospec, _layer_slab((4 * MIX, D_MODEL), layer)]
        args += [oa, ob, oc, od, wout]
    if final_g is not None:
        in_specs.append(_full((1, D_MODEL)))
        args.append(final_g.reshape(1, D_MODEL))
    kern = functools.partial(_ffn_kernel, k0=k0, split_in=split_in, has_pre=pre is not None,
                             has_final=final_g is not None)
    return pl.pallas_call(
        kern,
        out_shape=jax.ShapeDtypeStruct((nseg, SEG, D_MODEL), F32),
        grid=(nseg, SEG // tm),
        in_specs=in_specs,
        out_specs=pl.BlockSpec((None, tm, D_MODEL), lambda s, i: (s, i, 0)),
        scratch_shapes=[pltpu.VMEM((tm, FFN_DIM), BF16)],
        compiler_params=_params(("arbitrary", "arbitrary")),
        name=name,
    )(*args)


PROJ_TM = 512
COL_AB = 0
COL_C = IN_AB
COL_D = COL_C + 512
PROJ_W = COL_D + MIX + D_CONV_CH + LANE


def _swap_halves(x, half):
    lane = lax.broadcasted_iota(jnp.int32, x.shape, 1)
    first = (lane & half) == 0
    up = pltpu.roll(x, LANE - half, axis=1)
    dn = pltpu.roll(x, half, axis=1)
    return jnp.where(first, up, dn)


def _rope(x, cos, sin, half):
    return x * cos + _swap_halves(x, half) * sin


def _head_rms(x, bd, g):
    sq = x * x
    hi = sq.astype(BF16)
    lo = (sq - hi.astype(F32)).astype(BF16)
    ms = _dot(hi, bd) + _dot(lo, bd)
    return x * lax.rsqrt(ms + EPS) * g


def _inproj_kernel(h_ref, mods_ref, g_ref, w_ref, cab_ref, sab_ref, cm_ref, sm_ref, bd_ref,
                   bqn_ref, bkn_ref, cqn_ref, wuq_ref, ckvn_ref, wukvk_ref, wukvv_ref,
                   qa_ref, ka_ref, va_ref, qb_ref, kb_ref, vb_ref, qc_ref, kc_ref, vc_ref,
                   z_ref, xbc_ref, dt_ref):
    x = h_ref[...]
    m = mods_ref[...]
    n = _rms(x, g_ref[...]) * (1.0 + m[4:5]) + m[3:4]
    nb = n.astype(BF16)
    is_ctx = pl.program_id(0) == 0
    cab, sab = jnp.where(is_ctx, 1.0, cab_ref[...]), jnp.where(is_ctx, 0.0, sab_ref[...])
    cm, sm = jnp.where(is_ctx, 1.0, cm_ref[...]), jnp.where(is_ctx, 0.0, sm_ref[...])
    bd = bd_ref[...]
    scale_ab = HEAD_DIM ** -0.5 * LOG2E
    scale_c = (C_NOPE + C_ROPE) ** -0.5 * LOG2E

    p_ab = [_dot(nb, w_ref[:, COL_AB + mixer * 512:COL_AB + (mixer + 1) * 512]) for mixer in range(2)]
    p_c = _dot(nb, w_ref[:, COL_C:COL_C + 512])
    z_ref[...] = _dot(nb, w_ref[:, COL_D:COL_D + MIX])
    for c in range(D_CONV_CH // 256):
        lo = COL_D + MIX + c * 256
        xbc_ref[:, c * 256:(c + 1) * 256] = _dot(nb, w_ref[:, lo:lo + 256])
    lo = COL_D + MIX + D_CONV_CH
    dt_ref[...] = _dot(nb, w_ref[:, lo:lo + LANE])

    for mixer, (q_ref, k_ref, v_ref) in enumerate(((qa_ref, ka_ref, va_ref), (qb_ref, kb_ref, vb_ref))):
        p = p_ab[mixer]
        for s in range(2):
            q = p[:, s * LANE:(s + 1) * LANE]
            if mixer == 1:
                q = _head_rms(q, bd, bqn_ref[...])
            q = _rope(q, cab, sab, 16) * scale_ab
            q_ref[:, s * LANE:(s + 1) * LANE] = q.astype(BF16)
        k = p[:, 2 * LANE:3 * LANE]
        if mixer == 1:
            k = _head_rms(k, bd, bkn_ref[...])
        k_ref[...] = _rope(k, cab, sab, 16).astype(BF16)
        v_ref[...] = p[:, 3 * LANE:4 * LANE].astype(BF16)

    p = p_c
    cq = _rms(p[:, 0:C_Q_LORA], cqn_ref[...]).astype(BF16)
    q = _dot(cq, wuq_ref[...])
    ckv = _rms(p[:, C_Q_LORA:C_Q_LORA + C_KV_LORA], ckvn_ref[...]).astype(BF16)
    kn = _dot(ckv, wukvk_ref[...])
    vc_ref[...] = _dot(ckv, wukvv_ref[...]).astype(BF16)
    kr = _rope(p[:, 3 * LANE:4 * LANE], cm, sm, 8)
    for hh in range(C_HEADS):
        sl = slice(hh * LANE, (hh + 1) * LANE)
        qc_ref[:, sl] = (_rope(q[:, sl], cm, sm, 8) * scale_c).astype(BF16)
        kc_ref[:, sl] = (kn[:, sl] + kr).astype(BF16)


PROJ_OUT = ([(MIX, BF16), (LANE, BF16), (LANE, BF16)] * 2 + [(512, BF16), (512, BF16), (MIX, BF16)]
            + [(MIX, F32), (D_CONV_CH, F32), (LANE, F32)])


def _inproj_call(h, mods_l, g, w, layer, tabs, bd, bqn, bkn, cqn, wuq, ckvn, wukvk, wukvv):
    tm = PROJ_TM
    cab, sab, cm, sm = tabs
    tok = lambda s, i: (s, i, 0)
    tab = pl.BlockSpec((tm, LANE), lambda s, i: (i, 0))
    return pl.pallas_call(
        _inproj_kernel,
        out_shape=[jax.ShapeDtypeStruct((NSEG, SEG, wd), dt) for wd, dt in PROJ_OUT],
        grid=(NSEG, SEG // tm),
        in_specs=[
            pl.BlockSpec((None, tm, D_MODEL), tok),
            pl.BlockSpec((None, N_MOD, D_MODEL), lambda s, i: (s, 0, 0)),
            _full((1, D_MODEL)),
            _layer_slab((D_MODEL, PROJ_W), layer),
            tab, tab, tab, tab,
            _full((LANE, LANE)),
            _full((1, LANE)), _full((1, LANE)),
            _full((1, C_Q_LORA)), _layer_slab((C_Q_LORA, 512), layer),
            _full((1, C_KV_LORA)), _layer_slab((C_KV_LORA, 512), layer), _layer_slab((C_KV_LORA, MIX), layer),
        ],
        out_specs=[pl.BlockSpec((None, tm, wd), tok) for wd, _ in PROJ_OUT],
        compiler_params=_params(("arbitrary", "arbitrary")),
        name="inproj",
    )(h, mods_l, g.reshape(1, D_MODEL), w, cab, sab, cm, sm, bd,
      bqn, bkn, cqn, wuq, ckvn, wukvk, wukvv)


HEADS_GQA = tuple((g, kv, 0, 0, g, kv) for g in range(2) for kv in range(2))
HEADS_MLA = tuple((h, None, h, h // 2, h // 2, h % 2) for h in range(C_HEADS))


def _masked_q(q, half):
    if half is None:
        return q
    lane = lax.broadcasted_iota(jnp.int32, q.shape, 1)
    keep = (lane < HEAD_DIM) if half == 0 else (lane >= HEAD_DIM)
    return jnp.where(keep, q, jnp.zeros_like(q))


def _store_heads(o_ref, rows, outs):
    lane = lax.broadcasted_iota(jnp.int32, outs[(0, 0)].shape, 1)
    for s in range(2):
        o = jnp.where(lane < HEAD_DIM, outs[(s, 0)], outs[(s, 1)])
        o_ref[rows, s * LANE:(s + 1) * LANE] = o.astype(o_ref.dtype)


ATT_SUB = 256
ATT_LOOKAHEAD = 2


def _attn_dense_kernel(q_ref, kc_ref, kx_ref, vc_ref, vx_ref, o_ref, *, heads, ctx_first):
    n_sub = q_ref.shape[0] // ATT_SUB
    stages = [(sub, hd) for sub in range(n_sub) for hd in heads]

    def run(with_x):
        def scores(stage):
            sub, (qs, qhalf, ks) = stage[0], stage[1][:3]
            qh = _masked_q(q_ref[sub * ATT_SUB:(sub + 1) * ATT_SUB, qs * LANE:(qs + 1) * LANE], qhalf)
            ksl = slice(ks * LANE, (ks + 1) * LANE)
            return _dot_nt(qh, kc_ref[:, ksl]), (_dot_nt(qh, kx_ref[:, ksl]) if with_x else None)

        outs = {}
        pending = [scores(st) for st in stages[:ATT_LOOKAHEAD]]
        for i, (sub, (_, _, _, vs, os_, ohalf)) in enumerate(stages):
            s_c, s_x = pending.pop(0)
            if i + ATT_LOOKAHEAD < len(stages):
                pending.append(scores(stages[i + ATT_LOOKAHEAD]))
            vsl = slice(vs * LANE, (vs + 1) * LANE)
            mx = jnp.max(s_c, axis=-1, keepdims=True)
            if with_x:
                mx = jnp.maximum(mx, jnp.max(s_x, axis=-1, keepdims=True))
            p_c = jnp.exp2(s_c - mx)
            den = jnp.sum(p_c, axis=-1, keepdims=True)
            o = _dot(p_c.astype(BF16), vc_ref[:, vsl])
            if with_x:
                p_x = jnp.exp2(s_x - mx)
                den = den + jnp.sum(p_x, axis=-1, keepdims=True)
                o = o + _dot(p_x.astype(BF16), vx_ref[:, vsl])
            outs[(os_, ohalf)] = o * (1.0 / den)
            if len(outs) == len(heads):
                _store_heads(o_ref, slice(sub * ATT_SUB, (sub + 1) * ATT_SUB), outs)
                outs = {}

    if ctx_first:
        j = pl.program_id(1)

        @pl.when(j == 0)
        def _():
            run(False)

        @pl.when(j > 0)
        def _():
            run(True)
    else:
        run(True)


def _attn_specs(q, k, v, tq, with_ctx):
    nq = SEQ // tq
    ncb = CTX_LEN // tq if with_ctx else 0
    if with_ctx:
        qmap = lambda b, j: (jnp.where(j < ncb, 0, b + 1), jnp.where(j < ncb, ncb * b + j, j - ncb), 0)
        out_shape = jax.ShapeDtypeStruct((NSEG, SEG, MIX), BF16)
        omap = qmap
    else:
        qmap = lambda b, j: (b + 1, j, 0)
        out_shape = jax.ShapeDtypeStruct((BATCH, SEG, MIX), BF16)
        omap = lambda b, j: (b, j, 0)
    kw, vw = k.shape[-1], v.shape[-1]
    in_specs = [pl.BlockSpec((None, tq, q.shape[-1]), qmap),
                pl.BlockSpec((None, CTX_LEN, kw), lambda b, j: (0, b, 0)),
                pl.BlockSpec((None, SEQ, kw), lambda b, j: (b + 1, 0, 0)),
                pl.BlockSpec((None, CTX_LEN, vw), lambda b, j: (0, b, 0)),
                pl.BlockSpec((None, SEQ, vw), lambda b, j: (b + 1, 0, 0))]
    return (BATCH, nq + ncb), in_specs, pl.BlockSpec((None, tq, MIX), omap), out_shape, ncb


def _attn_dense_call(q, k, v, *, heads, with_ctx, name):
    grid, in_specs, out_spec, out_shape, _ = _attn_specs(q, k, v, ATT_SUB, with_ctx)
    kern = functools.partial(_attn_dense_kernel, heads=heads, ctx_first=with_ctx)
    return pl.pallas_call(
        kern,
        out_shape=out_shape,
        grid=grid,
        in_specs=in_specs,
        out_specs=out_spec,
        compiler_params=_params(("arbitrary", "arbitrary")),
        name=name,
    )(q, k, k, v, v)


WIN_TQ = 256


def _attn_win_kernel(sink_ref, q_ref, kc_ref, kx_ref, vc_ref, vx_ref, o_ref, *, n_ctx_blocks):
    j = pl.program_id(1)
    n_sub = WIN_TQ // WINDOW

    n_h = len(HEADS_GQA)
    head_of_row = lax.broadcasted_iota(jnp.int32, (n_h * WINDOW, 1), 0) // WINDOW
    sink = jnp.zeros((n_h * WINDOW, 1), F32)
    for i, (qs, qhalf, _, _, _, _) in enumerate(HEADS_GQA):
        sink = jnp.where(head_of_row == i, sink_ref[qhalf * 2 + qs] * LOG2E, sink)
    n_blk = SEQ // WINDOW
    q_in_blk = lax.broadcasted_iota(jnp.int32, (n_h * WINDOW, WINDOW), 0) % WINDOW
    k_in_blk = lax.broadcasted_iota(jnp.int32, (n_h * WINDOW, WINDOW), 1)
    key_ge_query = k_in_blk >= q_in_blk
    key_le_query = k_in_blk <= q_in_blk

    def run(local):
        def scores(sub):
            rows = slice(sub * WINDOW, (sub + 1) * WINDOW)
            q4 = jnp.concatenate([_masked_q(q_ref[rows, qs * LANE:(qs + 1) * LANE], qhalf)
                                  for qs, qhalf, _, _, _, _ in HEADS_GQA], axis=0)
            s_c = _dot_nt(q4, kc_ref[...])
            if not local:
                return s_c, None, None
            n = (j - n_ctx_blocks) * n_sub + sub
            blocks = [pl.ds(pl.multiple_of(b * WINDOW, WINDOW), WINDOW)
                      for b in (jnp.maximum(n - 1, 0), n, jnp.minimum(n + 1, n_blk - 1))]
            s = _dot_nt(q4, jnp.concatenate([kx_ref[blk, :] for blk in blocks], axis=0))
            s_l = jnp.concatenate([
                jnp.where(jnp.logical_and(key_ge_query, n > 0), s[:, 0:WINDOW], -jnp.inf),
                s[:, WINDOW:2 * WINDOW],
                jnp.where(jnp.logical_and(key_le_query, n < n_blk - 1), s[:, 2 * WINDOW:], -jnp.inf)], axis=1)
            return s_c, s_l, blocks

        all_scores = [scores(sub) for sub in range(n_sub)]
        for sub, (s_c, s_l, blocks) in enumerate(all_scores):
            rows = slice(sub * WINDOW, (sub + 1) * WINDOW)
            mx = jnp.maximum(jnp.max(s_c, axis=-1, keepdims=True), sink)
            if local:
                mx = jnp.maximum(mx, jnp.max(s_l, axis=-1, keepdims=True))
            p_c = jnp.exp2(s_c - mx)
            den = jnp.sum(p_c, axis=-1, keepdims=True) + jnp.exp2(sink - mx)
            o = _dot(p_c.astype(BF16), vc_ref[...])
            if local:
                p_l = jnp.exp2(s_l - mx)
                den = den + jnp.sum(p_l, axis=-1, keepdims=True)
                o = o + _dot(p_l.astype(BF16), jnp.concatenate([vx_ref[blk, :] for blk in blocks], axis=0))
            o = o * (1.0 / den)
            outs = {(os_, ohalf): o[i * WINDOW:(i + 1) * WINDOW]
                    for i, (_, _, _, _, os_, ohalf) in enumerate(HEADS_GQA)}
            _store_heads(o_ref, rows, outs)

    if n_ctx_blocks:
        @pl.when(j < n_ctx_blocks)
        def _():
            run(False)

        @pl.when(j >= n_ctx_blocks)
        def _():
            run(True)
    else:
        run(True)


def _attn_win_call(sink, q, k, v, *, with_ctx):
    grid, in_specs, out_spec, out_shape, ncb = _attn_specs(q, k, v, WIN_TQ, with_ctx)
    kern = functools.partial(_attn_win_kernel, n_ctx_blocks=ncb)
    return pl.pallas_call(
        kern,
        out_shape=out_shape,
        grid=grid,
        in_specs=[pl.BlockSpec(memory_space=pltpu.SMEM)] + in_specs,
        out_specs=out_spec,
        compiler_params=_params(("arbitrary", "arbitrary")),
        name="attn_win",
    )(sink, q, k, k, v, v)


Q = SSD_CHUNK
PAD = 8
U_CTX = PAD
U_X = PAD + CTX_LEN + PAD
U_ROWS = U_X + SEQ + PAD
N_CTX_CHUNK = CTX_LEN // Q
LOCAL_CHUNKS = 3
SCAN_STEPS = 2


def _split3(a):
    a1 = a.astype(BF16)
    r1 = a - a1.astype(F32)
    a2 = r1.astype(BF16)
    a3 = (r1 - a2.astype(F32)).astype(BF16)
    return a1, a2, a3


def _ssd_kernel(zc_ref, zx_ref, uc_ref, ux_ref, dtc_ref, dtx_ref, cw_ref, cb_ref, dtb_ref, alog_ref,
                dskip_ref, onorm_ref, yc_ref, yx_ref, upad, xs_s, bm_s, cm_s, dt_s, y_s, st_s, cme_s, da_s, h_s):
    n_slab = D_CONV_CH // LANE
    zpad = jnp.zeros((PAD, LANE), F32)
    for j in range(n_slab):
        sl = slice(j * LANE, (j + 1) * LANE)
        upad[j, 0:PAD, :] = zpad
        upad[j, U_CTX:U_CTX + CTX_LEN, :] = uc_ref[:, sl]
        upad[j, U_CTX + CTX_LEN:U_X, :] = zpad
        upad[j, U_X:U_X + SEQ, :] = ux_ref[:, sl]
        upad[j, U_X + SEQ:U_ROWS, :] = zpad

    for c in range(N_CHUNK):
        base = U_CTX + c * Q if c < N_CTX_CHUNK else U_X + (c - N_CTX_CHUNK) * Q
        for j in range(n_slab):
            sl = slice(j * LANE, (j + 1) * LANE)
            dst = (xs_s, bm_s, cm_s)[j // 2]
            acc = jnp.broadcast_to(cb_ref[:, sl], (Q, LANE))
            for k in range(D_CONV):
                lo = base + k - D_CONV // 2
                acc = acc + upad[j, lo:lo + Q, :] * cw_ref[k:k + 1, sl]
            dcol = (j % 2) * LANE
            dst[c * Q:(c + 1) * Q, dcol:dcol + LANE] = (acc * _sigmoid(acc)).astype(dst.dtype)

    def softplus(v):
        return jnp.maximum(v, 0.0) + jnp.log(1.0 + jnp.exp(-jnp.abs(v)))

    dt_s[0:CTX_LEN, :] = softplus(dtc_ref[...] + dtb_ref[...])
    dt_s[CTX_LEN:TOK, :] = softplus(dtx_ref[...] + dtb_ref[...])

    a_neg = -jnp.exp(alog_ref[...]) * LOG2E
    row = lax.broadcasted_iota(jnp.int32, (Q, Q), 0)
    col = lax.broadcasted_iota(jnp.int32, (Q, Q), 1)
    causal = (col <= row, col >= row)
    tri = (causal[0].astype(BF16), causal[1].astype(BF16))
    lo_half = col < HEAD_DIM
    last_row = (Q - 1, 0)

    def bcast_col(v, idx):
        return jnp.broadcast_to(v[:, idx:idx + 1], (Q, Q))

    def chunk_rows(c):
        return pl.ds(c * Q if isinstance(c, int) else pl.multiple_of(c * Q, Q), Q)

    def local_load(c):
        rows = chunk_rows(c)
        groups = [(cm_s[rows, g * Q:(g + 1) * Q], bm_s[rows, g * Q:(g + 1) * Q], xs_s[rows, g * Q:(g + 1) * Q])
                  for g in range(2)]
        return dt_s[rows, :], groups

    def local_sums(dt, groups):
        a1, a2, a3 = _split3(dt * a_neg)
        cs_f = _dot(tri[0], a1) + _dot(tri[0], a2) + _dot(tri[0], a3)
        cs_b = _dot(tri[1], a1) + _dot(tri[1], a2) + _dot(tri[1], a3)
        cs = jnp.where(col < D_HEADS, cs_f, cs_b)
        n_row = 2 * D_HEADS
        return cs, cs.T[0:n_row], dt.T[0:n_row], [_dot_nt(cmg, bmg) for cmg, bmg, _ in groups]

    def local_compute(groups, cs, cs_t, dt_t, cbms):
        ys, sts, das, cmes = [], {}, {}, {}
        for g, (cmg, bmg, xsg) in enumerate(groups):
            cbm = cbms[g]
            cmg32 = cmg.astype(F32)
            bm_t = bmg.astype(F32).T
            xh = (jnp.where(lo_half, xsg, 0.0).astype(BF16), jnp.where(lo_half, 0.0, xsg).astype(BF16))
            y = None
            for d in range(2):
                st = None
                da = []
                for hh in range(2):
                    idx = d * D_HEADS + g * 2 + hh
                    colb = bcast_col(cs, idx)
                    cs_row = cs_t[idx:idx + 1, :]
                    dt_row = dt_t[idx:idx + 1, :]
                    seg = colb - (cs_row - jnp.log2(dt_row))
                    dec = jnp.exp2(jnp.where(causal[d], seg, -jnp.inf))
                    yd = _dot((cbm * dec).astype(BF16), xh[hh])
                    y = yd if y is None else y + yd
                    last = cs_row[:, last_row[d]:last_row[d] + 1]
                    w_row = dt_row * jnp.exp2(last - cs_row)
                    sth = _dot((bm_t * w_row).astype(BF16), xh[hh])
                    st = sth if st is None else st + sth
                    cmes[d * 4 + g * 2 + hh] = (cmg32 * jnp.exp2(colb)).astype(BF16)
                    da.append(jnp.broadcast_to(jnp.exp2(last), (1, Q)))
                sts[d * 2 + g] = st
                das[d * 2 + g] = jnp.where(lo_half[0:1, :], da[0], da[1])
            ys.append(y)
        return ys, sts, das, cmes

    def local_store(c, res):
        ys, sts, das, cmes = res
        rows = chunk_rows(c)
        for g in range(2):
            y_s[rows, g * Q:(g + 1) * Q] = ys[g]
        for k, v in sts.items():
            st_s[c * 4 + k] = v
        for k, v in das.items():
            da_s[c * 4 + k, 0:1, :] = v
        for k, v in cmes.items():
            cme_s[c * 8 + k] = v

    n_iter = N_CHUNK // LOCAL_CHUNKS

    def iter_chunks(i):
        return [i * LOCAL_CHUNKS + u for u in range(LOCAL_CHUNKS)]

    def iter_sums(i):
        return [local_sums(*local_load(c)) for c in iter_chunks(i)]

    def local_body(i, sums):
        nxt = iter_sums(jnp.minimum(i + 1, n_iter - 1))
        chunks = iter_chunks(i)
        results = [local_compute(local_load(c)[1], *sm) for c, sm in zip(chunks, sums)]
        for c, res in zip(chunks, results):
            local_store(c, res)
        return nxt

    lax.fori_loop(0, n_iter, local_body, iter_sums(0))

    h_s[...] = jnp.zeros(h_s.shape, F32)

    def scan_steps(steps):
        work = []
        for c_fwd, c_bwd in steps:
            work += [(d, g, c) for d, c in ((0, c_fwd), (1, c_bwd)) for g in range(2)]
        loaded = []
        for d, g, c in work:
            k = c * 4 + d * 2 + g
            loaded.append((cme_s[2 * k], cme_s[2 * k + 1], da_s[k, 0:1, :], st_s[k],
                           y_s[chunk_rows(c), g * Q:(g + 1) * Q]))
        h = [h_s[k] for k in range(4)]
        ys = []
        for (d, g, c), (cme0, cme1, da, st, y) in zip(work, loaded):
            h_in = h[d * 2 + g]
            hb = h_in.astype(BF16)
            zero = jnp.zeros_like(hb)
            ys.append(y + _dot(cme0, jnp.where(lo_half, hb, zero)) + _dot(cme1, jnp.where(lo_half, zero, hb)))
            h[d * 2 + g] = da * h_in + st
        for (d, g, c), y in zip(work, ys):
            y_s[chunk_rows(c), g * Q:(g + 1) * Q] = y
        for k in range(4):
            h_s[k] = h[k]

    for i in range(N_CTX_CHUNK):
        scan_steps([(i, N_CTX_CHUNK - 1 - i)])

    def scan_body(t, carry):
        i0 = N_CTX_CHUNK + t * SCAN_STEPS
        scan_steps([(i0 + u, N_CHUNK - 1 + N_CTX_CHUNK - (i0 + u)) for u in range(SCAN_STEPS)])
        return carry

    lax.fori_loop(0, (N_CHUNK - N_CTX_CHUNK) // SCAN_STEPS, scan_body, 0)

    dskip = dskip_ref[...]
    onorm = onorm_ref[...]

    def finish(rows, z):
        y = y_s[rows, :] + dskip * xs_s[rows, :]
        return _rms(y * (z * _sigmoid(z)), onorm).astype(BF16)

    for c in range(N_CTX_CHUNK):
        yc_ref[c * Q:(c + 1) * Q, :] = finish(slice(c * Q, (c + 1) * Q), zc_ref[c * Q:(c + 1) * Q, :])

    def fin_body(c, carry):
        r0 = pl.multiple_of(c * Q, Q)
        yx_ref[pl.ds(r0, Q), :] = finish(pl.ds(CTX_LEN + r0, Q), zx_ref[pl.ds(r0, Q), :])
        return carry

    lax.fori_loop(0, SEQ // Q, fin_body, 0, unroll=2)


def _ssd_call(z, xbc, dt, cw, cb, dtb, alog, dskip, onorm):
    def cspec(wd):
        return pl.BlockSpec((None, CTX_LEN, wd), lambda b: (0, b, 0))

    def xspec(wd):
        return pl.BlockSpec((None, SEQ, wd), lambda b: (b + 1, 0, 0))

    return pl.pallas_call(
        _ssd_kernel,
        out_shape=[jax.ShapeDtypeStruct((BATCH, CTX_LEN, MIX), BF16),
                   jax.ShapeDtypeStruct((BATCH, SEQ, MIX), BF16)],
        grid=(BATCH,),
        in_specs=[cspec(MIX), xspec(MIX), cspec(D_CONV_CH), xspec(D_CONV_CH), cspec(LANE), xspec(LANE),
                  _full((8, D_CONV_CH)), _full((1, D_CONV_CH)), _full((1, LANE)), _full((1, LANE)),
                  _full((1, MIX)), _full((1, MIX))],
        out_specs=[pl.BlockSpec((None, CTX_LEN, MIX), lambda b: (b, 0, 0)),
                   pl.BlockSpec((None, SEQ, MIX), lambda b: (b, 0, 0))],
        scratch_shapes=[pltpu.VMEM((D_CONV_CH // LANE, U_ROWS, LANE), F32),
                        pltpu.VMEM((TOK, MIX), F32), pltpu.VMEM((TOK, MIX), BF16), pltpu.VMEM((TOK, MIX), BF16),
                        pltpu.VMEM((TOK, LANE), F32), pltpu.VMEM((TOK, MIX), F32),
                        pltpu.VMEM((N_CHUNK * 4, Q, Q), F32), pltpu.VMEM((N_CHUNK * 8, Q, Q), BF16),
                        pltpu.VMEM((N_CHUNK * 4, 8, LANE), F32), pltpu.VMEM((4, Q, Q), F32)],
        compiler_params=_params(("arbitrary",)),
        name="ssd",
    )(z, z, xbc, xbc, dt, dt, cw, cb, dtb, alog, dskip, onorm)


def _rope_tables():
    rows = SEQ // GRID_W
    r = np.repeat(np.arange(rows, dtype=np.float64), GRID_W)
    c = np.tile(np.arange(GRID_W, dtype=np.float64), rows)

    def tables(rot_dim):
        axis_dim = rot_dim // 2
        inv = ROPE_THETA ** (-np.arange(0, axis_dim, 2, dtype=np.float64) / axis_dim)
        ar = r[:, None] * inv[None, :]
        ac = c[:, None] * inv[None, :]
        cos = np.concatenate([np.cos(ar), np.cos(ar), np.cos(ac), np.cos(ac)], axis=-1)
        sin = np.concatenate([-np.sin(ar), np.sin(ar), -np.sin(ac), np.sin(ac)], axis=-1)
        return cos, sin

    c64, s64 = tables(HEAD_DIM)
    cab = np.tile(c64, (1, 2))
    sab = np.tile(s64, (1, 2))
    c32, s32 = tables(C_ROPE)
    cm = np.concatenate([np.ones((SEQ, C_NOPE)), c32, np.ones((SEQ, 32))], axis=-1)
    sm = np.concatenate([np.zeros((SEQ, C_NOPE)), s32, np.zeros((SEQ, 32))], axis=-1)
    return tuple(jnp.asarray(t, F32) for t in (cab, sab, cm, sm))


def _head_mean_matrix():
    lane = np.arange(LANE)
    same = (lane[:, None] // HEAD_DIM) == (lane[None, :] // HEAD_DIM)
    return jnp.asarray(np.where(same, 1.0 / HEAD_DIM, 0.0), BF16)


def _gqa_order(w, axis):
    shp = w.shape
    w = w.reshape(shp[:axis] + (2, 2, HEAD_DIM) + shp[axis + 1:])
    w = jnp.swapaxes(w, axis, axis + 1)
    return w.reshape(shp)


def _stacked_weights(w_in, c_w_uq, c_w_ukv, w_out):
    n_l, d = w_in.shape[:2]
    zc = lambda n: jnp.zeros((n_l, d, n), BF16)
    wb = w_in.astype(BF16)
    o_c = IN_AB
    o_d = IN_AB + IN_C
    w = jnp.concatenate([
        _gqa_order(wb[..., 0:MIX], 2), wb[..., MIX:512],
        _gqa_order(wb[..., 512:512 + MIX], 2), wb[..., 512 + MIX:o_c + C_Q_LORA + C_KV_LORA],
        zc(C_NOPE), wb[..., o_c + C_Q_LORA + C_KV_LORA:o_d], zc(LANE - C_NOPE - C_ROPE),
        wb[..., o_d:], zc(LANE - 2 * D_HEADS),
    ], axis=2)
    assert w.shape[2] == PROJ_W
    dq = C_NOPE + C_ROPE
    pad_last = lambda t, n: jnp.pad(t, ((0, 0),) * (t.ndim - 1) + ((0, n),))
    wuq = pad_last(c_w_uq.reshape(n_l, C_Q_LORA, C_HEADS, dq), LANE - dq)
    wkv = c_w_ukv.reshape(n_l, C_KV_LORA, C_HEADS, C_NOPE + C_V)
    wk = pad_last(wkv[..., :C_NOPE], LANE - C_NOPE)
    wv = wkv[..., C_NOPE:]
    wout = jnp.concatenate([_gqa_order(w_out[:, 0:MIX], 1), _gqa_order(w_out[:, MIX:2 * MIX], 1),
                            w_out[:, 2 * MIX:]], axis=1)
    return (w, wuq.reshape(n_l, C_Q_LORA, 512).astype(BF16), wk.reshape(n_l, C_KV_LORA, 512).astype(BF16),
            wv.reshape(n_l, C_KV_LORA, MIX).astype(BF16), wout.astype(BF16))


def _lane_row(v, width=LANE):
    v = v.reshape(1, -1).astype(F32)
    return jnp.pad(v, ((0, 0), (0, width - v.shape[1])))


def kernel(x, c, ctx, c_ctx, ada_w, ada_b, ffn1_norm, ffn1_wi, ffn1_wo, mix_norm, w_in, w_out, a_sink, b_q_norm, b_k_norm, c_q_norm, c_w_uq, c_kv_norm, c_w_ukv, d_conv_w, d_conv_b, d_a_log, d_dt_bias, d_skip, d_out_norm, ffn2_norm, ffn2_wi, ffn2_wo, final_norm):
    cvec = jnp.concatenate([c_ctx[None, :], c, jnp.zeros((16 - NSEG, D_MODEL), F32)], axis=0)
    mods = _mods_call(cvec, ada_w, ada_b).reshape(DEPTH, 16, N_MOD, D_MODEL)[:, :NSEG]
    tabs = _rope_tables()
    bd = _head_mean_matrix()
    wi1, wo1 = ffn1_wi.astype(BF16), ffn1_wo.astype(BF16)
    wi2, wo2 = ffn2_wi.astype(BF16), ffn2_wo.astype(BF16)
    w, wuq, wk, wv, wout = _stacked_weights(w_in, c_w_uq, c_w_ukv, w_out)

    h = (ctx.reshape(1, BATCH * CTX_LEN, D_MODEL), x)
    out = None
    for l in range(DEPTH):
        with_ctx = l < DEPTH - 1
        mods_l = mods[l]
        h = _ffn_call(h, mods_l, ffn1_norm[l], wi1, wo1, l, k0=0, seg_off=0, name=f"ffn1_{l}")
        (qa, ka, va, qb, kb, vb, qc, kc, vc, z, xbc, dt) = _inproj_call(
            h, mods_l, mix_norm[l], w, l, tabs, bd,
            jnp.tile(b_q_norm[l], 2).reshape(1, LANE), jnp.tile(b_k_norm[l], 2).reshape(1, LANE),
            c_q_norm[l].reshape(1, C_Q_LORA), wuq, c_kv_norm[l].reshape(1, C_KV_LORA), wk, wv)

        oa = _attn_win_call(a_sink[l], qa, ka, va, with_ctx=with_ctx)
        ob = _attn_dense_call(qb, kb, vb, heads=HEADS_GQA, with_ctx=with_ctx, name=f"attn_b_{l}")
        oc = _attn_dense_call(qc, kc, vc, heads=HEADS_MLA, with_ctx=with_ctx, name=f"attn_c_{l}")
        cw = jnp.pad(d_conv_w[l], ((0, 8 - D_CONV), (0, 0)))
        yc, yx = _ssd_call(z, xbc, dt, cw, d_conv_b[l].reshape(1, D_CONV_CH),
                           _lane_row(d_dt_bias[l]), _lane_row(d_a_log[l]),
                           jnp.repeat(d_skip[l], HEAD_DIM).reshape(1, MIX), d_out_norm[l].reshape(1, MIX))
        if with_ctx:
            od = jnp.concatenate([yc.reshape(1, SEG, MIX), yx], axis=0)
            h = _ffn_call(h, mods_l, ffn2_norm[l], wi2, wo2, l, k0=6, seg_off=0,
                          pre=(oa, ob, oc, od, wout), o_off=0, name=f"ffn2_{l}")
        else:
            out = _ffn_call(h, mods_l, ffn2_norm[l], wi2, wo2, l, k0=6, seg_off=1,
                            pre=(oa, ob, oc, yx, wout), o_off=0, final_g=final_norm, name=f"ffn2_{l}")
    return out
```

```python
import functools

import numpy as np
import jax
import jax.numpy as jnp
from jax import lax
from jax.experimental import pallas as pl
from jax.experimental.pallas import tpu as pltpu

D_MODEL = 1024
BATCH = 8
SEQ = 2048
DEPTH = 2
GRID_W = 64
CTX_LEN = 256
HEAD_DIM = 64
ROPE_THETA = 10000.0
EPS = 1e-6
FFN_DIM = 2816
N_MOD = 9
WINDOW = 128
C_HEADS = 4
C_Q_LORA = 256
C_KV_LORA = 128
C_NOPE = 64
C_ROPE = 32
C_V = 64
D_HEADS = 4
D_STATE = 128
D_CONV = 5
SSD_CHUNK = 128
MIX = 256
IN_AB = 1024
IN_C = C_Q_LORA + C_KV_LORA + C_ROPE
D_CONV_CH = MIX + 2 * 2 * D_STATE
IN_D = MIX + D_CONV_CH + 2 * D_HEADS

NSEG = BATCH + 1
SEG = SEQ
assert BATCH * CTX_LEN == SEG
TOK = CTX_LEN + SEQ
N_CHUNK = TOK // SSD_CHUNK

LANE = 128
VMEM_LIMIT = 56 * 1024 * 1024

F32 = jnp.float32
BF16 = jnp.bfloat16
LOG2E = 1.4426950408889634


def _dot(a, b):
    return jnp.dot(a, b, preferred_element_type=F32)


def _dot_nt(a, b):
    return lax.dot_general(a, b, (((1,), (1,)), ((), ())), preferred_element_type=F32)


def _dot_tn(a, b):
    return lax.dot_general(a, b, (((0,), (0,)), ((), ())), preferred_element_type=F32)


def _sigmoid(x):
    return 1.0 / (1.0 + jnp.exp(-x))


def _rms(x, g):
    return x * lax.rsqrt(jnp.mean(x * x, axis=-1, keepdims=True) + EPS) * g


def _full(shape):
    nd = len(shape)
    return pl.BlockSpec(shape, lambda *_: (0,) * nd)


def _layer_slab(shape, layer):
    nd = len(shape)
    return pl.BlockSpec((None,) + tuple(shape), lambda *_: (layer,) + (0,) * nd, pipeline_mode=pl.Buffered(1))


def _params(sem):
    return pltpu.CompilerParams(dimension_semantics=sem, vmem_limit_bytes=VMEM_LIMIT)


def _mods_kernel(c_ref, w_ref, b_ref, o_ref):
    c = c_ref[...]
    s = (c * _sigmoid(c)).astype(BF16)
    o_ref[...] = _dot(s, w_ref[...].astype(BF16)) + b_ref[...]


def _mods_call(cvec, ada_w, ada_b):
    n_l = ada_w.shape[0]
    return pl.pallas_call(
        _mods_kernel,
        out_shape=jax.ShapeDtypeStruct((n_l, 16, N_MOD * D_MODEL), F32),
        grid=(n_l, N_MOD),
        in_specs=[
            pl.BlockSpec((16, D_MODEL), lambda l, n: (0, 0)),
            pl.BlockSpec((None, D_MODEL, D_MODEL), lambda l, n: (l, 0, n)),
            pl.BlockSpec((None, 1, D_MODEL), lambda l, n: (l, 0, n)),
        ],
        out_specs=pl.BlockSpec((None, 16, D_MODEL), lambda l, n: (l, 0, n)),
        compiler_params=_params(("arbitrary", "arbitrary")),
        name="mods",
    )(cvec, ada_w, ada_b.reshape(n_l, 1, N_MOD * D_MODEL))


FFN_TM = 512
FFN_FC = 256


def _seg_specs(src, tm, width, seg_off):
    if isinstance(src, tuple):
        assert seg_off == 0
        ctx_arr, x_arr = src
        specs = [pl.BlockSpec((None, tm, width), lambda s, i: (0, jnp.where(s == 0, i, 0), 0)),
                 pl.BlockSpec((None, tm, width), lambda s, i: (jnp.maximum(s - 1, 0), jnp.where(s == 0, 0, i), 0))]
        return specs, [ctx_arr.reshape(1, SEG, width), x_arr]
    return [pl.BlockSpec((None, tm, width), lambda s, i: (s + seg_off, i, 0))], [src]


def _seg_read(refs):
    if len(refs) == 2:
        return jnp.where(pl.program_id(0) == 0, refs[0][...], refs[1][...])
    return refs[0][...]


def _ffn_kernel(*refs, k0, n_h, n_pre, has_final):
    refs = list(refs)
    h_refs = [refs.pop(0) for _ in range(n_h)]
    mods_ref, g_ref, wi_ref, wo_ref = (refs.pop(0) for _ in range(4))
    if n_pre:
        o_refs = [[refs.pop(0) for _ in range(n)] for n in n_pre]
        wout_ref = refs.pop(0)
    if has_final:
        gf_ref = refs.pop(0)
    out_ref, hm_ref = refs

    x = _seg_read(h_refs)
    m = mods_ref[...]
    if n_pre:
        o = None
        for k, o_ref in enumerate(o_refs):
            part = _dot(_seg_read(o_ref), wout_ref[k * MIX:(k + 1) * MIX, :])
            o = part if o is None else o + part
        x = x + m[5:6] * o
    n = _rms(x, g_ref[...]) * (1.0 + m[k0 + 1:k0 + 2]) + m[k0:k0 + 1]
    nb = n.astype(BF16)
    for c in range(FFN_DIM // FFN_FC):
        lo = c * FFN_FC
        a = _dot(nb, wi_ref[:, lo:lo + FFN_FC])
        b = _dot(nb, wi_ref[:, FFN_DIM + lo:FFN_DIM + lo + FFN_FC])
        hm_ref[:, lo:lo + FFN_FC] = (a * _sigmoid(a) * b).astype(BF16)
    y = _dot(hm_ref[...], wo_ref[...])
    out = x + 0.5 * m[k0 + 2:k0 + 3] * y
    if has_final:
        out = _rms(out, gf_ref[...])
    out_ref[...] = out


def _ffn_call(h, mods_l, g, wi, wo, layer, *, k0, seg_off, pre=None, final_g=None, name):
    tm = FFN_TM
    nseg = NSEG if isinstance(h, tuple) else h.shape[0] - seg_off
    in_specs, args = _seg_specs(h, tm, D_MODEL, seg_off)
    n_h = len(args)
    in_specs += [
        pl.BlockSpec((None, N_MOD, D_MODEL), lambda s, i: (s + seg_off, 0, 0)),
        _full((1, D_MODEL)),
        _layer_slab((D_MODEL, 2 * FFN_DIM), layer),
        _layer_slab((FFN_DIM, D_MODEL), layer),
    ]
    args += [mods_l, g.reshape(1, D_MODEL), wi, wo]
    n_pre = ()
    if pre is not None:
        *outs, wout = pre
        for o in outs:
            sp, ar = _seg_specs(o, tm, MIX, 0)
            in_specs += sp
            args += ar
            n_pre += (len(ar),)
        in_specs.append(_layer_slab((4 * MIX, D_MODEL), layer))
        args.append(wout)
    if final_g is not None:
        in_specs.append(_full((1, D_MODEL)))
        args.append(final_g.reshape(1, D_MODEL))
    kern = functools.partial(_ffn_kernel, k0=k0, n_h=n_h, n_pre=n_pre, has_final=final_g is not None)
    return pl.pallas_call(
        kern,
        out_shape=jax.ShapeDtypeStruct((nseg, SEG, D_MODEL), F32),
        grid=(nseg, SEG // tm),
        in_specs=in_specs,
        out_specs=pl.BlockSpec((None, tm, D_MODEL), lambda s, i: (s, i, 0)),
        scratch_shapes=[pltpu.VMEM((tm, FFN_DIM), BF16)],
        compiler_params=_params(("arbitrary", "arbitrary")),
        name=name,
    )(*args)


PROJ_TM = 512
COL_AB = 0
COL_C = IN_AB
COL_D = COL_C + 512
PROJ_W = COL_D + MIX + D_CONV_CH + LANE


def _swap_halves(x, half):
    lane = lax.broadcasted_iota(jnp.int32, x.shape, 1)
    first = (lane & half) == 0
    up = pltpu.roll(x, LANE - half, axis=1)
    dn = pltpu.roll(x, half, axis=1)
    return jnp.where(first, up, dn)


def _rope(x, cos, sin, half):
    return x * cos + _swap_halves(x, half) * sin


def _head_rms(x, bd, g):
    sq = x * x
    hi = sq.astype(BF16)
    lo = (sq - hi.astype(F32)).astype(BF16)
    ms = _dot(hi, bd) + _dot(lo, bd)
    return x * lax.rsqrt(ms + EPS) * g


def _inproj_kernel(h_ref, mods_ref, g_ref, w_ref, cab_ref, sab_ref, cm_ref, sm_ref, bd_ref,
                   bqn_ref, bkn_ref, cqn_ref, wuq_ref, ckvn_ref, wukvk_ref, wukvv_ref,
                   qa_ref, ka_ref, va_ref, qb_ref, kb_ref, vb_ref, qc_ref, kc_ref, vc_ref,
                   z_ref, xbc_ref, dt_ref):
    x = h_ref[...]
    m = mods_ref[...]
    n = _rms(x, g_ref[...]) * (1.0 + m[4:5]) + m[3:4]
    nb = n.astype(BF16)
    is_ctx = pl.program_id(0) == 0
    cab, sab = jnp.where(is_ctx, 1.0, cab_ref[...]), jnp.where(is_ctx, 0.0, sab_ref[...])
    cm, sm = jnp.where(is_ctx, 1.0, cm_ref[...]), jnp.where(is_ctx, 0.0, sm_ref[...])
    bd = bd_ref[...]
    scale_ab = HEAD_DIM ** -0.5 * LOG2E
    scale_c = (C_NOPE + C_ROPE) ** -0.5 * LOG2E

    p_ab = [_dot(nb, w_ref[:, COL_AB + mixer * 512:COL_AB + (mixer + 1) * 512]) for mixer in range(2)]
    p_c = _dot(nb, w_ref[:, COL_C:COL_C + 512])
    z_ref[...] = _dot(nb, w_ref[:, COL_D:COL_D + MIX])
    for c in range(D_CONV_CH // 256):
        lo = COL_D + MIX + c * 256
        xbc_ref[:, c * 256:(c + 1) * 256] = _dot(nb, w_ref[:, lo:lo + 256])
    lo = COL_D + MIX + D_CONV_CH
    dt_ref[...] = _dot(nb, w_ref[:, lo:lo + LANE])

    for mixer, (q_ref, k_ref, v_ref) in enumerate(((qa_ref, ka_ref, va_ref), (qb_ref, kb_ref, vb_ref))):
        p = p_ab[mixer]
        for s in range(2):
            q = p[:, s * LANE:(s + 1) * LANE]
            if mixer == 1:
                q = _head_rms(q, bd, bqn_ref[...])
            q = _rope(q, cab, sab, 16) * scale_ab
            q_ref[:, s * LANE:(s + 1) * LANE] = q.astype(BF16)
        k = p[:, 2 * LANE:3 * LANE]
        if mixer == 1:
            k = _head_rms(k, bd, bkn_ref[...])
        k_ref[...] = _rope(k, cab, sab, 16).astype(BF16)
        v = p[:, 3 * LANE:4 * LANE]
        lane = lax.broadcasted_iota(jnp.int32, v.shape, 1)
        v_ref[:, 0:LANE] = jnp.where(lane < HEAD_DIM, v, 1.0).astype(BF16)
        v_ref[:, LANE:2 * LANE] = jnp.where(lane < HEAD_DIM, 1.0, v).astype(BF16)

    p = p_c
    cq = _rms(p[:, 0:C_Q_LORA], cqn_ref[...]).astype(BF16)
    q = _dot(cq, wuq_ref[...])
    ckv = _rms(p[:, C_Q_LORA:C_Q_LORA + C_KV_LORA], ckvn_ref[...]).astype(BF16)
    kn = _dot(ckv, wukvk_ref[...])
    lane = lax.broadcasted_iota(jnp.int32, (1, C_HEADS * LANE), 1)
    own_half = ((lane // HEAD_DIM) % 2) == ((lane // LANE) % 2)
    vc_ref[...] = jnp.where(own_half, _dot(ckv, wukvv_ref[...]), 1.0).astype(BF16)
    kr = _rope(p[:, 3 * LANE:4 * LANE], cm, sm, 8)
    for hh in range(C_HEADS):
        sl = slice(hh * LANE, (hh + 1) * LANE)
        qc_ref[:, sl] = (_rope(q[:, sl], cm, sm, 8) * scale_c).astype(BF16)
        kc_ref[:, sl] = (kn[:, sl] + kr).astype(BF16)


PROJ_OUT = ([(MIX, BF16), (LANE, BF16), (MIX, BF16)] * 2 + [(512, BF16), (512, BF16), (512, BF16)]
            + [(MIX, F32), (D_CONV_CH, F32), (LANE, F32)])


def _inproj_call(h, mods_l, g, w, layer, tabs, bd, bqn, bkn, cqn, wuq, ckvn, wukvk, wukvv):
    tm = PROJ_TM
    cab, sab, cm, sm = tabs
    tok = lambda s, i: (s, i, 0)
    tab = pl.BlockSpec((tm, LANE), lambda s, i: (i, 0))
    return pl.pallas_call(
        _inproj_kernel,
        out_shape=[jax.ShapeDtypeStruct((NSEG, SEG, wd), dt) for wd, dt in PROJ_OUT],
        grid=(NSEG, SEG // tm),
        in_specs=[
            pl.BlockSpec((None, tm, D_MODEL), tok),
            pl.BlockSpec((None, N_MOD, D_MODEL), lambda s, i: (s, 0, 0)),
            _full((1, D_MODEL)),
            _layer_slab((D_MODEL, PROJ_W), layer),
            tab, tab, tab, tab,
            _full((LANE, LANE)),
            _full((1, LANE)), _full((1, LANE)),
            _full((1, C_Q_LORA)), _layer_slab((C_Q_LORA, 512), layer),
            _full((1, C_KV_LORA)), _layer_slab((C_KV_LORA, 512), layer), _layer_slab((C_KV_LORA, 512), layer),
        ],
        out_specs=[pl.BlockSpec((None, tm, wd), tok) for wd, _ in PROJ_OUT],
        compiler_params=_params(("arbitrary", "arbitrary")),
        name="inproj",
    )(h, mods_l, g.reshape(1, D_MODEL), w, cab, sab, cm, sm, bd,
      bqn, bkn, cqn, wuq, ckvn, wukvk, wukvv)


HEADS_GQA = tuple((g, kv, 0, kv, g, kv) for g in range(2) for kv in range(2))
HEADS_MLA = tuple((h, None, h, h, h // 2, h % 2) for h in range(C_HEADS))


def _normalise(o, half, extra_den=None):
    lane = lax.broadcasted_iota(jnp.int32, o.shape, 1)
    valid = (lane < HEAD_DIM) if half == 0 else (lane >= HEAD_DIM)
    den = pltpu.roll(o, HEAD_DIM, axis=1)
    if extra_den is not None:
        den = den + extra_den
    return o * (1.0 / jnp.where(valid, den, 1.0))


def _masked_q(q, half):
    if half is None:
        return q
    lane = lax.broadcasted_iota(jnp.int32, q.shape, 1)
    keep = (lane < HEAD_DIM) if half == 0 else (lane >= HEAD_DIM)
    return jnp.where(keep, q, jnp.zeros_like(q))


def _store_heads(o_ref, rows, outs):
    lane = lax.broadcasted_iota(jnp.int32, outs[(0, 0)].shape, 1)
    for s in range(2):
        o = jnp.where(lane < HEAD_DIM, outs[(s, 0)], outs[(s, 1)])
        o_ref[rows, s * LANE:(s + 1) * LANE] = o.astype(o_ref.dtype)


ATT_TQ = 512
ATT_SUB = 256
ATT_LOOKAHEAD = 2


def _attn_dense_kernel(q_ref, kc_ref, kx_ref, vc_ref, vx_ref, o_ref, *, heads):
    n_sub = q_ref.shape[0] // ATT_SUB
    stages = [(sub, hd) for sub in range(n_sub) for hd in heads]

    def run(with_x):
        def scores(stage):
            sub, (qs, qhalf, ks) = stage[0], stage[1][:3]
            qh = _masked_q(q_ref[sub * ATT_SUB:(sub + 1) * ATT_SUB, qs * LANE:(qs + 1) * LANE], qhalf)
            ksl = slice(ks * LANE, (ks + 1) * LANE)
            return _dot_nt(qh, kc_ref[:, ksl]), (_dot_nt(qh, kx_ref[:, ksl]) if with_x else None)

        outs = {}
        pending = [scores(st) for st in stages[:ATT_LOOKAHEAD]]
        for i, (sub, (_, _, _, vs, os_, ohalf)) in enumerate(stages):
            s_c, s_x = pending.pop(0)
            if i + ATT_LOOKAHEAD < len(stages):
                pending.append(scores(stages[i + ATT_LOOKAHEAD]))
            vsl = slice(vs * LANE, (vs + 1) * LANE)
            mx = jnp.max(s_c, axis=-1, keepdims=True)
            if with_x:
                mx = jnp.maximum(mx, jnp.max(s_x, axis=-1, keepdims=True))
            o = _dot(jnp.exp2((s_c - mx).astype(BF16)), vc_ref[:, vsl])
            if with_x:
                o = o + _dot(jnp.exp2((s_x - mx).astype(BF16)), vx_ref[:, vsl])
            outs[(os_, ohalf)] = _normalise(o, ohalf)
            if len(outs) == len(heads):
                _store_heads(o_ref, slice(sub * ATT_SUB, (sub + 1) * ATT_SUB), outs)
                outs = {}

    run(kx_ref is not None)


def _attn_ctx_kernel(q_ref, kc_ref, vc_ref, o_ref, *, heads):
    _attn_dense_kernel(q_ref, kc_ref, None, vc_ref, None, o_ref, heads=heads)


def _attn_dense_call(q, k, v, *, heads, name):
    tq = ATT_TQ
    kw, vw = k.shape[-1], v.shape[-1]
    return pl.pallas_call(
        functools.partial(_attn_dense_kernel, heads=heads),
        out_shape=jax.ShapeDtypeStruct((BATCH, SEQ, MIX), BF16),
        grid=(BATCH, SEQ // tq),
        in_specs=[pl.BlockSpec((None, tq, q.shape[-1]), lambda b, j: (b + 1, j, 0)),
                  pl.BlockSpec((None, CTX_LEN, kw), lambda b, j: (0, b, 0)),
                  pl.BlockSpec((None, SEQ, kw), lambda b, j: (b + 1, 0, 0)),
                  pl.BlockSpec((None, CTX_LEN, vw), lambda b, j: (0, b, 0)),
                  pl.BlockSpec((None, SEQ, vw), lambda b, j: (b + 1, 0, 0))],
        out_specs=pl.BlockSpec((None, tq, MIX), lambda b, j: (b, j, 0)),
        compiler_params=_params(("arbitrary", "arbitrary")),
        name=name,
    )(q, k, k, v, v)


def _attn_ctx_call(q, k, v, *, heads, name):
    ctx = lambda wd: pl.BlockSpec((None, CTX_LEN, wd), lambda b: (0, b, 0))
    return pl.pallas_call(
        functools.partial(_attn_ctx_kernel, heads=heads),
        out_shape=jax.ShapeDtypeStruct((BATCH, CTX_LEN, MIX), BF16),
        grid=(BATCH,),
        in_specs=[ctx(q.shape[-1]), ctx(k.shape[-1]), ctx(v.shape[-1])],
        out_specs=pl.BlockSpec((None, CTX_LEN, MIX), lambda b: (b, 0, 0)),
        compiler_params=_params(("arbitrary",)),
        name=name,
    )(q, k, v)


def _attn_specs(q, k, v, tq, with_ctx):
    nq = SEQ // tq
    ncb = CTX_LEN // tq if with_ctx else 0
    if with_ctx:
        qmap = lambda b, j: (jnp.where(j < ncb, 0, b + 1), jnp.where(j < ncb, ncb * b + j, j - ncb), 0)
        out_shape = jax.ShapeDtypeStruct((NSEG, SEG, MIX), BF16)
        omap = qmap
    else:
        qmap = lambda b, j: (b + 1, j, 0)
        out_shape = jax.ShapeDtypeStruct((BATCH, SEG, MIX), BF16)
        omap = lambda b, j: (b, j, 0)
    kw, vw = k.shape[-1], v.shape[-1]
    in_specs = [pl.BlockSpec((None, tq, q.shape[-1]), qmap),
                pl.BlockSpec((None, CTX_LEN, kw), lambda b, j: (0, b, 0)),
                pl.BlockSpec((None, SEQ, kw), lambda b, j: (b + 1, 0, 0)),
                pl.BlockSpec((None, CTX_LEN, vw), lambda b, j: (0, b, 0)),
                pl.BlockSpec((None, SEQ, vw), lambda b, j: (b + 1, 0, 0))]
    return (BATCH, nq + ncb), in_specs, pl.BlockSpec((None, tq, MIX), omap), out_shape, ncb


WIN_TQ = 256


def _attn_win_kernel(sink_ref, q_ref, kc_ref, kx_ref, vc_ref, vx_ref, o_ref, *, n_ctx_blocks):
    j = pl.program_id(1)
    n_sub = WIN_TQ // WINDOW

    n_h = len(HEADS_GQA)
    head_of_row = lax.broadcasted_iota(jnp.int32, (n_h * WINDOW, 1), 0) // WINDOW
    sink = jnp.zeros((n_h * WINDOW, 1), F32)
    for i, (qs, qhalf, _, _, _, _) in enumerate(HEADS_GQA):
        sink = jnp.where(head_of_row == i, sink_ref[qhalf * 2 + qs] * LOG2E, sink)
    n_blk = SEQ // WINDOW
    q_in_blk = lax.broadcasted_iota(jnp.int32, (n_h * WINDOW, WINDOW), 0) % WINDOW
    k_in_blk = lax.broadcasted_iota(jnp.int32, (n_h * WINDOW, WINDOW), 1)
    key_ge_query = k_in_blk >= q_in_blk
    key_le_query = k_in_blk <= q_in_blk

    def run(local):
        def scores(sub):
            rows = slice(sub * WINDOW, (sub + 1) * WINDOW)
            q4 = jnp.concatenate([_masked_q(q_ref[rows, qs * LANE:(qs + 1) * LANE], qhalf)
                                  for qs, qhalf, _, _, _, _ in HEADS_GQA], axis=0)
            s_c = _dot_nt(q4, kc_ref[...])
            if not local:
                return s_c, None, None
            n = (j - n_ctx_blocks) * n_sub + sub
            blocks = [pl.ds(pl.multiple_of(b * WINDOW, WINDOW), WINDOW)
                      for b in (jnp.maximum(n - 1, 0), n, jnp.minimum(n + 1, n_blk - 1))]
            s = _dot_nt(q4, jnp.concatenate([kx_ref[blk, :] for blk in blocks], axis=0))
            s_l = jnp.concatenate([
                jnp.where(jnp.logical_and(key_ge_query, n > 0), s[:, 0:WINDOW], -jnp.inf),
                s[:, WINDOW:2 * WINDOW],
                jnp.where(jnp.logical_and(key_le_query, n < n_blk - 1), s[:, 2 * WINDOW:], -jnp.inf)], axis=1)
            return s_c, s_l, blocks

        all_scores = [scores(sub) for sub in range(n_sub)]
        for sub, (s_c, s_l, blocks) in enumerate(all_scores):
            rows = slice(sub * WINDOW, (sub + 1) * WINDOW)
            mx = jnp.maximum(jnp.max(s_c, axis=-1, keepdims=True), sink)
            if local:
                mx = jnp.maximum(mx, jnp.max(s_l, axis=-1, keepdims=True))
            o = _dot(jnp.exp2((s_c - mx).astype(BF16)), vc_ref[...])
            if local:
                v3 = jnp.concatenate([vx_ref[blk, :] for blk in blocks], axis=0)
                o = o + _dot(jnp.exp2((s_l - mx).astype(BF16)), v3)
            sink_den = jnp.exp2(sink - mx)
            outs = {}
            for i, (_, _, _, vs, os_, ohalf) in enumerate(HEADS_GQA):
                blk = slice(i * WINDOW, (i + 1) * WINDOW)
                outs[(os_, ohalf)] = _normalise(o[blk, vs * LANE:(vs + 1) * LANE], ohalf, sink_den[blk])
            _store_heads(o_ref, rows, outs)

    if n_ctx_blocks:
        @pl.when(j < n_ctx_blocks)
        def _():
            run(False)

        @pl.when(j >= n_ctx_blocks)
        def _():
            run(True)
    else:
        run(True)


def _attn_win_call(sink, q, k, v, *, with_ctx):
    grid, in_specs, out_spec, out_shape, ncb = _attn_specs(q, k, v, WIN_TQ, with_ctx)
    kern = functools.partial(_attn_win_kernel, n_ctx_blocks=ncb)
    return pl.pallas_call(
        kern,
        out_shape=out_shape,
        grid=grid,
        in_specs=[pl.BlockSpec(memory_space=pltpu.SMEM)] + in_specs,
        out_specs=out_spec,
        compiler_params=_params(("arbitrary", "arbitrary")),
        name="attn_win",
    )(sink, q, k, k, v, v)


Q = SSD_CHUNK
PAD = 8
U_CTX = PAD
U_X = PAD + CTX_LEN + PAD
U_ROWS = U_X + SEQ + PAD
N_CTX_CHUNK = CTX_LEN // Q
LOCAL_CHUNKS = 3
SCAN_STEPS = 2


def _split3(a):
    a1 = a.astype(BF16)
    r1 = a - a1.astype(F32)
    a2 = r1.astype(BF16)
    a3 = (r1 - a2.astype(F32)).astype(BF16)
    return a1, a2, a3


def _ssd_kernel(zc_ref, zx_ref, uc_ref, ux_ref, dtc_ref, dtx_ref, cw_ref, cb_ref, dtb_ref, alog_ref,
                dskip_ref, onorm_ref, yc_ref, yx_ref, upad, xs_s, bm_s, cm_s, dt_s, y_s, st_s, cme_s, da_s, h_s):
    n_slab = D_CONV_CH // LANE
    zpad = jnp.zeros((PAD, LANE), F32)
    for j in range(n_slab):
        sl = slice(j * LANE, (j + 1) * LANE)
        upad[j, 0:PAD, :] = zpad
        upad[j, U_CTX:U_CTX + CTX_LEN, :] = uc_ref[:, sl]
        upad[j, U_CTX + CTX_LEN:U_X, :] = zpad
        upad[j, U_X:U_X + SEQ, :] = ux_ref[:, sl]
        upad[j, U_X + SEQ:U_ROWS, :] = zpad

    for c in range(N_CHUNK):
        base = U_CTX + c * Q if c < N_CTX_CHUNK else U_X + (c - N_CTX_CHUNK) * Q
        for j in range(n_slab):
            sl = slice(j * LANE, (j + 1) * LANE)
            dst = (xs_s, bm_s, cm_s)[j // 2]
            acc = jnp.broadcast_to(cb_ref[:, sl], (Q, LANE))
            for k in range(D_CONV):
                lo = base + k - D_CONV // 2
                acc = acc + upad[j, lo:lo + Q, :] * cw_ref[k:k + 1, sl]
            dcol = (j % 2) * LANE
            dst[c * Q:(c + 1) * Q, dcol:dcol + LANE] = (acc * _sigmoid(acc)).astype(dst.dtype)

    def softplus(v):
        return jnp.maximum(v, 0.0) + jnp.log(1.0 + jnp.exp(-jnp.abs(v)))

    dt_s[0:CTX_LEN, :] = softplus(dtc_ref[...] + dtb_ref[...])
    dt_s[CTX_LEN:TOK, :] = softplus(dtx_ref[...] + dtb_ref[...])

    a_neg = -jnp.exp(alog_ref[...]) * LOG2E
    row = lax.broadcasted_iota(jnp.int32, (Q, Q), 0)
    col = lax.broadcasted_iota(jnp.int32, (Q, Q), 1)
    causal = (col <= row, col >= row)
    tri = (causal[0].astype(BF16), causal[1].astype(BF16))
    lo_half = col < HEAD_DIM
    last_row = (Q - 1, 0)

    def bcast_col(v, idx):
        return jnp.broadcast_to(v[:, idx:idx + 1], (Q, Q))

    def chunk_rows(c):
        return pl.ds(c * Q if isinstance(c, int) else pl.multiple_of(c * Q, Q), Q)

    def local_load(c):
        rows = chunk_rows(c)
        groups = [(cm_s[rows, g * Q:(g + 1) * Q], bm_s[rows, g * Q:(g + 1) * Q], xs_s[rows, g * Q:(g + 1) * Q])
                  for g in range(2)]
        return dt_s[rows, :], groups

    def local_sums(dt, groups):
        a1, a2, a3 = _split3(dt * a_neg)
        cs_f = _dot(tri[0], a1) + _dot(tri[0], a2) + _dot(tri[0], a3)
        cs_b = _dot(tri[1], a1) + _dot(tri[1], a2) + _dot(tri[1], a3)
        cs = jnp.where(col < D_HEADS, cs_f, cs_b)
        n_row = 2 * D_HEADS
        return cs, cs.T[0:n_row], dt.T[0:n_row], [_dot_nt(cmg, bmg) for cmg, bmg, _ in groups]

    def local_compute(groups, cs, cs_t, dt_t, cbms):
        ys, sts, das, cmes = [], {}, {}, {}
        for g, (cmg, bmg, xsg) in enumerate(groups):
            cbm = cbms[g]
            cmg32 = cmg.astype(F32)
            bm_t = bmg.astype(F32).T
            xh = (jnp.where(lo_half, xsg, 0.0).astype(BF16), jnp.where(lo_half, 0.0, xsg).astype(BF16))
            y = None
            for d in range(2):
                st = None
                da = []
                for hh in range(2):
                    idx = d * D_HEADS + g * 2 + hh
                    colb = bcast_col(cs, idx)
                    cs_row = cs_t[idx:idx + 1, :]
                    dt_row = dt_t[idx:idx + 1, :]
                    seg = colb - (cs_row - jnp.log2(dt_row))
                    dec = jnp.exp2(jnp.where(causal[d], seg, -jnp.inf))
                    yd = _dot((cbm * dec).astype(BF16), xh[hh])
                    y = yd if y is None else y + yd
                    last = cs_row[:, last_row[d]:last_row[d] + 1]
                    w_row = dt_row * jnp.exp2(last - cs_row)
                    sth = _dot((bm_t * w_row).astype(BF16), xh[hh])
                    st = sth if st is None else st + sth
                    cmes[d * 4 + g * 2 + hh] = (cmg32 * jnp.exp2(colb)).astype(BF16)
                    da.append(jnp.broadcast_to(jnp.exp2(last), (1, Q)))
                sts[d * 2 + g] = st
                das[d * 2 + g] = jnp.where(lo_half[0:1, :], da[0], da[1])
            ys.append(y)
        return ys, sts, das, cmes

    def local_store(c, res):
        ys, sts, das, cmes = res
        rows = chunk_rows(c)
        for g in range(2):
            y_s[rows, g * Q:(g + 1) * Q] = ys[g]
        for k, v in sts.items():
            st_s[c * 4 + k] = v
        for k, v in das.items():
            da_s[c * 4 + k, 0:1, :] = v
        for k, v in cmes.items():
            cme_s[c * 8 + k] = v

    n_iter = N_CHUNK // LOCAL_CHUNKS

    def iter_chunks(i):
        return [i * LOCAL_CHUNKS + u for u in range(LOCAL_CHUNKS)]

    def iter_sums(i):
        return [local_sums(*local_load(c)) for c in iter_chunks(i)]

    def local_body(i, sums):
        nxt = iter_sums(jnp.minimum(i + 1, n_iter - 1))
        chunks = iter_chunks(i)
        results = [local_compute(local_load(c)[1], *sm) for c, sm in zip(chunks, sums)]
        for c, res in zip(chunks, results):
            local_store(c, res)
        return nxt

    lax.fori_loop(0, n_iter, local_body, iter_sums(0))

    h_s[...] = jnp.zeros(h_s.shape, F32)

    def scan_steps(steps):
        work = []
        for c_fwd, c_bwd in steps:
            work += [(d, g, c) for d, c in ((0, c_fwd), (1, c_bwd)) for g in range(2)]
        loaded = []
        for d, g, c in work:
            k = c * 4 + d * 2 + g
            loaded.append((cme_s[2 * k], cme_s[2 * k + 1], da_s[k, 0:1, :], st_s[k],
                           y_s[chunk_rows(c), g * Q:(g + 1) * Q]))
        h = [h_s[k] for k in range(4)]
        ys = []
        for (d, g, c), (cme0, cme1, da, st, y) in zip(work, loaded):
            h_in = h[d * 2 + g]
            hb = h_in.astype(BF16)
            zero = jnp.zeros_like(hb)
            ys.append(y + _dot(cme0, jnp.where(lo_half, hb, zero)) + _dot(cme1, jnp.where(lo_half, zero, hb)))
            h[d * 2 + g] = da * h_in + st
        for (d, g, c), y in zip(work, ys):
            y_s[chunk_rows(c), g * Q:(g + 1) * Q] = y
        for k in range(4):
            h_s[k] = h[k]

    for i in range(N_CTX_CHUNK):
        scan_steps([(i, N_CTX_CHUNK - 1 - i)])

    def scan_body(t, carry):
        i0 = N_CTX_CHUNK + t * SCAN_STEPS
        scan_steps([(i0 + u, N_CHUNK - 1 + N_CTX_CHUNK - (i0 + u)) for u in range(SCAN_STEPS)])
        return carry

    lax.fori_loop(0, (N_CHUNK - N_CTX_CHUNK) // SCAN_STEPS, scan_body, 0)

    dskip = dskip_ref[...]
    onorm = onorm_ref[...]

    def finish(rows, z):
        y = y_s[rows, :] + dskip * xs_s[rows, :]
        return _rms(y * (z * _sigmoid(z)), onorm).astype(BF16)

    for c in range(N_CTX_CHUNK):
        yc_ref[c * Q:(c + 1) * Q, :] = finish(slice(c * Q, (c + 1) * Q), zc_ref[c * Q:(c + 1) * Q, :])

    def fin_body(c, carry):
        r0 = pl.multiple_of(c * Q, Q)
        yx_ref[pl.ds(r0, Q), :] = finish(pl.ds(CTX_LEN + r0, Q), zx_ref[pl.ds(r0, Q), :])
        return carry

    lax.fori_loop(0, SEQ // Q, fin_body, 0, unroll=2)


def _ssd_call(z, xbc, dt, cw, cb, dtb, alog, dskip, onorm):
    def cspec(wd):
        return pl.BlockSpec((None, CTX_LEN, wd), lambda b: (0, b, 0))

    def xspec(wd):
        return pl.BlockSpec((None, SEQ, wd), lambda b: (b + 1, 0, 0))

    return pl.pallas_call(
        _ssd_kernel,
        out_shape=[jax.ShapeDtypeStruct((BATCH, CTX_LEN, MIX), BF16),
                   jax.ShapeDtypeStruct((BATCH, SEQ, MIX), BF16)],
        grid=(BATCH,),
        in_specs=[cspec(MIX), xspec(MIX), cspec(D_CONV_CH), xspec(D_CONV_CH), cspec(LANE), xspec(LANE),
                  _full((8, D_CONV_CH)), _full((1, D_CONV_CH)), _full((1, LANE)), _full((1, LANE)),
                  _full((1, MIX)), _full((1, MIX))],
        out_specs=[pl.BlockSpec((None, CTX_LEN, MIX), lambda b: (b, 0, 0)),
                   pl.BlockSpec((None, SEQ, MIX), lambda b: (b, 0, 0))],
        scratch_shapes=[pltpu.VMEM((D_CONV_CH // LANE, U_ROWS, LANE), F32),
                        pltpu.VMEM((TOK, MIX), F32), pltpu.VMEM((TOK, MIX), BF16), pltpu.VMEM((TOK, MIX), BF16),
                        pltpu.VMEM((TOK, LANE), F32), pltpu.VMEM((TOK, MIX), F32),
                        pltpu.VMEM((N_CHUNK * 4, Q, Q), F32), pltpu.VMEM((N_CHUNK * 8, Q, Q), BF16),
                        pltpu.VMEM((N_CHUNK * 4, 8, LANE), F32), pltpu.VMEM((4, Q, Q), F32)],
        compiler_params=_params(("arbitrary",)),
        name="ssd",
    )(z, z, xbc, xbc, dt, dt, cw, cb, dtb, alog, dskip, onorm)


def _rope_tables():
    rows = SEQ // GRID_W
    r = np.repeat(np.arange(rows, dtype=np.float64), GRID_W)
    c = np.tile(np.arange(GRID_W, dtype=np.float64), rows)

    def tables(rot_dim):
        axis_dim = rot_dim // 2
        inv = ROPE_THETA ** (-np.arange(0, axis_dim, 2, dtype=np.float64) / axis_dim)
        ar = r[:, None] * inv[None, :]
        ac = c[:, None] * inv[None, :]
        cos = np.concatenate([np.cos(ar), np.cos(ar), np.cos(ac), np.cos(ac)], axis=-1)
        sin = np.concatenate([-np.sin(ar), np.sin(ar), -np.sin(ac), np.sin(ac)], axis=-1)
        return cos, sin

    c64, s64 = tables(HEAD_DIM)
    cab = np.tile(c64, (1, 2))
    sab = np.tile(s64, (1, 2))
    c32, s32 = tables(C_ROPE)
    cm = np.concatenate([np.ones((SEQ, C_NOPE)), c32, np.ones((SEQ, 32))], axis=-1)
    sm = np.concatenate([np.zeros((SEQ, C_NOPE)), s32, np.zeros((SEQ, 32))], axis=-1)
    return tuple(jnp.asarray(t, F32) for t in (cab, sab, cm, sm))


def _head_mean_matrix():
    lane = np.arange(LANE)
    same = (lane[:, None] // HEAD_DIM) == (lane[None, :] // HEAD_DIM)
    return jnp.asarray(np.where(same, 1.0 / HEAD_DIM, 0.0), BF16)


def _gqa_order(w, axis):
    shp = w.shape
    w = w.reshape(shp[:axis] + (2, 2, HEAD_DIM) + shp[axis + 1:])
    w = jnp.swapaxes(w, axis, axis + 1)
    return w.reshape(shp)


def _stacked_weights(w_in, c_w_uq, c_w_ukv, w_out):
    n_l, d = w_in.shape[:2]
    zc = lambda n: jnp.zeros((n_l, d, n), BF16)
    wb = w_in.astype(BF16)
    o_c = IN_AB
    o_d = IN_AB + IN_C
    w = jnp.concatenate([
        _gqa_order(wb[..., 0:MIX], 2), wb[..., MIX:512],
        _gqa_order(wb[..., 512:512 + MIX], 2), wb[..., 512 + MIX:o_c + C_Q_LORA + C_KV_LORA],
        zc(C_NOPE), wb[..., o_c + C_Q_LORA + C_KV_LORA:o_d], zc(LANE - C_NOPE - C_ROPE),
        wb[..., o_d:], zc(LANE - 2 * D_HEADS),
    ], axis=2)
    assert w.shape[2] == PROJ_W
    dq = C_NOPE + C_ROPE
    pad_last = lambda t, n: jnp.pad(t, ((0, 0),) * (t.ndim - 1) + ((0, n),))
    wuq = pad_last(c_w_uq.reshape(n_l, C_Q_LORA, C_HEADS, dq), LANE - dq)
    wkv = c_w_ukv.reshape(n_l, C_KV_LORA, C_HEADS, C_NOPE + C_V)
    wk = pad_last(wkv[..., :C_NOPE], LANE - C_NOPE)
    head_parity = (jnp.arange(C_HEADS) % 2)[:, None]
    wv = jnp.stack([jnp.where(head_parity == half, wkv[..., C_NOPE:], 0.0) for half in range(2)], axis=3)
    wout = jnp.concatenate([_gqa_order(w_out[:, 0:MIX], 1), _gqa_order(w_out[:, MIX:2 * MIX], 1),
                            w_out[:, 2 * MIX:]], axis=1)
    return (w, wuq.reshape(n_l, C_Q_LORA, 512).astype(BF16), wk.reshape(n_l, C_KV_LORA, 512).astype(BF16),
            wv.reshape(n_l, C_KV_LORA, 512).astype(BF16), wout.astype(BF16))


def _lane_row(v, width=LANE):
    v = v.reshape(1, -1).astype(F32)
    return jnp.pad(v, ((0, 0), (0, width - v.shape[1])))


def kernel(x, c, ctx, c_ctx, ada_w, ada_b, ffn1_norm, ffn1_wi, ffn1_wo, mix_norm, w_in, w_out, a_sink, b_q_norm, b_k_norm, c_q_norm, c_w_uq, c_kv_norm, c_w_ukv, d_conv_w, d_conv_b, d_a_log, d_dt_bias, d_skip, d_out_norm, ffn2_norm, ffn2_wi, ffn2_wo, final_norm):
    cvec = jnp.concatenate([c_ctx[None, :], c, jnp.zeros((16 - NSEG, D_MODEL), F32)], axis=0)
    mods = _mods_call(cvec, ada_w, ada_b).reshape(DEPTH, 16, N_MOD, D_MODEL)[:, :NSEG]
    tabs = _rope_tables()
    bd = _head_mean_matrix()
    wi1, wo1 = ffn1_wi.astype(BF16), ffn1_wo.astype(BF16)
    wi2, wo2 = ffn2_wi.astype(BF16), ffn2_wo.astype(BF16)
    w, wuq, wk, wv, wout = _stacked_weights(w_in, c_w_uq, c_w_ukv, w_out)

    h = (ctx, x)
    out = None
    for l in range(DEPTH):
        with_ctx = l < DEPTH - 1
        mods_l = mods[l]
        h = _ffn_call(h, mods_l, ffn1_norm[l], wi1, wo1, l, k0=0, seg_off=0, name=f"ffn1_{l}")
        (qa, ka, va, qb, kb, vb, qc, kc, vc, z, xbc, dt) = _inproj_call(
            h, mods_l, mix_norm[l], w, l, tabs, bd,
            jnp.tile(b_q_norm[l], 2).reshape(1, LANE), jnp.tile(b_k_norm[l], 2).reshape(1, LANE),
            c_q_norm[l].reshape(1, C_Q_LORA), wuq, c_kv_norm[l].reshape(1, C_KV_LORA), wk, wv)

        oa = _attn_win_call(a_sink[l], qa, ka, va, with_ctx=with_ctx)
        ob = _attn_dense_call(qb, kb, vb, heads=HEADS_GQA, name=f"attn_b_{l}")
        oc = _attn_dense_call(qc, kc, vc, heads=HEADS_MLA, name=f"attn_c_{l}")
        cw = jnp.pad(d_conv_w[l], ((0, 8 - D_CONV), (0, 0)))
        yc, yx = _ssd_call(z, xbc, dt, cw, d_conv_b[l].reshape(1, D_CONV_CH),
                           _lane_row(d_dt_bias[l]), _lane_row(d_a_log[l]),
                           jnp.repeat(d_skip[l], HEAD_DIM).reshape(1, MIX), d_out_norm[l].reshape(1, MIX))
        if with_ctx:
            ob = (_attn_ctx_call(qb, kb, vb, heads=HEADS_GQA, name=f"attn_b_ctx_{l}"), ob)
            oc = (_attn_ctx_call(qc, kc, vc, heads=HEADS_MLA, name=f"attn_c_ctx_{l}"), oc)
            h = _ffn_call(h, mods_l, ffn2_norm[l], wi2, wo2, l, k0=6, seg_off=0,
                          pre=(oa, ob, oc, (yc, yx), wout), name=f"ffn2_{l}")
        else:
            out = _ffn_call(h, mods_l, ffn2_norm[l], wi2, wo2, l, k0=6, seg_off=1,
                            pre=(oa, ob, oc, yx, wout), final_g=final_norm, name=f"ffn2_{l}")
    return out
```

```python
import functools

import numpy as np
import jax
import jax.numpy as jnp
from jax import lax
from jax.experimental import pallas as pl
from jax.experimental.pallas import tpu as pltpu

D_MODEL = 1024
BATCH = 8
SEQ = 2048
DEPTH = 2
GRID_W = 64
CTX_LEN = 256
HEAD_DIM = 64
ROPE_THETA = 10000.0
EPS = 1e-6
FFN_DIM = 2816
N_MOD = 9
WINDOW = 128
C_HEADS = 4
C_Q_LORA = 256
C_KV_LORA = 128
C_NOPE = 64
C_ROPE = 32
C_V = 64
D_HEADS = 4
D_STATE = 128
D_CONV = 5
SSD_CHUNK = 128
MIX = 256
IN_AB = 1024
IN_C = C_Q_LORA + C_KV_LORA + C_ROPE
D_CONV_CH = MIX + 2 * 2 * D_STATE
IN_D = MIX + D_CONV_CH + 2 * D_HEADS

NSEG = BATCH + 1
SEG = SEQ
assert BATCH * CTX_LEN == SEG
TOK = CTX_LEN + SEQ
N_CHUNK = TOK // SSD_CHUNK

LANE = 128
VMEM_LIMIT = 56 * 1024 * 1024

F32 = jnp.float32
BF16 = jnp.bfloat16
LOG2E = 1.4426950408889634


def _dot(a, b):
    return jnp.dot(a, b, preferred_element_type=F32)


def _dot_nt(a, b):
    return lax.dot_general(a, b, (((1,), (1,)), ((), ())), preferred_element_type=F32)


def _dot_tn(a, b):
    return lax.dot_general(a, b, (((0,), (0,)), ((), ())), preferred_element_type=F32)


def _sigmoid(x):
    return 1.0 / (1.0 + jnp.exp(-x))


def _rms(x, g):
    return x * lax.rsqrt(jnp.mean(x * x, axis=-1, keepdims=True) + EPS) * g


def _full(shape):
    nd = len(shape)
    return pl.BlockSpec(shape, lambda *_: (0,) * nd)


def _layer_slab(shape, layer):
    nd = len(shape)
    return pl.BlockSpec((None,) + tuple(shape), lambda *_: (layer,) + (0,) * nd, pipeline_mode=pl.Buffered(1))


def _params(sem):
    return pltpu.CompilerParams(dimension_semantics=sem, vmem_limit_bytes=VMEM_LIMIT)


def _mods_kernel(c_ref, w_ref, b_ref, o_ref):
    c = c_ref[...]
    s = (c * _sigmoid(c)).astype(BF16)
    o_ref[...] = _dot(s, w_ref[...].astype(BF16)) + b_ref[...]


def _mods_call(cvec, ada_w, ada_b):
    n_l = ada_w.shape[0]
    return pl.pallas_call(
        _mods_kernel,
        out_shape=jax.ShapeDtypeStruct((n_l, 16, N_MOD * D_MODEL), F32),
        grid=(n_l, N_MOD),
        in_specs=[
            pl.BlockSpec((16, D_MODEL), lambda l, n: (0, 0)),
            pl.BlockSpec((None, D_MODEL, D_MODEL), lambda l, n: (l, 0, n)),
            pl.BlockSpec((None, 1, D_MODEL), lambda l, n: (l, 0, n)),
        ],
        out_specs=pl.BlockSpec((None, 16, D_MODEL), lambda l, n: (l, 0, n)),
        compiler_params=_params(("arbitrary", "arbitrary")),
        name="mods",
    )(cvec, ada_w, ada_b.reshape(n_l, 1, N_MOD * D_MODEL))


FFN_TM = 512
FFN_SUB = 256
FFN_FC = 256


def _seg_specs(src, tm, width, seg_off):
    if isinstance(src, tuple):
        assert seg_off == 0
        ctx_arr, x_arr = src
        specs = [pl.BlockSpec((None, tm, width), lambda s, i: (0, jnp.where(s == 0, i, 0), 0)),
                 pl.BlockSpec((None, tm, width), lambda s, i: (jnp.maximum(s - 1, 0), jnp.where(s == 0, 0, i), 0))]
        return specs, [ctx_arr.reshape(1, SEG, width), x_arr]
    return [pl.BlockSpec((None, tm, width), lambda s, i: (s + seg_off, i, 0))], [src]


def _seg_read(refs):
    if len(refs) == 2:
        return jnp.where(pl.program_id(0) == 0, refs[0][...], refs[1][...])
    return refs[0][...]


def _ffn_kernel(*refs, k0, n_h, n_pre, has_final):
    refs = list(refs)
    h_refs = [refs.pop(0) for _ in range(n_h)]
    mods_ref, g_ref, wi_ref, wo_ref = (refs.pop(0) for _ in range(4))
    if n_pre:
        o_refs = [[refs.pop(0) for _ in range(n)] for n in n_pre]
        wout_ref = refs.pop(0)
    if has_final:
        gf_ref = refs.pop(0)
    out_ref, hm_ref = refs

    x = _seg_read(h_refs)
    m = mods_ref[...]
    if n_pre:
        o = None
        for k, o_ref in enumerate(o_refs):
            part = _dot(_seg_read(o_ref), wout_ref[k * MIX:(k + 1) * MIX, :])
            o = part if o is None else o + part
        x = x + m[5:6] * o
    halves = [slice(r * FFN_SUB, (r + 1) * FFN_SUB) for r in range(x.shape[0] // FFN_SUB)]
    xs = [x[rows] for rows in halves]
    nbs = [(_rms(xr, g_ref[...]) * (1.0 + m[k0 + 1:k0 + 2]) + m[k0:k0 + 1]).astype(BF16) for xr in xs]
    for rows, nb in zip(halves, nbs):
        for c in range(FFN_DIM // FFN_FC):
            lo = c * FFN_FC
            a = _dot(nb, wi_ref[:, lo:lo + FFN_FC])
            b = _dot(nb, wi_ref[:, FFN_DIM + lo:FFN_DIM + lo + FFN_FC])
            hm_ref[rows, lo:lo + FFN_FC] = (a * _sigmoid(a) * b).astype(BF16)
    ys = [_dot(hm_ref[rows, :], wo_ref[...]) for rows in halves]
    for rows, xr, y in zip(halves, xs, ys):
        out = xr + 0.5 * m[k0 + 2:k0 + 3] * y
        if has_final:
            out = _rms(out, gf_ref[...])
        out_ref[rows, :] = out


def _ffn_call(h, mods_l, g, wi, wo, layer, *, k0, seg_off, pre=None, final_g=None, name):
    tm = FFN_TM
    nseg = NSEG if isinstance(h, tuple) else h.shape[0] - seg_off
    in_specs, args = _seg_specs(h, tm, D_MODEL, seg_off)
    n_h = len(args)
    in_specs += [
        pl.BlockSpec((None, N_MOD, D_MODEL), lambda s, i: (s + seg_off, 0, 0)),
        _full((1, D_MODEL)),
        _layer_slab((D_MODEL, 2 * FFN_DIM), layer),
        _layer_slab((FFN_DIM, D_MODEL), layer),
    ]
    args += [mods_l, g.reshape(1, D_MODEL), wi, wo]
    n_pre = ()
    if pre is not None:
        *outs, wout = pre
        for o in outs:
            sp, ar = _seg_specs(o, tm, MIX, 0)
            in_specs += sp
            args += ar
            n_pre += (len(ar),)
        in_specs.append(_layer_slab((4 * MIX, D_MODEL), layer))
        args.append(wout)
    if final_g is not None:
        in_specs.append(_full((1, D_MODEL)))
        args.append(final_g.reshape(1, D_MODEL))
    kern = functools.partial(_ffn_kernel, k0=k0, n_h=n_h, n_pre=n_pre, has_final=final_g is not None)
    return pl.pallas_call(
        kern,
        out_shape=jax.ShapeDtypeStruct((nseg, SEG, D_MODEL), F32),
        grid=(nseg, SEG // tm),
        in_specs=in_specs,
        out_specs=pl.BlockSpec((None, tm, D_MODEL), lambda s, i: (s, i, 0)),
        scratch_shapes=[pltpu.VMEM((tm, FFN_DIM), BF16)],
        compiler_params=_params(("arbitrary", "arbitrary")),
        name=name,
    )(*args)


PROJ_TM = 512
COL_AB = 0
COL_C = IN_AB
COL_D = COL_C + 512
PROJ_W = COL_D + MIX + D_CONV_CH + LANE


def _swap_halves(x, half):
    lane = lax.broadcasted_iota(jnp.int32, x.shape, 1)
    first = (lane & half) == 0
    up = pltpu.roll(x, LANE - half, axis=1)
    dn = pltpu.roll(x, half, axis=1)
    return jnp.where(first, up, dn)


def _rope(x, cos, sin, half):
    return x * cos + _swap_halves(x, half) * sin


def _head_rms(x, bd, g):
    sq = x * x
    hi = sq.astype(BF16)
    lo = (sq - hi.astype(F32)).astype(BF16)
    ms = _dot(hi, bd) + _dot(lo, bd)
    return x * lax.rsqrt(ms + EPS) * g


def _inproj_kernel(h_ref, mods_ref, g_ref, w_ref, cab_ref, sab_ref, cm_ref, sm_ref, bd_ref,
                   bqn_ref, bkn_ref, cqn_ref, wuq_ref, ckvn_ref, wukvk_ref, wukvv_ref,
                   qa_ref, ka_ref, va_ref, qb_ref, kb_ref, vb_ref, qc_ref, kc_ref, vc_ref,
                   z_ref, xbc_ref, dt_ref):
    x = h_ref[...]
    m = mods_ref[...]
    n = _rms(x, g_ref[...]) * (1.0 + m[4:5]) + m[3:4]
    nb = n.astype(BF16)
    is_ctx = pl.program_id(0) == 0
    cab, sab = jnp.where(is_ctx, 1.0, cab_ref[...]), jnp.where(is_ctx, 0.0, sab_ref[...])
    cm, sm = jnp.where(is_ctx, 1.0, cm_ref[...]), jnp.where(is_ctx, 0.0, sm_ref[...])
    bd = bd_ref[...]
    scale_ab = HEAD_DIM ** -0.5 * LOG2E
    scale_c = (C_NOPE + C_ROPE) ** -0.5 * LOG2E

    p_ab = [_dot(nb, w_ref[:, COL_AB + mixer * 512:COL_AB + (mixer + 1) * 512]) for mixer in range(2)]
    p_c = _dot(nb, w_ref[:, COL_C:COL_C + 512])
    z_ref[...] = _dot(nb, w_ref[:, COL_D:COL_D + MIX])
    for c in range(D_CONV_CH // 256):
        lo = COL_D + MIX + c * 256
        xbc_ref[:, c * 256:(c + 1) * 256] = _dot(nb, w_ref[:, lo:lo + 256])
    lo = COL_D + MIX + D_CONV_CH
    dt_ref[...] = _dot(nb, w_ref[:, lo:lo + LANE])

    for mixer, (q_ref, k_ref, v_ref) in enumerate(((qa_ref, ka_ref, va_ref), (qb_ref, kb_ref, vb_ref))):
        p = p_ab[mixer]
        for s in range(2):
            q = p[:, s * LANE:(s + 1) * LANE]
            if mixer == 1:
                q = _head_rms(q, bd, bqn_ref[...])
            q = _rope(q, cab, sab, 16) * scale_ab
            q_ref[:, s * LANE:(s + 1) * LANE] = q.astype(BF16)
        k = p[:, 2 * LANE:3 * LANE]
        if mixer == 1:
            k = _head_rms(k, bd, bkn_ref[...])
        k_ref[...] = _rope(k, cab, sab, 16).astype(BF16)
        v = p[:, 3 * LANE:4 * LANE]
        lane = lax.broadcasted_iota(jnp.int32, v.shape, 1)
        v_ref[:, 0:LANE] = jnp.where(lane < HEAD_DIM, v, 1.0).astype(BF16)
        v_ref[:, LANE:2 * LANE] = jnp.where(lane < HEAD_DIM, 1.0, v).astype(BF16)

    p = p_c
    cq = _rms(p[:, 0:C_Q_LORA], cqn_ref[...]).astype(BF16)
    q = _dot(cq, wuq_ref[...])
    ckv = _rms(p[:, C_Q_LORA:C_Q_LORA + C_KV_LORA], ckvn_ref[...]).astype(BF16)
    kn = _dot(ckv, wukvk_ref[...])
    lane = lax.broadcasted_iota(jnp.int32, (1, C_HEADS * LANE), 1)
    own_half = ((lane // HEAD_DIM) % 2) == ((lane // LANE) % 2)
    vc_ref[...] = jnp.where(own_half, _dot(ckv, wukvv_ref[...]), 1.0).astype(BF16)
    kr = _rope(p[:, 3 * LANE:4 * LANE], cm, sm, 8)
    for hh in range(C_HEADS):
        sl = slice(hh * LANE, (hh + 1) * LANE)
        qc_ref[:, sl] = (_rope(q[:, sl], cm, sm, 8) * scale_c).astype(BF16)
        kc_ref[:, sl] = (kn[:, sl] + kr).astype(BF16)


PROJ_OUT = ([(MIX, BF16), (LANE, BF16), (MIX, BF16)] * 2 + [(512, BF16), (512, BF16), (512, BF16)]
            + [(MIX, F32), (D_CONV_CH, F32), (LANE, F32)])


def _inproj_call(h, mods_l, g, w, layer, tabs, bd, bqn, bkn, cqn, wuq, ckvn, wukvk, wukvv):
    tm = PROJ_TM
    cab, sab, cm, sm = tabs
    tok = lambda s, i: (s, i, 0)
    tab = pl.BlockSpec((tm, LANE), lambda s, i: (i, 0))
    return pl.pallas_call(
        _inproj_kernel,
        out_shape=[jax.ShapeDtypeStruct((NSEG, SEG, wd), dt) for wd, dt in PROJ_OUT],
        grid=(NSEG, SEG // tm),
        in_specs=[
            pl.BlockSpec((None, tm, D_MODEL), tok),
            pl.BlockSpec((None, N_MOD, D_MODEL), lambda s, i: (s, 0, 0)),
            _full((1, D_MODEL)),
            _layer_slab((D_MODEL, PROJ_W), layer),
            tab, tab, tab, tab,
            _full((LANE, LANE)),
            _full((1, LANE)), _full((1, LANE)),
            _full((1, C_Q_LORA)), _layer_slab((C_Q_LORA, 512), layer),
            _full((1, C_KV_LORA)), _layer_slab((C_KV_LORA, 512), layer), _layer_slab((C_KV_LORA, 512), layer),
        ],
        out_specs=[pl.BlockSpec((None, tm, wd), tok) for wd, _ in PROJ_OUT],
        compiler_params=_params(("arbitrary", "arbitrary")),
        name="inproj",
    )(h, mods_l, g.reshape(1, D_MODEL), w, cab, sab, cm, sm, bd,
      bqn, bkn, cqn, wuq, ckvn, wukvk, wukvv)


HEADS_GQA = tuple((g, kv, 0, kv, g, kv) for g in range(2) for kv in range(2))
HEADS_MLA = tuple((h, None, h, h, h // 2, h % 2) for h in range(C_HEADS))


def _normalise(o, half, extra_den=None):
    lane = lax.broadcasted_iota(jnp.int32, o.shape, 1)
    valid = (lane < HEAD_DIM) if half == 0 else (lane >= HEAD_DIM)
    den = pltpu.roll(o, HEAD_DIM, axis=1)
    if extra_den is not None:
        den = den + extra_den
    return o * (1.0 / jnp.where(valid, den, 1.0))


def _masked_q(q, half):
    if half is None:
        return q
    lane = lax.broadcasted_iota(jnp.int32, q.shape, 1)
    keep = (lane < HEAD_DIM) if half == 0 else (lane >= HEAD_DIM)
    return jnp.where(keep, q, jnp.zeros_like(q))


def _store_heads(o_ref, rows, outs):
    lane = lax.broadcasted_iota(jnp.int32, outs[(0, 0)].shape, 1)
    for s in range(2):
        o = jnp.where(lane < HEAD_DIM, outs[(s, 0)], outs[(s, 1)])
        o_ref[rows, s * LANE:(s + 1) * LANE] = o.astype(o_ref.dtype)


ATT_TQ = 512
ATT_SUB = 256
ATT_LOOKAHEAD = 2


def _attn_dense_kernel(q_ref, kc_ref, kx_ref, vc_ref, vx_ref, o_ref, *, heads):
    n_sub = q_ref.shape[0] // ATT_SUB
    stages = [(sub, hd) for sub in range(n_sub) for hd in heads]

    def run(with_x):
        def scores(stage):
            sub, (qs, qhalf, ks) = stage[0], stage[1][:3]
            qh = _masked_q(q_ref[sub * ATT_SUB:(sub + 1) * ATT_SUB, qs * LANE:(qs + 1) * LANE], qhalf)
            ksl = slice(ks * LANE, (ks + 1) * LANE)
            return _dot_nt(qh, kc_ref[:, ksl]), (_dot_nt(qh, kx_ref[:, ksl]) if with_x else None)

        outs = {}
        pending = [scores(st) for st in stages[:ATT_LOOKAHEAD]]
        for i, (sub, (_, _, _, vs, os_, ohalf)) in enumerate(stages):
            s_c, s_x = pending.pop(0)
            if i + ATT_LOOKAHEAD < len(stages):
                pending.append(scores(stages[i + ATT_LOOKAHEAD]))
            vsl = slice(vs * LANE, (vs + 1) * LANE)
            mx = jnp.max(s_c, axis=-1, keepdims=True)
            if with_x:
                mx = jnp.maximum(mx, jnp.max(s_x, axis=-1, keepdims=True))
            o = _dot(jnp.exp2((s_c - mx).astype(BF16)), vc_ref[:, vsl])
            if with_x:
                o = o + _dot(jnp.exp2((s_x - mx).astype(BF16)), vx_ref[:, vsl])
            outs[(os_, ohalf)] = _normalise(o, ohalf)
            if len(outs) == len(heads):
                _store_heads(o_ref, slice(sub * ATT_SUB, (sub + 1) * ATT_SUB), outs)
                outs = {}

    run(kx_ref is not None)


def _attn_ctx_kernel(q_ref, kc_ref, vc_ref, o_ref, *, heads):
    _attn_dense_kernel(q_ref, kc_ref, None, vc_ref, None, o_ref, heads=heads)


def _attn_dense_call(q, k, v, *, heads, name):
    tq = ATT_TQ
    kw, vw = k.shape[-1], v.shape[-1]
    return pl.pallas_call(
        functools.partial(_attn_dense_kernel, heads=heads),
        out_shape=jax.ShapeDtypeStruct((BATCH, SEQ, MIX), BF16),
        grid=(BATCH, SEQ // tq),
        in_specs=[pl.BlockSpec((None, tq, q.shape[-1]), lambda b, j: (b + 1, j, 0)),
                  pl.BlockSpec((None, CTX_LEN, kw), lambda b, j: (0, b, 0)),
                  pl.BlockSpec((None, SEQ, kw), lambda b, j: (b + 1, 0, 0)),
                  pl.BlockSpec((None, CTX_LEN, vw), lambda b, j: (0, b, 0)),
                  pl.BlockSpec((None, SEQ, vw), lambda b, j: (b + 1, 0, 0))],
        out_specs=pl.BlockSpec((None, tq, MIX), lambda b, j: (b, j, 0)),
        compiler_params=_params(("arbitrary", "arbitrary")),
        name=name,
    )(q, k, k, v, v)


def _attn_ctx_call(q, k, v, *, heads, name):
    ctx = lambda wd: pl.BlockSpec((None, CTX_LEN, wd), lambda b: (0, b, 0))
    return pl.pallas_call(
        functools.partial(_attn_ctx_kernel, heads=heads),
        out_shape=jax.ShapeDtypeStruct((BATCH, CTX_LEN, MIX), BF16),
        grid=(BATCH,),
        in_specs=[ctx(q.shape[-1]), ctx(k.shape[-1]), ctx(v.shape[-1])],
        out_specs=pl.BlockSpec((None, CTX_LEN, MIX), lambda b: (b, 0, 0)),
        compiler_params=_params(("arbitrary",)),
        name=name,
    )(q, k, v)


def _attn_specs(q, k, v, tq, with_ctx):
    nq = SEQ // tq
    ncb = CTX_LEN // tq if with_ctx else 0
    if with_ctx:
        qmap = lambda b, j: (jnp.where(j < ncb, 0, b + 1), jnp.where(j < ncb, ncb * b + j, j - ncb), 0)
        out_shape = jax.ShapeDtypeStruct((NSEG, SEG, MIX), BF16)
        omap = qmap
    else:
        qmap = lambda b, j: (b + 1, j, 0)
        out_shape = jax.ShapeDtypeStruct((BATCH, SEG, MIX), BF16)
        omap = lambda b, j: (b, j, 0)
    kw, vw = k.shape[-1], v.shape[-1]
    in_specs = [pl.BlockSpec((None, tq, q.shape[-1]), qmap),
                pl.BlockSpec((None, CTX_LEN, kw), lambda b, j: (0, b, 0)),
                pl.BlockSpec((None, SEQ, kw), lambda b, j: (b + 1, 0, 0)),
                pl.BlockSpec((None, CTX_LEN, vw), lambda b, j: (0, b, 0)),
                pl.BlockSpec((None, SEQ, vw), lambda b, j: (b + 1, 0, 0))]
    return (BATCH, nq + ncb), in_specs, pl.BlockSpec((None, tq, MIX), omap), out_shape, ncb


WIN_TQ = 256


def _attn_win_kernel(sink_ref, q_ref, kc_ref, kx_ref, vc_ref, vx_ref, o_ref, *, n_ctx_blocks):
    j = pl.program_id(1)
    n_sub = WIN_TQ // WINDOW

    n_h = len(HEADS_GQA)
    head_of_row = lax.broadcasted_iota(jnp.int32, (n_h * WINDOW, 1), 0) // WINDOW
    sink = jnp.zeros((n_h * WINDOW, 1), F32)
    for i, (qs, qhalf, _, _, _, _) in enumerate(HEADS_GQA):
        sink = jnp.where(head_of_row == i, sink_ref[qhalf * 2 + qs] * LOG2E, sink)
    n_blk = SEQ // WINDOW
    q_in_blk = lax.broadcasted_iota(jnp.int32, (n_h * WINDOW, WINDOW), 0) % WINDOW
    k_in_blk = lax.broadcasted_iota(jnp.int32, (n_h * WINDOW, WINDOW), 1)
    key_ge_query = k_in_blk >= q_in_blk
    key_le_query = k_in_blk <= q_in_blk

    def both_kv(v2):
        lane = lax.broadcasted_iota(jnp.int32, (v2.shape[0], LANE), 1)
        return jnp.where(lane < HEAD_DIM, v2[:, 0:LANE], v2[:, LANE:2 * LANE])

    def run(local):
        def scores(sub):
            rows = slice(sub * WINDOW, (sub + 1) * WINDOW)
            q4 = jnp.concatenate([_masked_q(q_ref[rows, qs * LANE:(qs + 1) * LANE], qhalf)
                                  for qs, qhalf, _, _, _, _ in HEADS_GQA], axis=0)
            s_c = _dot_nt(q4, kc_ref[...])
            if not local:
                return s_c, None, None
            n = (j - n_ctx_blocks) * n_sub + sub
            blocks = [pl.ds(pl.multiple_of(b * WINDOW, WINDOW), WINDOW)
                      for b in (jnp.maximum(n - 1, 0), n, jnp.minimum(n + 1, n_blk - 1))]
            s = _dot_nt(q4, jnp.concatenate([kx_ref[blk, :] for blk in blocks], axis=0))
            s_l = jnp.concatenate([
                jnp.where(jnp.logical_and(key_ge_query, n > 0), s[:, 0:WINDOW], -jnp.inf),
                s[:, WINDOW:2 * WINDOW],
                jnp.where(jnp.logical_and(key_le_query, n < n_blk - 1), s[:, 2 * WINDOW:], -jnp.inf)], axis=1)
            return s_c, s_l, blocks

        all_scores = [scores(sub) for sub in range(n_sub)]
        for sub, (s_c, s_l, blocks) in enumerate(all_scores):
            rows = slice(sub * WINDOW, (sub + 1) * WINDOW)
            mx = jnp.maximum(jnp.max(s_c, axis=-1, keepdims=True), sink)
            if local:
                mx = jnp.maximum(mx, jnp.max(s_l, axis=-1, keepdims=True))
            p_c = jnp.exp2(s_c - mx)
            den = jnp.sum(p_c, axis=-1, keepdims=True) + jnp.exp2(sink - mx)
            o = _dot(p_c.astype(BF16), both_kv(vc_ref[...]))
            if local:
                p_l = jnp.exp2(s_l - mx)
                den = den + jnp.sum(p_l, axis=-1, keepdims=True)
                v3 = jnp.concatenate([both_kv(vx_ref[blk, :]) for blk in blocks], axis=0)
                o = o + _dot(p_l.astype(BF16), v3)
            o = o * (1.0 / den)
            outs = {(os_, ohalf): o[i * WINDOW:(i + 1) * WINDOW]
                    for i, (_, _, _, _, os_, ohalf) in enumerate(HEADS_GQA)}
            _store_heads(o_ref, rows, outs)

    if n_ctx_blocks:
        @pl.when(j < n_ctx_blocks)
        def _():
            run(False)

        @pl.when(j >= n_ctx_blocks)
        def _():
            run(True)
    else:
        run(True)


def _attn_win_call(sink, q, k, v, *, with_ctx):
    grid, in_specs, out_spec, out_shape, ncb = _attn_specs(q, k, v, WIN_TQ, with_ctx)
    kern = functools.partial(_attn_win_kernel, n_ctx_blocks=ncb)
    return pl.pallas_call(
        kern,
        out_shape=out_shape,
        grid=grid,
        in_specs=[pl.BlockSpec(memory_space=pltpu.SMEM)] + in_specs,
        out_specs=out_spec,
        compiler_params=_params(("arbitrary", "arbitrary")),
        name="attn_win",
    )(sink, q, k, k, v, v)


Q = SSD_CHUNK
PAD = 8
U_CTX = PAD
U_X = PAD + CTX_LEN + PAD
U_ROWS = U_X + SEQ + PAD
N_CTX_CHUNK = CTX_LEN // Q
LOCAL_CHUNKS = 3
SCAN_STEPS = 2


def _split3(a):
    a1 = a.astype(BF16)
    r1 = a - a1.astype(F32)
    a2 = r1.astype(BF16)
    a3 = (r1 - a2.astype(F32)).astype(BF16)
    return a1, a2, a3


def _ssd_kernel(zc_ref, zx_ref, uc_ref, ux_ref, dtc_ref, dtx_ref, cw_ref, cb_ref, dtb_ref, alog_ref,
                dskip_ref, onorm_ref, yc_ref, yx_ref, upad, xs_s, bm_s, cm_s, dt_s, y_s, st_s, cme_s, da_s, h_s):
    n_slab = D_CONV_CH // LANE
    zpad = jnp.zeros((PAD, LANE), F32)
    for j in range(n_slab):
        sl = slice(j * LANE, (j + 1) * LANE)
        upad[j, 0:PAD, :] = zpad
        upad[j, U_CTX:U_CTX + CTX_LEN, :] = uc_ref[:, sl]
        upad[j, U_CTX + CTX_LEN:U_X, :] = zpad
        upad[j, U_X:U_X + SEQ, :] = ux_ref[:, sl]
        upad[j, U_X + SEQ:U_ROWS, :] = zpad

    for c in range(N_CHUNK):
        base = U_CTX + c * Q if c < N_CTX_CHUNK else U_X + (c - N_CTX_CHUNK) * Q
        for j in range(n_slab):
            sl = slice(j * LANE, (j + 1) * LANE)
            dst = (xs_s, bm_s, cm_s)[j // 2]
            acc = jnp.broadcast_to(cb_ref[:, sl], (Q, LANE))
            for k in range(D_CONV):
                lo = base + k - D_CONV // 2
                acc = acc + upad[j, lo:lo + Q, :] * cw_ref[k:k + 1, sl]
            dcol = (j % 2) * LANE
            dst[c * Q:(c + 1) * Q, dcol:dcol + LANE] = (acc * _sigmoid(acc)).astype(dst.dtype)

    def softplus(v):
        return jnp.maximum(v, 0.0) + jnp.log(1.0 + jnp.exp(-jnp.abs(v)))

    dt_s[0:CTX_LEN, :] = softplus(dtc_ref[...] + dtb_ref[...])
    dt_s[CTX_LEN:TOK, :] = softplus(dtx_ref[...] + dtb_ref[...])

    a_neg = -jnp.exp(alog_ref[...]) * LOG2E
    row = lax.broadcasted_iota(jnp.int32, (Q, Q), 0)
    col = lax.broadcasted_iota(jnp.int32, (Q, Q), 1)
    causal = (col <= row, col >= row)
    tri = (causal[0].astype(BF16), causal[1].astype(BF16))
    lo_half = col < HEAD_DIM
    last_row = (Q - 1, 0)

    def bcast_col(v, idx):
        return jnp.broadcast_to(v[:, idx:idx + 1], (Q, Q))

    def chunk_rows(c):
        return pl.ds(c * Q if isinstance(c, int) else pl.multiple_of(c * Q, Q), Q)

    def local_load(c):
        rows = chunk_rows(c)
        groups = [(cm_s[rows, g * Q:(g + 1) * Q], bm_s[rows, g * Q:(g + 1) * Q], xs_s[rows, g * Q:(g + 1) * Q])
                  for g in range(2)]
        return dt_s[rows, :], groups

    def local_sums(dt, groups):
        a1, a2, a3 = _split3(dt * a_neg)
        cs_f = _dot(tri[0], a1) + _dot(tri[0], a2) + _dot(tri[0], a3)
        cs_b = _dot(tri[1], a1) + _dot(tri[1], a2) + _dot(tri[1], a3)
        cs = jnp.where(col < D_HEADS, cs_f, cs_b)
        n_row = 2 * D_HEADS
        return cs, cs.T[0:n_row], dt.T[0:n_row], [_dot_nt(cmg, bmg) for cmg, bmg, _ in groups]

    def local_compute(groups, cs, cs_t, dt_t, cbms):
        ys, sts, das, cmes = [], {}, {}, {}
        for g, (cmg, bmg, xsg) in enumerate(groups):
            cbm = cbms[g]
            cmg32 = cmg.astype(F32)
            bm_t = bmg.astype(F32).T
            xh = (jnp.where(lo_half, xsg, 0.0).astype(BF16), jnp.where(lo_half, 0.0, xsg).astype(BF16))
            y = None
            for d in range(2):
                st = None
                da = []
                for hh in range(2):
                    idx = d * D_HEADS + g * 2 + hh
                    colb = bcast_col(cs, idx)
                    cs_row = cs_t[idx:idx + 1, :]
                    dt_row = dt_t[idx:idx + 1, :]
                    seg = colb - (cs_row - jnp.log2(dt_row))
                    dec = jnp.exp2(jnp.where(causal[d], seg, -jnp.inf))
                    yd = _dot((cbm * dec).astype(BF16), xh[hh])
                    y = yd if y is None else y + yd
                    last = cs_row[:, last_row[d]:last_row[d] + 1]
                    w_row = dt_row * jnp.exp2(last - cs_row)
                    sth = _dot((bm_t * w_row).astype(BF16), xh[hh])
                    st = sth if st is None else st + sth
                    cmes[d * 4 + g * 2 + hh] = (cmg32 * jnp.exp2(colb)).astype(BF16)
                    da.append(jnp.broadcast_to(jnp.exp2(last), (1, Q)))
                sts[d * 2 + g] = st
                das[d * 2 + g] = jnp.where(lo_half[0:1, :], da[0], da[1])
            ys.append(y)
        return ys, sts, das, cmes

    def local_store(c, res):
        ys, sts, das, cmes = res
        rows = chunk_rows(c)
        for g in range(2):
            y_s[rows, g * Q:(g + 1) * Q] = ys[g]
        for k, v in sts.items():
            st_s[c * 4 + k] = v
        for k, v in das.items():
            da_s[c * 4 + k, 0:1, :] = v
        for k, v in cmes.items():
            cme_s[c * 8 + k] = v

    n_iter = N_CHUNK // LOCAL_CHUNKS

    def iter_chunks(i):
        return [i * LOCAL_CHUNKS + u for u in range(LOCAL_CHUNKS)]

    def iter_sums(i):
        return [local_sums(*local_load(c)) for c in iter_chunks(i)]

    def local_body(i, sums):
        nxt = iter_sums(jnp.minimum(i + 1, n_iter - 1))
        chunks = iter_chunks(i)
        results = [local_compute(local_load(c)[1], *sm) for c, sm in zip(chunks, sums)]
        for c, res in zip(chunks, results):
            local_store(c, res)
        return nxt

    lax.fori_loop(0, n_iter, local_body, iter_sums(0))

    h_s[...] = jnp.zeros(h_s.shape, F32)

    def scan_steps(steps):
        work = []
        for c_fwd, c_bwd in steps:
            work += [(d, g, c) for d, c in ((0, c_fwd), (1, c_bwd)) for g in range(2)]
        loaded = []
        for d, g, c in work:
            k = c * 4 + d * 2 + g
            loaded.append((cme_s[2 * k], cme_s[2 * k + 1], da_s[k, 0:1, :], st_s[k],
                           y_s[chunk_rows(c), g * Q:(g + 1) * Q]))
        h = [h_s[k] for k in range(4)]
        ys = []
        for (d, g, c), (cme0, cme1, da, st, y) in zip(work, loaded):
            h_in = h[d * 2 + g]
            hb = h_in.astype(BF16)
            zero = jnp.zeros_like(hb)
            ys.append(y + _dot(cme0, jnp.where(lo_half, hb, zero)) + _dot(cme1, jnp.where(lo_half, zero, hb)))
            h[d * 2 + g] = da * h_in + st
        for (d, g, c), y in zip(work, ys):
            y_s[chunk_rows(c), g * Q:(g + 1) * Q] = y
        for k in range(4):
            h_s[k] = h[k]

    for i in range(N_CTX_CHUNK):
        scan_steps([(i, N_CTX_CHUNK - 1 - i)])

    def scan_body(t, carry):
        i0 = N_CTX_CHUNK + t * SCAN_STEPS
        scan_steps([(i0 + u, N_CHUNK - 1 + N_CTX_CHUNK - (i0 + u)) for u in range(SCAN_STEPS)])
        return carry

    lax.fori_loop(0, (N_CHUNK - N_CTX_CHUNK) // SCAN_STEPS, scan_body, 0)

    dskip = dskip_ref[...]
    onorm = onorm_ref[...]

    def finish(rows, z):
        y = y_s[rows, :] + dskip * xs_s[rows, :]
        return _rms(y * (z * _sigmoid(z)), onorm).astype(BF16)

    for c in range(N_CTX_CHUNK):
        yc_ref[c * Q:(c + 1) * Q, :] = finish(slice(c * Q, (c + 1) * Q), zc_ref[c * Q:(c + 1) * Q, :])

    def fin_body(c, carry):
        r0 = pl.multiple_of(c * Q, Q)
        yx_ref[pl.ds(r0, Q), :] = finish(pl.ds(CTX_LEN + r0, Q), zx_ref[pl.ds(r0, Q), :])
        return carry

    lax.fori_loop(0, SEQ // Q, fin_body, 0, unroll=2)


def _ssd_call(z, xbc, dt, cw, cb, dtb, alog, dskip, onorm):
    def cspec(wd):
        return pl.BlockSpec((None, CTX_LEN, wd), lambda b: (0, b, 0))

    def xspec(wd):
        return pl.BlockSpec((None, SEQ, wd), lambda b: (b + 1, 0, 0))

    return pl.pallas_call(
        _ssd_kernel,
        out_shape=[jax.ShapeDtypeStruct((BATCH, CTX_LEN, MIX), BF16),
                   jax.ShapeDtypeStruct((BATCH, SEQ, MIX), BF16)],
        grid=(BATCH,),
        in_specs=[cspec(MIX), xspec(MIX), cspec(D_CONV_CH), xspec(D_CONV_CH), cspec(LANE), xspec(LANE),
                  _full((8, D_CONV_CH)), _full((1, D_CONV_CH)), _full((1, LANE)), _full((1, LANE)),
                  _full((1, MIX)), _full((1, MIX))],
        out_specs=[pl.BlockSpec((None, CTX_LEN, MIX), lambda b: (b, 0, 0)),
                   pl.BlockSpec((None, SEQ, MIX), lambda b: (b, 0, 0))],
        scratch_shapes=[pltpu.VMEM((D_CONV_CH // LANE, U_ROWS, LANE), F32),
                        pltpu.VMEM((TOK, MIX), F32), pltpu.VMEM((TOK, MIX), BF16), pltpu.VMEM((TOK, MIX), BF16),
                        pltpu.VMEM((TOK, LANE), F32), pltpu.VMEM((TOK, MIX), F32),
                        pltpu.VMEM((N_CHUNK * 4, Q, Q), F32), pltpu.VMEM((N_CHUNK * 8, Q, Q), BF16),
                        pltpu.VMEM((N_CHUNK * 4, 8, LANE), F32), pltpu.VMEM((4, Q, Q), F32)],
        compiler_params=_params(("arbitrary",)),
        name="ssd",
    )(z, z, xbc, xbc, dt, dt, cw, cb, dtb, alog, dskip, onorm)


def _rope_tables():
    rows = SEQ // GRID_W
    r = np.repeat(np.arange(rows, dtype=np.float64), GRID_W)
    c = np.tile(np.arange(GRID_W, dtype=np.float64), rows)

    def tables(rot_dim):
        axis_dim = rot_dim // 2
        inv = ROPE_THETA ** (-np.arange(0, axis_dim, 2, dtype=np.float64) / axis_dim)
        ar = r[:, None] * inv[None, :]
        ac = c[:, None] * inv[None, :]
        cos = np.concatenate([np.cos(ar), np.cos(ar), np.cos(ac), np.cos(ac)], axis=-1)
        sin = np.concatenate([-np.sin(ar), np.sin(ar), -np.sin(ac), np.sin(ac)], axis=-1)
        return cos, sin

    c64, s64 = tables(HEAD_DIM)
    cab = np.tile(c64, (1, 2))
    sab = np.tile(s64, (1, 2))
    c32, s32 = tables(C_ROPE)
    cm = np.concatenate([np.ones((SEQ, C_NOPE)), c32, np.ones((SEQ, 32))], axis=-1)
    sm = np.concatenate([np.zeros((SEQ, C_NOPE)), s32, np.zeros((SEQ, 32))], axis=-1)
    return tuple(jnp.asarray(t, F32) for t in (cab, sab, cm, sm))


def _head_mean_matrix():
    lane = np.arange(LANE)
    same = (lane[:, None] // HEAD_DIM) == (lane[None, :] // HEAD_DIM)
    return jnp.asarray(np.where(same, 1.0 / HEAD_DIM, 0.0), BF16)


def _gqa_order(w, axis):
    shp = w.shape
    w = w.reshape(shp[:axis] + (2, 2, HEAD_DIM) + shp[axis + 1:])
    w = jnp.swapaxes(w, axis, axis + 1)
    return w.reshape(shp)


def _stacked_weights(w_in, c_w_uq, c_w_ukv, w_out):
    n_l, d = w_in.shape[:2]
    zc = lambda n: jnp.zeros((n_l, d, n), w_in.dtype)
    wb = w_in
    o_c = IN_AB
    o_d = IN_AB + IN_C
    w = jnp.concatenate([
        _gqa_order(wb[..., 0:MIX], 2), wb[..., MIX:512],
        _gqa_order(wb[..., 512:512 + MIX], 2), wb[..., 512 + MIX:o_c + C_Q_LORA + C_KV_LORA],
        zc(C_NOPE), wb[..., o_c + C_Q_LORA + C_KV_LORA:o_d], zc(LANE - C_NOPE - C_ROPE),
        wb[..., o_d:], zc(LANE - 2 * D_HEADS),
    ], axis=2).astype(BF16)
    assert w.shape[2] == PROJ_W
    dq = C_NOPE + C_ROPE
    pad_last = lambda t, n: jnp.pad(t, ((0, 0),) * (t.ndim - 1) + ((0, n),))
    wuq = pad_last(c_w_uq.reshape(n_l, C_Q_LORA, C_HEADS, dq), LANE - dq)
    wkv = c_w_ukv.reshape(n_l, C_KV_LORA, C_HEADS, C_NOPE + C_V)
    wk = pad_last(wkv[..., :C_NOPE], LANE - C_NOPE)
    head_parity = (jnp.arange(C_HEADS) % 2)[:, None]
    wv = jnp.stack([jnp.where(head_parity == half, wkv[..., C_NOPE:], 0.0) for half in range(2)], axis=3)
    wout = jnp.concatenate([_gqa_order(w_out[:, 0:MIX], 1), _gqa_order(w_out[:, MIX:2 * MIX], 1),
                            w_out[:, 2 * MIX:]], axis=1)
    return (w, wuq.reshape(n_l, C_Q_LORA, 512).astype(BF16), wk.reshape(n_l, C_KV_LORA, 512).astype(BF16),
            wv.reshape(n_l, C_KV_LORA, 512).astype(BF16), wout.astype(BF16))


def _lane_row(v, width=LANE):
    v = v.reshape(1, -1).astype(F32)
    return jnp.pad(v, ((0, 0), (0, width - v.shape[1])))


def kernel(x, c, ctx, c_ctx, ada_w, ada_b, ffn1_norm, ffn1_wi, ffn1_wo, mix_norm, w_in, w_out, a_sink, b_q_norm, b_k_norm, c_q_norm, c_w_uq, c_kv_norm, c_w_ukv, d_conv_w, d_conv_b, d_a_log, d_dt_bias, d_skip, d_out_norm, ffn2_norm, ffn2_wi, ffn2_wo, final_norm):
    cvec = jnp.concatenate([c_ctx[None, :], c, jnp.zeros((16 - NSEG, D_MODEL), F32)], axis=0)
    mods = _mods_call(cvec, ada_w, ada_b).reshape(DEPTH, 16, N_MOD, D_MODEL)[:, :NSEG]
    tabs = _rope_tables()
    bd = _head_mean_matrix()
    wi1, wo1 = ffn1_wi.astype(BF16), ffn1_wo.astype(BF16)
    wi2, wo2 = ffn2_wi.astype(BF16), ffn2_wo.astype(BF16)
    w, wuq, wk, wv, wout = _stacked_weights(w_in, c_w_uq, c_w_ukv, w_out)

    h = (ctx, x)
    out = None
    for l in range(DEPTH):
        with_ctx = l < DEPTH - 1
        mods_l = mods[l]
        h = _ffn_call(h, mods_l, ffn1_norm[l], wi1, wo1, l, k0=0, seg_off=0, name=f"ffn1_{l}")
        (qa, ka, va, qb, kb, vb, qc, kc, vc, z, xbc, dt) = _inproj_call(
            h, mods_l, mix_norm[l], w, l, tabs, bd,
            jnp.tile(b_q_norm[l], 2).reshape(1, LANE), jnp.tile(b_k_norm[l], 2).reshape(1, LANE),
            c_q_norm[l].reshape(1, C_Q_LORA), wuq, c_kv_norm[l].reshape(1, C_KV_LORA), wk, wv)

        oa = _attn_win_call(a_sink[l], qa, ka, va, with_ctx=with_ctx)
        ob = _attn_dense_call(qb, kb, vb, heads=HEADS_GQA, name=f"attn_b_{l}")
        oc = _attn_dense_call(qc, kc, vc, heads=HEADS_MLA, name=f"attn_c_{l}")
        cw = jnp.pad(d_conv_w[l], ((0, 8 - D_CONV), (0, 0)))
        yc, yx = _ssd_call(z, xbc, dt, cw, d_conv_b[l].reshape(1, D_CONV_CH),
                           _lane_row(d_dt_bias[l]), _lane_row(d_a_log[l]),
                           jnp.repeat(d_skip[l], HEAD_DIM).reshape(1, MIX), d_out_norm[l].reshape(1, MIX))
        if with_ctx:
            ob = (_attn_ctx_call(qb, kb, vb, heads=HEADS_GQA, name=f"attn_b_ctx_{l}"), ob)
            oc = (_attn_ctx_call(qc, kc, vc, heads=HEADS_MLA, name=f"attn_c_ctx_{l}"), oc)
            h = _ffn_call(h, mods_l, ffn2_norm[l], wi2, wo2, l, k0=6, seg_off=0,
                          pre=(oa, ob, oc, (yc, yx), wout), name=f"ffn2_{l}")
        else:
            out = _ffn_call(h, mods_l, ffn2_norm[l], wi2, wo2, l, k0=6, seg_off=1,
                            pre=(oa, ob, oc, yx, wout), final_g=final_norm, name=f"ffn2_{l}")
    return out
```

```python
import functools

import numpy as np
import jax
import jax.numpy as jnp
from jax import lax
from jax.experimental import pallas as pl
from jax.experimental.pallas import tpu as pltpu

D_MODEL = 1024
BATCH = 8
SEQ = 2048
DEPTH = 2
GRID_W = 64
CTX_LEN = 256
HEAD_DIM = 64
ROPE_THETA = 10000.0
EPS = 1e-6
FFN_DIM = 2816
N_MOD = 9
WINDOW = 128
C_HEADS = 4
C_Q_LORA = 256
C_KV_LORA = 128
C_NOPE = 64
C_ROPE = 32
C_V = 64
D_HEADS = 4
D_STATE = 128
D_CONV = 5
SSD_CHUNK = 128
MIX = 256
IN_AB = 1024
IN_C = C_Q_LORA + C_KV_LORA + C_ROPE
D_CONV_CH = MIX + 2 * 2 * D_STATE
IN_D = MIX + D_CONV_CH + 2 * D_HEADS

NSEG = BATCH + 1
SEG = SEQ
assert BATCH * CTX_LEN == SEG
TOK = CTX_LEN + SEQ
N_CHUNK = TOK // SSD_CHUNK

LANE = 128
VMEM_LIMIT = 56 * 1024 * 1024

F32 = jnp.float32
BF16 = jnp.bfloat16
LOG2E = 1.4426950408889634


def _dot(a, b):
    return jnp.dot(a, b, preferred_element_type=F32)


def _dot_nt(a, b):
    return lax.dot_general(a, b, (((1,), (1,)), ((), ())), preferred_element_type=F32)


def _dot_tn(a, b):
    return lax.dot_general(a, b, (((0,), (0,)), ((), ())), preferred_element_type=F32)


def _sigmoid(x):
    return 1.0 / (1.0 + jnp.exp(-x))


def _rms(x, g):
    return x * lax.rsqrt(jnp.mean(x * x, axis=-1, keepdims=True) + EPS) * g


def _full(shape):
    nd = len(shape)
    return pl.BlockSpec(shape, lambda *_: (0,) * nd)


def _layer_slab(shape, layer):
    nd = len(shape)
    return pl.BlockSpec((None,) + tuple(shape), lambda *_: (layer,) + (0,) * nd, pipeline_mode=pl.Buffered(1))


def _params(sem):
    return pltpu.CompilerParams(dimension_semantics=sem, vmem_limit_bytes=VMEM_LIMIT)


def _mods_kernel(c_ref, w_ref, b_ref, o_ref):
    c = c_ref[...]
    s = (c * _sigmoid(c)).astype(BF16)
    o_ref[...] = _dot(s, w_ref[...].astype(BF16)) + b_ref[...]


def _mods_call(cvec, ada_w, ada_b):
    n_l = ada_w.shape[0]
    return pl.pallas_call(
        _mods_kernel,
        out_shape=jax.ShapeDtypeStruct((n_l, 16, N_MOD * D_MODEL), F32),
        grid=(n_l, N_MOD),
        in_specs=[
            pl.BlockSpec((16, D_MODEL), lambda l, n: (0, 0)),
            pl.BlockSpec((None, D_MODEL, D_MODEL), lambda l, n: (l, 0, n)),
            pl.BlockSpec((None, 1, D_MODEL), lambda l, n: (l, 0, n)),
        ],
        out_specs=pl.BlockSpec((None, 16, D_MODEL), lambda l, n: (l, 0, n)),
        compiler_params=_params(("arbitrary", "arbitrary")),
        name="mods",
    )(cvec, ada_w, ada_b.reshape(n_l, 1, N_MOD * D_MODEL))


FFN_TM = 512
FFN_SUB = 256
FFN_FC = 256
FFN_WI_ROWS = 128
FFN_WO_ROWS = 256
assert D_MODEL % FFN_WI_ROWS == 0 and FFN_DIM % FFN_WO_ROWS == 0


def _seg_specs(src, tm, width, seg_off):
    if isinstance(src, tuple):
        assert seg_off == 0
        ctx_arr, x_arr = src
        specs = [pl.BlockSpec((None, tm, width), lambda s, i: (0, jnp.where(s == 0, i, 0), 0)),
                 pl.BlockSpec((None, tm, width), lambda s, i: (jnp.maximum(s - 1, 0), jnp.where(s == 0, 0, i), 0))]
        return specs, [ctx_arr.reshape(1, SEG, width), x_arr]
    return [pl.BlockSpec((None, tm, width), lambda s, i: (s + seg_off, i, 0))], [src]


def _seg_read(refs):
    if len(refs) == 2:
        return jnp.where(pl.program_id(0) == 0, refs[0][...], refs[1][...])
    return refs[0][...]


def _fetch_as_bf16(src, dst, stage, sem, rows_per_copy):
    n_copy = src.shape[0] // rows_per_copy

    def copy(k):
        slot = k % 2
        return pltpu.make_async_copy(src.at[pl.ds(k * rows_per_copy, rows_per_copy), :], stage.at[slot],
                                     sem.at[slot])

    copy(0).start()
    for k in range(n_copy):
        if k + 1 < n_copy:
            copy(k + 1).start()
        copy(k).wait()
        dst[k * rows_per_copy:(k + 1) * rows_per_copy, :] = stage[k % 2].astype(BF16)


def _ffn_kernel(*refs, k0, layer, n_h, n_pre, has_final):
    refs = list(refs)
    h_refs = [refs.pop(0) for _ in range(n_h)]
    mods_ref, g_ref, wi_hbm, wo_hbm = (refs.pop(0) for _ in range(4))
    if n_pre:
        o_refs = [[refs.pop(0) for _ in range(n)] for n in n_pre]
        wout_ref = refs.pop(0)
    if has_final:
        gf_ref = refs.pop(0)
    out_ref, hm_ref, wi_ref, wo_ref, stage_i, stage_o, sem_i, sem_o = refs

    @pl.when(jnp.logical_and(pl.program_id(0) == 0, pl.program_id(1) == 0))
    def _():
        _fetch_as_bf16(wi_hbm.at[layer], wi_ref, stage_i, sem_i, FFN_WI_ROWS)
        _fetch_as_bf16(wo_hbm.at[layer], wo_ref, stage_o, sem_o, FFN_WO_ROWS)

    x = _seg_read(h_refs)
    m = mods_ref[...]
    if n_pre:
        o = None
        for k, o_ref in enumerate(o_refs):
            part = _dot(_seg_read(o_ref), wout_ref[k * MIX:(k + 1) * MIX, :])
            o = part if o is None else o + part
        x = x + m[5:6] * o
    halves = [slice(r * FFN_SUB, (r + 1) * FFN_SUB) for r in range(x.shape[0] // FFN_SUB)]
    xs = [x[rows] for rows in halves]
    nbs = [(_rms(xr, g_ref[...]) * (1.0 + m[k0 + 1:k0 + 2]) + m[k0:k0 + 1]).astype(BF16) for xr in xs]
    for rows, nb in zip(halves, nbs):
        for c in range(FFN_DIM // FFN_FC):
            lo = c * FFN_FC
            a = _dot(nb, wi_ref[:, lo:lo + FFN_FC])
            b = _dot(nb, wi_ref[:, FFN_DIM + lo:FFN_DIM + lo + FFN_FC])
            hm_ref[rows, lo:lo + FFN_FC] = (a * _sigmoid(a) * b).astype(BF16)
    ys = [_dot(hm_ref[rows, :], wo_ref[...]) for rows in halves]
    for rows, xr, y in zip(halves, xs, ys):
        out = xr + 0.5 * m[k0 + 2:k0 + 3] * y
        if has_final:
            out = _rms(out, gf_ref[...])
        out_ref[rows, :] = out


def _ffn_call(h, mods_l, g, wi, wo, layer, *, k0, seg_off, pre=None, final_g=None, name):
    tm = FFN_TM
    nseg = NSEG if isinstance(h, tuple) else h.shape[0] - seg_off
    in_specs, args = _seg_specs(h, tm, D_MODEL, seg_off)
    n_h = len(args)
    in_specs += [
        pl.BlockSpec((None, N_MOD, D_MODEL), lambda s, i: (s + seg_off, 0, 0)),
        _full((1, D_MODEL)),
        pl.BlockSpec(memory_space=pl.ANY),
        pl.BlockSpec(memory_space=pl.ANY),
    ]
    args += [mods_l, g.reshape(1, D_MODEL), wi, wo]
    n_pre = ()
    if pre is not None:
        *outs, wout = pre
        for o in outs:
            sp, ar = _seg_specs(o, tm, MIX, 0)
            in_specs += sp
            args += ar
            n_pre += (len(ar),)
        in_specs.append(_layer_slab((4 * MIX, D_MODEL), layer))
        args.append(wout)
    if final_g is not None:
        in_specs.append(_full((1, D_MODEL)))
        args.append(final_g.reshape(1, D_MODEL))
    kern = functools.partial(_ffn_kernel, k0=k0, layer=layer, n_h=n_h, n_pre=n_pre,
                             has_final=final_g is not None)
    return pl.pallas_call(
        kern,
        out_shape=jax.ShapeDtypeStruct((nseg, SEG, D_MODEL), F32),
        grid=(nseg, SEG // tm),
        in_specs=in_specs,
        out_specs=pl.BlockSpec((None, tm, D_MODEL), lambda s, i: (s, i, 0)),
        scratch_shapes=[pltpu.VMEM((tm, FFN_DIM), BF16),
                        pltpu.VMEM((D_MODEL, 2 * FFN_DIM), BF16), pltpu.VMEM((FFN_DIM, D_MODEL), BF16),
                        pltpu.VMEM((2, FFN_WI_ROWS, 2 * FFN_DIM), F32), pltpu.VMEM((2, FFN_WO_ROWS, D_MODEL), F32),
                        pltpu.SemaphoreType.DMA((2,)), pltpu.SemaphoreType.DMA((2,))],
        compiler_params=_params(("arbitrary", "arbitrary")),
        name=name,
    )(*args)


PROJ_TM = 512
PROJ_SUB = PROJ_TM
COL_AB = 0
COL_C = IN_AB
COL_D = COL_C + 512
PROJ_W = COL_D + MIX + D_CONV_CH + LANE


def _swap_halves(x, half):
    lane = lax.broadcasted_iota(jnp.int32, x.shape, 1)
    first = (lane & half) == 0
    up = pltpu.roll(x, LANE - half, axis=1)
    dn = pltpu.roll(x, half, axis=1)
    return jnp.where(first, up, dn)


def _rope(x, cos, sin, half):
    return x * cos + _swap_halves(x, half) * sin


def _head_rms(x, bd, g):
    sq = x * x
    hi = sq.astype(BF16)
    lo = (sq - hi.astype(F32)).astype(BF16)
    ms = _dot(hi, bd) + _dot(lo, bd)
    return x * lax.rsqrt(ms + EPS) * g


def _inproj_kernel(h_ref, mods_ref, g_ref, w_ref, cab_ref, sab_ref, cm_ref, sm_ref, bd_ref,
                   bqn_ref, bkn_ref, cqn_ref, wuq_ref, ckvn_ref, wukvk_ref, wukvv_ref,
                   qa_ref, ka_ref, va_ref, qb_ref, kb_ref, vb_ref, qc_ref, kc_ref, vc_ref,
                   z_ref, xbc_ref, dt_ref):
    m = mods_ref[...]
    is_ctx = pl.program_id(0) == 0
    bd = bd_ref[...]
    scale_ab = HEAD_DIM ** -0.5 * LOG2E
    scale_c = (C_NOPE + C_ROPE) ** -0.5 * LOG2E

    def project(rows):
        nb = (_rms(h_ref[rows, :], g_ref[...]) * (1.0 + m[4:5]) + m[3:4]).astype(BF16)
        p_ab = [_dot(nb, w_ref[:, COL_AB + mixer * 512:COL_AB + (mixer + 1) * 512]) for mixer in range(2)]
        p_c = _dot(nb, w_ref[:, COL_C:COL_C + 512])
        z_ref[rows, :] = _dot(nb, w_ref[:, COL_D:COL_D + MIX])
        for c in range(D_CONV_CH // 256):
            lo = COL_D + MIX + c * 256
            xbc_ref[rows, c * 256:(c + 1) * 256] = _dot(nb, w_ref[:, lo:lo + 256])
        lo = COL_D + MIX + D_CONV_CH
        dt_ref[rows, :] = _dot(nb, w_ref[:, lo:lo + LANE])
        return p_ab, p_c

    def epilogue(rows, p_ab, p_c):
        cab, sab = jnp.where(is_ctx, 1.0, cab_ref[rows, :]), jnp.where(is_ctx, 0.0, sab_ref[rows, :])
        cm, sm = jnp.where(is_ctx, 1.0, cm_ref[rows, :]), jnp.where(is_ctx, 0.0, sm_ref[rows, :])
        for mixer, (q_ref, k_ref, v_ref) in enumerate(((qa_ref, ka_ref, va_ref), (qb_ref, kb_ref, vb_ref))):
            p = p_ab[mixer]
            for s in range(2):
                q = p[:, s * LANE:(s + 1) * LANE]
                if mixer == 1:
                    q = _head_rms(q, bd, bqn_ref[...])
                q = _rope(q, cab, sab, 16) * scale_ab
                q_ref[rows, s * LANE:(s + 1) * LANE] = q.astype(BF16)
            k = p[:, 2 * LANE:3 * LANE]
            if mixer == 1:
                k = _head_rms(k, bd, bkn_ref[...])
            k_ref[rows, :] = _rope(k, cab, sab, 16).astype(BF16)
            v = p[:, 3 * LANE:4 * LANE]
            lane = lax.broadcasted_iota(jnp.int32, v.shape, 1)
            v_ref[rows, 0:LANE] = jnp.where(lane < HEAD_DIM, v, 1.0).astype(BF16)
            v_ref[rows, LANE:2 * LANE] = jnp.where(lane < HEAD_DIM, 1.0, v).astype(BF16)

        p = p_c
        cq = _rms(p[:, 0:C_Q_LORA], cqn_ref[...]).astype(BF16)
        q = _dot(cq, wuq_ref[...])
        ckv = _rms(p[:, C_Q_LORA:C_Q_LORA + C_KV_LORA], ckvn_ref[...]).astype(BF16)
        kn = _dot(ckv, wukvk_ref[...])
        lane = lax.broadcasted_iota(jnp.int32, (1, C_HEADS * LANE), 1)
        own_half = ((lane // HEAD_DIM) % 2) == ((lane // LANE) % 2)
        vc_ref[rows, :] = jnp.where(own_half, _dot(ckv, wukvv_ref[...]), 1.0).astype(BF16)
        kr = _rope(p[:, 3 * LANE:4 * LANE], cm, sm, 8)
        for hh in range(C_HEADS):
            sl = slice(hh * LANE, (hh + 1) * LANE)
            qc_ref[rows, sl] = (_rope(q[:, sl], cm, sm, 8) * scale_c).astype(BF16)
            kc_ref[rows, sl] = (kn[:, sl] + kr).astype(BF16)

    blocks = [slice(r * PROJ_SUB, (r + 1) * PROJ_SUB) for r in range(h_ref.shape[0] // PROJ_SUB)]
    projected = [project(rows) for rows in blocks]
    for rows, (p_ab, p_c) in zip(blocks, projected):
        epilogue(rows, p_ab, p_c)


PROJ_OUT = ([(MIX, BF16), (LANE, BF16), (MIX, BF16)] * 2 + [(512, BF16), (512, BF16), (512, BF16)]
            + [(MIX, F32), (D_CONV_CH, F32), (LANE, F32)])


def _inproj_call(h, mods_l, g, w, layer, tabs, bd, bqn, bkn, cqn, wuq, ckvn, wukvk, wukvv):
    tm = PROJ_TM
    cab, sab, cm, sm = tabs
    tok = lambda s, i: (s, i, 0)
    tab = pl.BlockSpec((tm, LANE), lambda s, i: (i, 0))
    return pl.pallas_call(
        _inproj_kernel,
        out_shape=[jax.ShapeDtypeStruct((NSEG, SEG, wd), dt) for wd, dt in PROJ_OUT],
        grid=(NSEG, SEG // tm),
        in_specs=[
            pl.BlockSpec((None, tm, D_MODEL), tok),
            pl.BlockSpec((None, N_MOD, D_MODEL), lambda s, i: (s, 0, 0)),
            _full((1, D_MODEL)),
            _layer_slab((D_MODEL, PROJ_W), layer),
            tab, tab, tab, tab,
            _full((LANE, LANE)),
            _full((1, LANE)), _full((1, LANE)),
            _full((1, C_Q_LORA)), _layer_slab((C_Q_LORA, 512), layer),
            _full((1, C_KV_LORA)), _layer_slab((C_KV_LORA, 512), layer), _layer_slab((C_KV_LORA, 512), layer),
        ],
        out_specs=[pl.BlockSpec((None, tm, wd), tok) for wd, _ in PROJ_OUT],
        compiler_params=_params(("arbitrary", "arbitrary")),
        name="inproj",
    )(h, mods_l, g.reshape(1, D_MODEL), w, cab, sab, cm, sm, bd,
      bqn, bkn, cqn, wuq, ckvn, wukvk, wukvv)


HEADS_GQA = tuple((g, kv, 0, kv, g, kv) for g in range(2) for kv in range(2))
HEADS_MLA = tuple((h, None, h, h, h // 2, h % 2) for h in range(C_HEADS))


def _normalise(o, half, extra_den=None):
    lane = lax.broadcasted_iota(jnp.int32, o.shape, 1)
    valid = (lane < HEAD_DIM) if half == 0 else (lane >= HEAD_DIM)
    den = pltpu.roll(o, HEAD_DIM, axis=1)
    if extra_den is not None:
        den = den + extra_den
    return o * (1.0 / jnp.where(valid, den, 1.0))


def _masked_q(q, half):
    if half is None:
        return q
    lane = lax.broadcasted_iota(jnp.int32, q.shape, 1)
    keep = (lane < HEAD_DIM) if half == 0 else (lane >= HEAD_DIM)
    return jnp.where(keep, q, jnp.zeros_like(q))


def _store_heads(o_ref, rows, outs):
    lane = lax.broadcasted_iota(jnp.int32, outs[(0, 0)].shape, 1)
    for s in range(2):
        o = jnp.where(lane < HEAD_DIM, outs[(s, 0)], outs[(s, 1)])
        o_ref[rows, s * LANE:(s + 1) * LANE] = o.astype(o_ref.dtype)


ATT_TQ = 1024
ATT_SUB = 256
ATT_LOOKAHEAD = 2


def _attn_dense_kernel(q_ref, kc_ref, kx_ref, vc_ref, vx_ref, o_ref, *, heads):
    n_sub = q_ref.shape[0] // ATT_SUB
    stages = [(sub, hd) for sub in range(n_sub) for hd in heads]

    def run(with_x):
        def scores(stage):
            sub, (qs, qhalf, ks) = stage[0], stage[1][:3]
            qh = _masked_q(q_ref[sub * ATT_SUB:(sub + 1) * ATT_SUB, qs * LANE:(qs + 1) * LANE], qhalf)
            ksl = slice(ks * LANE, (ks + 1) * LANE)
            return _dot_nt(qh, kc_ref[:, ksl]), (_dot_nt(qh, kx_ref[:, ksl]) if with_x else None)

        outs = {}
        pending = [scores(st) for st in stages[:ATT_LOOKAHEAD]]
        for i, (sub, (_, _, _, vs, os_, ohalf)) in enumerate(stages):
            s_c, s_x = pending.pop(0)
            if i + ATT_LOOKAHEAD < len(stages):
                pending.append(scores(stages[i + ATT_LOOKAHEAD]))
            vsl = slice(vs * LANE, (vs + 1) * LANE)
            mx = jnp.max(s_c, axis=-1, keepdims=True)
            if with_x:
                mx = jnp.maximum(mx, jnp.max(s_x, axis=-1, keepdims=True))
            o = _dot(jnp.exp2((s_c - mx).astype(BF16)), vc_ref[:, vsl])
            if with_x:
                o = o + _dot(jnp.exp2((s_x - mx).astype(BF16)), vx_ref[:, vsl])
            outs[(os_, ohalf)] = _normalise(o, ohalf)
            if len(outs) == len(heads):
                _store_heads(o_ref, slice(sub * ATT_SUB, (sub + 1) * ATT_SUB), outs)
                outs = {}

    run(kx_ref is not None)


def _attn_ctx_kernel(q_ref, kc_ref, vc_ref, o_ref, *, heads):
    _attn_dense_kernel(q_ref, kc_ref, None, vc_ref, None, o_ref, heads=heads)


def _attn_dense_call(q, k, v, *, heads, name):
    tq = ATT_TQ
    kw, vw = k.shape[-1], v.shape[-1]
    return pl.pallas_call(
        functools.partial(_attn_dense_kernel, heads=heads),
        out_shape=jax.ShapeDtypeStruct((BATCH, SEQ, MIX), BF16),
        grid=(BATCH, SEQ // tq),
        in_specs=[pl.BlockSpec((None, tq, q.shape[-1]), lambda b, j: (b + 1, j, 0)),
                  pl.BlockSpec((None, CTX_LEN, kw), lambda b, j: (0, b, 0)),
                  pl.BlockSpec((None, SEQ, kw), lambda b, j: (b + 1, 0, 0)),
                  pl.BlockSpec((None, CTX_LEN, vw), lambda b, j: (0, b, 0)),
                  pl.BlockSpec((None, SEQ, vw), lambda b, j: (b + 1, 0, 0))],
        out_specs=pl.BlockSpec((None, tq, MIX), lambda b, j: (b, j, 0)),
        compiler_params=_params(("arbitrary", "arbitrary")),
        name=name,
    )(q, k, k, v, v)


def _attn_ctx_call(q, k, v, *, heads, name):
    ctx = lambda wd: pl.BlockSpec((None, CTX_LEN, wd), lambda b: (0, b, 0))
    return pl.pallas_call(
        functools.partial(_attn_ctx_kernel, heads=heads),
        out_shape=jax.ShapeDtypeStruct((BATCH, CTX_LEN, MIX), BF16),
        grid=(BATCH,),
        in_specs=[ctx(q.shape[-1]), ctx(k.shape[-1]), ctx(v.shape[-1])],
        out_specs=pl.BlockSpec((None, CTX_LEN, MIX), lambda b: (b, 0, 0)),
        compiler_params=_params(("arbitrary",)),
        name=name,
    )(q, k, v)


def _attn_specs(q, k, v, tq, with_ctx):
    nq = SEQ // tq
    ncb = CTX_LEN // tq if with_ctx else 0
    if with_ctx:
        qmap = lambda b, j: (jnp.where(j < ncb, 0, b + 1), jnp.where(j < ncb, ncb * b + j, j - ncb), 0)
        out_shape = jax.ShapeDtypeStruct((NSEG, SEG, MIX), BF16)
        omap = qmap
    else:
        qmap = lambda b, j: (b + 1, j, 0)
        out_shape = jax.ShapeDtypeStruct((BATCH, SEG, MIX), BF16)
        omap = lambda b, j: (b, j, 0)
    kw, vw = k.shape[-1], v.shape[-1]
    in_specs = [pl.BlockSpec((None, tq, q.shape[-1]), qmap),
                pl.BlockSpec((None, CTX_LEN, kw), lambda b, j: (0, b, 0)),
                pl.BlockSpec((None, SEQ, kw), lambda b, j: (b + 1, 0, 0)),
                pl.BlockSpec((None, CTX_LEN, vw), lambda b, j: (0, b, 0)),
                pl.BlockSpec((None, SEQ, vw), lambda b, j: (b + 1, 0, 0))]
    return (BATCH, nq + ncb), in_specs, pl.BlockSpec((None, tq, MIX), omap), out_shape, ncb


WIN_TQ = 256


def _attn_win_kernel(sink_ref, q_ref, kc_ref, kx_ref, vc_ref, vx_ref, o_ref, *, n_ctx_blocks):
    j = pl.program_id(1)
    n_sub = WIN_TQ // WINDOW

    n_h = len(HEADS_GQA)
    head_of_row = lax.broadcasted_iota(jnp.int32, (n_h * WINDOW, 1), 0) // WINDOW
    sink = jnp.zeros((n_h * WINDOW, 1), F32)
    for i, (qs, qhalf, _, _, _, _) in enumerate(HEADS_GQA):
        sink = jnp.where(head_of_row == i, sink_ref[qhalf * 2 + qs] * LOG2E, sink)
    n_blk = SEQ // WINDOW
    q_in_blk = lax.broadcasted_iota(jnp.int32, (n_h * WINDOW, WINDOW), 0) % WINDOW
    k_in_blk = lax.broadcasted_iota(jnp.int32, (n_h * WINDOW, WINDOW), 1)
    key_ge_query = k_in_blk >= q_in_blk
    key_le_query = k_in_blk <= q_in_blk

    def both_kv(v2):
        lane = lax.broadcasted_iota(jnp.int32, (v2.shape[0], LANE), 1)
        return jnp.where(lane < HEAD_DIM, v2[:, 0:LANE], v2[:, LANE:2 * LANE])

    def run(local):
        def scores(sub):
            rows = slice(sub * WINDOW, (sub + 1) * WINDOW)
            q4 = jnp.concatenate([_masked_q(q_ref[rows, qs * LANE:(qs + 1) * LANE], qhalf)
                                  for qs, qhalf, _, _, _, _ in HEADS_GQA], axis=0)
            s_c = _dot_nt(q4, kc_ref[...])
            if not local:
                return s_c, None, None
            n = (j - n_ctx_blocks) * n_sub + sub
            blocks = [pl.ds(pl.multiple_of(b * WINDOW, WINDOW), WINDOW)
                      for b in (jnp.maximum(n - 1, 0), n, jnp.minimum(n + 1, n_blk - 1))]
            s = _dot_nt(q4, jnp.concatenate([kx_ref[blk, :] for blk in blocks], axis=0))
            s_l = jnp.concatenate([
                jnp.where(jnp.logical_and(key_ge_query, n > 0), s[:, 0:WINDOW], -jnp.inf),
                s[:, WINDOW:2 * WINDOW],
                jnp.where(jnp.logical_and(key_le_query, n < n_blk - 1), s[:, 2 * WINDOW:], -jnp.inf)], axis=1)
            return s_c, s_l, blocks

        all_scores = [scores(sub) for sub in range(n_sub)]
        for sub, (s_c, s_l, blocks) in enumerate(all_scores):
            rows = slice(sub * WINDOW, (sub + 1) * WINDOW)
            mx = jnp.maximum(jnp.max(s_c, axis=-1, keepdims=True), sink)
            if local:
                mx = jnp.maximum(mx, jnp.max(s_l, axis=-1, keepdims=True))
            p_c = jnp.exp2(s_c - mx)
            den = jnp.sum(p_c, axis=-1, keepdims=True) + jnp.exp2(sink - mx)
            o = _dot(p_c.astype(BF16), both_kv(vc_ref[...]))
            if local:
                p_l = jnp.exp2(s_l - mx)
                den = den + jnp.sum(p_l, axis=-1, keepdims=True)
                v3 = jnp.concatenate([both_kv(vx_ref[blk, :]) for blk in blocks], axis=0)
                o = o + _dot(p_l.astype(BF16), v3)
            o = o * (1.0 / den)
            outs = {(os_, ohalf): o[i * WINDOW:(i + 1) * WINDOW]
                    for i, (_, _, _, _, os_, ohalf) in enumerate(HEADS_GQA)}
            _store_heads(o_ref, rows, outs)

    if n_ctx_blocks:
        @pl.when(j < n_ctx_blocks)
        def _():
            run(False)

        @pl.when(j >= n_ctx_blocks)
        def _():
            run(True)
    else:
        run(True)


def _attn_win_call(sink, q, k, v, *, with_ctx):
    grid, in_specs, out_spec, out_shape, ncb = _attn_specs(q, k, v, WIN_TQ, with_ctx)
    kern = functools.partial(_attn_win_kernel, n_ctx_blocks=ncb)
    return pl.pallas_call(
        kern,
        out_shape=out_shape,
        grid=grid,
        in_specs=[pl.BlockSpec(memory_space=pltpu.SMEM)] + in_specs,
        out_specs=out_spec,
        compiler_params=_params(("arbitrary", "arbitrary")),
        name="attn_win",
    )(sink, q, k, k, v, v)


Q = SSD_CHUNK
PAD = 8
U_CTX = PAD
U_X = PAD + CTX_LEN + PAD
U_ROWS = U_X + SEQ + PAD
N_CTX_CHUNK = CTX_LEN // Q
LOCAL_CHUNKS = 3
SCAN_STEPS = 4


def _split3(a):
    a1 = a.astype(BF16)
    r1 = a - a1.astype(F32)
    a2 = r1.astype(BF16)
    a3 = (r1 - a2.astype(F32)).astype(BF16)
    return a1, a2, a3


def _ssd_kernel(zc_ref, zx_ref, uc_ref, ux_ref, dtc_ref, dtx_ref, cw_ref, cb_ref, dtb_ref, alog_ref,
                dskip_ref, onorm_ref, yc_ref, yx_ref, upad, xs_s, bm_s, cm_s, dt_s, y_s, st_s, cme_s, da_s, h_s):
    n_slab = D_CONV_CH // LANE
    zpad = jnp.zeros((PAD, LANE), F32)
    for j in range(n_slab):
        sl = slice(j * LANE, (j + 1) * LANE)
        upad[j, 0:PAD, :] = zpad
        upad[j, U_CTX:U_CTX + CTX_LEN, :] = uc_ref[:, sl]
        upad[j, U_CTX + CTX_LEN:U_X, :] = zpad
        upad[j, U_X:U_X + SEQ, :] = ux_ref[:, sl]
        upad[j, U_X + SEQ:U_ROWS, :] = zpad

    for c in range(N_CHUNK):
        base = U_CTX + c * Q if c < N_CTX_CHUNK else U_X + (c - N_CTX_CHUNK) * Q
        for j in range(n_slab):
            sl = slice(j * LANE, (j + 1) * LANE)
            dst = (xs_s, bm_s, cm_s)[j // 2]
            acc = jnp.broadcast_to(cb_ref[:, sl], (Q, LANE))
            for k in range(D_CONV):
                lo = base + k - D_CONV // 2
                acc = acc + upad[j, lo:lo + Q, :] * cw_ref[k:k + 1, sl]
            dcol = (j % 2) * LANE
            dst[c * Q:(c + 1) * Q, dcol:dcol + LANE] = (acc * _sigmoid(acc)).astype(dst.dtype)

    def softplus(v):
        return jnp.maximum(v, 0.0) + jnp.log(1.0 + jnp.exp(-jnp.abs(v)))

    dt_s[0:CTX_LEN, :] = softplus(dtc_ref[...] + dtb_ref[...])
    dt_s[CTX_LEN:TOK, :] = softplus(dtx_ref[...] + dtb_ref[...])

    a_neg = -jnp.exp(alog_ref[...]) * LOG2E
    row = lax.broadcasted_iota(jnp.int32, (Q, Q), 0)
    col = lax.broadcasted_iota(jnp.int32, (Q, Q), 1)
    causal = (col <= row, col >= row)
    tri = (causal[0].astype(BF16), causal[1].astype(BF16))
    lo_half = col < HEAD_DIM
    last_row = (Q - 1, 0)

    def bcast_col(v, idx):
        return jnp.broadcast_to(v[:, idx:idx + 1], (Q, Q))

    def chunk_rows(c):
        return pl.ds(c * Q if isinstance(c, int) else pl.multiple_of(c * Q, Q), Q)

    def local_load(c):
        rows = chunk_rows(c)
        groups = [(cm_s[rows, g * Q:(g + 1) * Q], bm_s[rows, g * Q:(g + 1) * Q], xs_s[rows, g * Q:(g + 1) * Q])
                  for g in range(2)]
        return dt_s[rows, :], groups

    def local_sums(dt, groups):
        a1, a2, a3 = _split3(dt * a_neg)
        cs_f = _dot(tri[0], a1) + _dot(tri[0], a2) + _dot(tri[0], a3)
        cs_b = _dot(tri[1], a1) + _dot(tri[1], a2) + _dot(tri[1], a3)
        cs = jnp.where(col < D_HEADS, cs_f, cs_b)
        n_row = 2 * D_HEADS
        return cs, cs.T[0:n_row], dt.T[0:n_row], [_dot_nt(cmg, bmg) for cmg, bmg, _ in groups]

    def local_compute(groups, cs, cs_t, dt_t, cbms):
        ys, sts, das, cmes = [], {}, {}, {}
        for g, (cmg, bmg, xsg) in enumerate(groups):
            cbm = cbms[g]
            cmg32 = cmg.astype(F32)
            bm_t = bmg.astype(F32).T
            xh = (jnp.where(lo_half, xsg, 0.0).astype(BF16), jnp.where(lo_half, 0.0, xsg).astype(BF16))
            y = None
            for d in range(2):
                st = None
                da = []
                for hh in range(2):
                    idx = d * D_HEADS + g * 2 + hh
                    colb = bcast_col(cs, idx)
                    cs_row = cs_t[idx:idx + 1, :]
                    dt_row = dt_t[idx:idx + 1, :]
                    seg = colb - (cs_row - jnp.log2(dt_row))
                    dec = jnp.exp2(jnp.where(causal[d], seg, -jnp.inf))
                    yd = _dot((cbm * dec).astype(BF16), xh[hh])
                    y = yd if y is None else y + yd
                    last = cs_row[:, last_row[d]:last_row[d] + 1]
                    w_row = dt_row * jnp.exp2(last - cs_row)
                    sth = _dot((bm_t * w_row).astype(BF16), xh[hh])
                    st = sth if st is None else st + sth
                    cmes[d * 4 + g * 2 + hh] = (cmg32 * jnp.exp2(colb)).astype(BF16)
                    da.append(jnp.broadcast_to(jnp.exp2(last), (1, Q)))
                sts[d * 2 + g] = st
                das[d * 2 + g] = jnp.where(lo_half[0:1, :], da[0], da[1])
            ys.append(y)
        return ys, sts, das, cmes

    def local_store(c, res):
        ys, sts, das, cmes = res
        rows = chunk_rows(c)
        for g in range(2):
            y_s[rows, g * Q:(g + 1) * Q] = ys[g]
        for k, v in sts.items():
            st_s[c * 4 + k] = v
        for k, v in das.items():
            da_s[c * 4 + k, 0:1, :] = v
        for k, v in cmes.items():
            cme_s[c * 8 + k] = v

    n_iter = N_CHUNK // LOCAL_CHUNKS

    def iter_chunks(i):
        return [i * LOCAL_CHUNKS + u for u in range(LOCAL_CHUNKS)]

    def iter_sums(i):
        return [local_sums(*local_load(c)) for c in iter_chunks(i)]

    def local_body(i, sums):
        nxt = iter_sums(jnp.minimum(i + 1, n_iter - 1))
        chunks = iter_chunks(i)
        results = [local_compute(local_load(c)[1], *sm) for c, sm in zip(chunks, sums)]
        for c, res in zip(chunks, results):
            local_store(c, res)
        return nxt

    lax.fori_loop(0, n_iter, local_body, iter_sums(0))

    h_s[...] = jnp.zeros(h_s.shape, F32)

    def scan_steps(steps):
        work = []
        for c_fwd, c_bwd in steps:
            work += [(d, g, c) for d, c in ((0, c_fwd), (1, c_bwd)) for g in range(2)]
        loaded = []
        for d, g, c in work:
            k = c * 4 + d * 2 + g
            loaded.append((cme_s[2 * k], cme_s[2 * k + 1], da_s[k, 0:1, :], st_s[k],
                           y_s[chunk_rows(c), g * Q:(g + 1) * Q]))
        h = [h_s[k] for k in range(4)]
        ys = []
        for (d, g, c), (cme0, cme1, da, st, y) in zip(work, loaded):
            h_in = h[d * 2 + g]
            hb = h_in.astype(BF16)
            zero = jnp.zeros_like(hb)
            ys.append(y + _dot(cme0, jnp.where(lo_half, hb, zero)) + _dot(cme1, jnp.where(lo_half, zero, hb)))
            h[d * 2 + g] = da * h_in + st
        for (d, g, c), y in zip(work, ys):
            y_s[chunk_rows(c), g * Q:(g + 1) * Q] = y
        for k in range(4):
            h_s[k] = h[k]

    for i in range(N_CTX_CHUNK):
        scan_steps([(i, N_CTX_CHUNK - 1 - i)])

    def scan_body(t, carry):
        i0 = N_CTX_CHUNK + t * SCAN_STEPS
        scan_steps([(i0 + u, N_CHUNK - 1 + N_CTX_CHUNK - (i0 + u)) for u in range(SCAN_STEPS)])
        return carry

    lax.fori_loop(0, (N_CHUNK - N_CTX_CHUNK) // SCAN_STEPS, scan_body, 0)

    dskip = dskip_ref[...]
    onorm = onorm_ref[...]

    def finish(rows, z):
        y = y_s[rows, :] + dskip * xs_s[rows, :]
        return _rms(y * (z * _sigmoid(z)), onorm).astype(BF16)

    for c in range(N_CTX_CHUNK):
        yc_ref[c * Q:(c + 1) * Q, :] = finish(slice(c * Q, (c + 1) * Q), zc_ref[c * Q:(c + 1) * Q, :])

    def fin_body(c, carry):
        r0 = pl.multiple_of(c * Q, Q)
        yx_ref[pl.ds(r0, Q), :] = finish(pl.ds(CTX_LEN + r0, Q), zx_ref[pl.ds(r0, Q), :])
        return carry

    lax.fori_loop(0, SEQ // Q, fin_body, 0, unroll=2)


def _ssd_call(z, xbc, dt, cw, cb, dtb, alog, dskip, onorm):
    def cspec(wd):
        return pl.BlockSpec((None, CTX_LEN, wd), lambda b: (0, b, 0))

    def xspec(wd):
        return pl.BlockSpec((None, SEQ, wd), lambda b: (b + 1, 0, 0))

    return pl.pallas_call(
        _ssd_kernel,
        out_shape=[jax.ShapeDtypeStruct((BATCH, CTX_LEN, MIX), BF16),
                   jax.ShapeDtypeStruct((BATCH, SEQ, MIX), BF16)],
        grid=(BATCH,),
        in_specs=[cspec(MIX), xspec(MIX), cspec(D_CONV_CH), xspec(D_CONV_CH), cspec(LANE), xspec(LANE),
                  _full((8, D_CONV_CH)), _full((1, D_CONV_CH)), _full((1, LANE)), _full((1, LANE)),
                  _full((1, MIX)), _full((1, MIX))],
        out_specs=[pl.BlockSpec((None, CTX_LEN, MIX), lambda b: (b, 0, 0)),
                   pl.BlockSpec((None, SEQ, MIX), lambda b: (b, 0, 0))],
        scratch_shapes=[pltpu.VMEM((D_CONV_CH // LANE, U_ROWS, LANE), F32),
                        pltpu.VMEM((TOK, MIX), F32), pltpu.VMEM((TOK, MIX), BF16), pltpu.VMEM((TOK, MIX), BF16),
                        pltpu.VMEM((TOK, LANE), F32), pltpu.VMEM((TOK, MIX), F32),
                        pltpu.VMEM((N_CHUNK * 4, Q, Q), F32), pltpu.VMEM((N_CHUNK * 8, Q, Q), BF16),
                        pltpu.VMEM((N_CHUNK * 4, 8, LANE), F32), pltpu.VMEM((4, Q, Q), F32)],
        compiler_params=_params(("arbitrary",)),
        name="ssd",
    )(z, z, xbc, xbc, dt, dt, cw, cb, dtb, alog, dskip, onorm)


def _rope_tables():
    rows = SEQ // GRID_W
    r = np.repeat(np.arange(rows, dtype=np.float64), GRID_W)
    c = np.tile(np.arange(GRID_W, dtype=np.float64), rows)

    def tables(rot_dim):
        axis_dim = rot_dim // 2
        inv = ROPE_THETA ** (-np.arange(0, axis_dim, 2, dtype=np.float64) / axis_dim)
        ar = r[:, None] * inv[None, :]
        ac = c[:, None] * inv[None, :]
        cos = np.concatenate([np.cos(ar), np.cos(ar), np.cos(ac), np.cos(ac)], axis=-1)
        sin = np.concatenate([-np.sin(ar), np.sin(ar), -np.sin(ac), np.sin(ac)], axis=-1)
        return cos, sin

    c64, s64 = tables(HEAD_DIM)
    cab = np.tile(c64, (1, 2))
    sab = np.tile(s64, (1, 2))
    c32, s32 = tables(C_ROPE)
    cm = np.concatenate([np.ones((SEQ, C_NOPE)), c32, np.ones((SEQ, 32))], axis=-1)
    sm = np.concatenate([np.zeros((SEQ, C_NOPE)), s32, np.zeros((SEQ, 32))], axis=-1)
    return tuple(jnp.asarray(t, F32) for t in (cab, sab, cm, sm))


def _head_mean_matrix():
    lane = np.arange(LANE)
    same = (lane[:, None] // HEAD_DIM) == (lane[None, :] // HEAD_DIM)
    return jnp.asarray(np.where(same, 1.0 / HEAD_DIM, 0.0), BF16)


def _gqa_order(w, axis):
    shp = w.shape
    w = w.reshape(shp[:axis] + (2, 2, HEAD_DIM) + shp[axis + 1:])
    w = jnp.swapaxes(w, axis, axis + 1)
    return w.reshape(shp)


def _stacked_weights(w_in, c_w_uq, c_w_ukv, w_out):
    n_l, d = w_in.shape[:2]
    zc = lambda n: jnp.zeros((n_l, d, n), w_in.dtype)
    wb = w_in
    o_c = IN_AB
    o_d = IN_AB + IN_C
    w = jnp.concatenate([
        _gqa_order(wb[..., 0:MIX], 2), wb[..., MIX:512],
        _gqa_order(wb[..., 512:512 + MIX], 2), wb[..., 512 + MIX:o_c + C_Q_LORA + C_KV_LORA],
        zc(C_NOPE), wb[..., o_c + C_Q_LORA + C_KV_LORA:o_d], zc(LANE - C_NOPE - C_ROPE),
        wb[..., o_d:], zc(LANE - 2 * D_HEADS),
    ], axis=2).astype(BF16)
    assert w.shape[2] == PROJ_W
    dq = C_NOPE + C_ROPE
    pad_last = lambda t, n: jnp.pad(t, ((0, 0),) * (t.ndim - 1) + ((0, n),))
    wuq = pad_last(c_w_uq.reshape(n_l, C_Q_LORA, C_HEADS, dq), LANE - dq)
    wkv = c_w_ukv.reshape(n_l, C_KV_LORA, C_HEADS, C_NOPE + C_V)
    wk = pad_last(wkv[..., :C_NOPE], LANE - C_NOPE)
    head_parity = (jnp.arange(C_HEADS) % 2)[:, None]
    wv = jnp.stack([jnp.where(head_parity == half, wkv[..., C_NOPE:], 0.0) for half in range(2)], axis=3)
    wout = jnp.concatenate([_gqa_order(w_out[:, 0:MIX], 1), _gqa_order(w_out[:, MIX:2 * MIX], 1),
                            w_out[:, 2 * MIX:]], axis=1)
    return (w, wuq.reshape(n_l, C_Q_LORA, 512).astype(BF16), wk.reshape(n_l, C_KV_LORA, 512).astype(BF16),
            wv.reshape(n_l, C_KV_LORA, 512).astype(BF16), wout.astype(BF16))


def _lane_row(v, width=LANE):
    v = v.reshape(1, -1).astype(F32)
    return jnp.pad(v, ((0, 0), (0, width - v.shape[1])))


def kernel(x, c, ctx, c_ctx, ada_w, ada_b, ffn1_norm, ffn1_wi, ffn1_wo, mix_norm, w_in, w_out, a_sink, b_q_norm, b_k_norm, c_q_norm, c_w_uq, c_kv_norm, c_w_ukv, d_conv_w, d_conv_b, d_a_log, d_dt_bias, d_skip, d_out_norm, ffn2_norm, ffn2_wi, ffn2_wo, final_norm):
    cvec = jnp.concatenate([c_ctx[None, :], c, jnp.zeros((16 - NSEG, D_MODEL), F32)], axis=0)
    mods = _mods_call(cvec, ada_w, ada_b).reshape(DEPTH, 16, N_MOD, D_MODEL)[:, :NSEG]
    tabs = _rope_tables()
    bd = _head_mean_matrix()
    wi1, wo1, wi2, wo2 = ffn1_wi, ffn1_wo, ffn2_wi, ffn2_wo
    w, wuq, wk, wv, wout = _stacked_weights(w_in, c_w_uq, c_w_ukv, w_out)

    h = (ctx, x)
    out = None
    for l in range(DEPTH):
        with_ctx = l < DEPTH - 1
        mods_l = mods[l]
        h = _ffn_call(h, mods_l, ffn1_norm[l], wi1, wo1, l, k0=0, seg_off=0, name=f"ffn1_{l}")
        (qa, ka, va, qb, kb, vb, qc, kc, vc, z, xbc, dt) = _inproj_call(
            h, mods_l, mix_norm[l], w, l, tabs, bd,
            jnp.tile(b_q_norm[l], 2).reshape(1, LANE), jnp.tile(b_k_norm[l], 2).reshape(1, LANE),
            c_q_norm[l].reshape(1, C_Q_LORA), wuq, c_kv_norm[l].reshape(1, C_KV_LORA), wk, wv)

        oa = _attn_win_call(a_sink[l], qa, ka, va, with_ctx=with_ctx)
        ob = _attn_dense_call(qb, kb, vb, heads=HEADS_GQA, name=f"attn_b_{l}")
        oc = _attn_dense_call(qc, kc, vc, heads=HEADS_MLA, name=f"attn_c_{l}")
        cw = jnp.pad(d_conv_w[l], ((0, 8 - D_CONV), (0, 0)))
        yc, yx = _ssd_call(z, xbc, dt, cw, d_conv_b[l].reshape(1, D_CONV_CH),
                           _lane_row(d_dt_bias[l]), _lane_row(d_a_log[l]),
                           jnp.repeat(d_skip[l], HEAD_DIM).reshape(1, MIX), d_out_norm[l].reshape(1, MIX))
        if with_ctx:
            ob = (_attn_ctx_call(qb, kb, vb, heads=HEADS_GQA, name=f"attn_b_ctx_{l}"), ob)
            oc = (_attn_ctx_call(qc, kc, vc, heads=HEADS_MLA, name=f"attn_c_ctx_{l}"), oc)
            h = _ffn_call(h, mods_l, ffn2_norm[l], wi2, wo2, l, k0=6, seg_off=0,
                          pre=(oa, ob, oc, (yc, yx), wout), name=f"ffn2_{l}")
        else:
            out = _ffn_call(h, mods_l, ffn2_norm[l], wi2, wo2, l, k0=6, seg_off=1,
                            pre=(oa, ob, oc, yx, wout), final_g=final_norm, name=f"ffn2_{l}")
    return out
```

```python
import functools

import numpy as np
import jax
import jax.numpy as jnp
from jax import lax
from jax.experimental import pallas as pl
from jax.experimental.pallas import tpu as pltpu

D_MODEL = 1024
BATCH = 8
SEQ = 2048
DEPTH = 2
GRID_W = 64
CTX_LEN = 256
HEAD_DIM = 64
ROPE_THETA = 10000.0
EPS = 1e-6
FFN_DIM = 2816
N_MOD = 9
WINDOW = 128
C_HEADS = 4
C_Q_LORA = 256
C_KV_LORA = 128
C_NOPE = 64
C_ROPE = 32
C_V = 64
D_HEADS = 4
D_STATE = 128
D_CONV = 5
SSD_CHUNK = 128
MIX = 256
IN_AB = 1024
IN_C = C_Q_LORA + C_KV_LORA + C_ROPE
D_CONV_CH = MIX + 2 * 2 * D_STATE
IN_D = MIX + D_CONV_CH + 2 * D_HEADS

NSEG = BATCH + 1
SEG = SEQ
assert BATCH * CTX_LEN == SEG
TOK = CTX_LEN + SEQ
N_CHUNK = TOK // SSD_CHUNK

LANE = 128
VMEM_LIMIT = 56 * 1024 * 1024

F32 = jnp.float32
BF16 = jnp.bfloat16
LOG2E = 1.4426950408889634


def _dot(a, b):
    return jnp.dot(a, b, preferred_element_type=F32)


def _dot_nt(a, b):
    return lax.dot_general(a, b, (((1,), (1,)), ((), ())), preferred_element_type=F32)


def _sigmoid(x):
    return 1.0 / (1.0 + jnp.exp(-x))


def _rms(x, g):
    return x * lax.rsqrt(jnp.mean(x * x, axis=-1, keepdims=True) + EPS) * g


def _full(shape):
    nd = len(shape)
    return pl.BlockSpec(shape, lambda *_: (0,) * nd)


def _layer_slab(shape, layer):
    nd = len(shape)
    return pl.BlockSpec((None,) + tuple(shape), lambda *_: (layer,) + (0,) * nd, pipeline_mode=pl.Buffered(1))


def _params(sem):
    return pltpu.CompilerParams(dimension_semantics=sem, vmem_limit_bytes=VMEM_LIMIT)


MODS_BN = 3 * D_MODEL


def _mods_kernel(c_ref, w_ref, b_ref, o_ref):
    c = c_ref[...]
    s = (c * _sigmoid(c)).astype(BF16)
    o_ref[...] = _dot(s, w_ref[...].astype(BF16)) + b_ref[...]


def _mods_call(cvec, ada_w, ada_b):
    n_l = ada_w.shape[0]
    bn = MODS_BN
    return pl.pallas_call(
        _mods_kernel,
        out_shape=jax.ShapeDtypeStruct((n_l, 16, N_MOD * D_MODEL), F32),
        grid=(n_l, N_MOD * D_MODEL // bn),
        in_specs=[
            pl.BlockSpec((16, D_MODEL), lambda l, n: (0, 0)),
            pl.BlockSpec((None, D_MODEL, bn), lambda l, n: (l, 0, n)),
            pl.BlockSpec((None, 1, bn), lambda l, n: (l, 0, n)),
        ],
        out_specs=pl.BlockSpec((None, 16, bn), lambda l, n: (l, 0, n)),
        compiler_params=_params(("arbitrary", "arbitrary")),
        name="mods",
    )(cvec, ada_w, ada_b.reshape(n_l, 1, N_MOD * D_MODEL))


FFN_TM = 512
FFN_SUB = 256
FFN_FC = 256
FFN_WI_ROWS = 128
FFN_WO_ROWS = 256
assert D_MODEL % FFN_WI_ROWS == 0 and FFN_DIM % FFN_WO_ROWS == 0


def _seg_specs(src, tm, width, seg_off):
    if isinstance(src, tuple):
        assert seg_off == 0
        ctx_arr, x_arr = src
        specs = [pl.BlockSpec((None, tm, width), lambda s, i: (0, jnp.where(s == 0, i, 0), 0)),
                 pl.BlockSpec((None, tm, width), lambda s, i: (jnp.maximum(s - 1, 0), jnp.where(s == 0, 0, i), 0))]
        return specs, [ctx_arr.reshape(1, SEG, width), x_arr]
    return [pl.BlockSpec((None, tm, width), lambda s, i: (s + seg_off, i, 0))], [src]


def _seg_read(refs):
    if len(refs) == 2:
        return jnp.where(pl.program_id(0) == 0, refs[0][...], refs[1][...])
    return refs[0][...]


def _fetch_as_bf16(src, dst, stage, sem, rows_per_copy):
    n_copy = src.shape[0] // rows_per_copy

    def copy(k):
        slot = k % 2
        return pltpu.make_async_copy(src.at[pl.ds(k * rows_per_copy, rows_per_copy), :], stage.at[slot],
                                     sem.at[slot])

    copy(0).start()
    for k in range(n_copy):
        if k + 1 < n_copy:
            copy(k + 1).start()
        copy(k).wait()
        dst[k * rows_per_copy:(k + 1) * rows_per_copy, :] = stage[k % 2].astype(BF16)


def _ffn_kernel(*refs, k0, layer, n_h, n_pre, has_final):
    refs = list(refs)
    h_refs = [refs.pop(0) for _ in range(n_h)]
    mods_ref, g_ref, wi_hbm, wo_hbm = (refs.pop(0) for _ in range(4))
    if n_pre:
        o_refs = [[refs.pop(0) for _ in range(n)] for n in n_pre]
        wout_ref = refs.pop(0)
    if has_final:
        gf_ref = refs.pop(0)
    out_ref, hm_ref, wi_ref, wo_ref, stage_i, stage_o, sem_i, sem_o = refs

    @pl.when(jnp.logical_and(pl.program_id(0) == 0, pl.program_id(1) == 0))
    def _():
        _fetch_as_bf16(wi_hbm.at[layer], wi_ref, stage_i, sem_i, FFN_WI_ROWS)
        _fetch_as_bf16(wo_hbm.at[layer], wo_ref, stage_o, sem_o, FFN_WO_ROWS)

    x = _seg_read(h_refs)
    m = mods_ref[...]
    if n_pre:
        o = None
        for k, o_ref in enumerate(o_refs):
            part = _dot(_seg_read(o_ref), wout_ref[k * MIX:(k + 1) * MIX, :])
            o = part if o is None else o + part
        x = x + m[5:6] * o
    halves = [slice(r * FFN_SUB, (r + 1) * FFN_SUB) for r in range(x.shape[0] // FFN_SUB)]
    xs = [x[rows] for rows in halves]
    nbs = [(_rms(xr, g_ref[...]) * (1.0 + m[k0 + 1:k0 + 2]) + m[k0:k0 + 1]).astype(BF16) for xr in xs]
    for rows, nb in zip(halves, nbs):
        for c in range(FFN_DIM // FFN_FC):
            lo = c * FFN_FC
            a = _dot(nb, wi_ref[:, lo:lo + FFN_FC])
            b = _dot(nb, wi_ref[:, FFN_DIM + lo:FFN_DIM + lo + FFN_FC])
            hm_ref[rows, lo:lo + FFN_FC] = (a * _sigmoid(a) * b).astype(BF16)
    ys = [_dot(hm_ref[rows, :], wo_ref[...]) for rows in halves]
    for rows, xr, y in zip(halves, xs, ys):
        out = xr + 0.5 * m[k0 + 2:k0 + 3] * y
        if has_final:
            out = _rms(out, gf_ref[...])
        out_ref[rows, :] = out


def _ffn_call(h, mods_l, g, wi, wo, layer, *, k0, seg_off, pre=None, final_g=None, name):
    tm = FFN_TM
    nseg = NSEG if isinstance(h, tuple) else h.shape[0] - seg_off
    in_specs, args = _seg_specs(h, tm, D_MODEL, seg_off)
    n_h = len(args)
    in_specs += [
        pl.BlockSpec((None, N_MOD, D_MODEL), lambda s, i: (s + seg_off, 0, 0)),
        _full((1, D_MODEL)),
        pl.BlockSpec(memory_space=pl.ANY),
        pl.BlockSpec(memory_space=pl.ANY),
    ]
    args += [mods_l, g.reshape(1, D_MODEL), wi, wo]
    n_pre = ()
    if pre is not None:
        *outs, wout = pre
        for o in outs:
            sp, ar = _seg_specs(o, tm, MIX, 0)
            in_specs += sp
            args += ar
            n_pre += (len(ar),)
        in_specs.append(_layer_slab((4 * MIX, D_MODEL), layer))
        args.append(wout)
    if final_g is not None:
        in_specs.append(_full((1, D_MODEL)))
        args.append(final_g.reshape(1, D_MODEL))
    kern = functools.partial(_ffn_kernel, k0=k0, layer=layer, n_h=n_h, n_pre=n_pre,
                             has_final=final_g is not None)
    return pl.pallas_call(
        kern,
        out_shape=jax.ShapeDtypeStruct((nseg, SEG, D_MODEL), F32),
        grid=(nseg, SEG // tm),
        in_specs=in_specs,
        out_specs=pl.BlockSpec((None, tm, D_MODEL), lambda s, i: (s, i, 0)),
        scratch_shapes=[pltpu.VMEM((tm, FFN_DIM), BF16),
                        pltpu.VMEM((D_MODEL, 2 * FFN_DIM), BF16), pltpu.VMEM((FFN_DIM, D_MODEL), BF16),
                        pltpu.VMEM((2, FFN_WI_ROWS, 2 * FFN_DIM), F32), pltpu.VMEM((2, FFN_WO_ROWS, D_MODEL), F32),
                        pltpu.SemaphoreType.DMA((2,)), pltpu.SemaphoreType.DMA((2,))],
        compiler_params=_params(("arbitrary", "arbitrary")),
        name=name,
    )(*args)


PROJ_TM = 512
PROJ_SUB = PROJ_TM
COL_AB = 0
COL_C = IN_AB
COL_D = COL_C + 512
PROJ_W = COL_D + MIX + D_CONV_CH + LANE


def _swap_halves(x, half):
    lane = lax.broadcasted_iota(jnp.int32, x.shape, 1)
    first = (lane & half) == 0
    up = pltpu.roll(x, LANE - half, axis=1)
    dn = pltpu.roll(x, half, axis=1)
    return jnp.where(first, up, dn)


def _rope(x, cos, sin, half):
    return x * cos + _swap_halves(x, half) * sin


def _head_rms(x, bd, g):
    sq = x * x
    hi = sq.astype(BF16)
    lo = (sq - hi.astype(F32)).astype(BF16)
    ms = _dot(hi, bd) + _dot(lo, bd)
    return x * lax.rsqrt(ms + EPS) * g


def _inproj_kernel(h_ref, mods_ref, g_ref, w_ref, cab_ref, sab_ref, cm_ref, sm_ref, bd_ref,
                   bqn_ref, bkn_ref, cqn_ref, wuq_ref, ckvn_ref, wukvk_ref, wukvv_ref,
                   qa_ref, ka_ref, va_ref, qb_ref, kb_ref, vb_ref, qc_ref, kc_ref, vc_ref,
                   z_ref, xbc_ref, dt_ref):
    m = mods_ref[...]
    is_ctx = pl.program_id(0) == 0
    bd = bd_ref[...]
    scale_ab = HEAD_DIM ** -0.5 * LOG2E
    scale_c = (C_NOPE + C_ROPE) ** -0.5 * LOG2E

    def project(rows):
        nb = (_rms(h_ref[rows, :], g_ref[...]) * (1.0 + m[4:5]) + m[3:4]).astype(BF16)
        p_ab = [_dot(nb, w_ref[:, COL_AB + mixer * 512:COL_AB + (mixer + 1) * 512]) for mixer in range(2)]
        p_c = _dot(nb, w_ref[:, COL_C:COL_C + 512])
        z_ref[rows, :] = _dot(nb, w_ref[:, COL_D:COL_D + MIX])
        for c in range(D_CONV_CH // 256):
            lo = COL_D + MIX + c * 256
            xbc_ref[rows, c * 256:(c + 1) * 256] = _dot(nb, w_ref[:, lo:lo + 256])
        lo = COL_D + MIX + D_CONV_CH
        dt_ref[rows, :] = _dot(nb, w_ref[:, lo:lo + LANE])
        return p_ab, p_c

    def epilogue(rows, p_ab, p_c):
        cab, sab = jnp.where(is_ctx, 1.0, cab_ref[rows, :]), jnp.where(is_ctx, 0.0, sab_ref[rows, :])
        cm, sm = jnp.where(is_ctx, 1.0, cm_ref[rows, :]), jnp.where(is_ctx, 0.0, sm_ref[rows, :])
        for mixer, (q_ref, k_ref, v_ref) in enumerate(((qa_ref, ka_ref, va_ref), (qb_ref, kb_ref, vb_ref))):
            p = p_ab[mixer]
            for s in range(2):
                q = p[:, s * LANE:(s + 1) * LANE]
                if mixer == 1:
                    q = _head_rms(q, bd, bqn_ref[...])
                q = _rope(q, cab, sab, 16) * scale_ab
                q_ref[rows, s * LANE:(s + 1) * LANE] = q.astype(BF16)
            k = p[:, 2 * LANE:3 * LANE]
            if mixer == 1:
                k = _head_rms(k, bd, bkn_ref[...])
            k_ref[rows, :] = _rope(k, cab, sab, 16).astype(BF16)
            v = p[:, 3 * LANE:4 * LANE]
            lane = lax.broadcasted_iota(jnp.int32, v.shape, 1)
            v_ref[rows, 0:LANE] = jnp.where(lane < HEAD_DIM, v, 1.0).astype(BF16)
            v_ref[rows, LANE:2 * LANE] = jnp.where(lane < HEAD_DIM, 1.0, v).astype(BF16)

        p = p_c
        cq = _rms(p[:, 0:C_Q_LORA], cqn_ref[...]).astype(BF16)
        q = _dot(cq, wuq_ref[...])
        ckv = _rms(p[:, C_Q_LORA:C_Q_LORA + C_KV_LORA], ckvn_ref[...]).astype(BF16)
        kn = _dot(ckv, wukvk_ref[...])
        lane = lax.broadcasted_iota(jnp.int32, (1, C_HEADS * LANE), 1)
        own_half = ((lane // HEAD_DIM) % 2) == ((lane // LANE) % 2)
        vc_ref[rows, :] = jnp.where(own_half, _dot(ckv, wukvv_ref[...]), 1.0).astype(BF16)
        kr = _rope(p[:, 3 * LANE:4 * LANE], cm, sm, 8)
        for hh in range(C_HEADS):
            sl = slice(hh * LANE, (hh + 1) * LANE)
            qc_ref[rows, sl] = (_rope(q[:, sl], cm, sm, 8) * scale_c).astype(BF16)
            kc_ref[rows, sl] = (kn[:, sl] + kr).astype(BF16)

    blocks = [slice(r * PROJ_SUB, (r + 1) * PROJ_SUB) for r in range(h_ref.shape[0] // PROJ_SUB)]
    projected = [project(rows) for rows in blocks]
    for rows, (p_ab, p_c) in zip(blocks, projected):
        epilogue(rows, p_ab, p_c)


PROJ_OUT = ([(MIX, BF16), (LANE, BF16), (MIX, BF16)] * 2 + [(512, BF16), (512, BF16), (512, BF16)]
            + [(MIX, F32), (D_CONV_CH, F32), (LANE, F32)])


def _inproj_call(h, mods_l, g, w, layer, tabs, bd, bqn, bkn, cqn, wuq, ckvn, wukvk, wukvv):
    tm = PROJ_TM
    cab, sab, cm, sm = tabs
    tok = lambda s, i: (s, i, 0)
    tab = pl.BlockSpec((tm, LANE), lambda s, i: (i, 0))
    return pl.pallas_call(
        _inproj_kernel,
        out_shape=[jax.ShapeDtypeStruct((NSEG, SEG, wd), dt) for wd, dt in PROJ_OUT],
        grid=(NSEG, SEG // tm),
        in_specs=[
            pl.BlockSpec((None, tm, D_MODEL), tok),
            pl.BlockSpec((None, N_MOD, D_MODEL), lambda s, i: (s, 0, 0)),
            _full((1, D_MODEL)),
            _layer_slab((D_MODEL, PROJ_W), layer),
            tab, tab, tab, tab,
            _full((LANE, LANE)),
            _full((1, LANE)), _full((1, LANE)),
            _full((1, C_Q_LORA)), _layer_slab((C_Q_LORA, 512), layer),
            _full((1, C_KV_LORA)), _layer_slab((C_KV_LORA, 512), layer), _layer_slab((C_KV_LORA, 512), layer),
        ],
        out_specs=[pl.BlockSpec((None, tm, wd), tok) for wd, _ in PROJ_OUT],
        compiler_params=_params(("arbitrary", "arbitrary")),
        name="inproj",
    )(h, mods_l, g.reshape(1, D_MODEL), w, cab, sab, cm, sm, bd,
      bqn, bkn, cqn, wuq, ckvn, wukvk, wukvv)


HEADS_GQA = tuple((g, kv, 0, kv, g, kv) for g in range(2) for kv in range(2))
HEADS_MLA = tuple((h, None, h, h, h // 2, h % 2) for h in range(C_HEADS))


def _normalise(o, half, extra_den=None):
    lane = lax.broadcasted_iota(jnp.int32, o.shape, 1)
    valid = (lane < HEAD_DIM) if half == 0 else (lane >= HEAD_DIM)
    den = pltpu.roll(o, HEAD_DIM, axis=1)
    if extra_den is not None:
        den = den + extra_den
    return o * (1.0 / jnp.where(valid, den, 1.0))


def _masked_q(q, half):
    if half is None:
        return q
    lane = lax.broadcasted_iota(jnp.int32, q.shape, 1)
    keep = (lane < HEAD_DIM) if half == 0 else (lane >= HEAD_DIM)
    return jnp.where(keep, q, jnp.zeros_like(q))


def _store_heads(o_ref, rows, outs):
    lane = lax.broadcasted_iota(jnp.int32, outs[(0, 0)].shape, 1)
    for s in range(2):
        o = jnp.where(lane < HEAD_DIM, outs[(s, 0)], outs[(s, 1)])
        o_ref[rows, s * LANE:(s + 1) * LANE] = o.astype(o_ref.dtype)


ATT_TQ = 1024
ATT_SUB = 256
ATT_LOOKAHEAD = 2


def _attn_dense_kernel(q_ref, kc_ref, kx_ref, vc_ref, vx_ref, o_ref, *, heads):
    n_sub = q_ref.shape[0] // ATT_SUB
    stages = [(sub, hd) for sub in range(n_sub) for hd in heads]

    def run(with_x):
        def scores(stage):
            sub, (qs, qhalf, ks) = stage[0], stage[1][:3]
            qh = _masked_q(q_ref[sub * ATT_SUB:(sub + 1) * ATT_SUB, qs * LANE:(qs + 1) * LANE], qhalf)
            ksl = slice(ks * LANE, (ks + 1) * LANE)
            return _dot_nt(qh, kc_ref[:, ksl]), (_dot_nt(qh, kx_ref[:, ksl]) if with_x else None)

        outs = {}
        pending = [scores(st) for st in stages[:ATT_LOOKAHEAD]]
        for i, (sub, (_, _, _, vs, os_, ohalf)) in enumerate(stages):
            s_c, s_x = pending.pop(0)
            if i + ATT_LOOKAHEAD < len(stages):
                pending.append(scores(stages[i + ATT_LOOKAHEAD]))
            vsl = slice(vs * LANE, (vs + 1) * LANE)
            mx = jnp.max(s_c, axis=-1, keepdims=True)
            if with_x:
                mx = jnp.maximum(mx, jnp.max(s_x, axis=-1, keepdims=True))
            o = _dot(jnp.exp2((s_c - mx).astype(BF16)), vc_ref[:, vsl])
            if with_x:
                o = o + _dot(jnp.exp2((s_x - mx).astype(BF16)), vx_ref[:, vsl])
            outs[(os_, ohalf)] = _normalise(o, ohalf)
            if len(outs) == len(heads):
                _store_heads(o_ref, slice(sub * ATT_SUB, (sub + 1) * ATT_SUB), outs)
                outs = {}

    run(kx_ref is not None)


def _attn_ctx_kernel(q_ref, kc_ref, vc_ref, o_ref, *, heads):
    _attn_dense_kernel(q_ref, kc_ref, None, vc_ref, None, o_ref, heads=heads)


def _attn_latent_call(kern, q, k, v, *, tq, name, smem=()):
    kw, vw = k.shape[-1], v.shape[-1]
    return pl.pallas_call(
        kern,
        out_shape=jax.ShapeDtypeStruct((BATCH, SEQ, MIX), BF16),
        grid=(BATCH, SEQ // tq),
        in_specs=[pl.BlockSpec(memory_space=pltpu.SMEM)] * len(smem) + [
            pl.BlockSpec((None, tq, q.shape[-1]), lambda b, j: (b + 1, j, 0)),
            pl.BlockSpec((None, CTX_LEN, kw), lambda b, j: (0, b, 0)),
            pl.BlockSpec((None, SEQ, kw), lambda b, j: (b + 1, 0, 0)),
            pl.BlockSpec((None, CTX_LEN, vw), lambda b, j: (0, b, 0)),
            pl.BlockSpec((None, SEQ, vw), lambda b, j: (b + 1, 0, 0))],
        out_specs=pl.BlockSpec((None, tq, MIX), lambda b, j: (b, j, 0)),
        compiler_params=_params(("arbitrary", "arbitrary")),
        name=name,
    )(*smem, q, k, k, v, v)


def _attn_ctx_call(kern, q, k, v, *, name, smem=()):
    ctx = lambda wd: pl.BlockSpec((None, CTX_LEN, wd), lambda b: (0, b, 0))
    return pl.pallas_call(
        kern,
        out_shape=jax.ShapeDtypeStruct((BATCH, CTX_LEN, MIX), BF16),
        grid=(BATCH,),
        in_specs=[pl.BlockSpec(memory_space=pltpu.SMEM)] * len(smem)
        + [ctx(q.shape[-1]), ctx(k.shape[-1]), ctx(v.shape[-1])],
        out_specs=pl.BlockSpec((None, CTX_LEN, MIX), lambda b: (b, 0, 0)),
        compiler_params=_params(("arbitrary",)),
        name=name,
    )(*smem, q, k, v)


def _attn_dense(q, k, v, *, heads, with_ctx, name):
    ox = _attn_latent_call(functools.partial(_attn_dense_kernel, heads=heads), q, k, v, tq=ATT_TQ, name=name)
    if not with_ctx:
        return ox
    return _attn_ctx_call(functools.partial(_attn_ctx_kernel, heads=heads), q, k, v, name=name + "_ctx"), ox


WIN_TQ = 512


def _attn_win_kernel(sink_ref, q_ref, kc_ref, kx_ref, vc_ref, vx_ref, o_ref):
    local = kx_ref is not None
    n_sub = q_ref.shape[0] // WINDOW

    n_h = len(HEADS_GQA)
    head_of_row = lax.broadcasted_iota(jnp.int32, (n_h * WINDOW, 1), 0) // WINDOW
    sink = jnp.zeros((n_h * WINDOW, 1), F32)
    for i, (qs, qhalf, _, _, _, _) in enumerate(HEADS_GQA):
        sink = jnp.where(head_of_row == i, sink_ref[qhalf * 2 + qs] * LOG2E, sink)
    n_blk = SEQ // WINDOW
    q_in_blk = lax.broadcasted_iota(jnp.int32, (n_h * WINDOW, WINDOW), 0) % WINDOW
    k_in_blk = lax.broadcasted_iota(jnp.int32, (n_h * WINDOW, WINDOW), 1)
    key_ge_query = k_in_blk >= q_in_blk
    key_le_query = k_in_blk <= q_in_blk

    def both_kv(v2):
        lane = lax.broadcasted_iota(jnp.int32, (v2.shape[0], LANE), 1)
        return jnp.where(lane < HEAD_DIM, v2[:, 0:LANE], v2[:, LANE:2 * LANE])

    def run():
        def scores(sub):
            rows = slice(sub * WINDOW, (sub + 1) * WINDOW)
            q4 = jnp.concatenate([_masked_q(q_ref[rows, qs * LANE:(qs + 1) * LANE], qhalf)
                                  for qs, qhalf, _, _, _, _ in HEADS_GQA], axis=0)
            s_c = _dot_nt(q4, kc_ref[...])
            if not local:
                return s_c, None, None
            n = pl.program_id(1) * n_sub + sub
            blocks = [pl.ds(pl.multiple_of(b * WINDOW, WINDOW), WINDOW)
                      for b in (jnp.maximum(n - 1, 0), n, jnp.minimum(n + 1, n_blk - 1))]
            s = _dot_nt(q4, jnp.concatenate([kx_ref[blk, :] for blk in blocks], axis=0))
            s_l = jnp.concatenate([
                jnp.where(jnp.logical_and(key_ge_query, n > 0), s[:, 0:WINDOW], -jnp.inf),
                s[:, WINDOW:2 * WINDOW],
                jnp.where(jnp.logical_and(key_le_query, n < n_blk - 1), s[:, 2 * WINDOW:], -jnp.inf)], axis=1)
            return s_c, s_l, blocks

        all_scores = [scores(sub) for sub in range(n_sub)]
        for sub, (s_c, s_l, blocks) in enumerate(all_scores):
            rows = slice(sub * WINDOW, (sub + 1) * WINDOW)
            mx = jnp.maximum(jnp.max(s_c, axis=-1, keepdims=True), sink)
            if local:
                mx = jnp.maximum(mx, jnp.max(s_l, axis=-1, keepdims=True))
            p_c = jnp.exp2(s_c - mx)
            den = jnp.sum(p_c, axis=-1, keepdims=True) + jnp.exp2(sink - mx)
            o = _dot(p_c.astype(BF16), both_kv(vc_ref[...]))
            if local:
                p_l = jnp.exp2(s_l - mx)
                den = den + jnp.sum(p_l, axis=-1, keepdims=True)
                v3 = jnp.concatenate([both_kv(vx_ref[blk, :]) for blk in blocks], axis=0)
                o = o + _dot(p_l.astype(BF16), v3)
            o = o * (1.0 / den)
            outs = {(os_, ohalf): o[i * WINDOW:(i + 1) * WINDOW]
                    for i, (_, _, _, _, os_, ohalf) in enumerate(HEADS_GQA)}
            _store_heads(o_ref, rows, outs)

    run()


def _attn_win_ctx_kernel(sink_ref, q_ref, kc_ref, vc_ref, o_ref):
    _attn_win_kernel(sink_ref, q_ref, kc_ref, None, vc_ref, None, o_ref)


def _attn_win(sink, q, k, v, *, with_ctx, name):
    ox = _attn_latent_call(_attn_win_kernel, q, k, v, tq=WIN_TQ, name=name, smem=(sink,))
    if not with_ctx:
        return ox
    return _attn_ctx_call(_attn_win_ctx_kernel, q, k, v, name=name + "_ctx", smem=(sink,)), ox


Q = SSD_CHUNK
PAD = 8
U_CTX = PAD
U_X = PAD + CTX_LEN + PAD
U_ROWS = U_X + SEQ + PAD
N_CTX_CHUNK = CTX_LEN // Q
LOCAL_CHUNKS = 3
SCAN_STEPS = 4


def _split3(a):
    a1 = a.astype(BF16)
    r1 = a - a1.astype(F32)
    a2 = r1.astype(BF16)
    a3 = (r1 - a2.astype(F32)).astype(BF16)
    return a1, a2, a3


def _ssd_kernel(zc_ref, zx_ref, uc_ref, ux_ref, dtc_ref, dtx_ref, cw_ref, cb_ref, dtb_ref, alog_ref,
                dskip_ref, onorm_ref, yc_ref, yx_ref, upad, xs_s, bm_s, cm_s, dt_s, y_s, st_s, cme_s, da_s, h_s):
    n_slab = D_CONV_CH // LANE
    zpad = jnp.zeros((PAD, LANE), F32)
    for j in range(n_slab):
        sl = slice(j * LANE, (j + 1) * LANE)
        upad[j, 0:PAD, :] = zpad
        upad[j, U_CTX:U_CTX + CTX_LEN, :] = uc_ref[:, sl]
        upad[j, U_CTX + CTX_LEN:U_X, :] = zpad
        upad[j, U_X:U_X + SEQ, :] = ux_ref[:, sl]
        upad[j, U_X + SEQ:U_ROWS, :] = zpad

    for c in range(N_CHUNK):
        base = U_CTX + c * Q if c < N_CTX_CHUNK else U_X + (c - N_CTX_CHUNK) * Q
        for j in range(n_slab):
            sl = slice(j * LANE, (j + 1) * LANE)
            dst = (xs_s, bm_s, cm_s)[j // 2]
            acc = jnp.broadcast_to(cb_ref[:, sl], (Q, LANE))
            for k in range(D_CONV):
                lo = base + k - D_CONV // 2
                acc = acc + upad[j, lo:lo + Q, :] * cw_ref[k:k + 1, sl]
            dcol = (j % 2) * LANE
            dst[c * Q:(c + 1) * Q, dcol:dcol + LANE] = (acc * _sigmoid(acc)).astype(dst.dtype)

    def softplus(v):
        return jnp.maximum(v, 0.0) + jnp.log(1.0 + jnp.exp(-jnp.abs(v)))

    dt_s[0:CTX_LEN, :] = softplus(dtc_ref[...] + dtb_ref[...])
    dt_s[CTX_LEN:TOK, :] = softplus(dtx_ref[...] + dtb_ref[...])

    a_neg = -jnp.exp(alog_ref[...]) * LOG2E
    row = lax.broadcasted_iota(jnp.int32, (Q, Q), 0)
    col = lax.broadcasted_iota(jnp.int32, (Q, Q), 1)
    causal = (col <= row, col >= row)
    tri = (causal[0].astype(BF16), causal[1].astype(BF16))
    lo_half = col < HEAD_DIM
    last_row = (Q - 1, 0)

    def bcast_col(v, idx):
        return jnp.broadcast_to(v[:, idx:idx + 1], (Q, Q))

    def chunk_rows(c):
        return pl.ds(c * Q if isinstance(c, int) else pl.multiple_of(c * Q, Q), Q)

    def local_load(c):
        rows = chunk_rows(c)
        groups = [(cm_s[rows, g * Q:(g + 1) * Q], bm_s[rows, g * Q:(g + 1) * Q], xs_s[rows, g * Q:(g + 1) * Q])
                  for g in range(2)]
        return dt_s[rows, :], groups

    def local_sums(dt, groups):
        a1, a2, a3 = _split3(dt * a_neg)
        cs_f = _dot(tri[0], a1) + _dot(tri[0], a2) + _dot(tri[0], a3)
        cs_b = _dot(tri[1], a1) + _dot(tri[1], a2) + _dot(tri[1], a3)
        cs = jnp.where(col < D_HEADS, cs_f, cs_b)
        n_row = 2 * D_HEADS
        return cs, cs.T[0:n_row], dt.T[0:n_row], [_dot_nt(cmg, bmg) for cmg, bmg, _ in groups]

    def local_compute(groups, cs, cs_t, dt_t, cbms):
        ys, sts, das, cmes = [], {}, {}, {}
        for g, (cmg, bmg, xsg) in enumerate(groups):
            cbm = cbms[g]
            cmg32 = cmg.astype(F32)
            bm_t = bmg.astype(F32).T
            xh = (jnp.where(lo_half, xsg, 0.0).astype(BF16), jnp.where(lo_half, 0.0, xsg).astype(BF16))
            y = None
            for d in range(2):
                st = None
                da = []
                for hh in range(2):
                    idx = d * D_HEADS + g * 2 + hh
                    colb = bcast_col(cs, idx)
                    cs_row = cs_t[idx:idx + 1, :]
                    dt_row = dt_t[idx:idx + 1, :]
                    seg = colb - (cs_row - jnp.log2(dt_row))
                    dec = jnp.exp2(jnp.where(causal[d], seg, -jnp.inf))
                    yd = _dot((cbm * dec).astype(BF16), xh[hh])
                    y = yd if y is None else y + yd
                    last = cs_row[:, last_row[d]:last_row[d] + 1]
                    w_row = dt_row * jnp.exp2(last - cs_row)
                    sth = _dot((bm_t * w_row).astype(BF16), xh[hh])
                    st = sth if st is None else st + sth
                    cmes[d * 4 + g * 2 + hh] = (cmg32 * jnp.exp2(colb)).astype(BF16)
                    da.append(jnp.broadcast_to(jnp.exp2(last), (1, Q)))
                sts[d * 2 + g] = st
                das[d * 2 + g] = jnp.where(lo_half[0:1, :], da[0], da[1])
            ys.append(y)
        return ys, sts, das, cmes

    def local_store(c, res):
        ys, sts, das, cmes = res
        rows = chunk_rows(c)
        for g in range(2):
            y_s[rows, g * Q:(g + 1) * Q] = ys[g]
        for k, v in sts.items():
            st_s[c * 4 + k] = v
        for k, v in das.items():
            da_s[c * 4 + k, 0:1, :] = v
        for k, v in cmes.items():
            cme_s[c * 8 + k] = v

    n_iter = N_CHUNK // LOCAL_CHUNKS

    def iter_chunks(i):
        return [i * LOCAL_CHUNKS + u for u in range(LOCAL_CHUNKS)]

    def iter_sums(i):
        return [local_sums(*local_load(c)) for c in iter_chunks(i)]

    def local_body(i, sums):
        nxt = iter_sums(jnp.minimum(i + 1, n_iter - 1))
        chunks = iter_chunks(i)
        results = [local_compute(local_load(c)[1], *sm) for c, sm in zip(chunks, sums)]
        for c, res in zip(chunks, results):
            local_store(c, res)
        return nxt

    lax.fori_loop(0, n_iter, local_body, iter_sums(0))

    h_s[...] = jnp.zeros(h_s.shape, F32)

    def scan_steps(steps):
        work = []
        for c_fwd, c_bwd in steps:
            work += [(d, g, c) for d, c in ((0, c_fwd), (1, c_bwd)) for g in range(2)]
        loaded = []
        for d, g, c in work:
            k = c * 4 + d * 2 + g
            loaded.append((cme_s[2 * k], cme_s[2 * k + 1], da_s[k, 0:1, :], st_s[k],
                           y_s[chunk_rows(c), g * Q:(g + 1) * Q]))
        h = [h_s[k] for k in range(4)]
        ys = []
        for (d, g, c), (cme0, cme1, da, st, y) in zip(work, loaded):
            h_in = h[d * 2 + g]
            hb = h_in.astype(BF16)
            zero = jnp.zeros_like(hb)
            ys.append(y + _dot(cme0, jnp.where(lo_half, hb, zero)) + _dot(cme1, jnp.where(lo_half, zero, hb)))
            h[d * 2 + g] = da * h_in + st
        for (d, g, c), y in zip(work, ys):
            y_s[chunk_rows(c), g * Q:(g + 1) * Q] = y
        for k in range(4):
            h_s[k] = h[k]

    for i in range(N_CTX_CHUNK):
        scan_steps([(i, N_CTX_CHUNK - 1 - i)])

    def scan_body(t, carry):
        i0 = N_CTX_CHUNK + t * SCAN_STEPS
        scan_steps([(i0 + u, N_CHUNK - 1 + N_CTX_CHUNK - (i0 + u)) for u in range(SCAN_STEPS)])
        return carry

    lax.fori_loop(0, (N_CHUNK - N_CTX_CHUNK) // SCAN_STEPS, scan_body, 0)

    dskip = dskip_ref[...]
    onorm = onorm_ref[...]

    def finish(rows, z):
        y = y_s[rows, :] + dskip * xs_s[rows, :]
        return _rms(y * (z * _sigmoid(z)), onorm).astype(BF16)

    for c in range(N_CTX_CHUNK):
        yc_ref[c * Q:(c + 1) * Q, :] = finish(slice(c * Q, (c + 1) * Q), zc_ref[c * Q:(c + 1) * Q, :])

    def fin_body(c, carry):
        r0 = pl.multiple_of(c * Q, Q)
        yx_ref[pl.ds(r0, Q), :] = finish(pl.ds(CTX_LEN + r0, Q), zx_ref[pl.ds(r0, Q), :])
        return carry

    lax.fori_loop(0, SEQ // Q, fin_body, 0, unroll=2)


def _ssd_call(z, xbc, dt, cw, cb, dtb, alog, dskip, onorm):
    def cspec(wd):
        return pl.BlockSpec((None, CTX_LEN, wd), lambda b: (0, b, 0))

    def xspec(wd):
        return pl.BlockSpec((None, SEQ, wd), lambda b: (b + 1, 0, 0))

    return pl.pallas_call(
        _ssd_kernel,
        out_shape=[jax.ShapeDtypeStruct((BATCH, CTX_LEN, MIX), BF16),
                   jax.ShapeDtypeStruct((BATCH, SEQ, MIX), BF16)],
        grid=(BATCH,),
        in_specs=[cspec(MIX), xspec(MIX), cspec(D_CONV_CH), xspec(D_CONV_CH), cspec(LANE), xspec(LANE),
                  _full((8, D_CONV_CH)), _full((1, D_CONV_CH)), _full((1, LANE)), _full((1, LANE)),
                  _full((1, MIX)), _full((1, MIX))],
        out_specs=[pl.BlockSpec((None, CTX_LEN, MIX), lambda b: (b, 0, 0)),
                   pl.BlockSpec((None, SEQ, MIX), lambda b: (b, 0, 0))],
        scratch_shapes=[pltpu.VMEM((D_CONV_CH // LANE, U_ROWS, LANE), F32),
                        pltpu.VMEM((TOK, MIX), F32), pltpu.VMEM((TOK, MIX), BF16), pltpu.VMEM((TOK, MIX), BF16),
                        pltpu.VMEM((TOK, LANE), F32), pltpu.VMEM((TOK, MIX), F32),
                        pltpu.VMEM((N_CHUNK * 4, Q, Q), F32), pltpu.VMEM((N_CHUNK * 8, Q, Q), BF16),
                        pltpu.VMEM((N_CHUNK * 4, 8, LANE), F32), pltpu.VMEM((4, Q, Q), F32)],
        compiler_params=_params(("arbitrary",)),
        name="ssd",
    )(z, z, xbc, xbc, dt, dt, cw, cb, dtb, alog, dskip, onorm)


def _rope_tables():
    rows = SEQ // GRID_W
    r = np.repeat(np.arange(rows, dtype=np.float64), GRID_W)
    c = np.tile(np.arange(GRID_W, dtype=np.float64), rows)

    def tables(rot_dim):
        axis_dim = rot_dim // 2
        inv = ROPE_THETA ** (-np.arange(0, axis_dim, 2, dtype=np.float64) / axis_dim)
        ar = r[:, None] * inv[None, :]
        ac = c[:, None] * inv[None, :]
        cos = np.concatenate([np.cos(ar), np.cos(ar), np.cos(ac), np.cos(ac)], axis=-1)
        sin = np.concatenate([-np.sin(ar), np.sin(ar), -np.sin(ac), np.sin(ac)], axis=-1)
        return cos, sin

    c64, s64 = tables(HEAD_DIM)
    cab = np.tile(c64, (1, 2))
    sab = np.tile(s64, (1, 2))
    c32, s32 = tables(C_ROPE)
    cm = np.concatenate([np.ones((SEQ, C_NOPE)), c32, np.ones((SEQ, 32))], axis=-1)
    sm = np.concatenate([np.zeros((SEQ, C_NOPE)), s32, np.zeros((SEQ, 32))], axis=-1)
    return tuple(jnp.asarray(t, F32) for t in (cab, sab, cm, sm))


def _head_mean_matrix():
    lane = np.arange(LANE)
    same = (lane[:, None] // HEAD_DIM) == (lane[None, :] // HEAD_DIM)
    return jnp.asarray(np.where(same, 1.0 / HEAD_DIM, 0.0), BF16)


def _gqa_order(w, axis):
    shp = w.shape
    w = w.reshape(shp[:axis] + (2, 2, HEAD_DIM) + shp[axis + 1:])
    w = jnp.swapaxes(w, axis, axis + 1)
    return w.reshape(shp)


def _stacked_weights(w_in, c_w_uq, c_w_ukv, w_out):
    n_l, d = w_in.shape[:2]
    zc = lambda n: jnp.zeros((n_l, d, n), w_in.dtype)
    wb = w_in
    o_c = IN_AB
    o_d = IN_AB + IN_C
    w = jnp.concatenate([
        _gqa_order(wb[..., 0:MIX], 2), wb[..., MIX:512],
        _gqa_order(wb[..., 512:512 + MIX], 2), wb[..., 512 + MIX:o_c + C_Q_LORA + C_KV_LORA],
        zc(C_NOPE), wb[..., o_c + C_Q_LORA + C_KV_LORA:o_d], zc(LANE - C_NOPE - C_ROPE),
        wb[..., o_d:], zc(LANE - 2 * D_HEADS),
    ], axis=2).astype(BF16)
    assert w.shape[2] == PROJ_W
    dq = C_NOPE + C_ROPE
    pad_last = lambda t, n: jnp.pad(t, ((0, 0),) * (t.ndim - 1) + ((0, n),))
    wuq = pad_last(c_w_uq.reshape(n_l, C_Q_LORA, C_HEADS, dq), LANE - dq)
    wkv = c_w_ukv.reshape(n_l, C_KV_LORA, C_HEADS, C_NOPE + C_V)
    wk = pad_last(wkv[..., :C_NOPE], LANE - C_NOPE)
    head_parity = (jnp.arange(C_HEADS) % 2)[:, None]
    wv = jnp.stack([jnp.where(head_parity == half, wkv[..., C_NOPE:], 0.0) for half in range(2)], axis=3)
    wout = jnp.concatenate([_gqa_order(w_out[:, 0:MIX], 1), _gqa_order(w_out[:, MIX:2 * MIX], 1),
                            w_out[:, 2 * MIX:]], axis=1)
    return (w, wuq.reshape(n_l, C_Q_LORA, 512).astype(BF16), wk.reshape(n_l, C_KV_LORA, 512).astype(BF16),
            wv.reshape(n_l, C_KV_LORA, 512).astype(BF16), wout.astype(BF16))


def _lane_row(v, width=LANE):
    v = v.reshape(1, -1).astype(F32)
    return jnp.pad(v, ((0, 0), (0, width - v.shape[1])))


def kernel(x, c, ctx, c_ctx, ada_w, ada_b, ffn1_norm, ffn1_wi, ffn1_wo, mix_norm, w_in, w_out, a_sink, b_q_norm, b_k_norm, c_q_norm, c_w_uq, c_kv_norm, c_w_ukv, d_conv_w, d_conv_b, d_a_log, d_dt_bias, d_skip, d_out_norm, ffn2_norm, ffn2_wi, ffn2_wo, final_norm):
    cvec = jnp.concatenate([c_ctx[None, :], c, jnp.zeros((16 - NSEG, D_MODEL), F32)], axis=0)
    mods = _mods_call(cvec, ada_w, ada_b).reshape(DEPTH, 16, N_MOD, D_MODEL)[:, :NSEG]
    tabs = _rope_tables()
    bd = _head_mean_matrix()
    wi1, wo1, wi2, wo2 = ffn1_wi, ffn1_wo, ffn2_wi, ffn2_wo
    w, wuq, wk, wv, wout = _stacked_weights(w_in, c_w_uq, c_w_ukv, w_out)

    h = (ctx, x)
    out = None
    for l in range(DEPTH):
        with_ctx = l < DEPTH - 1
        mods_l = mods[l]
        h = _ffn_call(h, mods_l, ffn1_norm[l], wi1, wo1, l, k0=0, seg_off=0, name=f"ffn1_{l}")
        (qa, ka, va, qb, kb, vb, qc, kc, vc, z, xbc, dt) = _inproj_call(
            h, mods_l, mix_norm[l], w, l, tabs, bd,
            jnp.tile(b_q_norm[l], 2).reshape(1, LANE), jnp.tile(b_k_norm[l], 2).reshape(1, LANE),
            c_q_norm[l].reshape(1, C_Q_LORA), wuq, c_kv_norm[l].reshape(1, C_KV_LORA), wk, wv)

        oa = _attn_win(a_sink[l], qa, ka, va, with_ctx=with_ctx, name=f"attn_a_{l}")
        ob = _attn_dense(qb, kb, vb, heads=HEADS_GQA, with_ctx=with_ctx, name=f"attn_b_{l}")
        oc = _attn_dense(qc, kc, vc, heads=HEADS_MLA, with_ctx=with_ctx, name=f"attn_c_{l}")
        cw = jnp.pad(d_conv_w[l], ((0, 8 - D_CONV), (0, 0)))
        yc, yx = _ssd_call(z, xbc, dt, cw, d_conv_b[l].reshape(1, D_CONV_CH),
                           _lane_row(d_dt_bias[l]), _lane_row(d_a_log[l]),
                           jnp.repeat(d_skip[l], HEAD_DIM).reshape(1, MIX), d_out_norm[l].reshape(1, MIX))
        if with_ctx:
            h = _ffn_call(h, mods_l, ffn2_norm[l], wi2, wo2, l, k0=6, seg_off=0,
                          pre=(oa, ob, oc, (yc, yx), wout), name=f"ffn2_{l}")
        else:
            out = _ffn_call(h, mods_l, ffn2_norm[l], wi2, wo2, l, k0=6, seg_off=1,
                            pre=(oa, ob, oc, yx, wout), final_g=final_norm, name=f"ffn2_{l}")
    return out
```

```python
import functools

import numpy as np
import jax
import jax.numpy as jnp
from jax import lax
from jax.experimental import pallas as pl
from jax.experimental.pallas import tpu as pltpu

D_MODEL = 1024
BATCH = 8
SEQ = 2048
DEPTH = 2
GRID_W = 64
CTX_LEN = 256
HEAD_DIM = 64
ROPE_THETA = 10000.0
EPS = 1e-6
FFN_DIM = 2816
N_MOD = 9
WINDOW = 128
C_HEADS = 4
C_Q_LORA = 256
C_KV_LORA = 128
C_NOPE = 64
C_ROPE = 32
C_V = 64
D_HEADS = 4
D_STATE = 128
D_CONV = 5
SSD_CHUNK = 128
MIX = 256
IN_AB = 1024
IN_C = C_Q_LORA + C_KV_LORA + C_ROPE
D_CONV_CH = MIX + 2 * 2 * D_STATE
IN_D = MIX + D_CONV_CH + 2 * D_HEADS

NSEG = BATCH + 1
SEG = SEQ
assert BATCH * CTX_LEN == SEG
TOK = CTX_LEN + SEQ
N_CHUNK = TOK // SSD_CHUNK

LANE = 128
VMEM_LIMIT = 56 * 1024 * 1024

F32 = jnp.float32
BF16 = jnp.bfloat16
LOG2E = 1.4426950408889634


def _dot(a, b):
    return jnp.dot(a, b, preferred_element_type=F32)


def _dot_nt(a, b):
    return lax.dot_general(a, b, (((1,), (1,)), ((), ())), preferred_element_type=F32)


def _sigmoid(x):
    return 1.0 / (1.0 + jnp.exp(-x))


def _rms(x, g):
    return x * lax.rsqrt(jnp.mean(x * x, axis=-1, keepdims=True) + EPS) * g


def _full(shape):
    nd = len(shape)
    return pl.BlockSpec(shape, lambda *_: (0,) * nd)


def _layer_slab(shape, layer):
    nd = len(shape)
    return pl.BlockSpec((None,) + tuple(shape), lambda *_: (layer,) + (0,) * nd, pipeline_mode=pl.Buffered(1))


def _params(sem):
    return pltpu.CompilerParams(dimension_semantics=sem, vmem_limit_bytes=VMEM_LIMIT)


MODS_BN = 3 * D_MODEL


def _mods_kernel(c_ref, w_ref, b_ref, o_ref):
    c = c_ref[...]
    s = (c * _sigmoid(c)).astype(BF16)
    o_ref[...] = _dot(s, w_ref[...].astype(BF16)) + b_ref[...]


def _mods_call(cvec, ada_w, ada_b):
    n_l = ada_w.shape[0]
    bn = MODS_BN
    return pl.pallas_call(
        _mods_kernel,
        out_shape=jax.ShapeDtypeStruct((n_l, 16, N_MOD * D_MODEL), F32),
        grid=(n_l, N_MOD * D_MODEL // bn),
        in_specs=[
            pl.BlockSpec((16, D_MODEL), lambda l, n: (0, 0)),
            pl.BlockSpec((None, D_MODEL, bn), lambda l, n: (l, 0, n)),
            pl.BlockSpec((None, 1, bn), lambda l, n: (l, 0, n)),
        ],
        out_specs=pl.BlockSpec((None, 16, bn), lambda l, n: (l, 0, n)),
        compiler_params=_params(("arbitrary", "arbitrary")),
        name="mods",
    )(cvec, ada_w, ada_b.reshape(n_l, 1, N_MOD * D_MODEL))


FFN_TM = 512
FFN_SUB = 256
FFN_FC = 256
FFN_WI_ROWS = 128
FFN_WO_ROWS = 256
assert D_MODEL % FFN_WI_ROWS == 0 and FFN_DIM % FFN_WO_ROWS == 0


def _seg_specs(src, tm, width, seg_off):
    if isinstance(src, tuple):
        assert seg_off == 0
        ctx_arr, x_arr = src
        specs = [pl.BlockSpec((None, tm, width), lambda s, i: (0, jnp.where(s == 0, i, 0), 0)),
                 pl.BlockSpec((None, tm, width), lambda s, i: (jnp.maximum(s - 1, 0), jnp.where(s == 0, 0, i), 0))]
        return specs, [ctx_arr.reshape(1, SEG, width), x_arr]
    return [pl.BlockSpec((None, tm, width), lambda s, i: (s + seg_off, i, 0))], [src]


def _seg_read(refs):
    if len(refs) == 2:
        return jnp.where(pl.program_id(0) == 0, refs[0][...], refs[1][...])
    return refs[0][...]


def _fetch_as_bf16(src, dst, stage, sem, rows_per_copy):
    n_copy = src.shape[0] // rows_per_copy

    def copy(k):
        slot = k % 2
        return pltpu.make_async_copy(src.at[pl.ds(k * rows_per_copy, rows_per_copy), :], stage.at[slot],
                                     sem.at[slot])

    copy(0).start()
    for k in range(n_copy):
        if k + 1 < n_copy:
            copy(k + 1).start()
        copy(k).wait()
        dst[k * rows_per_copy:(k + 1) * rows_per_copy, :] = stage[k % 2].astype(BF16)


def _ffn_kernel(*refs, k0, layer, n_h, n_pre, has_final):
    refs = list(refs)
    h_refs = [refs.pop(0) for _ in range(n_h)]
    mods_ref, g_ref, wi_hbm, wo_hbm = (refs.pop(0) for _ in range(4))
    if n_pre:
        o_refs = [[refs.pop(0) for _ in range(n)] for n in n_pre]
        wout_ref = refs.pop(0)
    if has_final:
        gf_ref = refs.pop(0)
    out_ref, hm_ref, wi_ref, wo_ref, stage_i, stage_o, sem_i, sem_o = refs

    @pl.when(jnp.logical_and(pl.program_id(0) == 0, pl.program_id(1) == 0))
    def _():
        _fetch_as_bf16(wi_hbm.at[layer], wi_ref, stage_i, sem_i, FFN_WI_ROWS)
        _fetch_as_bf16(wo_hbm.at[layer], wo_ref, stage_o, sem_o, FFN_WO_ROWS)

    x = _seg_read(h_refs)
    m = mods_ref[...]
    if n_pre:
        o = None
        for k, o_ref in enumerate(o_refs):
            part = _dot(_seg_read(o_ref), wout_ref[k * MIX:(k + 1) * MIX, :])
            o = part if o is None else o + part
        x = x + m[5:6] * o
    halves = [slice(r * FFN_SUB, (r + 1) * FFN_SUB) for r in range(x.shape[0] // FFN_SUB)]
    xs = [x[rows] for rows in halves]
    nbs = [(_rms(xr, g_ref[...]) * (1.0 + m[k0 + 1:k0 + 2]) + m[k0:k0 + 1]).astype(BF16) for xr in xs]
    for rows, nb in zip(halves, nbs):
        for c in range(FFN_DIM // FFN_FC):
            lo = c * FFN_FC
            a = _dot(nb, wi_ref[:, lo:lo + FFN_FC])
            b = _dot(nb, wi_ref[:, FFN_DIM + lo:FFN_DIM + lo + FFN_FC])
            hm_ref[rows, lo:lo + FFN_FC] = (a * _sigmoid(a) * b).astype(BF16)
    ys = [_dot(hm_ref[rows, :], wo_ref[...]) for rows in halves]
    for rows, xr, y in zip(halves, xs, ys):
        out = xr + 0.5 * m[k0 + 2:k0 + 3] * y
        if has_final:
            out = _rms(out, gf_ref[...])
        out_ref[rows, :] = out


def _ffn_call(h, mods_l, g, wi, wo, layer, *, k0, seg_off, pre=None, final_g=None, name):
    tm = FFN_TM
    nseg = NSEG if isinstance(h, tuple) else h.shape[0] - seg_off
    in_specs, args = _seg_specs(h, tm, D_MODEL, seg_off)
    n_h = len(args)
    in_specs += [
        pl.BlockSpec((None, N_MOD, D_MODEL), lambda s, i: (s + seg_off, 0, 0)),
        _full((1, D_MODEL)),
        pl.BlockSpec(memory_space=pl.ANY),
        pl.BlockSpec(memory_space=pl.ANY),
    ]
    args += [mods_l, g.reshape(1, D_MODEL), wi, wo]
    n_pre = ()
    if pre is not None:
        *outs, wout = pre
        for o in outs:
            sp, ar = _seg_specs(o, tm, MIX, 0)
            in_specs += sp
            args += ar
            n_pre += (len(ar),)
        in_specs.append(_layer_slab((4 * MIX, D_MODEL), layer))
        args.append(wout)
    if final_g is not None:
        in_specs.append(_full((1, D_MODEL)))
        args.append(final_g.reshape(1, D_MODEL))
    kern = functools.partial(_ffn_kernel, k0=k0, layer=layer, n_h=n_h, n_pre=n_pre,
                             has_final=final_g is not None)
    return pl.pallas_call(
        kern,
        out_shape=jax.ShapeDtypeStruct((nseg, SEG, D_MODEL), F32),
        grid=(nseg, SEG // tm),
        in_specs=in_specs,
        out_specs=pl.BlockSpec((None, tm, D_MODEL), lambda s, i: (s, i, 0)),
        scratch_shapes=[pltpu.VMEM((tm, FFN_DIM), BF16),
                        pltpu.VMEM((D_MODEL, 2 * FFN_DIM), BF16), pltpu.VMEM((FFN_DIM, D_MODEL), BF16),
                        pltpu.VMEM((2, FFN_WI_ROWS, 2 * FFN_DIM), F32), pltpu.VMEM((2, FFN_WO_ROWS, D_MODEL), F32),
                        pltpu.SemaphoreType.DMA((2,)), pltpu.SemaphoreType.DMA((2,))],
        compiler_params=_params(("arbitrary", "arbitrary")),
        name=name,
    )(*args)


PROJ_TM = 1024
PROJ_SUB = PROJ_TM
COL_AB = 0
COL_C = IN_AB
COL_D = COL_C + 512
PROJ_W = COL_D + MIX + D_CONV_CH + LANE


def _swap_halves(x, half):
    lane = lax.broadcasted_iota(jnp.int32, x.shape, 1)
    first = (lane & half) == 0
    up = pltpu.roll(x, LANE - half, axis=1)
    dn = pltpu.roll(x, half, axis=1)
    return jnp.where(first, up, dn)


def _rope(x, cos, sin, half):
    return x * cos + _swap_halves(x, half) * sin


def _head_rms(x, bd, g):
    sq = x * x
    hi = sq.astype(BF16)
    lo = (sq - hi.astype(F32)).astype(BF16)
    ms = _dot(hi, bd) + _dot(lo, bd)
    return x * lax.rsqrt(ms + EPS) * g


def _inproj_kernel(h_ref, mods_ref, g_ref, w_ref, cab_ref, sab_ref, cm_ref, sm_ref, bd_ref,
                   bqn_ref, bkn_ref, cqn_ref, wuq_ref, ckvn_ref, wukvk_ref, wukvv_ref,
                   qa_ref, ka_ref, va_ref, qb_ref, kb_ref, vb_ref, qc_ref, kc_ref, vc_ref,
                   z_ref, xbc_ref, dt_ref):
    m = mods_ref[...]
    is_ctx = pl.program_id(0) == 0
    bd = bd_ref[...]
    scale_ab = HEAD_DIM ** -0.5 * LOG2E
    scale_c = (C_NOPE + C_ROPE) ** -0.5 * LOG2E

    def project(rows):
        nb = (_rms(h_ref[rows, :], g_ref[...]) * (1.0 + m[4:5]) + m[3:4]).astype(BF16)
        p_ab = [_dot(nb, w_ref[:, COL_AB + mixer * 512:COL_AB + (mixer + 1) * 512]) for mixer in range(2)]
        p_c = _dot(nb, w_ref[:, COL_C:COL_C + 512])
        z_ref[rows, :] = _dot(nb, w_ref[:, COL_D:COL_D + MIX])
        for c in range(D_CONV_CH // 256):
            lo = COL_D + MIX + c * 256
            xbc_ref[rows, c * 256:(c + 1) * 256] = _dot(nb, w_ref[:, lo:lo + 256])
        lo = COL_D + MIX + D_CONV_CH
        dt_ref[rows, :] = _dot(nb, w_ref[:, lo:lo + LANE])
        return p_ab, p_c

    def epilogue(rows, p_ab, p_c):
        cab, sab = jnp.where(is_ctx, 1.0, cab_ref[rows, :]), jnp.where(is_ctx, 0.0, sab_ref[rows, :])
        cm, sm = jnp.where(is_ctx, 1.0, cm_ref[rows, :]), jnp.where(is_ctx, 0.0, sm_ref[rows, :])
        for mixer, (q_ref, k_ref, v_ref) in enumerate(((qa_ref, ka_ref, va_ref), (qb_ref, kb_ref, vb_ref))):
            p = p_ab[mixer]
            for s in range(2):
                q = p[:, s * LANE:(s + 1) * LANE]
                if mixer == 1:
                    q = _head_rms(q, bd, bqn_ref[...])
                q = _rope(q, cab, sab, 16) * scale_ab
                q_ref[rows, s * LANE:(s + 1) * LANE] = q.astype(BF16)
            k = p[:, 2 * LANE:3 * LANE]
            if mixer == 1:
                k = _head_rms(k, bd, bkn_ref[...])
            k_ref[rows, :] = _rope(k, cab, sab, 16).astype(BF16)
            v = p[:, 3 * LANE:4 * LANE]
            lane = lax.broadcasted_iota(jnp.int32, v.shape, 1)
            v_ref[rows, 0:LANE] = jnp.where(lane < HEAD_DIM, v, 1.0).astype(BF16)
            v_ref[rows, LANE:2 * LANE] = jnp.where(lane < HEAD_DIM, 1.0, v).astype(BF16)

        p = p_c
        cq = _rms(p[:, 0:C_Q_LORA], cqn_ref[...]).astype(BF16)
        q = _dot(cq, wuq_ref[...])
        ckv = _rms(p[:, C_Q_LORA:C_Q_LORA + C_KV_LORA], ckvn_ref[...]).astype(BF16)
        kn = _dot(ckv, wukvk_ref[...])
        lane = lax.broadcasted_iota(jnp.int32, (1, C_HEADS * LANE), 1)
        own_half = ((lane // HEAD_DIM) % 2) == ((lane // LANE) % 2)
        vc_ref[rows, :] = jnp.where(own_half, _dot(ckv, wukvv_ref[...]), 1.0).astype(BF16)
        kr = _rope(p[:, 3 * LANE:4 * LANE], cm, sm, 8)
        for hh in range(C_HEADS):
            sl = slice(hh * LANE, (hh + 1) * LANE)
            qc_ref[rows, sl] = (_rope(q[:, sl], cm, sm, 8) * scale_c).astype(BF16)
            kc_ref[rows, sl] = (kn[:, sl] + kr).astype(BF16)

    blocks = [slice(r * PROJ_SUB, (r + 1) * PROJ_SUB) for r in range(h_ref.shape[0] // PROJ_SUB)]
    projected = [project(rows) for rows in blocks]
    for rows, (p_ab, p_c) in zip(blocks, projected):
        epilogue(rows, p_ab, p_c)


PROJ_OUT = ([(MIX, BF16), (LANE, BF16), (MIX, BF16)] * 2 + [(512, BF16), (512, BF16), (512, BF16)]
            + [(MIX, F32), (D_CONV_CH, F32), (LANE, F32)])


def _inproj_call(h, mods_l, g, w, layer, tabs, bd, bqn, bkn, cqn, wuq, ckvn, wukvk, wukvv):
    tm = PROJ_TM
    cab, sab, cm, sm = tabs
    tok = lambda s, i: (s, i, 0)
    tab = pl.BlockSpec((tm, LANE), lambda s, i: (i, 0))
    return pl.pallas_call(
        _inproj_kernel,
        out_shape=[jax.ShapeDtypeStruct((NSEG, SEG, wd), dt) for wd, dt in PROJ_OUT],
        grid=(NSEG, SEG // tm),
        in_specs=[
            pl.BlockSpec((None, tm, D_MODEL), tok),
            pl.BlockSpec((None, N_MOD, D_MODEL), lambda s, i: (s, 0, 0)),
            _full((1, D_MODEL)),
            _layer_slab((D_MODEL, PROJ_W), layer),
            tab, tab, tab, tab,
            _full((LANE, LANE)),
            _full((1, LANE)), _full((1, LANE)),
            _full((1, C_Q_LORA)), _layer_slab((C_Q_LORA, 512), layer),
            _full((1, C_KV_LORA)), _layer_slab((C_KV_LORA, 512), layer), _layer_slab((C_KV_LORA, 512), layer),
        ],
        out_specs=[pl.BlockSpec((None, tm, wd), tok) for wd, _ in PROJ_OUT],
        compiler_params=_params(("arbitrary", "arbitrary")),
        name="inproj",
    )(h, mods_l, g.reshape(1, D_MODEL), w, cab, sab, cm, sm, bd,
      bqn, bkn, cqn, wuq, ckvn, wukvk, wukvv)


HEADS_GQA = tuple((g, kv, 0, kv, g, kv) for g in range(2) for kv in range(2))
HEADS_MLA = tuple((h, None, h, h, h // 2, h % 2) for h in range(C_HEADS))


def _normalise(o, half, extra_den=None):
    lane = lax.broadcasted_iota(jnp.int32, o.shape, 1)
    valid = (lane < HEAD_DIM) if half == 0 else (lane >= HEAD_DIM)
    den = pltpu.roll(o, HEAD_DIM, axis=1)
    if extra_den is not None:
        den = den + extra_den
    return o * (1.0 / jnp.where(valid, den, 1.0))


def _masked_q(q, half):
    if half is None:
        return q
    lane = lax.broadcasted_iota(jnp.int32, q.shape, 1)
    keep = (lane < HEAD_DIM) if half == 0 else (lane >= HEAD_DIM)
    return jnp.where(keep, q, jnp.zeros_like(q))


def _store_heads(o_ref, rows, outs):
    lane = lax.broadcasted_iota(jnp.int32, outs[(0, 0)].shape, 1)
    for s in range(2):
        o = jnp.where(lane < HEAD_DIM, outs[(s, 0)], outs[(s, 1)])
        o_ref[rows, s * LANE:(s + 1) * LANE] = o.astype(o_ref.dtype)


ATT_TQ = 1024
ATT_SUB = 256
ATT_LOOKAHEAD = 2


def _attn_dense_kernel(q_ref, kc_ref, kx_ref, vc_ref, vx_ref, o_ref, *, heads):
    n_sub = q_ref.shape[0] // ATT_SUB
    stages = [(sub, hd) for sub in range(n_sub) for hd in heads]

    def run(with_x):
        def scores(stage):
            sub, (qs, qhalf, ks) = stage[0], stage[1][:3]
            qh = _masked_q(q_ref[sub * ATT_SUB:(sub + 1) * ATT_SUB, qs * LANE:(qs + 1) * LANE], qhalf)
            ksl = slice(ks * LANE, (ks + 1) * LANE)
            return _dot_nt(qh, kc_ref[:, ksl]), (_dot_nt(qh, kx_ref[:, ksl]) if with_x else None)

        outs = {}
        pending = [scores(st) for st in stages[:ATT_LOOKAHEAD]]
        for i, (sub, (_, _, _, vs, os_, ohalf)) in enumerate(stages):
            s_c, s_x = pending.pop(0)
            if i + ATT_LOOKAHEAD < len(stages):
                pending.append(scores(stages[i + ATT_LOOKAHEAD]))
            vsl = slice(vs * LANE, (vs + 1) * LANE)
            mx = jnp.max(s_c, axis=-1, keepdims=True)
            if with_x:
                mx = jnp.maximum(mx, jnp.max(s_x, axis=-1, keepdims=True))
            o = _dot(jnp.exp2((s_c - mx).astype(BF16)), vc_ref[:, vsl])
            if with_x:
                o = o + _dot(jnp.exp2((s_x - mx).astype(BF16)), vx_ref[:, vsl])
            outs[(os_, ohalf)] = _normalise(o, ohalf)
            if len(outs) == len(heads):
                _store_heads(o_ref, slice(sub * ATT_SUB, (sub + 1) * ATT_SUB), outs)
                outs = {}

    run(kx_ref is not None)


def _attn_ctx_kernel(q_ref, kc_ref, vc_ref, o_ref, *, heads):
    _attn_dense_kernel(q_ref, kc_ref, None, vc_ref, None, o_ref, heads=heads)


def _attn_latent_call(kern, q, k, v, *, tq, name, smem=()):
    kw, vw = k.shape[-1], v.shape[-1]
    return pl.pallas_call(
        kern,
        out_shape=jax.ShapeDtypeStruct((BATCH, SEQ, MIX), BF16),
        grid=(BATCH, SEQ // tq),
        in_specs=[pl.BlockSpec(memory_space=pltpu.SMEM)] * len(smem) + [
            pl.BlockSpec((None, tq, q.shape[-1]), lambda b, j: (b + 1, j, 0)),
            pl.BlockSpec((None, CTX_LEN, kw), lambda b, j: (0, b, 0)),
            pl.BlockSpec((None, SEQ, kw), lambda b, j: (b + 1, 0, 0)),
            pl.BlockSpec((None, CTX_LEN, vw), lambda b, j: (0, b, 0)),
            pl.BlockSpec((None, SEQ, vw), lambda b, j: (b + 1, 0, 0))],
        out_specs=pl.BlockSpec((None, tq, MIX), lambda b, j: (b, j, 0)),
        compiler_params=_params(("arbitrary", "arbitrary")),
        name=name,
    )(*smem, q, k, k, v, v)


def _attn_ctx_call(kern, q, k, v, *, name, smem=()):
    ctx = lambda wd: pl.BlockSpec((None, CTX_LEN, wd), lambda b: (0, b, 0))
    return pl.pallas_call(
        kern,
        out_shape=jax.ShapeDtypeStruct((BATCH, CTX_LEN, MIX), BF16),
        grid=(BATCH,),
        in_specs=[pl.BlockSpec(memory_space=pltpu.SMEM)] * len(smem)
        + [ctx(q.shape[-1]), ctx(k.shape[-1]), ctx(v.shape[-1])],
        out_specs=pl.BlockSpec((None, CTX_LEN, MIX), lambda b: (b, 0, 0)),
        compiler_params=_params(("arbitrary",)),
        name=name,
    )(*smem, q, k, v)


def _attn_dense(q, k, v, *, heads, with_ctx, name):
    ox = _attn_latent_call(functools.partial(_attn_dense_kernel, heads=heads), q, k, v, tq=ATT_TQ, name=name)
    if not with_ctx:
        return ox
    return _attn_ctx_call(functools.partial(_attn_ctx_kernel, heads=heads), q, k, v, name=name + "_ctx"), ox


WIN_TQ = 512


def _attn_win_kernel(sink_ref, q_ref, kc_ref, kx_ref, vc_ref, vx_ref, o_ref):
    local = kx_ref is not None
    n_sub = q_ref.shape[0] // WINDOW

    n_h = len(HEADS_GQA)
    head_of_row = lax.broadcasted_iota(jnp.int32, (n_h * WINDOW, 1), 0) // WINDOW
    sink = jnp.zeros((n_h * WINDOW, 1), F32)
    for i, (qs, qhalf, _, _, _, _) in enumerate(HEADS_GQA):
        sink = jnp.where(head_of_row == i, sink_ref[qhalf * 2 + qs] * LOG2E, sink)
    n_blk = SEQ // WINDOW
    q_in_blk = lax.broadcasted_iota(jnp.int32, (n_h * WINDOW, WINDOW), 0) % WINDOW
    k_in_blk = lax.broadcasted_iota(jnp.int32, (n_h * WINDOW, WINDOW), 1)
    key_ge_query = k_in_blk >= q_in_blk
    key_le_query = k_in_blk <= q_in_blk

    def both_kv(v2):
        lane = lax.broadcasted_iota(jnp.int32, (v2.shape[0], LANE), 1)
        return jnp.where(lane < HEAD_DIM, v2[:, 0:LANE], v2[:, LANE:2 * LANE])

    def run():
        def scores(sub):
            rows = slice(sub * WINDOW, (sub + 1) * WINDOW)
            q4 = jnp.concatenate([_masked_q(q_ref[rows, qs * LANE:(qs + 1) * LANE], qhalf)
                                  for qs, qhalf, _, _, _, _ in HEADS_GQA], axis=0)
            s_c = _dot_nt(q4, kc_ref[...])
            if not local:
                return s_c, None, None
            n = pl.program_id(1) * n_sub + sub
            blocks = [pl.ds(pl.multiple_of(b * WINDOW, WINDOW), WINDOW)
                      for b in (jnp.maximum(n - 1, 0), n, jnp.minimum(n + 1, n_blk - 1))]
            s = _dot_nt(q4, jnp.concatenate([kx_ref[blk, :] for blk in blocks], axis=0))
            s_l = jnp.concatenate([
                jnp.where(jnp.logical_and(key_ge_query, n > 0), s[:, 0:WINDOW], -jnp.inf),
                s[:, WINDOW:2 * WINDOW],
                jnp.where(jnp.logical_and(key_le_query, n < n_blk - 1), s[:, 2 * WINDOW:], -jnp.inf)], axis=1)
            return s_c, s_l, blocks

        all_scores = [scores(sub) for sub in range(n_sub)]
        for sub, (s_c, s_l, blocks) in enumerate(all_scores):
            rows = slice(sub * WINDOW, (sub + 1) * WINDOW)
            mx = jnp.maximum(jnp.max(s_c, axis=-1, keepdims=True), sink)
            if local:
                mx = jnp.maximum(mx, jnp.max(s_l, axis=-1, keepdims=True))
            p_c = jnp.exp2(s_c - mx)
            den = jnp.sum(p_c, axis=-1, keepdims=True) + jnp.exp2(sink - mx)
            o = _dot(p_c.astype(BF16), both_kv(vc_ref[...]))
            if local:
                p_l = jnp.exp2(s_l - mx)
                den = den + jnp.sum(p_l, axis=-1, keepdims=True)
                v3 = jnp.concatenate([both_kv(vx_ref[blk, :]) for blk in blocks], axis=0)
                o = o + _dot(p_l.astype(BF16), v3)
            o = o * (1.0 / den)
            outs = {(os_, ohalf): o[i * WINDOW:(i + 1) * WINDOW]
                    for i, (_, _, _, _, os_, ohalf) in enumerate(HEADS_GQA)}
            _store_heads(o_ref, rows, outs)

    run()


def _attn_win_ctx_kernel(sink_ref, q_ref, kc_ref, vc_ref, o_ref):
    _attn_win_kernel(sink_ref, q_ref, kc_ref, None, vc_ref, None, o_ref)


def _attn_win(sink, q, k, v, *, with_ctx, name):
    ox = _attn_latent_call(_attn_win_kernel, q, k, v, tq=WIN_TQ, name=name, smem=(sink,))
    if not with_ctx:
        return ox
    return _attn_ctx_call(_attn_win_ctx_kernel, q, k, v, name=name + "_ctx", smem=(sink,)), ox


Q = SSD_CHUNK
PAD = 8
U_CTX = PAD
U_X = PAD + CTX_LEN + PAD
U_ROWS = U_X + SEQ + PAD
N_CTX_CHUNK = CTX_LEN // Q
LOCAL_CHUNKS = 6
SCAN_STEPS = 4


def _split3(a):
    a1 = a.astype(BF16)
    r1 = a - a1.astype(F32)
    a2 = r1.astype(BF16)
    a3 = (r1 - a2.astype(F32)).astype(BF16)
    return a1, a2, a3


def _ssd_kernel(zc_ref, zx_ref, uc_ref, ux_ref, dtc_ref, dtx_ref, cw_ref, cb_ref, dtb_ref, alog_ref,
                dskip_ref, onorm_ref, yc_ref, yx_ref, upad, xs_s, bm_s, cm_s, dt_s, y_s, st_s, cme_s, da_s, h_s):
    n_slab = D_CONV_CH // LANE
    zpad = jnp.zeros((PAD, LANE), F32)
    for j in range(n_slab):
        sl = slice(j * LANE, (j + 1) * LANE)
        upad[j, 0:PAD, :] = zpad
        upad[j, U_CTX:U_CTX + CTX_LEN, :] = uc_ref[:, sl]
        upad[j, U_CTX + CTX_LEN:U_X, :] = zpad
        upad[j, U_X:U_X + SEQ, :] = ux_ref[:, sl]
        upad[j, U_X + SEQ:U_ROWS, :] = zpad

    for c in range(N_CHUNK):
        base = U_CTX + c * Q if c < N_CTX_CHUNK else U_X + (c - N_CTX_CHUNK) * Q
        for j in range(n_slab):
            sl = slice(j * LANE, (j + 1) * LANE)
            dst = (xs_s, bm_s, cm_s)[j // 2]
            acc = jnp.broadcast_to(cb_ref[:, sl], (Q, LANE))
            for k in range(D_CONV):
                lo = base + k - D_CONV // 2
                acc = acc + upad[j, lo:lo + Q, :] * cw_ref[k:k + 1, sl]
            dcol = (j % 2) * LANE
            dst[c * Q:(c + 1) * Q, dcol:dcol + LANE] = (acc * _sigmoid(acc)).astype(dst.dtype)

    def softplus(v):
        return jnp.maximum(v, 0.0) + jnp.log(1.0 + jnp.exp(-jnp.abs(v)))

    dt_s[0:CTX_LEN, :] = softplus(dtc_ref[...] + dtb_ref[...])
    dt_s[CTX_LEN:TOK, :] = softplus(dtx_ref[...] + dtb_ref[...])

    a_neg = -jnp.exp(alog_ref[...]) * LOG2E
    row = lax.broadcasted_iota(jnp.int32, (Q, Q), 0)
    col = lax.broadcasted_iota(jnp.int32, (Q, Q), 1)
    causal = (col <= row, col >= row)
    tri = (causal[0].astype(BF16), causal[1].astype(BF16))
    lo_half = col < HEAD_DIM
    last_row = (Q - 1, 0)

    def bcast_col(v, idx):
        return jnp.broadcast_to(v[:, idx:idx + 1], (Q, Q))

    def chunk_rows(c):
        return pl.ds(c * Q if isinstance(c, int) else pl.multiple_of(c * Q, Q), Q)

    def local_load(c):
        rows = chunk_rows(c)
        groups = [(cm_s[rows, g * Q:(g + 1) * Q], bm_s[rows, g * Q:(g + 1) * Q], xs_s[rows, g * Q:(g + 1) * Q])
                  for g in range(2)]
        return dt_s[rows, :], groups

    def local_sums(dt, groups):
        a1, a2, a3 = _split3(dt * a_neg)
        cs_f = _dot(tri[0], a1) + _dot(tri[0], a2) + _dot(tri[0], a3)
        cs_b = _dot(tri[1], a1) + _dot(tri[1], a2) + _dot(tri[1], a3)
        cs = jnp.where(col < D_HEADS, cs_f, cs_b)
        n_row = 2 * D_HEADS
        return cs, cs.T[0:n_row], dt.T[0:n_row], [_dot_nt(cmg, bmg) for cmg, bmg, _ in groups]

    def local_compute(groups, cs, cs_t, dt_t, cbms):
        ys, sts, das, cmes = [], {}, {}, {}
        for g, (cmg, bmg, xsg) in enumerate(groups):
            cbm = cbms[g]
            cmg32 = cmg.astype(F32)
            bm_t = bmg.astype(F32).T
            xh = (jnp.where(lo_half, xsg, 0.0).astype(BF16), jnp.where(lo_half, 0.0, xsg).astype(BF16))
            y = None
            for d in range(2):
                st = None
                da = []
                for hh in range(2):
                    idx = d * D_HEADS + g * 2 + hh
                    colb = bcast_col(cs, idx)
                    cs_row = cs_t[idx:idx + 1, :]
                    dt_row = dt_t[idx:idx + 1, :]
                    seg = colb - (cs_row - jnp.log2(dt_row))
                    dec = jnp.exp2(jnp.where(causal[d], seg, -jnp.inf))
                    yd = _dot((cbm * dec).astype(BF16), xh[hh])
                    y = yd if y is None else y + yd
                    last = cs_row[:, last_row[d]:last_row[d] + 1]
                    w_row = dt_row * jnp.exp2(last - cs_row)
                    sth = _dot((bm_t * w_row).astype(BF16), xh[hh])
                    st = sth if st is None else st + sth
                    cmes[d * 4 + g * 2 + hh] = (cmg32 * jnp.exp2(colb)).astype(BF16)
                    da.append(jnp.broadcast_to(jnp.exp2(last), (1, Q)))
                sts[d * 2 + g] = st
                das[d * 2 + g] = jnp.where(lo_half[0:1, :], da[0], da[1])
            ys.append(y)
        return ys, sts, das, cmes

    def local_store(c, res):
        ys, sts, das, cmes = res
        rows = chunk_rows(c)
        for g in range(2):
            y_s[rows, g * Q:(g + 1) * Q] = ys[g]
        for k, v in sts.items():
            st_s[c * 4 + k] = v
        for k, v in das.items():
            da_s[c * 4 + k, 0:1, :] = v
        for k, v in cmes.items():
            cme_s[c * 8 + k] = v

    n_iter = N_CHUNK // LOCAL_CHUNKS

    def iter_chunks(i):
        return [i * LOCAL_CHUNKS + u for u in range(LOCAL_CHUNKS)]

    def iter_sums(i):
        return [local_sums(*local_load(c)) for c in iter_chunks(i)]

    def local_body(i, sums):
        nxt = iter_sums(jnp.minimum(i + 1, n_iter - 1))
        chunks = iter_chunks(i)
        results = [local_compute(local_load(c)[1], *sm) for c, sm in zip(chunks, sums)]
        for c, res in zip(chunks, results):
            local_store(c, res)
        return nxt

    lax.fori_loop(0, n_iter, local_body, iter_sums(0))

    h_s[...] = jnp.zeros(h_s.shape, F32)

    def scan_steps(steps):
        work = []
        for c_fwd, c_bwd in steps:
            work += [(d, g, c) for d, c in ((0, c_fwd), (1, c_bwd)) for g in range(2)]
        loaded = []
        for d, g, c in work:
            k = c * 4 + d * 2 + g
            loaded.append((cme_s[2 * k], cme_s[2 * k + 1], da_s[k, 0:1, :], st_s[k],
                           y_s[chunk_rows(c), g * Q:(g + 1) * Q]))
        h = [h_s[k] for k in range(4)]
        ys = []
        for (d, g, c), (cme0, cme1, da, st, y) in zip(work, loaded):
            h_in = h[d * 2 + g]
            hb = h_in.astype(BF16)
            zero = jnp.zeros_like(hb)
            ys.append(y + _dot(cme0, jnp.where(lo_half, hb, zero)) + _dot(cme1, jnp.where(lo_half, zero, hb)))
            h[d * 2 + g] = da * h_in + st
        for (d, g, c), y in zip(work, ys):
            y_s[chunk_rows(c), g * Q:(g + 1) * Q] = y
        for k in range(4):
            h_s[k] = h[k]

    for i in range(N_CTX_CHUNK):
        scan_steps([(i, N_CTX_CHUNK - 1 - i)])

    def scan_body(t, carry):
        i0 = N_CTX_CHUNK + t * SCAN_STEPS
        scan_steps([(i0 + u, N_CHUNK - 1 + N_CTX_CHUNK - (i0 + u)) for u in range(SCAN_STEPS)])
        return carry

    lax.fori_loop(0, (N_CHUNK - N_CTX_CHUNK) // SCAN_STEPS, scan_body, 0)

    dskip = dskip_ref[...]
    onorm = onorm_ref[...]

    def finish(rows, z):
        y = y_s[rows, :] + dskip * xs_s[rows, :]
        return _rms(y * (z * _sigmoid(z)), onorm).astype(BF16)

    for c in range(N_CTX_CHUNK):
        yc_ref[c * Q:(c + 1) * Q, :] = finish(slice(c * Q, (c + 1) * Q), zc_ref[c * Q:(c + 1) * Q, :])

    def fin_body(c, carry):
        r0 = pl.multiple_of(c * Q, Q)
        yx_ref[pl.ds(r0, Q), :] = finish(pl.ds(CTX_LEN + r0, Q), zx_ref[pl.ds(r0, Q), :])
        return carry

    lax.fori_loop(0, SEQ // Q, fin_body, 0, unroll=2)


def _ssd_call(z, xbc, dt, cw, cb, dtb, alog, dskip, onorm):
    def cspec(wd):
        return pl.BlockSpec((None, CTX_LEN, wd), lambda b: (0, b, 0))

    def xspec(wd):
        return pl.BlockSpec((None, SEQ, wd), lambda b: (b + 1, 0, 0))

    return pl.pallas_call(
        _ssd_kernel,
        out_shape=[jax.ShapeDtypeStruct((BATCH, CTX_LEN, MIX), BF16),
                   jax.ShapeDtypeStruct((BATCH, SEQ, MIX), BF16)],
        grid=(BATCH,),
        in_specs=[cspec(MIX), xspec(MIX), cspec(D_CONV_CH), xspec(D_CONV_CH), cspec(LANE), xspec(LANE),
                  _full((8, D_CONV_CH)), _full((1, D_CONV_CH)), _full((1, LANE)), _full((1, LANE)),
                  _full((1, MIX)), _full((1, MIX))],
        out_specs=[pl.BlockSpec((None, CTX_LEN, MIX), lambda b: (b, 0, 0)),
                   pl.BlockSpec((None, SEQ, MIX), lambda b: (b, 0, 0))],
        scratch_shapes=[pltpu.VMEM((D_CONV_CH // LANE, U_ROWS, LANE), F32),
                        pltpu.VMEM((TOK, MIX), F32), pltpu.VMEM((TOK, MIX), BF16), pltpu.VMEM((TOK, MIX), BF16),
                        pltpu.VMEM((TOK, LANE), F32), pltpu.VMEM((TOK, MIX), F32),
                        pltpu.VMEM((N_CHUNK * 4, Q, Q), F32), pltpu.VMEM((N_CHUNK * 8, Q, Q), BF16),
                        pltpu.VMEM((N_CHUNK * 4, 8, LANE), F32), pltpu.VMEM((4, Q, Q), F32)],
        compiler_params=_params(("arbitrary",)),
        name="ssd",
    )(z, z, xbc, xbc, dt, dt, cw, cb, dtb, alog, dskip, onorm)


def _rope_tables():
    rows = SEQ // GRID_W
    r = np.repeat(np.arange(rows, dtype=np.float64), GRID_W)
    c = np.tile(np.arange(GRID_W, dtype=np.float64), rows)

    def tables(rot_dim):
        axis_dim = rot_dim // 2
        inv = ROPE_THETA ** (-np.arange(0, axis_dim, 2, dtype=np.float64) / axis_dim)
        ar = r[:, None] * inv[None, :]
        ac = c[:, None] * inv[None, :]
        cos = np.concatenate([np.cos(ar), np.cos(ar), np.cos(ac), np.cos(ac)], axis=-1)
        sin = np.concatenate([-np.sin(ar), np.sin(ar), -np.sin(ac), np.sin(ac)], axis=-1)
        return cos, sin

    c64, s64 = tables(HEAD_DIM)
    cab = np.tile(c64, (1, 2))
    sab = np.tile(s64, (1, 2))
    c32, s32 = tables(C_ROPE)
    cm = np.concatenate([np.ones((SEQ, C_NOPE)), c32, np.ones((SEQ, 32))], axis=-1)
    sm = np.concatenate([np.zeros((SEQ, C_NOPE)), s32, np.zeros((SEQ, 32))], axis=-1)
    return tuple(jnp.asarray(t, F32) for t in (cab, sab, cm, sm))


def _head_mean_matrix():
    lane = np.arange(LANE)
    same = (lane[:, None] // HEAD_DIM) == (lane[None, :] // HEAD_DIM)
    return jnp.asarray(np.where(same, 1.0 / HEAD_DIM, 0.0), BF16)


def _gqa_order(w, axis):
    shp = w.shape
    w = w.reshape(shp[:axis] + (2, 2, HEAD_DIM) + shp[axis + 1:])
    w = jnp.swapaxes(w, axis, axis + 1)
    return w.reshape(shp)


def _stacked_weights(w_in, c_w_uq, c_w_ukv, w_out):
    n_l, d = w_in.shape[:2]
    zc = lambda n: jnp.zeros((n_l, d, n), w_in.dtype)
    wb = w_in
    o_c = IN_AB
    o_d = IN_AB + IN_C
    w = jnp.concatenate([
        _gqa_order(wb[..., 0:MIX], 2), wb[..., MIX:512],
        _gqa_order(wb[..., 512:512 + MIX], 2), wb[..., 512 + MIX:o_c + C_Q_LORA + C_KV_LORA],
        zc(C_NOPE), wb[..., o_c + C_Q_LORA + C_KV_LORA:o_d], zc(LANE - C_NOPE - C_ROPE),
        wb[..., o_d:], zc(LANE - 2 * D_HEADS),
    ], axis=2).astype(BF16)
    assert w.shape[2] == PROJ_W
    dq = C_NOPE + C_ROPE
    pad_last = lambda t, n: jnp.pad(t, ((0, 0),) * (t.ndim - 1) + ((0, n),))
    wuq = pad_last(c_w_uq.reshape(n_l, C_Q_LORA, C_HEADS, dq), LANE - dq)
    wkv = c_w_ukv.reshape(n_l, C_KV_LORA, C_HEADS, C_NOPE + C_V)
    wk = pad_last(wkv[..., :C_NOPE], LANE - C_NOPE)
    head_parity = (jnp.arange(C_HEADS) % 2)[:, None]
    wv = jnp.stack([jnp.where(head_parity == half, wkv[..., C_NOPE:], 0.0) for half in range(2)], axis=3)
    wout = jnp.concatenate([_gqa_order(w_out[:, 0:MIX], 1), _gqa_order(w_out[:, MIX:2 * MIX], 1),
                            w_out[:, 2 * MIX:]], axis=1)
    return (w, wuq.reshape(n_l, C_Q_LORA, 512).astype(BF16), wk.reshape(n_l, C_KV_LORA, 512).astype(BF16),
            wv.reshape(n_l, C_KV_LORA, 512).astype(BF16), wout.astype(BF16))


def _lane_row(v, width=LANE):
    v = v.reshape(1, -1).astype(F32)
    return jnp.pad(v, ((0, 0), (0, width - v.shape[1])))


def kernel(x, c, ctx, c_ctx, ada_w, ada_b, ffn1_norm, ffn1_wi, ffn1_wo, mix_norm, w_in, w_out, a_sink, b_q_norm, b_k_norm, c_q_norm, c_w_uq, c_kv_norm, c_w_ukv, d_conv_w, d_conv_b, d_a_log, d_dt_bias, d_skip, d_out_norm, ffn2_norm, ffn2_wi, ffn2_wo, final_norm):
    cvec = jnp.concatenate([c_ctx[None, :], c, jnp.zeros((16 - NSEG, D_MODEL), F32)], axis=0)
    mods = _mods_call(cvec, ada_w, ada_b).reshape(DEPTH, 16, N_MOD, D_MODEL)[:, :NSEG]
    tabs = _rope_tables()
    bd = _head_mean_matrix()
    wi1, wo1, wi2, wo2 = ffn1_wi, ffn1_wo, ffn2_wi, ffn2_wo
    w, wuq, wk, wv, wout = _stacked_weights(w_in, c_w_uq, c_w_ukv, w_out)

    h = (ctx, x)
    out = None
    for l in range(DEPTH):
        with_ctx = l < DEPTH - 1
        mods_l = mods[l]
        h = _ffn_call(h, mods_l, ffn1_norm[l], wi1, wo1, l, k0=0, seg_off=0, name=f"ffn1_{l}")
        (qa, ka, va, qb, kb, vb, qc, kc, vc, z, xbc, dt) = _inproj_call(
            h, mods_l, mix_norm[l], w, l, tabs, bd,
            jnp.tile(b_q_norm[l], 2).reshape(1, LANE), jnp.tile(b_k_norm[l], 2).reshape(1, LANE),
            c_q_norm[l].reshape(1, C_Q_LORA), wuq, c_kv_norm[l].reshape(1, C_KV_LORA), wk, wv)

        oa = _attn_win(a_sink[l], qa, ka, va, with_ctx=with_ctx, name=f"attn_a_{l}")
        ob = _attn_dense(qb, kb, vb, heads=HEADS_GQA, with_ctx=with_ctx, name=f"attn_b_{l}")
        oc = _attn_dense(qc, kc, vc, heads=HEADS_MLA, with_ctx=with_ctx, name=f"attn_c_{l}")
        cw = jnp.pad(d_conv_w[l], ((0, 8 - D_CONV), (0, 0)))
        yc, yx = _ssd_call(z, xbc, dt, cw, d_conv_b[l].reshape(1, D_CONV_CH),
                           _lane_row(d_dt_bias[l]), _lane_row(d_a_log[l]),
                           jnp.repeat(d_skip[l], HEAD_DIM).reshape(1, MIX), d_out_norm[l].reshape(1, MIX))
        if with_ctx:
            h = _ffn_call(h, mods_l, ffn2_norm[l], wi2, wo2, l, k0=6, seg_off=0,
                          pre=(oa, ob, oc, (yc, yx), wout), name=f"ffn2_{l}")
        else:
            out = _ffn_call(h, mods_l, ffn2_norm[l], wi2, wo2, l, k0=6, seg_off=1,
                            pre=(oa, ob, oc, yx, wout), final_g=final_norm, name=f"ffn2_{l}")
    return out
```

```python
import functools

import numpy as np
import jax
import jax.numpy as jnp
from jax import lax
from jax.experimental import pallas as pl
from jax.experimental.pallas import tpu as pltpu

D_MODEL = 1024
BATCH = 8
SEQ = 2048
DEPTH = 2
GRID_W = 64
CTX_LEN = 256
HEAD_DIM = 64
ROPE_THETA = 10000.0
EPS = 1e-6
FFN_DIM = 2816
N_MOD = 9
WINDOW = 128
C_HEADS = 4
C_Q_LORA = 256
C_KV_LORA = 128
C_NOPE = 64
C_ROPE = 32
C_V = 64
D_HEADS = 4
D_STATE = 128
D_CONV = 5
SSD_CHUNK = 128
MIX = 256
IN_AB = 1024
IN_C = C_Q_LORA + C_KV_LORA + C_ROPE
D_CONV_CH = MIX + 2 * 2 * D_STATE
IN_D = MIX + D_CONV_CH + 2 * D_HEADS

NSEG = BATCH + 1
SEG = SEQ
assert BATCH * CTX_LEN == SEG
TOK = CTX_LEN + SEQ
N_CHUNK = TOK // SSD_CHUNK

LANE = 128
VMEM_LIMIT = 56 * 1024 * 1024

F32 = jnp.float32
BF16 = jnp.bfloat16
LOG2E = 1.4426950408889634


def _dot(a, b):
    return jnp.dot(a, b, preferred_element_type=F32)


def _dot_nt(a, b):
    return lax.dot_general(a, b, (((1,), (1,)), ((), ())), preferred_element_type=F32)


def _sigmoid(x):
    return 1.0 / (1.0 + jnp.exp(-x))


def _rms(x, g):
    return x * lax.rsqrt(jnp.mean(x * x, axis=-1, keepdims=True) + EPS) * g


def _full(shape):
    nd = len(shape)
    return pl.BlockSpec(shape, lambda *_: (0,) * nd)


def _layer_slab(shape, layer):
    nd = len(shape)
    return pl.BlockSpec((None,) + tuple(shape), lambda *_: (layer,) + (0,) * nd, pipeline_mode=pl.Buffered(1))


def _params(sem):
    return pltpu.CompilerParams(dimension_semantics=sem, vmem_limit_bytes=VMEM_LIMIT)


MODS_BN = 3 * D_MODEL


def _mods_kernel(c_ref, w_ref, b_ref, o_ref):
    c = c_ref[...]
    s = (c * _sigmoid(c)).astype(BF16)
    o_ref[...] = _dot(s, w_ref[...].astype(BF16)) + b_ref[...]


def _mods_call(cvec, ada_w, ada_b):
    n_l = ada_w.shape[0]
    bn = MODS_BN
    return pl.pallas_call(
        _mods_kernel,
        out_shape=jax.ShapeDtypeStruct((n_l, 16, N_MOD * D_MODEL), F32),
        grid=(n_l, N_MOD * D_MODEL // bn),
        in_specs=[
            pl.BlockSpec((16, D_MODEL), lambda l, n: (0, 0)),
            pl.BlockSpec((None, D_MODEL, bn), lambda l, n: (l, 0, n)),
            pl.BlockSpec((None, 1, bn), lambda l, n: (l, 0, n)),
        ],
        out_specs=pl.BlockSpec((None, 16, bn), lambda l, n: (l, 0, n)),
        compiler_params=_params(("arbitrary", "arbitrary")),
        name="mods",
    )(cvec, ada_w, ada_b.reshape(n_l, 1, N_MOD * D_MODEL))


FFN_TM = 512
FFN_SUB = 256
FFN_FC = 256
FFN_WI_ROWS = 128
FFN_WO_ROWS = 256
assert D_MODEL % FFN_WI_ROWS == 0 and FFN_DIM % FFN_WO_ROWS == 0


def _seg_specs(src, tm, width, seg_off):
    if isinstance(src, tuple):
        assert seg_off == 0
        ctx_arr, x_arr = src
        specs = [pl.BlockSpec((None, tm, width), lambda s, i: (0, jnp.where(s == 0, i, 0), 0)),
                 pl.BlockSpec((None, tm, width), lambda s, i: (jnp.maximum(s - 1, 0), jnp.where(s == 0, 0, i), 0))]
        return specs, [ctx_arr.reshape(1, SEG, width), x_arr]
    return [pl.BlockSpec((None, tm, width), lambda s, i: (s + seg_off, i, 0))], [src]


def _seg_read(refs):
    if len(refs) == 2:
        return jnp.where(pl.program_id(0) == 0, refs[0][...], refs[1][...])
    return refs[0][...]


def _fetch_as_bf16(src, dst, stage, sem, rows_per_copy):
    n_copy = src.shape[0] // rows_per_copy

    def copy(k):
        slot = k % 2
        return pltpu.make_async_copy(src.at[pl.ds(k * rows_per_copy, rows_per_copy), :], stage.at[slot],
                                     sem.at[slot])

    copy(0).start()
    for k in range(n_copy):
        if k + 1 < n_copy:
            copy(k + 1).start()
        copy(k).wait()
        dst[k * rows_per_copy:(k + 1) * rows_per_copy, :] = stage[k % 2].astype(BF16)


def _ffn_kernel(*refs, k0, layer, n_h, n_pre, has_final):
    refs = list(refs)
    h_refs = [refs.pop(0) for _ in range(n_h)]
    mods_ref, g_ref, wi_hbm, wo_hbm = (refs.pop(0) for _ in range(4))
    if n_pre:
        o_refs = [[refs.pop(0) for _ in range(n)] for n in n_pre]
        wout_ref = refs.pop(0)
    if has_final:
        gf_ref = refs.pop(0)
    out_ref, hm_ref, wi_ref, wo_ref, stage_i, stage_o, sem_i, sem_o = refs

    @pl.when(jnp.logical_and(pl.program_id(0) == 0, pl.program_id(1) == 0))
    def _():
        _fetch_as_bf16(wi_hbm.at[layer], wi_ref, stage_i, sem_i, FFN_WI_ROWS)
        _fetch_as_bf16(wo_hbm.at[layer], wo_ref, stage_o, sem_o, FFN_WO_ROWS)

    x = _seg_read(h_refs)
    m = mods_ref[...]
    if n_pre:
        o = None
        for k, o_ref in enumerate(o_refs):
            part = _dot(_seg_read(o_ref), wout_ref[k * MIX:(k + 1) * MIX, :])
            o = part if o is None else o + part
        x = x + m[5:6] * o
    halves = [slice(r * FFN_SUB, (r + 1) * FFN_SUB) for r in range(x.shape[0] // FFN_SUB)]
    xs = [x[rows] for rows in halves]
    nbs = [(_rms(xr, g_ref[...]) * (1.0 + m[k0 + 1:k0 + 2]) + m[k0:k0 + 1]).astype(BF16) for xr in xs]
    for rows, nb in zip(halves, nbs):
        for c in range(FFN_DIM // FFN_FC):
            lo = c * FFN_FC
            a = _dot(nb, wi_ref[:, lo:lo + FFN_FC])
            b = _dot(nb, wi_ref[:, FFN_DIM + lo:FFN_DIM + lo + FFN_FC])
            hm_ref[rows, lo:lo + FFN_FC] = (a * _sigmoid(a) * b).astype(BF16)
    ys = [_dot(hm_ref[rows, :], wo_ref[...]) for rows in halves]
    for rows, xr, y in zip(halves, xs, ys):
        out = xr + 0.5 * m[k0 + 2:k0 + 3] * y
        if has_final:
            out = _rms(out, gf_ref[...])
        out_ref[rows, :] = out


def _ffn_call(h, mods_l, g, wi, wo, layer, *, k0, seg_off, pre=None, final_g=None, name):
    tm = FFN_TM
    nseg = NSEG if isinstance(h, tuple) else h.shape[0] - seg_off
    in_specs, args = _seg_specs(h, tm, D_MODEL, seg_off)
    n_h = len(args)
    in_specs += [
        pl.BlockSpec((None, N_MOD, D_MODEL), lambda s, i: (s + seg_off, 0, 0)),
        _full((1, D_MODEL)),
        pl.BlockSpec(memory_space=pl.ANY),
        pl.BlockSpec(memory_space=pl.ANY),
    ]
    args += [mods_l, g.reshape(1, D_MODEL), wi, wo]
    n_pre = ()
    if pre is not None:
        *outs, wout = pre
        for o in outs:
            sp, ar = _seg_specs(o, tm, MIX, 0)
            in_specs += sp
            args += ar
            n_pre += (len(ar),)
        in_specs.append(_layer_slab((4 * MIX, D_MODEL), layer))
        args.append(wout)
    if final_g is not None:
        in_specs.append(_full((1, D_MODEL)))
        args.append(final_g.reshape(1, D_MODEL))
    kern = functools.partial(_ffn_kernel, k0=k0, layer=layer, n_h=n_h, n_pre=n_pre,
                             has_final=final_g is not None)
    return pl.pallas_call(
        kern,
        out_shape=jax.ShapeDtypeStruct((nseg, SEG, D_MODEL), F32),
        grid=(nseg, SEG // tm),
        in_specs=in_specs,
        out_specs=pl.BlockSpec((None, tm, D_MODEL), lambda s, i: (s, i, 0)),
        scratch_shapes=[pltpu.VMEM((tm, FFN_DIM), BF16),
                        pltpu.VMEM((D_MODEL, 2 * FFN_DIM), BF16), pltpu.VMEM((FFN_DIM, D_MODEL), BF16),
                        pltpu.VMEM((2, FFN_WI_ROWS, 2 * FFN_DIM), F32), pltpu.VMEM((2, FFN_WO_ROWS, D_MODEL), F32),
                        pltpu.SemaphoreType.DMA((2,)), pltpu.SemaphoreType.DMA((2,))],
        compiler_params=_params(("arbitrary", "arbitrary")),
        name=name,
    )(*args)


PROJ_TM = 1024
PROJ_SUB = PROJ_TM
COL_AB = 0
COL_C = IN_AB
COL_D = COL_C + 512
PROJ_W = COL_D + MIX + D_CONV_CH + LANE


def _swap_halves(x, half):
    lane = lax.broadcasted_iota(jnp.int32, x.shape, 1)
    first = (lane & half) == 0
    up = pltpu.roll(x, LANE - half, axis=1)
    dn = pltpu.roll(x, half, axis=1)
    return jnp.where(first, up, dn)


def _rope(x, cos, sin, half):
    return x * cos + _swap_halves(x, half) * sin


def _head_rms(x, bd, g):
    sq = x * x
    hi = sq.astype(BF16)
    lo = (sq - hi.astype(F32)).astype(BF16)
    ms = _dot(hi, bd) + _dot(lo, bd)
    return x * lax.rsqrt(ms + EPS) * g


def _inproj_kernel(h_ref, mods_ref, g_ref, w_ref, cab_ref, sab_ref, cm_ref, sm_ref, bd_ref,
                   bqn_ref, bkn_ref, cqn_ref, wuq_ref, ckvn_ref, wukvk_ref, wukvv_ref,
                   qa_ref, ka_ref, va_ref, qb_ref, kb_ref, vb_ref, qc_ref, kc_ref, vc_ref,
                   z_ref, xbc_ref, dt_ref):
    m = mods_ref[...]
    is_ctx = pl.program_id(0) == 0
    bd = bd_ref[...]
    scale_ab = HEAD_DIM ** -0.5 * LOG2E
    scale_c = (C_NOPE + C_ROPE) ** -0.5 * LOG2E

    def project(rows):
        nb = (_rms(h_ref[rows, :], g_ref[...]) * (1.0 + m[4:5]) + m[3:4]).astype(BF16)
        p_ab = [_dot(nb, w_ref[:, COL_AB + mixer * 512:COL_AB + (mixer + 1) * 512]) for mixer in range(2)]
        p_c = _dot(nb, w_ref[:, COL_C:COL_C + 512])
        z_ref[rows, :] = _dot(nb, w_ref[:, COL_D:COL_D + MIX])
        for c in range(D_CONV_CH // 256):
            lo = COL_D + MIX + c * 256
            xbc_ref[rows, c * 256:(c + 1) * 256] = _dot(nb, w_ref[:, lo:lo + 256])
        lo = COL_D + MIX + D_CONV_CH
        dt_ref[rows, :] = _dot(nb, w_ref[:, lo:lo + LANE])
        return p_ab, p_c

    def epilogue(rows, p_ab, p_c):
        cab, sab = jnp.where(is_ctx, 1.0, cab_ref[rows, :]), jnp.where(is_ctx, 0.0, sab_ref[rows, :])
        cm, sm = jnp.where(is_ctx, 1.0, cm_ref[rows, :]), jnp.where(is_ctx, 0.0, sm_ref[rows, :])
        for mixer, (q_ref, k_ref, v_ref) in enumerate(((qa_ref, ka_ref, va_ref), (qb_ref, kb_ref, vb_ref))):
            p = p_ab[mixer]
            for s in range(2):
                q = p[:, s * LANE:(s + 1) * LANE]
                if mixer == 1:
                    q = _head_rms(q, bd, bqn_ref[...])
                q = _rope(q, cab, sab, 16) * scale_ab
                q_ref[rows, s * LANE:(s + 1) * LANE] = q.astype(BF16)
            k = p[:, 2 * LANE:3 * LANE]
            if mixer == 1:
                k = _head_rms(k, bd, bkn_ref[...])
            k_ref[rows, :] = _rope(k, cab, sab, 16).astype(BF16)
            v = p[:, 3 * LANE:4 * LANE]
            lane = lax.broadcasted_iota(jnp.int32, v.shape, 1)
            v_ref[rows, 0:LANE] = jnp.where(lane < HEAD_DIM, v, 1.0).astype(BF16)
            v_ref[rows, LANE:2 * LANE] = jnp.where(lane < HEAD_DIM, 1.0, v).astype(BF16)

        p = p_c
        cq = _rms(p[:, 0:C_Q_LORA], cqn_ref[...]).astype(BF16)
        q = _dot(cq, wuq_ref[...])
        ckv = _rms(p[:, C_Q_LORA:C_Q_LORA + C_KV_LORA], ckvn_ref[...]).astype(BF16)
        kn = _dot(ckv, wukvk_ref[...])
        lane = lax.broadcasted_iota(jnp.int32, (1, C_HEADS * LANE), 1)
        own_half = ((lane // HEAD_DIM) % 2) == ((lane // LANE) % 2)
        vc_ref[rows, :] = jnp.where(own_half, _dot(ckv, wukvv_ref[...]), 1.0).astype(BF16)
        kr = _rope(p[:, 3 * LANE:4 * LANE], cm, sm, 8)
        for hh in range(C_HEADS):
            sl = slice(hh * LANE, (hh + 1) * LANE)
            qc_ref[rows, sl] = (_rope(q[:, sl], cm, sm, 8) * scale_c).astype(BF16)
            kc_ref[rows, sl] = (kn[:, sl] + kr).astype(BF16)

    blocks = [slice(r * PROJ_SUB, (r + 1) * PROJ_SUB) for r in range(h_ref.shape[0] // PROJ_SUB)]
    projected = [project(rows) for rows in blocks]
    for rows, (p_ab, p_c) in zip(blocks, projected):
        epilogue(rows, p_ab, p_c)


PROJ_OUT = ([(MIX, BF16), (LANE, BF16), (MIX, BF16)] * 2 + [(512, BF16), (512, BF16), (512, BF16)]
            + [(MIX, F32), (D_CONV_CH, F32), (LANE, F32)])


def _inproj_call(h, mods_l, g, w, layer, tabs, bd, bqn, bkn, cqn, wuq, ckvn, wukvk, wukvv):
    tm = PROJ_TM
    cab, sab, cm, sm = tabs
    tok = lambda s, i: (s, i, 0)
    tab = pl.BlockSpec((tm, LANE), lambda s, i: (i, 0))
    return pl.pallas_call(
        _inproj_kernel,
        out_shape=[jax.ShapeDtypeStruct((NSEG, SEG, wd), dt) for wd, dt in PROJ_OUT],
        grid=(NSEG, SEG // tm),
        in_specs=[
            pl.BlockSpec((None, tm, D_MODEL), tok),
            pl.BlockSpec((None, N_MOD, D_MODEL), lambda s, i: (s, 0, 0)),
            _full((1, D_MODEL)),
            _layer_slab((D_MODEL, PROJ_W), layer),
            tab, tab, tab, tab,
            _full((LANE, LANE)),
            _full((1, LANE)), _full((1, LANE)),
            _full((1, C_Q_LORA)), _layer_slab((C_Q_LORA, 512), layer),
            _full((1, C_KV_LORA)), _layer_slab((C_KV_LORA, 512), layer), _layer_slab((C_KV_LORA, 512), layer),
        ],
        out_specs=[pl.BlockSpec((None, tm, wd), tok) for wd, _ in PROJ_OUT],
        compiler_params=_params(("arbitrary", "arbitrary")),
        name="inproj",
    )(h, mods_l, g.reshape(1, D_MODEL), w, cab, sab, cm, sm, bd,
      bqn, bkn, cqn, wuq, ckvn, wukvk, wukvv)


HEADS_GQA = tuple((g, kv, 0, kv, g, kv) for g in range(2) for kv in range(2))
HEADS_MLA = tuple((h, None, h, h, h // 2, h % 2) for h in range(C_HEADS))


def _normalise(o, half, extra_den=None):
    lane = lax.broadcasted_iota(jnp.int32, o.shape, 1)
    valid = (lane < HEAD_DIM) if half == 0 else (lane >= HEAD_DIM)
    den = pltpu.roll(o, HEAD_DIM, axis=1)
    if extra_den is not None:
        den = den + extra_den
    return o * (1.0 / jnp.where(valid, den, 1.0))


def _masked_q(q, half):
    if half is None:
        return q
    lane = lax.broadcasted_iota(jnp.int32, q.shape, 1)
    keep = (lane < HEAD_DIM) if half == 0 else (lane >= HEAD_DIM)
    return jnp.where(keep, q, jnp.zeros_like(q))


def _store_heads(o_ref, rows, outs):
    lane = lax.broadcasted_iota(jnp.int32, outs[(0, 0)].shape, 1)
    for s in range(2):
        o = jnp.where(lane < HEAD_DIM, outs[(s, 0)], outs[(s, 1)])
        o_ref[rows, s * LANE:(s + 1) * LANE] = o.astype(o_ref.dtype)


ATT_TQ = 1024
ATT_SUB = 256
ATT_LOOKAHEAD = 2


def _attn_dense_kernel(q_ref, kc_ref, kx_ref, vc_ref, vx_ref, o_ref, *, heads):
    n_sub = q_ref.shape[0] // ATT_SUB
    stages = [(sub, hd) for sub in range(n_sub) for hd in heads]

    def run(with_x):
        def scores(stage):
            sub, (qs, qhalf, ks) = stage[0], stage[1][:3]
            qh = _masked_q(q_ref[sub * ATT_SUB:(sub + 1) * ATT_SUB, qs * LANE:(qs + 1) * LANE], qhalf)
            ksl = slice(ks * LANE, (ks + 1) * LANE)
            return _dot_nt(qh, kc_ref[:, ksl]), (_dot_nt(qh, kx_ref[:, ksl]) if with_x else None)

        outs = {}
        pending = [scores(st) for st in stages[:ATT_LOOKAHEAD]]
        for i, (sub, (_, _, _, vs, os_, ohalf)) in enumerate(stages):
            s_c, s_x = pending.pop(0)
            if i + ATT_LOOKAHEAD < len(stages):
                pending.append(scores(stages[i + ATT_LOOKAHEAD]))
            vsl = slice(vs * LANE, (vs + 1) * LANE)
            mx = jnp.max(s_c, axis=-1, keepdims=True)
            if with_x:
                mx = jnp.maximum(mx, jnp.max(s_x, axis=-1, keepdims=True))
            o = _dot(jnp.exp2((s_c - mx).astype(BF16)), vc_ref[:, vsl])
            if with_x:
                o = o + _dot(jnp.exp2((s_x - mx).astype(BF16)), vx_ref[:, vsl])
            outs[(os_, ohalf)] = _normalise(o, ohalf)
            if len(outs) == len(heads):
                _store_heads(o_ref, slice(sub * ATT_SUB, (sub + 1) * ATT_SUB), outs)
                outs = {}

    run(kx_ref is not None)


def _attn_ctx_kernel(q_ref, kc_ref, vc_ref, o_ref, *, heads):
    _attn_dense_kernel(q_ref, kc_ref, None, vc_ref, None, o_ref, heads=heads)


def _attn_latent_call(kern, q, k, v, *, tq, name, smem=()):
    kw, vw = k.shape[-1], v.shape[-1]
    return pl.pallas_call(
        kern,
        out_shape=jax.ShapeDtypeStruct((BATCH, SEQ, MIX), BF16),
        grid=(BATCH, SEQ // tq),
        in_specs=[pl.BlockSpec(memory_space=pltpu.SMEM)] * len(smem) + [
            pl.BlockSpec((None, tq, q.shape[-1]), lambda b, j: (b + 1, j, 0)),
            pl.BlockSpec((None, CTX_LEN, kw), lambda b, j: (0, b, 0)),
            pl.BlockSpec((None, SEQ, kw), lambda b, j: (b + 1, 0, 0)),
            pl.BlockSpec((None, CTX_LEN, vw), lambda b, j: (0, b, 0)),
            pl.BlockSpec((None, SEQ, vw), lambda b, j: (b + 1, 0, 0))],
        out_specs=pl.BlockSpec((None, tq, MIX), lambda b, j: (b, j, 0)),
        compiler_params=_params(("arbitrary", "arbitrary")),
        name=name,
    )(*smem, q, k, k, v, v)


def _attn_ctx_call(kern, q, k, v, *, name, smem=()):
    ctx = lambda wd: pl.BlockSpec((None, CTX_LEN, wd), lambda b: (0, b, 0))
    return pl.pallas_call(
        kern,
        out_shape=jax.ShapeDtypeStruct((BATCH, CTX_LEN, MIX), BF16),
        grid=(BATCH,),
        in_specs=[pl.BlockSpec(memory_space=pltpu.SMEM)] * len(smem)
        + [ctx(q.shape[-1]), ctx(k.shape[-1]), ctx(v.shape[-1])],
        out_specs=pl.BlockSpec((None, CTX_LEN, MIX), lambda b: (b, 0, 0)),
        compiler_params=_params(("arbitrary",)),
        name=name,
    )(*smem, q, k, v)


def _attn_dense(q, k, v, *, heads, with_ctx, name):
    ox = _attn_latent_call(functools.partial(_attn_dense_kernel, heads=heads), q, k, v, tq=ATT_TQ, name=name)
    if not with_ctx:
        return ox
    return _attn_ctx_call(functools.partial(_attn_ctx_kernel, heads=heads), q, k, v, name=name + "_ctx"), ox


WIN_TQ = 1024


def _attn_win_kernel(sink_ref, q_ref, kc_ref, kx_ref, vc_ref, vx_ref, o_ref):
    local = kx_ref is not None
    n_sub = q_ref.shape[0] // WINDOW

    n_h = len(HEADS_GQA)
    head_of_row = lax.broadcasted_iota(jnp.int32, (n_h * WINDOW, 1), 0) // WINDOW
    sink = jnp.zeros((n_h * WINDOW, 1), F32)
    for i, (qs, qhalf, _, _, _, _) in enumerate(HEADS_GQA):
        sink = jnp.where(head_of_row == i, sink_ref[qhalf * 2 + qs] * LOG2E, sink)
    n_blk = SEQ // WINDOW
    q_in_blk = lax.broadcasted_iota(jnp.int32, (n_h * WINDOW, WINDOW), 0) % WINDOW
    k_in_blk = lax.broadcasted_iota(jnp.int32, (n_h * WINDOW, WINDOW), 1)
    key_ge_query = k_in_blk >= q_in_blk
    key_le_query = k_in_blk <= q_in_blk

    def run():
        def scores(sub):
            rows = slice(sub * WINDOW, (sub + 1) * WINDOW)
            q4 = jnp.concatenate([_masked_q(q_ref[rows, qs * LANE:(qs + 1) * LANE], qhalf)
                                  for qs, qhalf, _, _, _, _ in HEADS_GQA], axis=0)
            s_c = _dot_nt(q4, kc_ref[...])
            if not local:
                return s_c, None, None
            n = pl.program_id(1) * n_sub + sub
            blocks = [pl.ds(pl.multiple_of(b * WINDOW, WINDOW), WINDOW)
                      for b in (jnp.maximum(n - 1, 0), n, jnp.minimum(n + 1, n_blk - 1))]
            s = _dot_nt(q4, jnp.concatenate([kx_ref[blk, :] for blk in blocks], axis=0))
            s_l = jnp.concatenate([
                jnp.where(jnp.logical_and(key_ge_query, n > 0), s[:, 0:WINDOW], -jnp.inf),
                s[:, WINDOW:2 * WINDOW],
                jnp.where(jnp.logical_and(key_le_query, n < n_blk - 1), s[:, 2 * WINDOW:], -jnp.inf)], axis=1)
            return s_c, s_l, blocks

        pending = [scores(sub) for sub in range(min(ATT_LOOKAHEAD, n_sub))]
        for sub in range(n_sub):
            s_c, s_l, blocks = pending.pop(0)
            if sub + ATT_LOOKAHEAD < n_sub:
                pending.append(scores(sub + ATT_LOOKAHEAD))
            rows = slice(sub * WINDOW, (sub + 1) * WINDOW)
            mx = jnp.maximum(jnp.max(s_c, axis=-1, keepdims=True), sink)
            if local:
                mx = jnp.maximum(mx, jnp.max(s_l, axis=-1, keepdims=True))
            o = _dot(jnp.exp2((s_c - mx).astype(BF16)), vc_ref[...])
            if local:
                v3 = jnp.concatenate([vx_ref[blk, :] for blk in blocks], axis=0)
                o = o + _dot(jnp.exp2((s_l - mx).astype(BF16)), v3)
            sink_den = jnp.exp2(sink - mx)
            outs = {}
            for i, (_, _, _, vs, os_, ohalf) in enumerate(HEADS_GQA):
                blk = slice(i * WINDOW, (i + 1) * WINDOW)
                outs[(os_, ohalf)] = _normalise(o[blk, vs * LANE:(vs + 1) * LANE], ohalf, sink_den[blk])
            _store_heads(o_ref, rows, outs)

    run()


def _attn_win_ctx_kernel(sink_ref, q_ref, kc_ref, vc_ref, o_ref):
    _attn_win_kernel(sink_ref, q_ref, kc_ref, None, vc_ref, None, o_ref)


def _attn_win(sink, q, k, v, *, with_ctx, name):
    ox = _attn_latent_call(_attn_win_kernel, q, k, v, tq=WIN_TQ, name=name, smem=(sink,))
    if not with_ctx:
        return ox
    return _attn_ctx_call(_attn_win_ctx_kernel, q, k, v, name=name + "_ctx", smem=(sink,)), ox


Q = SSD_CHUNK
PAD = 8
U_CTX = PAD
U_X = PAD + CTX_LEN + PAD
U_ROWS = U_X + SEQ + PAD
N_CTX_CHUNK = CTX_LEN // Q
LOCAL_CHUNKS = 6
SCAN_STEPS = 4


def _split3(a):
    a1 = a.astype(BF16)
    r1 = a - a1.astype(F32)
    a2 = r1.astype(BF16)
    a3 = (r1 - a2.astype(F32)).astype(BF16)
    return a1, a2, a3


def _ssd_kernel(zc_ref, zx_ref, uc_ref, ux_ref, dtc_ref, dtx_ref, cw_ref, cb_ref, dtb_ref, alog_ref,
                dskip_ref, onorm_ref, yc_ref, yx_ref, upad, xs_s, bm_s, cm_s, dt_s, y_s, st_s, cme_s, da_s, h_s):
    n_slab = D_CONV_CH // LANE
    zpad = jnp.zeros((PAD, LANE), F32)
    for j in range(n_slab):
        sl = slice(j * LANE, (j + 1) * LANE)
        upad[j, 0:PAD, :] = zpad
        upad[j, U_CTX:U_CTX + CTX_LEN, :] = uc_ref[:, sl]
        upad[j, U_CTX + CTX_LEN:U_X, :] = zpad
        upad[j, U_X:U_X + SEQ, :] = ux_ref[:, sl]
        upad[j, U_X + SEQ:U_ROWS, :] = zpad

    for c in range(N_CHUNK):
        base = U_CTX + c * Q if c < N_CTX_CHUNK else U_X + (c - N_CTX_CHUNK) * Q
        for j in range(n_slab):
            sl = slice(j * LANE, (j + 1) * LANE)
            dst = (xs_s, bm_s, cm_s)[j // 2]
            acc = jnp.broadcast_to(cb_ref[:, sl], (Q, LANE))
            for k in range(D_CONV):
                lo = base + k - D_CONV // 2
                acc = acc + upad[j, lo:lo + Q, :] * cw_ref[k:k + 1, sl]
            dcol = (j % 2) * LANE
            dst[c * Q:(c + 1) * Q, dcol:dcol + LANE] = (acc * _sigmoid(acc)).astype(dst.dtype)

    def softplus(v):
        return jnp.maximum(v, 0.0) + jnp.log(1.0 + jnp.exp(-jnp.abs(v)))

    dt_s[0:CTX_LEN, :] = softplus(dtc_ref[...] + dtb_ref[...])
    dt_s[CTX_LEN:TOK, :] = softplus(dtx_ref[...] + dtb_ref[...])

    a_neg = -jnp.exp(alog_ref[...]) * LOG2E
    row = lax.broadcasted_iota(jnp.int32, (Q, Q), 0)
    col = lax.broadcasted_iota(jnp.int32, (Q, Q), 1)
    causal = (col <= row, col >= row)
    tri = (causal[0].astype(BF16), causal[1].astype(BF16))
    lo_half = col < HEAD_DIM
    last_row = (Q - 1, 0)

    def bcast_col(v, idx):
        return jnp.broadcast_to(v[:, idx:idx + 1], (Q, Q))

    def chunk_rows(c):
        return pl.ds(c * Q if isinstance(c, int) else pl.multiple_of(c * Q, Q), Q)

    def local_load(c):
        rows = chunk_rows(c)
        groups = [(cm_s[rows, g * Q:(g + 1) * Q], bm_s[rows, g * Q:(g + 1) * Q], xs_s[rows, g * Q:(g + 1) * Q])
                  for g in range(2)]
        return dt_s[rows, :], groups

    def local_sums(dt, groups):
        a1, a2, a3 = _split3(dt * a_neg)
        cs_f = _dot(tri[0], a1) + _dot(tri[0], a2) + _dot(tri[0], a3)
        cs_b = _dot(tri[1], a1) + _dot(tri[1], a2) + _dot(tri[1], a3)
        cs = jnp.where(col < D_HEADS, cs_f, cs_b)
        n_row = 2 * D_HEADS
        return cs, cs.T[0:n_row], dt.T[0:n_row], [_dot_nt(cmg, bmg) for cmg, bmg, _ in groups]

    def local_compute(groups, cs, cs_t, dt_t, cbms):
        ys, sts, das, cmes = [], {}, {}, {}
        for g, (cmg, bmg, xsg) in enumerate(groups):
            cbm = cbms[g]
            cmg32 = cmg.astype(F32)
            bm_t = bmg.astype(F32).T
            xh = (jnp.where(lo_half, xsg, 0.0).astype(BF16), jnp.where(lo_half, 0.0, xsg).astype(BF16))
            y = None
            for d in range(2):
                st = None
                da = []
                for hh in range(2):
                    idx = d * D_HEADS + g * 2 + hh
                    colb = bcast_col(cs, idx)
                    cs_row = cs_t[idx:idx + 1, :]
                    dt_row = dt_t[idx:idx + 1, :]
                    seg = colb - (cs_row - jnp.log2(dt_row))
                    dec = jnp.exp2(jnp.where(causal[d], seg, -jnp.inf))
                    yd = _dot((cbm * dec).astype(BF16), xh[hh])
                    y = yd if y is None else y + yd
                    last = cs_row[:, last_row[d]:last_row[d] + 1]
                    w_row = dt_row * jnp.exp2(last - cs_row)
                    sth = _dot((bm_t * w_row).astype(BF16), xh[hh])
                    st = sth if st is None else st + sth
                    cmes[d * 4 + g * 2 + hh] = (cmg32 * jnp.exp2(colb)).astype(BF16)
                    da.append(jnp.broadcast_to(jnp.exp2(last), (1, Q)))
                sts[d * 2 + g] = st
                das[d * 2 + g] = jnp.where(lo_half[0:1, :], da[0], da[1])
            ys.append(y)
        return ys, sts, das, cmes

    def local_store(c, res):
        ys, sts, das, cmes = res
        rows = chunk_rows(c)
        for g in range(2):
            y_s[rows, g * Q:(g + 1) * Q] = ys[g]
        for k, v in sts.items():
            st_s[c * 4 + k] = v
        for k, v in das.items():
            da_s[c * 4 + k, 0:1, :] = v
        for k, v in cmes.items():
            cme_s[c * 8 + k] = v

    n_iter = N_CHUNK // LOCAL_CHUNKS

    def iter_chunks(i):
        return [i * LOCAL_CHUNKS + u for u in range(LOCAL_CHUNKS)]

    def iter_sums(i):
        return [local_sums(*local_load(c)) for c in iter_chunks(i)]

    def local_body(i, sums):
        nxt = iter_sums(jnp.minimum(i + 1, n_iter - 1))
        chunks = iter_chunks(i)
        results = [local_compute(local_load(c)[1], *sm) for c, sm in zip(chunks, sums)]
        for c, res in zip(chunks, results):
            local_store(c, res)
        return nxt

    lax.fori_loop(0, n_iter, local_body, iter_sums(0))

    h_s[...] = jnp.zeros(h_s.shape, F32)

    def scan_steps(steps):
        work = []
        for c_fwd, c_bwd in steps:
            work += [(d, g, c) for d, c in ((0, c_fwd), (1, c_bwd)) for g in range(2)]
        loaded = []
        for d, g, c in work:
            k = c * 4 + d * 2 + g
            loaded.append((cme_s[2 * k], cme_s[2 * k + 1], da_s[k, 0:1, :], st_s[k],
                           y_s[chunk_rows(c), g * Q:(g + 1) * Q]))
        h = [h_s[k] for k in range(4)]
        ys = []
        for (d, g, c), (cme0, cme1, da, st, y) in zip(work, loaded):
            h_in = h[d * 2 + g]
            hb = h_in.astype(BF16)
            zero = jnp.zeros_like(hb)
            ys.append(y + _dot(cme0, jnp.where(lo_half, hb, zero)) + _dot(cme1, jnp.where(lo_half, zero, hb)))
            h[d * 2 + g] = da * h_in + st
        for (d, g, c), y in zip(work, ys):
            y_s[chunk_rows(c), g * Q:(g + 1) * Q] = y
        for k in range(4):
            h_s[k] = h[k]

    for i in range(N_CTX_CHUNK):
        scan_steps([(i, N_CTX_CHUNK - 1 - i)])

    def scan_body(t, carry):
        i0 = N_CTX_CHUNK + t * SCAN_STEPS
        scan_steps([(i0 + u, N_CHUNK - 1 + N_CTX_CHUNK - (i0 + u)) for u in range(SCAN_STEPS)])
        return carry

    lax.fori_loop(0, (N_CHUNK - N_CTX_CHUNK) // SCAN_STEPS, scan_body, 0)

    dskip = dskip_ref[...]
    onorm = onorm_ref[...]

    def finish(rows, z):
        y = y_s[rows, :] + dskip * xs_s[rows, :]
        return _rms(y * (z * _sigmoid(z)), onorm).astype(BF16)

    for c in range(N_CTX_CHUNK):
        yc_ref[c * Q:(c + 1) * Q, :] = finish(slice(c * Q, (c + 1) * Q), zc_ref[c * Q:(c + 1) * Q, :])

    def fin_body(c, carry):
        r0 = pl.multiple_of(c * Q, Q)
        yx_ref[pl.ds(r0, Q), :] = finish(pl.ds(CTX_LEN + r0, Q), zx_ref[pl.ds(r0, Q), :])
        return carry

    lax.fori_loop(0, SEQ // Q, fin_body, 0, unroll=2)


def _ssd_call(z, xbc, dt, cw, cb, dtb, alog, dskip, onorm):
    def cspec(wd):
        return pl.BlockSpec((None, CTX_LEN, wd), lambda b: (0, b, 0))

    def xspec(wd):
        return pl.BlockSpec((None, SEQ, wd), lambda b: (b + 1, 0, 0))

    return pl.pallas_call(
        _ssd_kernel,
        out_shape=[jax.ShapeDtypeStruct((BATCH, CTX_LEN, MIX), BF16),
                   jax.ShapeDtypeStruct((BATCH, SEQ, MIX), BF16)],
        grid=(BATCH,),
        in_specs=[cspec(MIX), xspec(MIX), cspec(D_CONV_CH), xspec(D_CONV_CH), cspec(LANE), xspec(LANE),
                  _full((8, D_CONV_CH)), _full((1, D_CONV_CH)), _full((1, LANE)), _full((1, LANE)),
                  _full((1, MIX)), _full((1, MIX))],
        out_specs=[pl.BlockSpec((None, CTX_LEN, MIX), lambda b: (b, 0, 0)),
                   pl.BlockSpec((None, SEQ, MIX), lambda b: (b, 0, 0))],
        scratch_shapes=[pltpu.VMEM((D_CONV_CH // LANE, U_ROWS, LANE), F32),
                        pltpu.VMEM((TOK, MIX), F32), pltpu.VMEM((TOK, MIX), BF16), pltpu.VMEM((TOK, MIX), BF16),
                        pltpu.VMEM((TOK, LANE), F32), pltpu.VMEM((TOK, MIX), F32),
                        pltpu.VMEM((N_CHUNK * 4, Q, Q), F32), pltpu.VMEM((N_CHUNK * 8, Q, Q), BF16),
                        pltpu.VMEM((N_CHUNK * 4, 8, LANE), F32), pltpu.VMEM((4, Q, Q), F32)],
        compiler_params=_params(("arbitrary",)),
        name="ssd",
    )(z, z, xbc, xbc, dt, dt, cw, cb, dtb, alog, dskip, onorm)


def _rope_tables():
    rows = SEQ // GRID_W
    r = np.repeat(np.arange(rows, dtype=np.float64), GRID_W)
    c = np.tile(np.arange(GRID_W, dtype=np.float64), rows)

    def tables(rot_dim):
        axis_dim = rot_dim // 2
        inv = ROPE_THETA ** (-np.arange(0, axis_dim, 2, dtype=np.float64) / axis_dim)
        ar = r[:, None] * inv[None, :]
        ac = c[:, None] * inv[None, :]
        cos = np.concatenate([np.cos(ar), np.cos(ar), np.cos(ac), np.cos(ac)], axis=-1)
        sin = np.concatenate([-np.sin(ar), np.sin(ar), -np.sin(ac), np.sin(ac)], axis=-1)
        return cos, sin

    c64, s64 = tables(HEAD_DIM)
    cab = np.tile(c64, (1, 2))
    sab = np.tile(s64, (1, 2))
    c32, s32 = tables(C_ROPE)
    cm = np.concatenate([np.ones((SEQ, C_NOPE)), c32, np.ones((SEQ, 32))], axis=-1)
    sm = np.concatenate([np.zeros((SEQ, C_NOPE)), s32, np.zeros((SEQ, 32))], axis=-1)
    return tuple(jnp.asarray(t, F32) for t in (cab, sab, cm, sm))


def _head_mean_matrix():
    lane = np.arange(LANE)
    same = (lane[:, None] // HEAD_DIM) == (lane[None, :] // HEAD_DIM)
    return jnp.asarray(np.where(same, 1.0 / HEAD_DIM, 0.0), BF16)


def _gqa_order(w, axis):
    shp = w.shape
    w = w.reshape(shp[:axis] + (2, 2, HEAD_DIM) + shp[axis + 1:])
    w = jnp.swapaxes(w, axis, axis + 1)
    return w.reshape(shp)


def _stacked_weights(w_in, c_w_uq, c_w_ukv, w_out):
    n_l, d = w_in.shape[:2]
    zc = lambda n: jnp.zeros((n_l, d, n), w_in.dtype)
    wb = w_in
    o_c = IN_AB
    o_d = IN_AB + IN_C
    w = jnp.concatenate([
        _gqa_order(wb[..., 0:MIX], 2), wb[..., MIX:512],
        _gqa_order(wb[..., 512:512 + MIX], 2), wb[..., 512 + MIX:o_c + C_Q_LORA + C_KV_LORA],
        zc(C_NOPE), wb[..., o_c + C_Q_LORA + C_KV_LORA:o_d], zc(LANE - C_NOPE - C_ROPE),
        wb[..., o_d:], zc(LANE - 2 * D_HEADS),
    ], axis=2).astype(BF16)
    assert w.shape[2] == PROJ_W
    dq = C_NOPE + C_ROPE
    pad_last = lambda t, n: jnp.pad(t, ((0, 0),) * (t.ndim - 1) + ((0, n),))
    wuq = pad_last(c_w_uq.reshape(n_l, C_Q_LORA, C_HEADS, dq), LANE - dq)
    wkv = c_w_ukv.reshape(n_l, C_KV_LORA, C_HEADS, C_NOPE + C_V)
    wk = pad_last(wkv[..., :C_NOPE], LANE - C_NOPE)
    head_parity = (jnp.arange(C_HEADS) % 2)[:, None]
    wv = jnp.stack([jnp.where(head_parity == half, wkv[..., C_NOPE:], 0.0) for half in range(2)], axis=3)
    wout = jnp.concatenate([_gqa_order(w_out[:, 0:MIX], 1), _gqa_order(w_out[:, MIX:2 * MIX], 1),
                            w_out[:, 2 * MIX:]], axis=1)
    return (w, wuq.reshape(n_l, C_Q_LORA, 512).astype(BF16), wk.reshape(n_l, C_KV_LORA, 512).astype(BF16),
            wv.reshape(n_l, C_KV_LORA, 512).astype(BF16), wout.astype(BF16))


def _lane_row(v, width=LANE):
    v = v.reshape(1, -1).astype(F32)
    return jnp.pad(v, ((0, 0), (0, width - v.shape[1])))


def kernel(x, c, ctx, c_ctx, ada_w, ada_b, ffn1_norm, ffn1_wi, ffn1_wo, mix_norm, w_in, w_out, a_sink, b_q_norm, b_k_norm, c_q_norm, c_w_uq, c_kv_norm, c_w_ukv, d_conv_w, d_conv_b, d_a_log, d_dt_bias, d_skip, d_out_norm, ffn2_norm, ffn2_wi, ffn2_wo, final_norm):
    cvec = jnp.concatenate([c_ctx[None, :], c, jnp.zeros((16 - NSEG, D_MODEL), F32)], axis=0)
    mods = _mods_call(cvec, ada_w, ada_b).reshape(DEPTH, 16, N_MOD, D_MODEL)[:, :NSEG]
    tabs = _rope_tables()
    bd = _head_mean_matrix()
    wi1, wo1, wi2, wo2 = ffn1_wi, ffn1_wo, ffn2_wi, ffn2_wo
    w, wuq, wk, wv, wout = _stacked_weights(w_in, c_w_uq, c_w_ukv, w_out)

    h = (ctx, x)
    out = None
    for l in range(DEPTH):
        with_ctx = l < DEPTH - 1
        mods_l = mods[l]
        h = _ffn_call(h, mods_l, ffn1_norm[l], wi1, wo1, l, k0=0, seg_off=0, name=f"ffn1_{l}")
        (qa, ka, va, qb, kb, vb, qc, kc, vc, z, xbc, dt) = _inproj_call(
            h, mods_l, mix_norm[l], w, l, tabs, bd,
            jnp.tile(b_q_norm[l], 2).reshape(1, LANE), jnp.tile(b_k_norm[l], 2).reshape(1, LANE),
            c_q_norm[l].reshape(1, C_Q_LORA), wuq, c_kv_norm[l].reshape(1, C_KV_LORA), wk, wv)

        oa = _attn_win(a_sink[l], qa, ka, va, with_ctx=with_ctx, name=f"attn_a_{l}")
        ob = _attn_dense(qb, kb, vb, heads=HEADS_GQA, with_ctx=with_ctx, name=f"attn_b_{l}")
        oc = _attn_dense(qc, kc, vc, heads=HEADS_MLA, with_ctx=with_ctx, name=f"attn_c_{l}")
        cw = jnp.pad(d_conv_w[l], ((0, 8 - D_CONV), (0, 0)))
        yc, yx = _ssd_call(z, xbc, dt, cw, d_conv_b[l].reshape(1, D_CONV_CH),
                           _lane_row(d_dt_bias[l]), _lane_row(d_a_log[l]),
                           jnp.repeat(d_skip[l], HEAD_DIM).reshape(1, MIX), d_out_norm[l].reshape(1, MIX))
        if with_ctx:
            h = _ffn_call(h, mods_l, ffn2_norm[l], wi2, wo2, l, k0=6, seg_off=0,
                          pre=(oa, ob, oc, (yc, yx), wout), name=f"ffn2_{l}")
        else:
            out = _ffn_call(h, mods_l, ffn2_norm[l], wi2, wo2, l, k0=6, seg_off=1,
                            pre=(oa, ob, oc, yx, wout), final_g=final_norm, name=f"ffn2_{l}")
    return out
```

```python
import functools

import numpy as np
import jax
import jax.numpy as jnp
from jax import lax
from jax.experimental import pallas as pl
from jax.experimental.pallas import tpu as pltpu

D_MODEL = 1024
BATCH = 8
SEQ = 2048
DEPTH = 2
GRID_W = 64
CTX_LEN = 256
HEAD_DIM = 64
ROPE_THETA = 10000.0
EPS = 1e-6
FFN_DIM = 2816
N_MOD = 9
WINDOW = 128
C_HEADS = 4
C_Q_LORA = 256
C_KV_LORA = 128
C_NOPE = 64
C_ROPE = 32
C_V = 64
D_HEADS = 4
D_STATE = 128
D_CONV = 5
SSD_CHUNK = 128
MIX = 256
IN_AB = 1024
IN_C = C_Q_LORA + C_KV_LORA + C_ROPE
D_CONV_CH = MIX + 2 * 2 * D_STATE
IN_D = MIX + D_CONV_CH + 2 * D_HEADS

NSEG = BATCH + 1
SEG = SEQ
assert BATCH * CTX_LEN == SEG
TOK = CTX_LEN + SEQ
N_CHUNK = TOK // SSD_CHUNK

LANE = 128
VMEM_LIMIT = 56 * 1024 * 1024

F32 = jnp.float32
BF16 = jnp.bfloat16
LOG2E = 1.4426950408889634


def _dot(a, b):
    return jnp.dot(a, b, preferred_element_type=F32)


def _dot_nt(a, b):
    return lax.dot_general(a, b, (((1,), (1,)), ((), ())), preferred_element_type=F32)


def _sigmoid(x):
    return 1.0 / (1.0 + jnp.exp(-x))


def _rms(x, g):
    return x * lax.rsqrt(jnp.mean(x * x, axis=-1, keepdims=True) + EPS) * g


def _full(shape):
    nd = len(shape)
    return pl.BlockSpec(shape, lambda *_: (0,) * nd)


def _layer_slab(shape, layer):
    nd = len(shape)
    return pl.BlockSpec((None,) + tuple(shape), lambda *_: (layer,) + (0,) * nd, pipeline_mode=pl.Buffered(1))


def _params(sem):
    return pltpu.CompilerParams(dimension_semantics=sem, vmem_limit_bytes=VMEM_LIMIT)


MODS_BN = 3 * D_MODEL


def _mods_kernel(c_ref, w_ref, b_ref, o_ref):
    c = c_ref[...]
    s = (c * _sigmoid(c)).astype(BF16)
    o_ref[...] = _dot(s, w_ref[...].astype(BF16)) + b_ref[...]


def _mods_call(cvec, ada_w, ada_b):
    n_l = ada_w.shape[0]
    bn = MODS_BN
    return pl.pallas_call(
        _mods_kernel,
        out_shape=jax.ShapeDtypeStruct((n_l, 16, N_MOD * D_MODEL), F32),
        grid=(n_l, N_MOD * D_MODEL // bn),
        in_specs=[
            pl.BlockSpec((16, D_MODEL), lambda l, n: (0, 0)),
            pl.BlockSpec((None, D_MODEL, bn), lambda l, n: (l, 0, n)),
            pl.BlockSpec((None, 1, bn), lambda l, n: (l, 0, n)),
        ],
        out_specs=pl.BlockSpec((None, 16, bn), lambda l, n: (l, 0, n)),
        compiler_params=_params(("arbitrary", "arbitrary")),
        name="mods",
    )(cvec, ada_w, ada_b.reshape(n_l, 1, N_MOD * D_MODEL))


FFN_TM = 512
FFN_SUB = 256
FFN_FC = 256
FFN_WI_ROWS = 128
FFN_WO_ROWS = 256
assert D_MODEL % FFN_WI_ROWS == 0 and FFN_DIM % FFN_WO_ROWS == 0


def _seg_specs(src, tm, width, seg_off):
    if isinstance(src, tuple):
        assert seg_off == 0
        ctx_arr, x_arr = src
        specs = [pl.BlockSpec((None, tm, width), lambda s, i: (0, jnp.where(s == 0, i, 0), 0)),
                 pl.BlockSpec((None, tm, width), lambda s, i: (jnp.maximum(s - 1, 0), jnp.where(s == 0, 0, i), 0))]
        return specs, [ctx_arr.reshape(1, SEG, width), x_arr]
    return [pl.BlockSpec((None, tm, width), lambda s, i: (s + seg_off, i, 0))], [src]


def _seg_read(refs):
    if len(refs) == 2:
        return jnp.where(pl.program_id(0) == 0, refs[0][...], refs[1][...])
    return refs[0][...]


def _fetch_as_bf16(src, dst, stage, sem, rows_per_copy):
    n_copy = src.shape[0] // rows_per_copy

    def copy(k):
        slot = k % 2
        return pltpu.make_async_copy(src.at[pl.ds(k * rows_per_copy, rows_per_copy), :], stage.at[slot],
                                     sem.at[slot])

    copy(0).start()
    for k in range(n_copy):
        if k + 1 < n_copy:
            copy(k + 1).start()
        copy(k).wait()
        dst[k * rows_per_copy:(k + 1) * rows_per_copy, :] = stage[k % 2].astype(BF16)


def _ffn_kernel(*refs, k0, layer, n_h, n_pre, has_final):
    refs = list(refs)
    h_refs = [refs.pop(0) for _ in range(n_h)]
    mods_ref, g_ref, wi_hbm, wo_hbm = (refs.pop(0) for _ in range(4))
    if n_pre:
        o_refs = [[refs.pop(0) for _ in range(n)] for n in n_pre]
        wout_ref = refs.pop(0)
    if has_final:
        gf_ref = refs.pop(0)
    out_ref, hm_ref, wi_ref, wo_ref, stage_i, stage_o, sem_i, sem_o = refs

    @pl.when(jnp.logical_and(pl.program_id(0) == 0, pl.program_id(1) == 0))
    def _():
        _fetch_as_bf16(wi_hbm.at[layer], wi_ref, stage_i, sem_i, FFN_WI_ROWS)
        _fetch_as_bf16(wo_hbm.at[layer], wo_ref, stage_o, sem_o, FFN_WO_ROWS)

    x = _seg_read(h_refs)
    m = mods_ref[...]
    if n_pre:
        o = None
        for k, o_ref in enumerate(o_refs):
            part = _dot(_seg_read(o_ref), wout_ref[k * MIX:(k + 1) * MIX, :])
            o = part if o is None else o + part
        x = x + m[5:6] * o
    halves = [slice(r * FFN_SUB, (r + 1) * FFN_SUB) for r in range(x.shape[0] // FFN_SUB)]
    xs = [x[rows] for rows in halves]
    nbs = [(_rms(xr, g_ref[...]) * (1.0 + m[k0 + 1:k0 + 2]) + m[k0:k0 + 1]).astype(BF16) for xr in xs]
    for rows, nb in zip(halves, nbs):
        for c in range(FFN_DIM // FFN_FC):
            lo = c * FFN_FC
            a = _dot(nb, wi_ref[:, lo:lo + FFN_FC])
            b = _dot(nb, wi_ref[:, FFN_DIM + lo:FFN_DIM + lo + FFN_FC])
            hm_ref[rows, lo:lo + FFN_FC] = (a * _sigmoid(a) * b).astype(BF16)
    ys = [_dot(hm_ref[rows, :], wo_ref[...]) for rows in halves]
    for rows, xr, y in zip(halves, xs, ys):
        out = xr + 0.5 * m[k0 + 2:k0 + 3] * y
        if has_final:
            out = _rms(out, gf_ref[...])
        out_ref[rows, :] = out


def _ffn_call(h, mods_l, g, wi, wo, layer, *, k0, seg_off, pre=None, final_g=None, name):
    tm = FFN_TM
    nseg = NSEG if isinstance(h, tuple) else h.shape[0] - seg_off
    in_specs, args = _seg_specs(h, tm, D_MODEL, seg_off)
    n_h = len(args)
    in_specs += [
        pl.BlockSpec((None, N_MOD, D_MODEL), lambda s, i: (s + seg_off, 0, 0)),
        _full((1, D_MODEL)),
        pl.BlockSpec(memory_space=pl.ANY),
        pl.BlockSpec(memory_space=pl.ANY),
    ]
    args += [mods_l, g.reshape(1, D_MODEL), wi, wo]
    n_pre = ()
    if pre is not None:
        *outs, wout = pre
        for o in outs:
            sp, ar = _seg_specs(o, tm, MIX, 0)
            in_specs += sp
            args += ar
            n_pre += (len(ar),)
        in_specs.append(_layer_slab((4 * MIX, D_MODEL), layer))
        args.append(wout)
    if final_g is not None:
        in_specs.append(_full((1, D_MODEL)))
        args.append(final_g.reshape(1, D_MODEL))
    kern = functools.partial(_ffn_kernel, k0=k0, layer=layer, n_h=n_h, n_pre=n_pre,
                             has_final=final_g is not None)
    return pl.pallas_call(
        kern,
        out_shape=jax.ShapeDtypeStruct((nseg, SEG, D_MODEL), F32),
        grid=(nseg, SEG // tm),
        in_specs=in_specs,
        out_specs=pl.BlockSpec((None, tm, D_MODEL), lambda s, i: (s, i, 0)),
        scratch_shapes=[pltpu.VMEM((tm, FFN_DIM), BF16),
                        pltpu.VMEM((D_MODEL, 2 * FFN_DIM), BF16), pltpu.VMEM((FFN_DIM, D_MODEL), BF16),
                        pltpu.VMEM((2, FFN_WI_ROWS, 2 * FFN_DIM), F32), pltpu.VMEM((2, FFN_WO_ROWS, D_MODEL), F32),
                        pltpu.SemaphoreType.DMA((2,)), pltpu.SemaphoreType.DMA((2,))],
        compiler_params=_params(("arbitrary", "arbitrary")),
        name=name,
    )(*args)


PROJ_TM = 1024
PROJ_SUB = PROJ_TM
COL_AB = 0
COL_C = IN_AB
COL_D = COL_C + 512
PROJ_W = COL_D + MIX + D_CONV_CH + LANE


def _swap_halves(x, half):
    lane = lax.broadcasted_iota(jnp.int32, x.shape, 1)
    first = (lane & half) == 0
    up = pltpu.roll(x, LANE - half, axis=1)
    dn = pltpu.roll(x, half, axis=1)
    return jnp.where(first, up, dn)


def _rope(x, cos, sin, half):
    return x * cos + _swap_halves(x, half) * sin


def _head_rms(x, bd, g):
    sq = x * x
    hi = sq.astype(BF16)
    lo = (sq - hi.astype(F32)).astype(BF16)
    ms = _dot(hi, bd) + _dot(lo, bd)
    return x * lax.rsqrt(ms + EPS) * g


def _inproj_kernel(h_ref, mods_ref, g_ref, w_ref, cab_ref, sab_ref, cm_ref, sm_ref, bd_ref,
                   bqn_ref, bkn_ref, cqn_ref, wuq_ref, ckvn_ref, wukvk_ref, wukvv_ref,
                   qa_ref, ka_ref, va_ref, qb_ref, kb_ref, vb_ref, qc_ref, kc_ref, vc_ref,
                   z_ref, xbc_ref, dt_ref):
    m = mods_ref[...]
    is_ctx = pl.program_id(0) == 0
    bd = bd_ref[...]
    scale_ab = HEAD_DIM ** -0.5 * LOG2E
    scale_c = (C_NOPE + C_ROPE) ** -0.5 * LOG2E

    def project(rows):
        nb = (_rms(h_ref[rows, :], g_ref[...]) * (1.0 + m[4:5]) + m[3:4]).astype(BF16)
        p_ab = [_dot(nb, w_ref[:, COL_AB + mixer * 512:COL_AB + (mixer + 1) * 512]) for mixer in range(2)]
        p_c = _dot(nb, w_ref[:, COL_C:COL_C + 512])
        z_ref[rows, :] = _dot(nb, w_ref[:, COL_D:COL_D + MIX])
        for c in range(D_CONV_CH // 256):
            lo = COL_D + MIX + c * 256
            xbc_ref[rows, c * 256:(c + 1) * 256] = _dot(nb, w_ref[:, lo:lo + 256])
        lo = COL_D + MIX + D_CONV_CH
        dt_ref[rows, :] = _dot(nb, w_ref[:, lo:lo + LANE])
        return p_ab, p_c

    def epilogue(rows, p_ab, p_c):
        cab, sab = jnp.where(is_ctx, 1.0, cab_ref[rows, :]), jnp.where(is_ctx, 0.0, sab_ref[rows, :])
        cm, sm = jnp.where(is_ctx, 1.0, cm_ref[rows, :]), jnp.where(is_ctx, 0.0, sm_ref[rows, :])
        for mixer, (q_ref, k_ref, v_ref) in enumerate(((qa_ref, ka_ref, va_ref), (qb_ref, kb_ref, vb_ref))):
            p = p_ab[mixer]
            for s in range(2):
                q = p[:, s * LANE:(s + 1) * LANE]
                if mixer == 1:
                    q = _head_rms(q, bd, bqn_ref[...])
                q = _rope(q, cab, sab, 16) * scale_ab
                q_ref[rows, s * LANE:(s + 1) * LANE] = q.astype(BF16)
            k = p[:, 2 * LANE:3 * LANE]
            if mixer == 1:
                k = _head_rms(k, bd, bkn_ref[...])
            k_ref[rows, :] = _rope(k, cab, sab, 16).astype(BF16)
            v = p[:, 3 * LANE:4 * LANE]
            lane = lax.broadcasted_iota(jnp.int32, v.shape, 1)
            v_ref[rows, 0:LANE] = jnp.where(lane < HEAD_DIM, v, 1.0).astype(BF16)
            v_ref[rows, LANE:2 * LANE] = jnp.where(lane < HEAD_DIM, 1.0, v).astype(BF16)

        p = p_c
        cq = _rms(p[:, 0:C_Q_LORA], cqn_ref[...]).astype(BF16)
        q = _dot(cq, wuq_ref[...])
        ckv = _rms(p[:, C_Q_LORA:C_Q_LORA + C_KV_LORA], ckvn_ref[...]).astype(BF16)
        kn = _dot(ckv, wukvk_ref[...])
        lane = lax.broadcasted_iota(jnp.int32, (1, C_HEADS * LANE), 1)
        own_half = ((lane // HEAD_DIM) % 2) == ((lane // LANE) % 2)
        vc_ref[rows, :] = jnp.where(own_half, _dot(ckv, wukvv_ref[...]), 1.0).astype(BF16)
        kr = _rope(p[:, 3 * LANE:4 * LANE], cm, sm, 8)
        for hh in range(C_HEADS):
            sl = slice(hh * LANE, (hh + 1) * LANE)
            qc_ref[rows, sl] = (_rope(q[:, sl], cm, sm, 8) * scale_c).astype(BF16)
            kc_ref[rows, sl] = (kn[:, sl] + kr).astype(BF16)

    blocks = [slice(r * PROJ_SUB, (r + 1) * PROJ_SUB) for r in range(h_ref.shape[0] // PROJ_SUB)]
    projected = [project(rows) for rows in blocks]
    for rows, (p_ab, p_c) in zip(blocks, projected):
        epilogue(rows, p_ab, p_c)


PROJ_OUT = ([(MIX, BF16), (LANE, BF16), (MIX, BF16)] * 2 + [(512, BF16), (512, BF16), (512, BF16)]
            + [(MIX, F32), (D_CONV_CH, F32), (LANE, F32)])


def _inproj_call(h, mods_l, g, w, layer, tabs, bd, bqn, bkn, cqn, wuq, ckvn, wukvk, wukvv):
    tm = PROJ_TM
    cab, sab, cm, sm = tabs
    tok = lambda s, i: (s, i, 0)
    tab = pl.BlockSpec((tm, LANE), lambda s, i: (i, 0))
    return pl.pallas_call(
        _inproj_kernel,
        out_shape=[jax.ShapeDtypeStruct((NSEG, SEG, wd), dt) for wd, dt in PROJ_OUT],
        grid=(NSEG, SEG // tm),
        in_specs=[
            pl.BlockSpec((None, tm, D_MODEL), tok),
            pl.BlockSpec((None, N_MOD, D_MODEL), lambda s, i: (s, 0, 0)),
            _full((1, D_MODEL)),
            _layer_slab((D_MODEL, PROJ_W), layer),
            tab, tab, tab, tab,
            _full((LANE, LANE)),
            _full((1, LANE)), _full((1, LANE)),
            _full((1, C_Q_LORA)), _layer_slab((C_Q_LORA, 512), layer),
            _full((1, C_KV_LORA)), _layer_slab((C_KV_LORA, 512), layer), _layer_slab((C_KV_LORA, 512), layer),
        ],
        out_specs=[pl.BlockSpec((None, tm, wd), tok) for wd, _ in PROJ_OUT],
        compiler_params=_params(("arbitrary", "arbitrary")),
        name="inproj",
    )(h, mods_l, g.reshape(1, D_MODEL), w, cab, sab, cm, sm, bd,
      bqn, bkn, cqn, wuq, ckvn, wukvk, wukvv)


HEADS_GQA = tuple((g, kv, 0, kv, g, kv) for g in range(2) for kv in range(2))
HEADS_MLA = tuple((h, None, h, h, h // 2, h % 2) for h in range(C_HEADS))


def _normalise(o, half, extra_den=None):
    lane = lax.broadcasted_iota(jnp.int32, o.shape, 1)
    valid = (lane < HEAD_DIM) if half == 0 else (lane >= HEAD_DIM)
    den = pltpu.roll(o, HEAD_DIM, axis=1)
    if extra_den is not None:
        den = den + extra_den
    return o * (1.0 / jnp.where(valid, den, 1.0))


def _masked_q(q, half):
    if half is None:
        return q
    lane = lax.broadcasted_iota(jnp.int32, q.shape, 1)
    keep = (lane < HEAD_DIM) if half == 0 else (lane >= HEAD_DIM)
    return jnp.where(keep, q, jnp.zeros_like(q))


def _store_heads(o_ref, rows, outs):
    lane = lax.broadcasted_iota(jnp.int32, outs[(0, 0)].shape, 1)
    for s in range(2):
        o = jnp.where(lane < HEAD_DIM, outs[(s, 0)], outs[(s, 1)])
        o_ref[rows, s * LANE:(s + 1) * LANE] = o.astype(o_ref.dtype)


ATT_TQ = 2048
ATT_SUB = 256
ATT_LOOKAHEAD = 2


def _attn_dense_kernel(q_ref, kc_ref, kx_ref, vc_ref, vx_ref, o_ref, *, heads):
    n_sub = q_ref.shape[0] // ATT_SUB
    stages = [(sub, hd) for sub in range(n_sub) for hd in heads]

    def run(with_x):
        def scores(stage):
            sub, (qs, qhalf, ks) = stage[0], stage[1][:3]
            qh = _masked_q(q_ref[sub * ATT_SUB:(sub + 1) * ATT_SUB, qs * LANE:(qs + 1) * LANE], qhalf)
            ksl = slice(ks * LANE, (ks + 1) * LANE)
            return _dot_nt(qh, kc_ref[:, ksl]), (_dot_nt(qh, kx_ref[:, ksl]) if with_x else None)

        outs = {}
        pending = [scores(st) for st in stages[:ATT_LOOKAHEAD]]
        for i, (sub, (_, _, _, vs, os_, ohalf)) in enumerate(stages):
            s_c, s_x = pending.pop(0)
            if i + ATT_LOOKAHEAD < len(stages):
                pending.append(scores(stages[i + ATT_LOOKAHEAD]))
            vsl = slice(vs * LANE, (vs + 1) * LANE)
            mx = jnp.max(s_c, axis=-1, keepdims=True)
            if with_x:
                mx = jnp.maximum(mx, jnp.max(s_x, axis=-1, keepdims=True))
            o = _dot(jnp.exp2((s_c - mx).astype(BF16)), vc_ref[:, vsl])
            if with_x:
                o = o + _dot(jnp.exp2((s_x - mx).astype(BF16)), vx_ref[:, vsl])
            outs[(os_, ohalf)] = _normalise(o, ohalf)
            if len(outs) == len(heads):
                _store_heads(o_ref, slice(sub * ATT_SUB, (sub + 1) * ATT_SUB), outs)
                outs = {}

    run(kx_ref is not None)


def _attn_ctx_kernel(q_ref, kc_ref, vc_ref, o_ref, *, heads):
    _attn_dense_kernel(q_ref, kc_ref, None, vc_ref, None, o_ref, heads=heads)


def _attn_latent_call(kern, q, k, v, *, tq, name, smem=()):
    kw, vw = k.shape[-1], v.shape[-1]
    return pl.pallas_call(
        kern,
        out_shape=jax.ShapeDtypeStruct((BATCH, SEQ, MIX), BF16),
        grid=(BATCH, SEQ // tq),
        in_specs=[pl.BlockSpec(memory_space=pltpu.SMEM)] * len(smem) + [
            pl.BlockSpec((None, tq, q.shape[-1]), lambda b, j: (b + 1, j, 0)),
            pl.BlockSpec((None, CTX_LEN, kw), lambda b, j: (0, b, 0)),
            pl.BlockSpec((None, SEQ, kw), lambda b, j: (b + 1, 0, 0)),
            pl.BlockSpec((None, CTX_LEN, vw), lambda b, j: (0, b, 0)),
            pl.BlockSpec((None, SEQ, vw), lambda b, j: (b + 1, 0, 0))],
        out_specs=pl.BlockSpec((None, tq, MIX), lambda b, j: (b, j, 0)),
        compiler_params=_params(("arbitrary", "arbitrary")),
        name=name,
    )(*smem, q, k, k, v, v)


def _attn_ctx_call(kern, q, k, v, *, name, smem=()):
    ctx = lambda wd: pl.BlockSpec((None, CTX_LEN, wd), lambda b: (0, b, 0))
    return pl.pallas_call(
        kern,
        out_shape=jax.ShapeDtypeStruct((BATCH, CTX_LEN, MIX), BF16),
        grid=(BATCH,),
        in_specs=[pl.BlockSpec(memory_space=pltpu.SMEM)] * len(smem)
        + [ctx(q.shape[-1]), ctx(k.shape[-1]), ctx(v.shape[-1])],
        out_specs=pl.BlockSpec((None, CTX_LEN, MIX), lambda b: (b, 0, 0)),
        compiler_params=_params(("arbitrary",)),
        name=name,
    )(*smem, q, k, v)


def _attn_dense(q, k, v, *, heads, with_ctx, name):
    ox = _attn_latent_call(functools.partial(_attn_dense_kernel, heads=heads), q, k, v, tq=ATT_TQ, name=name)
    if not with_ctx:
        return ox
    return _attn_ctx_call(functools.partial(_attn_ctx_kernel, heads=heads), q, k, v, name=name + "_ctx"), ox


WIN_TQ = 1024


def _attn_win_kernel(sink_ref, q_ref, kc_ref, kx_ref, vc_ref, vx_ref, o_ref):
    local = kx_ref is not None
    n_sub = q_ref.shape[0] // WINDOW

    n_h = len(HEADS_GQA)
    head_of_row = lax.broadcasted_iota(jnp.int32, (n_h * WINDOW, 1), 0) // WINDOW
    sink = jnp.zeros((n_h * WINDOW, 1), F32)
    for i, (qs, qhalf, _, _, _, _) in enumerate(HEADS_GQA):
        sink = jnp.where(head_of_row == i, sink_ref[qhalf * 2 + qs] * LOG2E, sink)
    n_blk = SEQ // WINDOW
    q_in_blk = lax.broadcasted_iota(jnp.int32, (n_h * WINDOW, WINDOW), 0) % WINDOW
    k_in_blk = lax.broadcasted_iota(jnp.int32, (n_h * WINDOW, WINDOW), 1)
    key_ge_query = k_in_blk >= q_in_blk
    key_le_query = k_in_blk <= q_in_blk

    def run():
        def scores(sub):
            rows = slice(sub * WINDOW, (sub + 1) * WINDOW)
            q4 = jnp.concatenate([_masked_q(q_ref[rows, qs * LANE:(qs + 1) * LANE], qhalf)
                                  for qs, qhalf, _, _, _, _ in HEADS_GQA], axis=0)
            s_c = _dot_nt(q4, kc_ref[...])
            if not local:
                return s_c, None, None
            n = pl.program_id(1) * n_sub + sub
            blocks = [pl.ds(pl.multiple_of(b * WINDOW, WINDOW), WINDOW)
                      for b in (jnp.maximum(n - 1, 0), n, jnp.minimum(n + 1, n_blk - 1))]
            s = _dot_nt(q4, jnp.concatenate([kx_ref[blk, :] for blk in blocks], axis=0))
            s_l = jnp.concatenate([
                jnp.where(jnp.logical_and(key_ge_query, n > 0), s[:, 0:WINDOW], -jnp.inf),
                s[:, WINDOW:2 * WINDOW],
                jnp.where(jnp.logical_and(key_le_query, n < n_blk - 1), s[:, 2 * WINDOW:], -jnp.inf)], axis=1)
            return s_c, s_l, blocks

        pending = [scores(sub) for sub in range(min(ATT_LOOKAHEAD, n_sub))]
        for sub in range(n_sub):
            s_c, s_l, blocks = pending.pop(0)
            if sub + ATT_LOOKAHEAD < n_sub:
                pending.append(scores(sub + ATT_LOOKAHEAD))
            rows = slice(sub * WINDOW, (sub + 1) * WINDOW)
            mx = jnp.maximum(jnp.max(s_c, axis=-1, keepdims=True), sink)
            if local:
                mx = jnp.maximum(mx, jnp.max(s_l, axis=-1, keepdims=True))
            o = _dot(jnp.exp2((s_c - mx).astype(BF16)), vc_ref[...])
            if local:
                v3 = jnp.concatenate([vx_ref[blk, :] for blk in blocks], axis=0)
                o = o + _dot(jnp.exp2((s_l - mx).astype(BF16)), v3)
            sink_den = jnp.exp2(sink - mx)
            outs = {}
            for i, (_, _, _, vs, os_, ohalf) in enumerate(HEADS_GQA):
                blk = slice(i * WINDOW, (i + 1) * WINDOW)
                outs[(os_, ohalf)] = _normalise(o[blk, vs * LANE:(vs + 1) * LANE], ohalf, sink_den[blk])
            _store_heads(o_ref, rows, outs)

    run()


def _attn_win_ctx_kernel(sink_ref, q_ref, kc_ref, vc_ref, o_ref):
    _attn_win_kernel(sink_ref, q_ref, kc_ref, None, vc_ref, None, o_ref)


def _attn_win(sink, q, k, v, *, with_ctx, name):
    ox = _attn_latent_call(_attn_win_kernel, q, k, v, tq=WIN_TQ, name=name, smem=(sink,))
    if not with_ctx:
        return ox
    return _attn_ctx_call(_attn_win_ctx_kernel, q, k, v, name=name + "_ctx", smem=(sink,)), ox


Q = SSD_CHUNK
PAD = 8
U_CTX = PAD
U_X = PAD + CTX_LEN + PAD
U_ROWS = U_X + SEQ + PAD
N_CTX_CHUNK = CTX_LEN // Q
LOCAL_CHUNKS = 6
SCAN_STEPS = 8


def _split3(a):
    a1 = a.astype(BF16)
    r1 = a - a1.astype(F32)
    a2 = r1.astype(BF16)
    a3 = (r1 - a2.astype(F32)).astype(BF16)
    return a1, a2, a3


def _ssd_kernel(zc_ref, zx_ref, uc_ref, ux_ref, dtc_ref, dtx_ref, cw_ref, cb_ref, dtb_ref, alog_ref,
                dskip_ref, onorm_ref, yc_ref, yx_ref, upad, xs_s, bm_s, cm_s, dt_s, y_s, st_s, cme_s, da_s, h_s):
    n_slab = D_CONV_CH // LANE
    zpad = jnp.zeros((PAD, LANE), F32)
    for j in range(n_slab):
        sl = slice(j * LANE, (j + 1) * LANE)
        upad[j, 0:PAD, :] = zpad
        upad[j, U_CTX:U_CTX + CTX_LEN, :] = uc_ref[:, sl]
        upad[j, U_CTX + CTX_LEN:U_X, :] = zpad
        upad[j, U_X:U_X + SEQ, :] = ux_ref[:, sl]
        upad[j, U_X + SEQ:U_ROWS, :] = zpad

    for c in range(N_CHUNK):
        base = U_CTX + c * Q if c < N_CTX_CHUNK else U_X + (c - N_CTX_CHUNK) * Q
        for j in range(n_slab):
            sl = slice(j * LANE, (j + 1) * LANE)
            dst = (xs_s, bm_s, cm_s)[j // 2]
            acc = jnp.broadcast_to(cb_ref[:, sl], (Q, LANE))
            for k in range(D_CONV):
                lo = base + k - D_CONV // 2
                acc = acc + upad[j, lo:lo + Q, :] * cw_ref[k:k + 1, sl]
            dcol = (j % 2) * LANE
            dst[c * Q:(c + 1) * Q, dcol:dcol + LANE] = (acc * _sigmoid(acc)).astype(dst.dtype)

    def softplus(v):
        return jnp.maximum(v, 0.0) + jnp.log(1.0 + jnp.exp(-jnp.abs(v)))

    dt_s[0:CTX_LEN, :] = softplus(dtc_ref[...] + dtb_ref[...])
    dt_s[CTX_LEN:TOK, :] = softplus(dtx_ref[...] + dtb_ref[...])

    a_neg = -jnp.exp(alog_ref[...]) * LOG2E
    row = lax.broadcasted_iota(jnp.int32, (Q, Q), 0)
    col = lax.broadcasted_iota(jnp.int32, (Q, Q), 1)
    causal = (col <= row, col >= row)
    tri = (causal[0].astype(BF16), causal[1].astype(BF16))
    lo_half = col < HEAD_DIM
    last_row = (Q - 1, 0)

    def bcast_col(v, idx):
        return jnp.broadcast_to(v[:, idx:idx + 1], (Q, Q))

    def chunk_rows(c):
        return pl.ds(c * Q if isinstance(c, int) else pl.multiple_of(c * Q, Q), Q)

    def local_load(c):
        rows = chunk_rows(c)
        groups = [(cm_s[rows, g * Q:(g + 1) * Q], bm_s[rows, g * Q:(g + 1) * Q], xs_s[rows, g * Q:(g + 1) * Q])
                  for g in range(2)]
        return dt_s[rows, :], groups

    def local_sums(dt, groups):
        a1, a2, a3 = _split3(dt * a_neg)
        cs_f = _dot(tri[0], a1) + _dot(tri[0], a2) + _dot(tri[0], a3)
        cs_b = _dot(tri[1], a1) + _dot(tri[1], a2) + _dot(tri[1], a3)
        cs = jnp.where(col < D_HEADS, cs_f, cs_b)
        n_row = 2 * D_HEADS
        return cs, cs.T[0:n_row], dt.T[0:n_row], [_dot_nt(cmg, bmg) for cmg, bmg, _ in groups]

    def local_compute(groups, cs, cs_t, dt_t, cbms):
        ys, sts, das, cmes = [], {}, {}, {}
        for g, (cmg, bmg, xsg) in enumerate(groups):
            cbm = cbms[g]
            cmg32 = cmg.astype(F32)
            bm_t = bmg.astype(F32).T
            xh = (jnp.where(lo_half, xsg, 0.0).astype(BF16), jnp.where(lo_half, 0.0, xsg).astype(BF16))
            y = None
            for d in range(2):
                st = None
                da = []
                for hh in range(2):
                    idx = d * D_HEADS + g * 2 + hh
                    colb = bcast_col(cs, idx)
                    cs_row = cs_t[idx:idx + 1, :]
                    dt_row = dt_t[idx:idx + 1, :]
                    seg = colb - (cs_row - jnp.log2(dt_row))
                    dec = jnp.exp2(jnp.where(causal[d], seg, -jnp.inf))
                    yd = _dot((cbm * dec).astype(BF16), xh[hh])
                    y = yd if y is None else y + yd
                    last = cs_row[:, last_row[d]:last_row[d] + 1]
                    w_row = dt_row * jnp.exp2(last - cs_row)
                    sth = _dot((bm_t * w_row).astype(BF16), xh[hh])
                    st = sth if st is None else st + sth
                    cmes[d * 4 + g * 2 + hh] = (cmg32 * jnp.exp2(colb)).astype(BF16)
                    da.append(jnp.broadcast_to(jnp.exp2(last), (1, Q)))
                sts[d * 2 + g] = st
                das[d * 2 + g] = jnp.where(lo_half[0:1, :], da[0], da[1])
            ys.append(y)
        return ys, sts, das, cmes

    def local_store(c, res):
        ys, sts, das, cmes = res
        rows = chunk_rows(c)
        for g in range(2):
            y_s[rows, g * Q:(g + 1) * Q] = ys[g]
        for k, v in sts.items():
            st_s[c * 4 + k] = v
        for k, v in das.items():
            da_s[c * 4 + k, 0:1, :] = v
        for k, v in cmes.items():
            cme_s[c * 8 + k] = v

    n_iter = N_CHUNK // LOCAL_CHUNKS

    def iter_chunks(i):
        return [i * LOCAL_CHUNKS + u for u in range(LOCAL_CHUNKS)]

    def iter_sums(i):
        return [local_sums(*local_load(c)) for c in iter_chunks(i)]

    def local_body(i, sums):
        nxt = iter_sums(jnp.minimum(i + 1, n_iter - 1))
        chunks = iter_chunks(i)
        results = [local_compute(local_load(c)[1], *sm) for c, sm in zip(chunks, sums)]
        for c, res in zip(chunks, results):
            local_store(c, res)
        return nxt

    lax.fori_loop(0, n_iter, local_body, iter_sums(0))

    h_s[...] = jnp.zeros(h_s.shape, F32)

    def scan_steps(steps):
        work = []
        for c_fwd, c_bwd in steps:
            work += [(d, g, c) for d, c in ((0, c_fwd), (1, c_bwd)) for g in range(2)]
        loaded = []
        for d, g, c in work:
            k = c * 4 + d * 2 + g
            loaded.append((cme_s[2 * k], cme_s[2 * k + 1], da_s[k, 0:1, :], st_s[k],
                           y_s[chunk_rows(c), g * Q:(g + 1) * Q]))
        h = [h_s[k] for k in range(4)]
        ys = []
        for (d, g, c), (cme0, cme1, da, st, y) in zip(work, loaded):
            h_in = h[d * 2 + g]
            hb = h_in.astype(BF16)
            zero = jnp.zeros_like(hb)
            ys.append(y + _dot(cme0, jnp.where(lo_half, hb, zero)) + _dot(cme1, jnp.where(lo_half, zero, hb)))
            h[d * 2 + g] = da * h_in + st
        for (d, g, c), y in zip(work, ys):
            y_s[chunk_rows(c), g * Q:(g + 1) * Q] = y
        for k in range(4):
            h_s[k] = h[k]

    for i in range(N_CTX_CHUNK):
        scan_steps([(i, N_CTX_CHUNK - 1 - i)])

    def scan_body(t, carry):
        i0 = N_CTX_CHUNK + t * SCAN_STEPS
        scan_steps([(i0 + u, N_CHUNK - 1 + N_CTX_CHUNK - (i0 + u)) for u in range(SCAN_STEPS)])
        return carry

    lax.fori_loop(0, (N_CHUNK - N_CTX_CHUNK) // SCAN_STEPS, scan_body, 0)

    dskip = dskip_ref[...]
    onorm = onorm_ref[...]

    def finish(rows, z):
        y = y_s[rows, :] + dskip * xs_s[rows, :]
        return _rms(y * (z * _sigmoid(z)), onorm).astype(BF16)

    for c in range(N_CTX_CHUNK):
        yc_ref[c * Q:(c + 1) * Q, :] = finish(slice(c * Q, (c + 1) * Q), zc_ref[c * Q:(c + 1) * Q, :])

    def fin_body(c, carry):
        r0 = pl.multiple_of(c * Q, Q)
        yx_ref[pl.ds(r0, Q), :] = finish(pl.ds(CTX_LEN + r0, Q), zx_ref[pl.ds(r0, Q), :])
        return carry

    lax.fori_loop(0, SEQ // Q, fin_body, 0, unroll=2)


def _ssd_call(z, xbc, dt, cw, cb, dtb, alog, dskip, onorm):
    def cspec(wd):
        return pl.BlockSpec((None, CTX_LEN, wd), lambda b: (0, b, 0))

    def xspec(wd):
        return pl.BlockSpec((None, SEQ, wd), lambda b: (b + 1, 0, 0))

    return pl.pallas_call(
        _ssd_kernel,
        out_shape=[jax.ShapeDtypeStruct((BATCH, CTX_LEN, MIX), BF16),
                   jax.ShapeDtypeStruct((BATCH, SEQ, MIX), BF16)],
        grid=(BATCH,),
        in_specs=[cspec(MIX), xspec(MIX), cspec(D_CONV_CH), xspec(D_CONV_CH), cspec(LANE), xspec(LANE),
                  _full((8, D_CONV_CH)), _full((1, D_CONV_CH)), _full((1, LANE)), _full((1, LANE)),
                  _full((1, MIX)), _full((1, MIX))],
        out_specs=[pl.BlockSpec((None, CTX_LEN, MIX), lambda b: (b, 0, 0)),
                   pl.BlockSpec((None, SEQ, MIX), lambda b: (b, 0, 0))],
        scratch_shapes=[pltpu.VMEM((D_CONV_CH // LANE, U_ROWS, LANE), F32),
                        pltpu.VMEM((TOK, MIX), F32), pltpu.VMEM((TOK, MIX), BF16), pltpu.VMEM((TOK, MIX), BF16),
                        pltpu.VMEM((TOK, LANE), F32), pltpu.VMEM((TOK, MIX), F32),
                        pltpu.VMEM((N_CHUNK * 4, Q, Q), F32), pltpu.VMEM((N_CHUNK * 8, Q, Q), BF16),
                        pltpu.VMEM((N_CHUNK * 4, 8, LANE), F32), pltpu.VMEM((4, Q, Q), F32)],
        compiler_params=_params(("arbitrary",)),
        name="ssd",
    )(z, z, xbc, xbc, dt, dt, cw, cb, dtb, alog, dskip, onorm)


def _rope_tables():
    rows = SEQ // GRID_W
    r = np.repeat(np.arange(rows, dtype=np.float64), GRID_W)
    c = np.tile(np.arange(GRID_W, dtype=np.float64), rows)

    def tables(rot_dim):
        axis_dim = rot_dim // 2
        inv = ROPE_THETA ** (-np.arange(0, axis_dim, 2, dtype=np.float64) / axis_dim)
        ar = r[:, None] * inv[None, :]
        ac = c[:, None] * inv[None, :]
        cos = np.concatenate([np.cos(ar), np.cos(ar), np.cos(ac), np.cos(ac)], axis=-1)
        sin = np.concatenate([-np.sin(ar), np.sin(ar), -np.sin(ac), np.sin(ac)], axis=-1)
        return cos, sin

    c64, s64 = tables(HEAD_DIM)
    cab = np.tile(c64, (1, 2))
    sab = np.tile(s64, (1, 2))
    c32, s32 = tables(C_ROPE)
    cm = np.concatenate([np.ones((SEQ, C_NOPE)), c32, np.ones((SEQ, 32))], axis=-1)
    sm = np.concatenate([np.zeros((SEQ, C_NOPE)), s32, np.zeros((SEQ, 32))], axis=-1)
    return tuple(jnp.asarray(t, F32) for t in (cab, sab, cm, sm))


def _head_mean_matrix():
    lane = np.arange(LANE)
    same = (lane[:, None] // HEAD_DIM) == (lane[None, :] // HEAD_DIM)
    return jnp.asarray(np.where(same, 1.0 / HEAD_DIM, 0.0), BF16)


def _gqa_order(w, axis):
    shp = w.shape
    w = w.reshape(shp[:axis] + (2, 2, HEAD_DIM) + shp[axis + 1:])
    w = jnp.swapaxes(w, axis, axis + 1)
    return w.reshape(shp)


def _stacked_weights(w_in, c_w_uq, c_w_ukv, w_out):
    n_l, d = w_in.shape[:2]
    zc = lambda n: jnp.zeros((n_l, d, n), w_in.dtype)
    wb = w_in
    o_c = IN_AB
    o_d = IN_AB + IN_C
    w = jnp.concatenate([
        _gqa_order(wb[..., 0:MIX], 2), wb[..., MIX:512],
        _gqa_order(wb[..., 512:512 + MIX], 2), wb[..., 512 + MIX:o_c + C_Q_LORA + C_KV_LORA],
        zc(C_NOPE), wb[..., o_c + C_Q_LORA + C_KV_LORA:o_d], zc(LANE - C_NOPE - C_ROPE),
        wb[..., o_d:], zc(LANE - 2 * D_HEADS),
    ], axis=2).astype(BF16)
    assert w.shape[2] == PROJ_W
    dq = C_NOPE + C_ROPE
    pad_last = lambda t, n: jnp.pad(t, ((0, 0),) * (t.ndim - 1) + ((0, n),))
    wuq = pad_last(c_w_uq.reshape(n_l, C_Q_LORA, C_HEADS, dq), LANE - dq)
    wkv = c_w_ukv.reshape(n_l, C_KV_LORA, C_HEADS, C_NOPE + C_V)
    wk = pad_last(wkv[..., :C_NOPE], LANE - C_NOPE)
    head_parity = (jnp.arange(C_HEADS) % 2)[:, None]
    wv = jnp.stack([jnp.where(head_parity == half, wkv[..., C_NOPE:], 0.0) for half in range(2)], axis=3)
    wout = jnp.concatenate([_gqa_order(w_out[:, 0:MIX], 1), _gqa_order(w_out[:, MIX:2 * MIX], 1),
                            w_out[:, 2 * MIX:]], axis=1)
    return (w, wuq.reshape(n_l, C_Q_LORA, 512).astype(BF16), wk.reshape(n_l, C_KV_LORA, 512).astype(BF16),
            wv.reshape(n_l, C_KV_LORA, 512).astype(BF16), wout.astype(BF16))


def _lane_row(v, width=LANE):
    v = v.reshape(1, -1).astype(F32)
    return jnp.pad(v, ((0, 0), (0, width - v.shape[1])))


def kernel(x, c, ctx, c_ctx, ada_w, ada_b, ffn1_norm, ffn1_wi, ffn1_wo, mix_norm, w_in, w_out, a_sink, b_q_norm, b_k_norm, c_q_norm, c_w_uq, c_kv_norm, c_w_ukv, d_conv_w, d_conv_b, d_a_log, d_dt_bias, d_skip, d_out_norm, ffn2_norm, ffn2_wi, ffn2_wo, final_norm):
    cvec = jnp.concatenate([c_ctx[None, :], c, jnp.zeros((16 - NSEG, D_MODEL), F32)], axis=0)
    mods = _mods_call(cvec, ada_w, ada_b).reshape(DEPTH, 16, N_MOD, D_MODEL)[:, :NSEG]
    tabs = _rope_tables()
    bd = _head_mean_matrix()
    wi1, wo1, wi2, wo2 = ffn1_wi, ffn1_wo, ffn2_wi, ffn2_wo
    w, wuq, wk, wv, wout = _stacked_weights(w_in, c_w_uq, c_w_ukv, w_out)

    h = (ctx, x)
    out = None
    for l in range(DEPTH):
        with_ctx = l < DEPTH - 1
        mods_l = mods[l]
        h = _ffn_call(h, mods_l, ffn1_norm[l], wi1, wo1, l, k0=0, seg_off=0, name=f"ffn1_{l}")
        (qa, ka, va, qb, kb, vb, qc, kc, vc, z, xbc, dt) = _inproj_call(
            h, mods_l, mix_norm[l], w, l, tabs, bd,
            jnp.tile(b_q_norm[l], 2).reshape(1, LANE), jnp.tile(b_k_norm[l], 2).reshape(1, LANE),
            c_q_norm[l].reshape(1, C_Q_LORA), wuq, c_kv_norm[l].reshape(1, C_KV_LORA), wk, wv)

        oa = _attn_win(a_sink[l], qa, ka, va, with_ctx=with_ctx, name=f"attn_a_{l}")
        ob = _attn_dense(qb, kb, vb, heads=HEADS_GQA, with_ctx=with_ctx, name=f"attn_b_{l}")
        oc = _attn_dense(qc, kc, vc, heads=HEADS_MLA, with_ctx=with_ctx, name=f"attn_c_{l}")
        cw = jnp.pad(d_conv_w[l], ((0, 8 - D_CONV), (0, 0)))
        yc, yx = _ssd_call(z, xbc, dt, cw, d_conv_b[l].reshape(1, D_CONV_CH),
                           _lane_row(d_dt_bias[l]), _lane_row(d_a_log[l]),
                           jnp.repeat(d_skip[l], HEAD_DIM).reshape(1, MIX), d_out_norm[l].reshape(1, MIX))
        if with_ctx:
            h = _ffn_call(h, mods_l, ffn2_norm[l], wi2, wo2, l, k0=6, seg_off=0,
                          pre=(oa, ob, oc, (yc, yx), wout), name=f"ffn2_{l}")
        else:
            out = _ffn_call(h, mods_l, ffn2_norm[l], wi2, wo2, l, k0=6, seg_off=1,
                            pre=(oa, ob, oc, yx, wout), final_g=final_norm, name=f"ffn2_{l}")
    return out
```

```python
import functools

import numpy as np
import jax
import jax.numpy as jnp
from jax import lax
from jax.experimental import pallas as pl
from jax.experimental.pallas import tpu as pltpu

D_MODEL = 1024
BATCH = 8
SEQ = 2048
DEPTH = 2
GRID_W = 64
CTX_LEN = 256
HEAD_DIM = 64
ROPE_THETA = 10000.0
EPS = 1e-6
FFN_DIM = 2816
N_MOD = 9
WINDOW = 128
C_HEADS = 4
C_Q_LORA = 256
C_KV_LORA = 128
C_NOPE = 64
C_ROPE = 32
C_V = 64
D_HEADS = 4
D_STATE = 128
D_CONV = 5
SSD_CHUNK = 128
MIX = 256
IN_AB = 1024
IN_C = C_Q_LORA + C_KV_LORA + C_ROPE
D_CONV_CH = MIX + 2 * 2 * D_STATE
IN_D = MIX + D_CONV_CH + 2 * D_HEADS

NSEG = BATCH + 1
SEG = SEQ
assert BATCH * CTX_LEN == SEG
TOK = CTX_LEN + SEQ
N_CHUNK = TOK // SSD_CHUNK

LANE = 128
VMEM_LIMIT = 56 * 1024 * 1024

F32 = jnp.float32
BF16 = jnp.bfloat16
LOG2E = 1.4426950408889634


def _dot(a, b):
    return jnp.dot(a, b, preferred_element_type=F32)


def _dot_nt(a, b):
    return lax.dot_general(a, b, (((1,), (1,)), ((), ())), preferred_element_type=F32)


def _sigmoid(x):
    return 1.0 / (1.0 + jnp.exp(-x))


def _rms(x, g):
    return x * lax.rsqrt(jnp.mean(x * x, axis=-1, keepdims=True) + EPS) * g


def _full(shape):
    nd = len(shape)
    return pl.BlockSpec(shape, lambda *_: (0,) * nd)


def _layer_slab(shape, layer):
    nd = len(shape)
    return pl.BlockSpec((None,) + tuple(shape), lambda *_: (layer,) + (0,) * nd, pipeline_mode=pl.Buffered(1))


def _params(sem):
    return pltpu.CompilerParams(dimension_semantics=sem, vmem_limit_bytes=VMEM_LIMIT)


MODS_BN = 3 * D_MODEL


def _mods_kernel(c_ref, w_ref, b_ref, o_ref):
    c = c_ref[...]
    s = (c * _sigmoid(c)).astype(BF16)
    o_ref[...] = _dot(s, w_ref[...].astype(BF16)) + b_ref[...]


def _mods_call(cvec, ada_w, ada_b):
    n_l = ada_w.shape[0]
    bn = MODS_BN
    return pl.pallas_call(
        _mods_kernel,
        out_shape=jax.ShapeDtypeStruct((n_l, 16, N_MOD * D_MODEL), F32),
        grid=(n_l, N_MOD * D_MODEL // bn),
        in_specs=[
            pl.BlockSpec((16, D_MODEL), lambda l, n: (0, 0)),
            pl.BlockSpec((None, D_MODEL, bn), lambda l, n: (l, 0, n)),
            pl.BlockSpec((None, 1, bn), lambda l, n: (l, 0, n)),
        ],
        out_specs=pl.BlockSpec((None, 16, bn), lambda l, n: (l, 0, n)),
        compiler_params=_params(("arbitrary", "arbitrary")),
        name="mods",
    )(cvec, ada_w, ada_b.reshape(n_l, 1, N_MOD * D_MODEL))


FFN_TM = 512
FFN_SUB = 256
FFN_FC = 256
FFN_WI_ROWS = 128
FFN_WO_ROWS = 256
assert D_MODEL % FFN_WI_ROWS == 0 and FFN_DIM % FFN_WO_ROWS == 0


def _seg_specs(src, tm, width, seg_off):
    if isinstance(src, tuple):
        assert seg_off == 0
        ctx_arr, x_arr = src
        specs = [pl.BlockSpec((None, tm, width), lambda s, i: (0, jnp.where(s == 0, i, 0), 0)),
                 pl.BlockSpec((None, tm, width), lambda s, i: (jnp.maximum(s - 1, 0), jnp.where(s == 0, 0, i), 0))]
        return specs, [ctx_arr.reshape(1, SEG, width), x_arr]
    return [pl.BlockSpec((None, tm, width), lambda s, i: (s + seg_off, i, 0))], [src]


def _seg_read(refs):
    if len(refs) == 2:
        return jnp.where(pl.program_id(0) == 0, refs[0][...], refs[1][...])
    return refs[0][...]


def _fetch_as_bf16(src, dst, stage, sem, rows_per_copy):
    n_copy = src.shape[0] // rows_per_copy

    def copy(k):
        slot = k % 2
        return pltpu.make_async_copy(src.at[pl.ds(k * rows_per_copy, rows_per_copy), :], stage.at[slot],
                                     sem.at[slot])

    copy(0).start()
    for k in range(n_copy):
        if k + 1 < n_copy:
            copy(k + 1).start()
        copy(k).wait()
        dst[k * rows_per_copy:(k + 1) * rows_per_copy, :] = stage[k % 2].astype(BF16)


def _ffn_kernel(*refs, k0, layer, n_h, n_pre, has_final):
    refs = list(refs)
    h_refs = [refs.pop(0) for _ in range(n_h)]
    mods_ref, g_ref, wi_hbm, wo_hbm = (refs.pop(0) for _ in range(4))
    if n_pre:
        o_refs = [[refs.pop(0) for _ in range(n)] for n in n_pre]
        wout_ref = refs.pop(0)
    if has_final:
        gf_ref = refs.pop(0)
    out_ref, hm_ref, wi_ref, wo_ref, stage_i, stage_o, sem_i, sem_o = refs

    @pl.when(jnp.logical_and(pl.program_id(0) == 0, pl.program_id(1) == 0))
    def _():
        _fetch_as_bf16(wi_hbm.at[layer], wi_ref, stage_i, sem_i, FFN_WI_ROWS)
        _fetch_as_bf16(wo_hbm.at[layer], wo_ref, stage_o, sem_o, FFN_WO_ROWS)

    x = _seg_read(h_refs)
    m = mods_ref[...]
    if n_pre:
        o = None
        for k, o_ref in enumerate(o_refs):
            part = _dot(_seg_read(o_ref), wout_ref[k * MIX:(k + 1) * MIX, :])
            o = part if o is None else o + part
        x = x + m[5:6] * o
    halves = [slice(r * FFN_SUB, (r + 1) * FFN_SUB) for r in range(x.shape[0] // FFN_SUB)]
    xs = [x[rows] for rows in halves]
    nbs = [(_rms(xr, g_ref[...]) * (1.0 + m[k0 + 1:k0 + 2]) + m[k0:k0 + 1]).astype(BF16) for xr in xs]
    for rows, nb in zip(halves, nbs):
        for c in range(FFN_DIM // FFN_FC):
            lo = c * FFN_FC
            a = _dot(nb, wi_ref[:, lo:lo + FFN_FC])
            b = _dot(nb, wi_ref[:, FFN_DIM + lo:FFN_DIM + lo + FFN_FC])
            hm_ref[rows, lo:lo + FFN_FC] = (a * _sigmoid(a) * b).astype(BF16)
    ys = [_dot(hm_ref[rows, :], wo_ref[...]) for rows in halves]
    for rows, xr, y in zip(halves, xs, ys):
        out = xr + 0.5 * m[k0 + 2:k0 + 3] * y
        if has_final:
            out = _rms(out, gf_ref[...])
        out_ref[rows, :] = out


def _ffn_call(h, mods_l, g, wi, wo, layer, *, k0, seg_off, pre=None, final_g=None, name):
    tm = FFN_TM
    nseg = NSEG if isinstance(h, tuple) else h.shape[0] - seg_off
    in_specs, args = _seg_specs(h, tm, D_MODEL, seg_off)
    n_h = len(args)
    in_specs += [
        pl.BlockSpec((None, N_MOD, D_MODEL), lambda s, i: (s + seg_off, 0, 0)),
        _full((1, D_MODEL)),
        pl.BlockSpec(memory_space=pl.ANY),
        pl.BlockSpec(memory_space=pl.ANY),
    ]
    args += [mods_l, g.reshape(1, D_MODEL), wi, wo]
    n_pre = ()
    if pre is not None:
        *outs, wout = pre
        for o in outs:
            sp, ar = _seg_specs(o, tm, MIX, 0)
            in_specs += sp
            args += ar
            n_pre += (len(ar),)
        in_specs.append(_layer_slab((4 * MIX, D_MODEL), layer))
        args.append(wout)
    if final_g is not None:
        in_specs.append(_full((1, D_MODEL)))
        args.append(final_g.reshape(1, D_MODEL))
    kern = functools.partial(_ffn_kernel, k0=k0, layer=layer, n_h=n_h, n_pre=n_pre,
                             has_final=final_g is not None)
    return pl.pallas_call(
        kern,
        out_shape=jax.ShapeDtypeStruct((nseg, SEG, D_MODEL), F32),
        grid=(nseg, SEG // tm),
        in_specs=in_specs,
        out_specs=pl.BlockSpec((None, tm, D_MODEL), lambda s, i: (s, i, 0)),
        scratch_shapes=[pltpu.VMEM((tm, FFN_DIM), BF16),
                        pltpu.VMEM((D_MODEL, 2 * FFN_DIM), BF16), pltpu.VMEM((FFN_DIM, D_MODEL), BF16),
                        pltpu.VMEM((2, FFN_WI_ROWS, 2 * FFN_DIM), F32), pltpu.VMEM((2, FFN_WO_ROWS, D_MODEL), F32),
                        pltpu.SemaphoreType.DMA((2,)), pltpu.SemaphoreType.DMA((2,))],
        compiler_params=_params(("arbitrary", "arbitrary")),
        name=name,
    )(*args)


PROJ_TM = 1024
PROJ_SUB = PROJ_TM
COL_AB = 0
COL_C = IN_AB
COL_D = COL_C + 512
PROJ_W = COL_D + MIX + D_CONV_CH + LANE


def _swap_halves(x, half):
    lane = lax.broadcasted_iota(jnp.int32, x.shape, 1)
    first = (lane & half) == 0
    up = pltpu.roll(x, LANE - half, axis=1)
    dn = pltpu.roll(x, half, axis=1)
    return jnp.where(first, up, dn)


def _rope(x, cos, sin, half):
    return x * cos + _swap_halves(x, half) * sin


def _head_rms(x, bd, g):
    sq = x * x
    hi = sq.astype(BF16)
    lo = (sq - hi.astype(F32)).astype(BF16)
    ms = _dot(hi, bd) + _dot(lo, bd)
    return x * lax.rsqrt(ms + EPS) * g


def _inproj_kernel(h_ref, mods_ref, g_ref, w_ref, cab_ref, sab_ref, cm_ref, sm_ref, bd_ref,
                   bqn_ref, bkn_ref, cqn_ref, wuq_ref, ckvn_ref, wukvk_ref, wukvv_ref,
                   qa_ref, ka_ref, va_ref, qb_ref, kb_ref, vb_ref, qc_ref, kc_ref, vc_ref,
                   z_ref, xbc_ref, dt_ref):
    m = mods_ref[...]
    is_ctx = pl.program_id(0) == 0
    bd = bd_ref[...]
    scale_ab = HEAD_DIM ** -0.5 * LOG2E
    scale_c = (C_NOPE + C_ROPE) ** -0.5 * LOG2E

    def project(rows):
        nb = (_rms(h_ref[rows, :], g_ref[...]) * (1.0 + m[4:5]) + m[3:4]).astype(BF16)
        p_ab = [_dot(nb, w_ref[:, COL_AB + mixer * 512:COL_AB + (mixer + 1) * 512]) for mixer in range(2)]
        p_c = _dot(nb, w_ref[:, COL_C:COL_C + 512])
        z_ref[rows, :] = _dot(nb, w_ref[:, COL_D:COL_D + MIX])
        for c in range(D_CONV_CH // 256):
            lo = COL_D + MIX + c * 256
            xbc_ref[rows, c * 256:(c + 1) * 256] = _dot(nb, w_ref[:, lo:lo + 256])
        lo = COL_D + MIX + D_CONV_CH
        dt_ref[rows, :] = _dot(nb, w_ref[:, lo:lo + LANE])
        return p_ab, p_c

    def epilogue(rows, p_ab, p_c):
        cab, sab = jnp.where(is_ctx, 1.0, cab_ref[rows, :]), jnp.where(is_ctx, 0.0, sab_ref[rows, :])
        cm, sm = jnp.where(is_ctx, 1.0, cm_ref[rows, :]), jnp.where(is_ctx, 0.0, sm_ref[rows, :])
        for mixer, (q_ref, k_ref, v_ref) in enumerate(((qa_ref, ka_ref, va_ref), (qb_ref, kb_ref, vb_ref))):
            p = p_ab[mixer]
            for s in range(2):
                q = p[:, s * LANE:(s + 1) * LANE]
                if mixer == 1:
                    q = _head_rms(q, bd, bqn_ref[...])
                q = _rope(q, cab, sab, 16) * scale_ab
                q_ref[rows, s * LANE:(s + 1) * LANE] = q.astype(BF16)
            k = p[:, 2 * LANE:3 * LANE]
            if mixer == 1:
                k = _head_rms(k, bd, bkn_ref[...])
            k_ref[rows, :] = _rope(k, cab, sab, 16).astype(BF16)
            v = p[:, 3 * LANE:4 * LANE]
            lane = lax.broadcasted_iota(jnp.int32, v.shape, 1)
            v_ref[rows, 0:LANE] = jnp.where(lane < HEAD_DIM, v, 1.0).astype(BF16)
            v_ref[rows, LANE:2 * LANE] = jnp.where(lane < HEAD_DIM, 1.0, v).astype(BF16)

        p = p_c
        cq = _rms(p[:, 0:C_Q_LORA], cqn_ref[...]).astype(BF16)
        q = _dot(cq, wuq_ref[...])
        ckv = _rms(p[:, C_Q_LORA:C_Q_LORA + C_KV_LORA], ckvn_ref[...]).astype(BF16)
        kn = _dot(ckv, wukvk_ref[...])
        lane = lax.broadcasted_iota(jnp.int32, (1, C_HEADS * LANE), 1)
        own_half = ((lane // HEAD_DIM) % 2) == ((lane // LANE) % 2)
        vc_ref[rows, :] = jnp.where(own_half, _dot(ckv, wukvv_ref[...]), 1.0).astype(BF16)
        kr = _rope(p[:, 3 * LANE:4 * LANE], cm, sm, 8)
        for hh in range(C_HEADS):
            sl = slice(hh * LANE, (hh + 1) * LANE)
            qc_ref[rows, sl] = (_rope(q[:, sl], cm, sm, 8) * scale_c).astype(BF16)
            kc_ref[rows, sl] = (kn[:, sl] + kr).astype(BF16)

    blocks = [slice(r * PROJ_SUB, (r + 1) * PROJ_SUB) for r in range(h_ref.shape[0] // PROJ_SUB)]
    projected = [project(rows) for rows in blocks]
    for rows, (p_ab, p_c) in zip(blocks, projected):
        epilogue(rows, p_ab, p_c)


PROJ_OUT = ([(MIX, BF16), (LANE, BF16), (MIX, BF16)] * 2 + [(512, BF16), (512, BF16), (512, BF16)]
            + [(MIX, F32), (D_CONV_CH, F32), (LANE, F32)])


def _inproj_call(h, mods_l, g, w, layer, tabs, bd, bqn, bkn, cqn, wuq, ckvn, wukvk, wukvv):
    tm = PROJ_TM
    cab, sab, cm, sm = tabs
    tok = lambda s, i: (s, i, 0)
    tab = pl.BlockSpec((tm, LANE), lambda s, i: (i, 0))
    return pl.pallas_call(
        _inproj_kernel,
        out_shape=[jax.ShapeDtypeStruct((NSEG, SEG, wd), dt) for wd, dt in PROJ_OUT],
        grid=(NSEG, SEG // tm),
        in_specs=[
            pl.BlockSpec((None, tm, D_MODEL), tok),
            pl.BlockSpec((None, N_MOD, D_MODEL), lambda s, i: (s, 0, 0)),
            _full((1, D_MODEL)),
            _layer_slab((D_MODEL, PROJ_W), layer),
            tab, tab, tab, tab,
            _full((LANE, LANE)),
            _full((1, LANE)), _full((1, LANE)),
            _full((1, C_Q_LORA)), _layer_slab((C_Q_LORA, 512), layer),
            _full((1, C_KV_LORA)), _layer_slab((C_KV_LORA, 512), layer), _layer_slab((C_KV_LORA, 512), layer),
        ],
        out_specs=[pl.BlockSpec((None, tm, wd), tok) for wd, _ in PROJ_OUT],
        compiler_params=_params(("arbitrary", "arbitrary")),
        name="inproj",
    )(h, mods_l, g.reshape(1, D_MODEL), w, cab, sab, cm, sm, bd,
      bqn, bkn, cqn, wuq, ckvn, wukvk, wukvv)


HEADS_GQA = tuple((g, kv, 0, kv, g, kv) for g in range(2) for kv in range(2))
HEADS_MLA = tuple((h, None, h, h, h // 2, h % 2) for h in range(C_HEADS))


def _normalise(o, half, extra_den=None):
    lane = lax.broadcasted_iota(jnp.int32, o.shape, 1)
    valid = (lane < HEAD_DIM) if half == 0 else (lane >= HEAD_DIM)
    den = pltpu.roll(o, HEAD_DIM, axis=1)
    if extra_den is not None:
        den = den + extra_den
    return o * (1.0 / jnp.where(valid, den, 1.0))


def _masked_q(q, half):
    if half is None:
        return q
    lane = lax.broadcasted_iota(jnp.int32, q.shape, 1)
    keep = (lane < HEAD_DIM) if half == 0 else (lane >= HEAD_DIM)
    return jnp.where(keep, q, jnp.zeros_like(q))


def _store_heads(o_ref, rows, outs):
    lane = lax.broadcasted_iota(jnp.int32, outs[(0, 0)].shape, 1)
    for s in range(2):
        o = jnp.where(lane < HEAD_DIM, outs[(s, 0)], outs[(s, 1)])
        o_ref[rows, s * LANE:(s + 1) * LANE] = o.astype(o_ref.dtype)


ATT_TQ = 1024
ATT_SUB = 256
ATT_LOOKAHEAD = 2


def _attn_dense_kernel(q_ref, kc_ref, kx_ref, vc_ref, vx_ref, o_ref, *, heads):
    n_sub = q_ref.shape[0] // ATT_SUB
    stages = [(sub, hd) for sub in range(n_sub) for hd in heads]

    def run(with_x):
        def scores(stage):
            sub, (qs, qhalf, ks) = stage[0], stage[1][:3]
            qh = _masked_q(q_ref[sub * ATT_SUB:(sub + 1) * ATT_SUB, qs * LANE:(qs + 1) * LANE], qhalf)
            ksl = slice(ks * LANE, (ks + 1) * LANE)
            return _dot_nt(qh, kc_ref[:, ksl]), (_dot_nt(qh, kx_ref[:, ksl]) if with_x else None)

        outs = {}
        pending = [scores(st) for st in stages[:ATT_LOOKAHEAD]]
        for i, (sub, (_, _, _, vs, os_, ohalf)) in enumerate(stages):
            s_c, s_x = pending.pop(0)
            if i + ATT_LOOKAHEAD < len(stages):
                pending.append(scores(stages[i + ATT_LOOKAHEAD]))
            vsl = slice(vs * LANE, (vs + 1) * LANE)
            mx = jnp.max(s_c, axis=-1, keepdims=True)
            if with_x:
                mx = jnp.maximum(mx, jnp.max(s_x, axis=-1, keepdims=True))
            o = _dot(jnp.exp2((s_c - mx).astype(BF16)), vc_ref[:, vsl])
            if with_x:
                o = o + _dot(jnp.exp2((s_x - mx).astype(BF16)), vx_ref[:, vsl])
            outs[(os_, ohalf)] = _normalise(o, ohalf)
            if len(outs) == len(heads):
                _store_heads(o_ref, slice(sub * ATT_SUB, (sub + 1) * ATT_SUB), outs)
                outs = {}

    run(kx_ref is not None)


def _attn_ctx_kernel(q_ref, kc_ref, vc_ref, o_ref, *, heads):
    _attn_dense_kernel(q_ref, kc_ref, None, vc_ref, None, o_ref, heads=heads)


def _attn_latent_call(kern, q, k, v, *, tq, name, smem=()):
    kw, vw = k.shape[-1], v.shape[-1]
    return pl.pallas_call(
        kern,
        out_shape=jax.ShapeDtypeStruct((BATCH, SEQ, MIX), BF16),
        grid=(BATCH, SEQ // tq),
        in_specs=[pl.BlockSpec(memory_space=pltpu.SMEM)] * len(smem) + [
            pl.BlockSpec((None, tq, q.shape[-1]), lambda b, j: (b + 1, j, 0)),
            pl.BlockSpec((None, CTX_LEN, kw), lambda b, j: (0, b, 0)),
            pl.BlockSpec((None, SEQ, kw), lambda b, j: (b + 1, 0, 0)),
            pl.BlockSpec((None, CTX_LEN, vw), lambda b, j: (0, b, 0)),
            pl.BlockSpec((None, SEQ, vw), lambda b, j: (b + 1, 0, 0))],
        out_specs=pl.BlockSpec((None, tq, MIX), lambda b, j: (b, j, 0)),
        compiler_params=_params(("arbitrary", "arbitrary")),
        name=name,
    )(*smem, q, k, k, v, v)


def _attn_ctx_call(kern, q, k, v, *, name, smem=()):
    ctx = lambda wd: pl.BlockSpec((None, CTX_LEN, wd), lambda b: (0, b, 0))
    return pl.pallas_call(
        kern,
        out_shape=jax.ShapeDtypeStruct((BATCH, CTX_LEN, MIX), BF16),
        grid=(BATCH,),
        in_specs=[pl.BlockSpec(memory_space=pltpu.SMEM)] * len(smem)
        + [ctx(q.shape[-1]), ctx(k.shape[-1]), ctx(v.shape[-1])],
        out_specs=pl.BlockSpec((None, CTX_LEN, MIX), lambda b: (b, 0, 0)),
        compiler_params=_params(("arbitrary",)),
        name=name,
    )(*smem, q, k, v)


def _attn_dense(q, k, v, *, heads, with_ctx, name):
    ox = _attn_latent_call(functools.partial(_attn_dense_kernel, heads=heads), q, k, v, tq=ATT_TQ, name=name)
    if not with_ctx:
        return ox
    return _attn_ctx_call(functools.partial(_attn_ctx_kernel, heads=heads), q, k, v, name=name + "_ctx"), ox


WIN_TQ = 1024


def _attn_win_kernel(sink_ref, q_ref, kc_ref, kx_ref, vc_ref, vx_ref, o_ref):
    local = kx_ref is not None
    n_sub = q_ref.shape[0] // WINDOW

    n_h = len(HEADS_GQA)
    head_of_row = lax.broadcasted_iota(jnp.int32, (n_h * WINDOW, 1), 0) // WINDOW
    sink = jnp.zeros((n_h * WINDOW, 1), F32)
    for i, (qs, qhalf, _, _, _, _) in enumerate(HEADS_GQA):
        sink = jnp.where(head_of_row == i, sink_ref[qhalf * 2 + qs] * LOG2E, sink)
    n_blk = SEQ // WINDOW
    q_in_blk = lax.broadcasted_iota(jnp.int32, (n_h * WINDOW, WINDOW), 0) % WINDOW
    k_in_blk = lax.broadcasted_iota(jnp.int32, (n_h * WINDOW, WINDOW), 1)
    key_ge_query = k_in_blk >= q_in_blk
    key_le_query = k_in_blk <= q_in_blk

    def run():
        def scores(sub):
            rows = slice(sub * WINDOW, (sub + 1) * WINDOW)
            q4 = jnp.concatenate([_masked_q(q_ref[rows, qs * LANE:(qs + 1) * LANE], qhalf)
                                  for qs, qhalf, _, _, _, _ in HEADS_GQA], axis=0)
            s_c = _dot_nt(q4, kc_ref[...])
            if not local:
                return s_c, None, None
            n = pl.program_id(1) * n_sub + sub
            blocks = [pl.ds(pl.multiple_of(b * WINDOW, WINDOW), WINDOW)
                      for b in (jnp.maximum(n - 1, 0), n, jnp.minimum(n + 1, n_blk - 1))]
            s = _dot_nt(q4, jnp.concatenate([kx_ref[blk, :] for blk in blocks], axis=0))
            s_l = jnp.concatenate([
                jnp.where(jnp.logical_and(key_ge_query, n > 0), s[:, 0:WINDOW], -jnp.inf),
                s[:, WINDOW:2 * WINDOW],
                jnp.where(jnp.logical_and(key_le_query, n < n_blk - 1), s[:, 2 * WINDOW:], -jnp.inf)], axis=1)
            return s_c, s_l, blocks

        pending = [scores(sub) for sub in range(min(ATT_LOOKAHEAD, n_sub))]
        for sub in range(n_sub):
            s_c, s_l, blocks = pending.pop(0)
            if sub + ATT_LOOKAHEAD < n_sub:
                pending.append(scores(sub + ATT_LOOKAHEAD))
            rows = slice(sub * WINDOW, (sub + 1) * WINDOW)
            mx = jnp.maximum(jnp.max(s_c, axis=-1, keepdims=True), sink)
            if local:
                mx = jnp.maximum(mx, jnp.max(s_l, axis=-1, keepdims=True))
            o = _dot(jnp.exp2((s_c - mx).astype(BF16)), vc_ref[...])
            if local:
                v3 = jnp.concatenate([vx_ref[blk, :] for blk in blocks], axis=0)
                o = o + _dot(jnp.exp2((s_l - mx).astype(BF16)), v3)
            sink_den = jnp.exp2(sink - mx)
            outs = {}
            for i, (_, _, _, vs, os_, ohalf) in enumerate(HEADS_GQA):
                blk = slice(i * WINDOW, (i + 1) * WINDOW)
                outs[(os_, ohalf)] = _normalise(o[blk, vs * LANE:(vs + 1) * LANE], ohalf, sink_den[blk])
            _store_heads(o_ref, rows, outs)

    run()


def _attn_win_ctx_kernel(sink_ref, q_ref, kc_ref, vc_ref, o_ref):
    _attn_win_kernel(sink_ref, q_ref, kc_ref, None, vc_ref, None, o_ref)


def _attn_win(sink, q, k, v, *, with_ctx, name):
    ox = _attn_latent_call(_attn_win_kernel, q, k, v, tq=WIN_TQ, name=name, smem=(sink,))
    if not with_ctx:
        return ox
    return _attn_ctx_call(_attn_win_ctx_kernel, q, k, v, name=name + "_ctx", smem=(sink,)), ox


Q = SSD_CHUNK
PAD = 8
U_CTX = PAD
U_X = PAD + CTX_LEN + PAD
U_ROWS = U_X + SEQ + PAD
N_CTX_CHUNK = CTX_LEN // Q
LOCAL_CHUNKS = 6
SCAN_STEPS = 8


def _split3(a):
    a1 = a.astype(BF16)
    r1 = a - a1.astype(F32)
    a2 = r1.astype(BF16)
    a3 = (r1 - a2.astype(F32)).astype(BF16)
    return a1, a2, a3


def _ssd_kernel(zc_ref, zx_ref, uc_ref, ux_ref, dtc_ref, dtx_ref, cw_ref, cb_ref, dtb_ref, alog_ref,
                dskip_ref, onorm_ref, yc_ref, yx_ref, upad, xs_s, bm_s, cm_s, dt_s, y_s, st_s, cme_s, da_s, h_s):
    n_slab = D_CONV_CH // LANE
    zpad = jnp.zeros((PAD, LANE), F32)
    for j in range(n_slab):
        sl = slice(j * LANE, (j + 1) * LANE)
        upad[j, 0:PAD, :] = zpad
        upad[j, U_CTX:U_CTX + CTX_LEN, :] = uc_ref[:, sl]
        upad[j, U_CTX + CTX_LEN:U_X, :] = zpad
        upad[j, U_X:U_X + SEQ, :] = ux_ref[:, sl]
        upad[j, U_X + SEQ:U_ROWS, :] = zpad

    for c in range(N_CHUNK):
        base = U_CTX + c * Q if c < N_CTX_CHUNK else U_X + (c - N_CTX_CHUNK) * Q
        for j in range(n_slab):
            sl = slice(j * LANE, (j + 1) * LANE)
            dst = (xs_s, bm_s, cm_s)[j // 2]
            acc = jnp.broadcast_to(cb_ref[:, sl], (Q, LANE))
            for k in range(D_CONV):
                lo = base + k - D_CONV // 2
                acc = acc + upad[j, lo:lo + Q, :] * cw_ref[k:k + 1, sl]
            dcol = (j % 2) * LANE
            dst[c * Q:(c + 1) * Q, dcol:dcol + LANE] = (acc * _sigmoid(acc)).astype(dst.dtype)

    def softplus(v):
        return jnp.maximum(v, 0.0) + jnp.log(1.0 + jnp.exp(-jnp.abs(v)))

    dt_s[0:CTX_LEN, :] = softplus(dtc_ref[...] + dtb_ref[...])
    dt_s[CTX_LEN:TOK, :] = softplus(dtx_ref[...] + dtb_ref[...])

    a_neg = -jnp.exp(alog_ref[...]) * LOG2E
    row = lax.broadcasted_iota(jnp.int32, (Q, Q), 0)
    col = lax.broadcasted_iota(jnp.int32, (Q, Q), 1)
    causal = (col <= row, col >= row)
    tri = (causal[0].astype(BF16), causal[1].astype(BF16))
    lo_half = col < HEAD_DIM
    last_row = (Q - 1, 0)

    def bcast_col(v, idx):
        return jnp.broadcast_to(v[:, idx:idx + 1], (Q, Q))

    def chunk_rows(c):
        return pl.ds(c * Q if isinstance(c, int) else pl.multiple_of(c * Q, Q), Q)

    def local_load(c):
        rows = chunk_rows(c)
        groups = [(cm_s[rows, g * Q:(g + 1) * Q], bm_s[rows, g * Q:(g + 1) * Q], xs_s[rows, g * Q:(g + 1) * Q])
                  for g in range(2)]
        return dt_s[rows, :], groups

    def local_sums(dt, groups):
        a1, a2, a3 = _split3(dt * a_neg)
        cs_f = _dot(tri[0], a1) + _dot(tri[0], a2) + _dot(tri[0], a3)
        cs_b = _dot(tri[1], a1) + _dot(tri[1], a2) + _dot(tri[1], a3)
        cs = jnp.where(col < D_HEADS, cs_f, cs_b)
        n_row = 2 * D_HEADS
        return cs, cs.T[0:n_row], dt.T[0:n_row], [_dot_nt(cmg, bmg) for cmg, bmg, _ in groups]

    def local_compute(groups, cs, cs_t, dt_t, cbms):
        ys, sts, das, cmes = [], {}, {}, {}
        for g, (cmg, bmg, xsg) in enumerate(groups):
            cbm = cbms[g]
            cmg32 = cmg.astype(F32)
            bm_t = bmg.astype(F32).T
            xh = (jnp.where(lo_half, xsg, 0.0).astype(BF16), jnp.where(lo_half, 0.0, xsg).astype(BF16))
            y = None
            for d in range(2):
                st = None
                da = []
                for hh in range(2):
                    idx = d * D_HEADS + g * 2 + hh
                    colb = bcast_col(cs, idx)
                    cs_row = cs_t[idx:idx + 1, :]
                    dt_row = dt_t[idx:idx + 1, :]
                    seg = colb - (cs_row - jnp.log2(dt_row))
                    dec = jnp.exp2(jnp.where(causal[d], seg, -jnp.inf))
                    yd = _dot((cbm * dec).astype(BF16), xh[hh])
                    y = yd if y is None else y + yd
                    last = cs_row[:, last_row[d]:last_row[d] + 1]
                    w_row = dt_row * jnp.exp2(last - cs_row)
                    sth = _dot((bm_t * w_row).astype(BF16), xh[hh])
                    st = sth if st is None else st + sth
                    cmes[d * 4 + g * 2 + hh] = (cmg32 * jnp.exp2(colb)).astype(BF16)
                    da.append(jnp.broadcast_to(jnp.exp2(last), (1, Q)))
                sts[d * 2 + g] = st
                das[d * 2 + g] = jnp.where(lo_half[0:1, :], da[0], da[1])
            ys.append(y)
        return ys, sts, das, cmes

    def local_store(c, res):
        ys, sts, das, cmes = res
        rows = chunk_rows(c)
        for g in range(2):
            y_s[rows, g * Q:(g + 1) * Q] = ys[g]
        for k, v in sts.items():
            st_s[c * 4 + k] = v
        for k, v in das.items():
            da_s[c * 4 + k, 0:1, :] = v
        for k, v in cmes.items():
            cme_s[c * 8 + k] = v

    n_iter = N_CHUNK // LOCAL_CHUNKS

    def iter_chunks(i):
        return [i * LOCAL_CHUNKS + u for u in range(LOCAL_CHUNKS)]

    def iter_sums(i):
        return [local_sums(*local_load(c)) for c in iter_chunks(i)]

    def local_body(i, sums):
        nxt = iter_sums(jnp.minimum(i + 1, n_iter - 1))
        chunks = iter_chunks(i)
        results = [local_compute(local_load(c)[1], *sm) for c, sm in zip(chunks, sums)]
        for c, res in zip(chunks, results):
            local_store(c, res)
        return nxt

    lax.fori_loop(0, n_iter, local_body, iter_sums(0))

    h_s[...] = jnp.zeros(h_s.shape, F32)

    def scan_steps(steps):
        work = []
        for c_fwd, c_bwd in steps:
            work += [(d, g, c) for d, c in ((0, c_fwd), (1, c_bwd)) for g in range(2)]
        loaded = []
        for d, g, c in work:
            k = c * 4 + d * 2 + g
            loaded.append((cme_s[2 * k], cme_s[2 * k + 1], da_s[k, 0:1, :], st_s[k],
                           y_s[chunk_rows(c), g * Q:(g + 1) * Q]))
        h = [h_s[k] for k in range(4)]
        ys = []
        for (d, g, c), (cme0, cme1, da, st, y) in zip(work, loaded):
            h_in = h[d * 2 + g]
            hb = h_in.astype(BF16)
            zero = jnp.zeros_like(hb)
            ys.append(y + _dot(cme0, jnp.where(lo_half, hb, zero)) + _dot(cme1, jnp.where(lo_half, zero, hb)))
            h[d * 2 + g] = da * h_in + st
        for (d, g, c), y in zip(work, ys):
            y_s[chunk_rows(c), g * Q:(g + 1) * Q] = y
        for k in range(4):
            h_s[k] = h[k]

    for i in range(N_CTX_CHUNK):
        scan_steps([(i, N_CTX_CHUNK - 1 - i)])

    def scan_body(t, carry):
        i0 = N_CTX_CHUNK + t * SCAN_STEPS
        scan_steps([(i0 + u, N_CHUNK - 1 + N_CTX_CHUNK - (i0 + u)) for u in range(SCAN_STEPS)])
        return carry

    lax.fori_loop(0, (N_CHUNK - N_CTX_CHUNK) // SCAN_STEPS, scan_body, 0)

    dskip = dskip_ref[...]
    onorm = onorm_ref[...]

    def finish(rows, z):
        y = y_s[rows, :] + dskip * xs_s[rows, :]
        return _rms(y * (z * _sigmoid(z)), onorm).astype(BF16)

    for c in range(N_CTX_CHUNK):
        yc_ref[c * Q:(c + 1) * Q, :] = finish(slice(c * Q, (c + 1) * Q), zc_ref[c * Q:(c + 1) * Q, :])

    def fin_body(c, carry):
        r0 = pl.multiple_of(c * Q, Q)
        yx_ref[pl.ds(r0, Q), :] = finish(pl.ds(CTX_LEN + r0, Q), zx_ref[pl.ds(r0, Q), :])
        return carry

    lax.fori_loop(0, SEQ // Q, fin_body, 0, unroll=2)


def _ssd_call(z, xbc, dt, cw, cb, dtb, alog, dskip, onorm):
    def cspec(wd):
        return pl.BlockSpec((None, CTX_LEN, wd), lambda b: (0, b, 0))

    def xspec(wd):
        return pl.BlockSpec((None, SEQ, wd), lambda b: (b + 1, 0, 0))

    return pl.pallas_call(
        _ssd_kernel,
        out_shape=[jax.ShapeDtypeStruct((BATCH, CTX_LEN, MIX), BF16),
                   jax.ShapeDtypeStruct((BATCH, SEQ, MIX), BF16)],
        grid=(BATCH,),
        in_specs=[cspec(MIX), xspec(MIX), cspec(D_CONV_CH), xspec(D_CONV_CH), cspec(LANE), xspec(LANE),
                  _full((8, D_CONV_CH)), _full((1, D_CONV_CH)), _full((1, LANE)), _full((1, LANE)),
                  _full((1, MIX)), _full((1, MIX))],
        out_specs=[pl.BlockSpec((None, CTX_LEN, MIX), lambda b: (b, 0, 0)),
                   pl.BlockSpec((None, SEQ, MIX), lambda b: (b, 0, 0))],
        scratch_shapes=[pltpu.VMEM((D_CONV_CH // LANE, U_ROWS, LANE), F32),
                        pltpu.VMEM((TOK, MIX), F32), pltpu.VMEM((TOK, MIX), BF16), pltpu.VMEM((TOK, MIX), BF16),
                        pltpu.VMEM((TOK, LANE), F32), pltpu.VMEM((TOK, MIX), F32),
                        pltpu.VMEM((N_CHUNK * 4, Q, Q), F32), pltpu.VMEM((N_CHUNK * 8, Q, Q), BF16),
                        pltpu.VMEM((N_CHUNK * 4, 8, LANE), F32), pltpu.VMEM((4, Q, Q), F32)],
        compiler_params=_params(("arbitrary",)),
        name="ssd",
    )(z, z, xbc, xbc, dt, dt, cw, cb, dtb, alog, dskip, onorm)


def _rope_tables():
    rows = SEQ // GRID_W
    r = np.repeat(np.arange(rows, dtype=np.float64), GRID_W)
    c = np.tile(np.arange(GRID_W, dtype=np.float64), rows)

    def tables(rot_dim):
        axis_dim = rot_dim // 2
        inv = ROPE_THETA ** (-np.arange(0, axis_dim, 2, dtype=np.float64) / axis_dim)
        ar = r[:, None] * inv[None, :]
        ac = c[:, None] * inv[None, :]
        cos = np.concatenate([np.cos(ar), np.cos(ar), np.cos(ac), np.cos(ac)], axis=-1)
        sin = np.concatenate([-np.sin(ar), np.sin(ar), -np.sin(ac), np.sin(ac)], axis=-1)
        return cos, sin

    c64, s64 = tables(HEAD_DIM)
    cab = np.tile(c64, (1, 2))
    sab = np.tile(s64, (1, 2))
    c32, s32 = tables(C_ROPE)
    cm = np.concatenate([np.ones((SEQ, C_NOPE)), c32, np.ones((SEQ, 32))], axis=-1)
    sm = np.concatenate([np.zeros((SEQ, C_NOPE)), s32, np.zeros((SEQ, 32))], axis=-1)
    return tuple(jnp.asarray(t, F32) for t in (cab, sab, cm, sm))


def _head_mean_matrix():
    lane = np.arange(LANE)
    same = (lane[:, None] // HEAD_DIM) == (lane[None, :] // HEAD_DIM)
    return jnp.asarray(np.where(same, 1.0 / HEAD_DIM, 0.0), BF16)


def _gqa_order(w, axis):
    shp = w.shape
    w = w.reshape(shp[:axis] + (2, 2, HEAD_DIM) + shp[axis + 1:])
    w = jnp.swapaxes(w, axis, axis + 1)
    return w.reshape(shp)


def _stacked_weights(w_in, c_w_uq, c_w_ukv, w_out):
    n_l, d = w_in.shape[:2]
    zc = lambda n: jnp.zeros((n_l, d, n), w_in.dtype)
    wb = w_in
    o_c = IN_AB
    o_d = IN_AB + IN_C
    w = jnp.concatenate([
        _gqa_order(wb[..., 0:MIX], 2), wb[..., MIX:512],
        _gqa_order(wb[..., 512:512 + MIX], 2), wb[..., 512 + MIX:o_c + C_Q_LORA + C_KV_LORA],
        zc(C_NOPE), wb[..., o_c + C_Q_LORA + C_KV_LORA:o_d], zc(LANE - C_NOPE - C_ROPE),
        wb[..., o_d:], zc(LANE - 2 * D_HEADS),
    ], axis=2).astype(BF16)
    assert w.shape[2] == PROJ_W
    dq = C_NOPE + C_ROPE
    pad_last = lambda t, n: jnp.pad(t, ((0, 0),) * (t.ndim - 1) + ((0, n),))
    wuq = pad_last(c_w_uq.reshape(n_l, C_Q_LORA, C_HEADS, dq), LANE - dq)
    wkv = c_w_ukv.reshape(n_l, C_KV_LORA, C_HEADS, C_NOPE + C_V)
    wk = pad_last(wkv[..., :C_NOPE], LANE - C_NOPE)
    head_parity = (jnp.arange(C_HEADS) % 2)[:, None]
    wv = jnp.stack([jnp.where(head_parity == half, wkv[..., C_NOPE:], 0.0) for half in range(2)], axis=3)
    wout = jnp.concatenate([_gqa_order(w_out[:, 0:MIX], 1), _gqa_order(w_out[:, MIX:2 * MIX], 1),
                            w_out[:, 2 * MIX:]], axis=1)
    return (w, wuq.reshape(n_l, C_Q_LORA, 512).astype(BF16), wk.reshape(n_l, C_KV_LORA, 512).astype(BF16),
            wv.reshape(n_l, C_KV_LORA, 512).astype(BF16), wout.astype(BF16))


def _lane_row(v, width=LANE):
    v = v.reshape(1, -1).astype(F32)
    return jnp.pad(v, ((0, 0), (0, width - v.shape[1])))


def kernel(x, c, ctx, c_ctx, ada_w, ada_b, ffn1_norm, ffn1_wi, ffn1_wo, mix_norm, w_in, w_out, a_sink, b_q_norm, b_k_norm, c_q_norm, c_w_uq, c_kv_norm, c_w_ukv, d_conv_w, d_conv_b, d_a_log, d_dt_bias, d_skip, d_out_norm, ffn2_norm, ffn2_wi, ffn2_wo, final_norm):
    cvec = jnp.concatenate([c_ctx[None, :], c, jnp.zeros((16 - NSEG, D_MODEL), F32)], axis=0)
    mods = _mods_call(cvec, ada_w, ada_b).reshape(DEPTH, 16, N_MOD, D_MODEL)[:, :NSEG]
    tabs = _rope_tables()
    bd = _head_mean_matrix()
    wi1, wo1, wi2, wo2 = ffn1_wi, ffn1_wo, ffn2_wi, ffn2_wo
    w, wuq, wk, wv, wout = _stacked_weights(w_in, c_w_uq, c_w_ukv, w_out)

    h = (ctx, x)
    out = None
    for l in range(DEPTH):
        with_ctx = l < DEPTH - 1
        mods_l = mods[l]
        h = _ffn_call(h, mods_l, ffn1_norm[l], wi1, wo1, l, k0=0, seg_off=0, name=f"ffn1_{l}")
        (qa, ka, va, qb, kb, vb, qc, kc, vc, z, xbc, dt) = _inproj_call(
            h, mods_l, mix_norm[l], w, l, tabs, bd,
            jnp.tile(b_q_norm[l], 2).reshape(1, LANE), jnp.tile(b_k_norm[l], 2).reshape(1, LANE),
            c_q_norm[l].reshape(1, C_Q_LORA), wuq, c_kv_norm[l].reshape(1, C_KV_LORA), wk, wv)

        oa = _attn_win(a_sink[l], qa, ka, va, with_ctx=with_ctx, name=f"attn_a_{l}")
        ob = _attn_dense(qb, kb, vb, heads=HEADS_GQA, with_ctx=with_ctx, name=f"attn_b_{l}")
        oc = _attn_dense(qc, kc, vc, heads=HEADS_MLA, with_ctx=with_ctx, name=f"attn_c_{l}")
        cw = jnp.pad(d_conv_w[l], ((0, 8 - D_CONV), (0, 0)))
        yc, yx = _ssd_call(z, xbc, dt, cw, d_conv_b[l].reshape(1, D_CONV_CH),
                           _lane_row(d_dt_bias[l]), _lane_row(d_a_log[l]),
                           jnp.repeat(d_skip[l], HEAD_DIM).reshape(1, MIX), d_out_norm[l].reshape(1, MIX))
        if with_ctx:
            h = _ffn_call(h, mods_l, ffn2_norm[l], wi2, wo2, l, k0=6, seg_off=0,
                          pre=(oa, ob, oc, (yc, yx), wout), name=f"ffn2_{l}")
        else:
            out = _ffn_call(h, mods_l, ffn2_norm[l], wi2, wo2, l, k0=6, seg_off=1,
                            pre=(oa, ob, oc, yx, wout), final_g=final_norm, name=f"ffn2_{l}")
    return out
```
